```python
import jax, jax.numpy as jnp
from jax import lax
import numpy as np

D_MODEL = 2048
BATCH = 4
SEQ = 2048
DEPTH = 1

MEM_LEN = 256
CHUNK = 64
EPS = 1e-6
GLA_HEADS = 4
GLA_DK = D_MODEL // 16
GLA_DV = D_MODEL // 8
GLA_QK = GLA_HEADS * GLA_DK
GLA_V = GLA_HEADS * GLA_DV
GLA_RANK = 16
GLA_TAU = 16.0
GDN_HEADS = 8
GDN_DK = D_MODEL // 16
GDN_DV = D_MODEL // 16
GDN_QK = GDN_HEADS * GDN_DK
GDN_V = GDN_HEADS * GDN_DV
CONV_W = 4
XA_HEADS = 4
XA_DH = D_MODEL // 8
XA_W = XA_HEADS * XA_DH
N_BRANCH = 3
D_FF = ((8 * D_MODEL // 3 + 255) // 256) * 256
IN_SIZES = (GLA_QK, GLA_QK, GLA_V, GLA_RANK, GLA_V,
            2 * GDN_QK + GDN_V, GDN_HEADS, GDN_HEADS, GDN_V,
            XA_W, N_BRANCH * D_MODEL)
W_IN_COLS = sum(IN_SIZES)

kernel_name = "hybrid_gla_gdn_memxattn_macaron"


def rmsnorm(x, w):
    xf = x.astype(jnp.float32)
    y = xf * lax.rsqrt(jnp.mean(xf * xf, axis=-1, keepdims=True) + EPS)
    return (y * w.astype(jnp.float32)).astype(x.dtype)


def l2norm(x):
    return x * lax.rsqrt(jnp.sum(x * x, axis=-1, keepdims=True) + EPS)


def swiglu(x, w_gu, w_down):
    g, u = jnp.split(x @ w_gu, 2, axis=-1)
    return (jax.nn.silu(g) * u) @ w_down


def to_chunks(t):
    b, s, h, d = t.shape
    return t.reshape(b, s // CHUNK, CHUNK, h, d).transpose(1, 0, 3, 2, 4)


def from_chunks(t):
    n, b, h, c, d = t.shape
    return t.transpose(1, 0, 3, 2, 4).reshape(b, n * c, h, d)


def heads_to_chunks(t):
    b, s, h = t.shape
    return t.reshape(b, s // CHUNK, CHUNK, h).transpose(1, 0, 3, 2)


def causal_short_conv(x, w):
    ch = x.shape[-1]
    y = lax.conv_general_dilated(x, w[:, None, :].astype(x.dtype), window_strides=(1,),
                                 padding=[(CONV_W - 1, 0)],
                                 dimension_numbers=('NWC', 'WIO', 'NWC'),
                                 feature_group_count=ch)
    return jax.nn.silu(y)


def gla_chunked(q, k, v, logg):
    b, s, h, dk = q.shape
    dv = v.shape[-1]
    causal = jnp.tril(jnp.ones((CHUNK, CHUNK), dtype=bool))

    def step(state, inp):
        q_, k_, v_, g_ = inp
        cb = jnp.cumsum(g_, axis=2)
        o_inter = jnp.einsum('bhcd,bhde->bhce', q_ * jnp.exp(cb), state)
        diff = cb[:, :, :, None, :] - cb[:, :, None, :, :]
        dec = jnp.exp(jnp.where(causal[:, :, None], diff, -jnp.inf))
        attn = jnp.einsum('bhid,bhjd,bhijd->bhij', q_, k_, dec)
        o = o_inter + jnp.einsum('bhij,bhje->bhie', attn, v_)
        c_last = cb[:, :, -1:, :]
        state = (jnp.exp(c_last[:, :, 0, :])[..., None] * state
                 + jnp.einsum('bhcd,bhce->bhde', k_ * jnp.exp(c_last - cb), v_))
        return state, o

    init = jnp.zeros((b, h, dk, dv), jnp.float32)
    _, o = lax.scan(step, init, (to_chunks(q), to_chunks(k), to_chunks(v), to_chunks(logg)))
    return from_chunks(o)


def gdn_chunked(q, k, v, g, beta):
    b, s, h, dk = q.shape
    dv = v.shape[-1]
    qc, kc, vc = to_chunks(q), to_chunks(k), to_chunks(v)
    gcum = jnp.cumsum(heads_to_chunks(g), axis=-1)
    bc = heads_to_chunks(beta)[..., None]
    causal = jnp.tril(jnp.ones((CHUNK, CHUNK), dtype=bool))
    strict = jnp.tril(jnp.ones((CHUNK, CHUNK), dtype=bool), -1)
    diff = gcum[..., :, None] - gcum[..., None, :]
    decay = jnp.where(causal, jnp.exp(jnp.where(causal, diff, 0.0)), 0.0)
    k_beta = kc * bc
    lower = jnp.where(strict, jnp.einsum('nbhid,nbhjd->nbhij', k_beta, kc) * decay, 0.0)
    tmat = lower + jnp.eye(CHUNK, dtype=lower.dtype)
    u = lax.linalg.triangular_solve(tmat, vc * bc, left_side=True, lower=True, unit_diagonal=True)
    w = lax.linalg.triangular_solve(tmat, k_beta * jnp.exp(gcum)[..., None],
                                    left_side=True, lower=True, unit_diagonal=True)

    def step(state, inp):
        q_, k_, u_, w_, gc_, dec_ = inp
        attn = jnp.where(causal, jnp.einsum('bhid,bhjd->bhij', q_, k_) * dec_, 0.0)
        v_new = u_ - jnp.einsum('bhcd,bhde->bhce', w_, state)
        o = (jnp.einsum('bhcd,bhde->bhce', q_ * jnp.exp(gc_)[..., None], state)
             + jnp.einsum('bhij,bhje->bhie', attn, v_new))
        g_last = gc_[..., -1:]
        state = (state * jnp.exp(g_last)[..., None]
                 + jnp.einsum('bhcd,bhce->bhde', k_ * jnp.exp(g_last - gc_)[..., None], v_new))
        return state, o

    init = jnp.zeros((b, h, dk, dv), jnp.float32)
    _, o = lax.scan(step, init, (qc, kc, u, w, gcum, decay))
    return from_chunks(o)


def hybrid_layer(h, mem, n_ffn1_pre, w_ffn1_gu, w_ffn1_down, n_ffn1_post, n_mix_pre, w_in,
                 gla_w_lr2, gla_b_lr, gla_norm, gdn_conv, gdn_a_log, gdn_dt_bias, gdn_norm,
                 mem_norm, w_mem_kv, w_up_gla, w_up_gdn, w_up_xa, w_out, n_mix_post,
                 n_ffn2_pre, w_ffn2_gu, w_ffn2_down, n_ffn2_post):
    f32 = jnp.float32
    bsz, s, _ = h.shape
    h = h + 0.5 * rmsnorm(swiglu(rmsnorm(h, n_ffn1_pre), w_ffn1_gu, w_ffn1_down), n_ffn1_post)

    u = rmsnorm(h, n_mix_pre)
    idx = [int(i) for i in np.cumsum(IN_SIZES)[:-1]]
    (gla_q, gla_k, gla_v, gla_lr, gla_r, gdn_qkv, gdn_b, gdn_a, gdn_z,
     xa_q, gate_logits) = jnp.split(u @ w_in, idx, axis=-1)

    q = gla_q.reshape(bsz, s, GLA_HEADS, GLA_DK).astype(f32) * (GLA_DK ** -0.5)
    k = gla_k.reshape(bsz, s, GLA_HEADS, GLA_DK).astype(f32)
    v = gla_v.reshape(bsz, s, GLA_HEADS, GLA_DV).astype(f32)
    logg = jax.nn.log_sigmoid((gla_lr @ gla_w_lr2 + gla_b_lr).astype(f32)) / GLA_TAU
    o = gla_chunked(q, k, v, logg.reshape(bsz, s, GLA_HEADS, GLA_DK))
    o = rmsnorm(o, gla_norm) * jax.nn.silu(gla_r.reshape(bsz, s, GLA_HEADS, GLA_DV).astype(f32))
    y_gla = o.reshape(bsz, s, GLA_V).astype(h.dtype) @ w_up_gla

    qkv = causal_short_conv(gdn_qkv, gdn_conv)
    q, k, v = jnp.split(qkv, [GDN_QK, 2 * GDN_QK], axis=-1)
    q = l2norm(q.reshape(bsz, s, GDN_HEADS, GDN_DK).astype(f32)) * (GDN_DK ** -0.5)
    k = l2norm(k.reshape(bsz, s, GDN_HEADS, GDN_DK).astype(f32))
    v = v.reshape(bsz, s, GDN_HEADS, GDN_DV).astype(f32)
    beta = jax.nn.sigmoid(gdn_b.astype(f32))
    g = -jnp.exp(gdn_a_log.astype(f32)) * jax.nn.softplus(gdn_a.astype(f32) + gdn_dt_bias.astype(f32))
    o = gdn_chunked(q, k, v, g, beta)
    o = rmsnorm(o, gdn_norm) * jax.nn.silu(gdn_z.reshape(bsz, s, GDN_HEADS, GDN_DV).astype(f32))
    y_gdn = o.reshape(bsz, s, GDN_V).astype(h.dtype) @ w_up_gdn

    mk, mv = jnp.split(rmsnorm(mem, mem_norm) @ w_mem_kv, 2, axis=-1)
    mk = mk.reshape(bsz, -1, XA_HEADS, XA_DH)
    mv = mv.reshape(bsz, -1, XA_HEADS, XA_DH)
    qx = xa_q.reshape(bsz, s, XA_HEADS, XA_DH)
    sc = jnp.einsum('bshd,bmhd->bhsm', qx, mk).astype(f32) * (XA_DH ** -0.5)
    pr = jax.nn.softmax(sc, axis=-1).astype(mv.dtype)
    o = jnp.einsum('bhsm,bmhd->bshd', pr, mv).reshape(bsz, s, XA_W)
    y_xa = o @ w_up_xa

    g_a, g_b, g_c = jnp.split(jax.nn.sigmoid(gate_logits), N_BRANCH, axis=-1)
    mixed = (g_a * y_gla + g_b * y_gdn + g_c * y_xa) @ w_out
    h = h + rmsnorm(mixed, n_mix_post)

    h = h + 0.5 * rmsnorm(swiglu(rmsnorm(h, n_ffn2_pre), w_ffn2_gu, w_ffn2_down), n_ffn2_post)
    return h


def setup_inputs(seed: int = 0) -> dict:
    key = jax.random.key(seed)
    ks = iter(list(jax.random.split(key, 40)))
    L = DEPTH

    def dense(shape, fan_in):
        return jax.random.normal(next(ks), shape, jnp.float32) * (fan_in ** -0.5)

    def gain(n):
        return 1.0 + 0.02 * jax.random.normal(next(ks), (L, n), jnp.float32)

    x = jax.random.normal(next(ks), (BATCH, SEQ, D_MODEL), jnp.float32)
    mem = jax.random.normal(next(ks), (BATCH, MEM_LEN, D_MODEL), jnp.float32)
    dt = jnp.exp(jax.random.uniform(next(ks), (L, GDN_HEADS), jnp.float32,
                                    jnp.log(1e-3), jnp.log(1e-1)))
    gdn_dt_bias = dt + jnp.log(-jnp.expm1(-dt))
    gdn_a_log = jnp.log(jax.random.uniform(next(ks), (L, GDN_HEADS), jnp.float32, 1.0, 16.0))
    return {
        "x": x,
        "mem": mem,
        "n_ffn1_pre": gain(D_MODEL),
        "w_ffn1_gu": dense((L, D_MODEL, 2 * D_FF), D_MODEL),
        "w_ffn1_down": dense((L, D_FF, D_MODEL), D_FF),
        "n_ffn1_post": gain(D_MODEL),
        "n_mix_pre": gain(D_MODEL),
        "w_in": dense((L, D_MODEL, W_IN_COLS), D_MODEL),
        "gla_w_lr2": dense((L, GLA_RANK, GLA_QK), GLA_RANK),
        "gla_b_lr": 0.1 * jax.random.normal(next(ks), (L, GLA_QK), jnp.float32),
        "gla_norm": gain(GLA_DV),
        "gdn_conv": dense((L, CONV_W, 2 * GDN_QK + GDN_V), CONV_W),
        "gdn_a_log": gdn_a_log,
        "gdn_dt_bias": gdn_dt_bias,
        "gdn_norm": gain(GDN_DV),
        "mem_norm": gain(D_MODEL),
        "w_mem_kv": dense((L, D_MODEL, 2 * XA_W), D_MODEL),
        "w_up_gla": dense((L, GLA_V, D_MODEL), GLA_V),
        "w_up_gdn": dense((L, GDN_V, D_MODEL), GDN_V),
        "w_up_xa": dense((L, XA_W, D_MODEL), XA_W),
        "w_out": dense((L, D_MODEL, D_MODEL), D_MODEL),
        "n_mix_post": gain(D_MODEL),
        "n_ffn2_pre": gain(D_MODEL),
        "w_ffn2_gu": dense((L, D_MODEL, 2 * D_FF), D_MODEL),
        "w_ffn2_down": dense((L, D_FF, D_MODEL), D_FF),
        "n_ffn2_post": gain(D_MODEL),
    }


def reference(x, mem, n_ffn1_pre, w_ffn1_gu, w_ffn1_down, n_ffn1_post, n_mix_pre, w_in,
              gla_w_lr2, gla_b_lr, gla_norm, gdn_conv, gdn_a_log, gdn_dt_bias, gdn_norm,
              mem_norm, w_mem_kv, w_up_gla, w_up_gdn, w_up_xa, w_out, n_mix_post,
              n_ffn2_pre, w_ffn2_gu, w_ffn2_down, n_ffn2_post):
    h = x
    for l in range(DEPTH):
        h = hybrid_layer(h, mem, n_ffn1_pre[l], w_ffn1_gu[l], w_ffn1_down[l], n_ffn1_post[l],
                         n_mix_pre[l], w_in[l], gla_w_lr2[l], gla_b_lr[l], gla_norm[l],
                         gdn_conv[l], gdn_a_log[l], gdn_dt_bias[l], gdn_norm[l],
                         mem_norm[l], w_mem_kv[l], w_up_gla[l], w_up_gdn[l], w_up_xa[l],
                         w_out[l], n_mix_post[l], n_ffn2_pre[l], w_ffn2_gu[l],
                         w_ffn2_down[l], n_ffn2_post[l])
    return h
```

```python
import functools

import numpy as np
import jax
import jax.numpy as jnp
from jax import lax
from jax.experimental import pallas as pl
from jax.experimental.pallas import tpu as pltpu

F32 = jnp.float32
BF16 = jnp.bfloat16

D_MODEL = 2048
MEM_LEN = 256
EPS = 1e-6
GLA_HEADS = 4
GLA_DK = 128
GLA_DV = 256
GLA_QK = GLA_HEADS * GLA_DK
GLA_V = GLA_HEADS * GLA_DV
GLA_RANK = 16
GLA_TAU = 16.0
GDN_HEADS = 8
GDN_DK = 128
GDN_DV = 128
GDN_QK = GDN_HEADS * GDN_DK
GDN_V = GDN_HEADS * GDN_DV
CONV_W = 4
XA_HEADS = 4
XA_DH = 256
XA_W = XA_HEADS * XA_DH
N_BRANCH = 3
D_FF = 5632

LANES = 128
GLA_CHUNK = 64
GDN_CHUNK = 128
VMEM_LIMIT = 56 * 1024 * 1024

_IN_SIZES = (GLA_QK, GLA_QK, GLA_V, GLA_RANK, GLA_V, 2 * GDN_QK + GDN_V, GDN_HEADS, GDN_HEADS, GDN_V,
             XA_W, N_BRANCH * D_MODEL)
_IN_OFF = tuple(int(v) for v in np.cumsum((0,) + _IN_SIZES))
SM_LR, SM_B, SM_A = 0, GLA_RANK, GLA_RANK + GDN_HEADS


def _dot(a, b):
    return jnp.dot(a, b, preferred_element_type=F32)


def _dot_nt(a, b):
    return lax.dot_general(a, b, (((1,), (1,)), ((), ())), preferred_element_type=F32)


def _dot_tn(a, b):
    return lax.dot_general(a, b, (((0,), (0,)), ((), ())), preferred_element_type=F32)


def _mm(a, b):
    return _dot(a.astype(BF16), b.astype(BF16))


def _split2(x):
    hi = x.astype(BF16)
    mid = (x - hi.astype(F32)).astype(BF16)
    return hi, mid


def _dot01(m01, x):
    hi, mid = _split2(x)
    return _dot(m01, hi) + _dot(m01, mid)


def _rms(x, gain):
    return x * lax.rsqrt(jnp.mean(x * x, axis=-1, keepdims=True) + EPS) * gain


def _silu(x):
    return x * jax.nn.sigmoid(x)


def _softplus(x):
    return jnp.maximum(x, 0.0) + jnp.log1p(jnp.exp(-jnp.abs(x)))


def _params(sem):
    return pltpu.CompilerParams(dimension_semantics=sem, vmem_limit_bytes=VMEM_LIMIT)


def _ffn_kernel(x_ref, gpre_ref, wg_ref, wu_ref, wd_ref, gpost_ref, gnext_ref, *out_and_scratch, emit_next):
    if emit_next:
        h_ref, un_ref, xn_scr, acc_scr = out_and_scratch
    else:
        h_ref, xn_scr, acc_scr = out_and_scratch
    j = pl.program_id(1)

    @pl.when(j == 0)
    def _():
        xn_scr[...] = _rms(x_ref[...], gpre_ref[...]).astype(BF16)
        acc_scr[...] = jnp.zeros_like(acc_scr)

    xn = xn_scr[...]
    g = _dot(xn, wg_ref[...])
    u = _dot(xn, wu_ref[...])
    hm = (_silu(g) * u).astype(BF16)
    acc_scr[...] += _dot(hm, wd_ref[...])

    @pl.when(j == pl.num_programs(1) - 1)
    def _():
        h = x_ref[...] + 0.5 * _rms(acc_scr[...], gpost_ref[...])
        h_ref[...] = h
        if emit_next:
            un_ref[...] = _rms(h, gnext_ref[...]).astype(BF16)


def _ffn(x, gpre, w_gu, w_down, gpost, gnext, emit_next, tm=512, tf=512):
    t, d = x.shape
    nf = D_FF // tf
    row = lambda i, j: (i, 0)
    const = lambda i, j: (0, 0)
    out_shape = [jax.ShapeDtypeStruct((t, d), F32)]
    out_specs = [pl.BlockSpec((tm, d), row)]
    if emit_next:
        out_shape.append(jax.ShapeDtypeStruct((t, d), BF16))
        out_specs.append(pl.BlockSpec((tm, d), row))
    res = pl.pallas_call(
        functools.partial(_ffn_kernel, emit_next=emit_next),
        grid=(t // tm, nf),
        in_specs=[
            pl.BlockSpec((tm, d), row),
            pl.BlockSpec((1, d), const),
            pl.BlockSpec((d, tf), lambda i, j: (0, j)),
            pl.BlockSpec((d, tf), lambda i, j: (0, nf + j)),
            pl.BlockSpec((tf, d), lambda i, j: (j, 0)),
            pl.BlockSpec((1, d), const),
            pl.BlockSpec((1, d), const),
        ],
        out_specs=out_specs,
        out_shape=out_shape,
        scratch_shapes=[pltpu.VMEM((tm, d), BF16), pltpu.VMEM((tm, d), F32)],
        compiler_params=_params(("parallel", "arbitrary")),
        name="ffn",
    )(x, gpre, w_gu, w_gu, w_down, gpost, gnext)
    return res if emit_next else res[0]


def _matmul_kernel(a_ref, w_ref, o_ref):
    o_ref[...] = _dot(a_ref[...], w_ref[...]).astype(o_ref.dtype)


def _matmul(a, w, out_dtype, tm, tn, name):
    m, k = a.shape
    n = w.shape[1]
    return pl.pallas_call(
        _matmul_kernel,
        grid=(m // tm, n // tn),
        in_specs=[pl.BlockSpec((tm, k), lambda i, j: (i, 0)), pl.BlockSpec((k, tn), lambda i, j: (0, j))],
        out_specs=pl.BlockSpec((tm, tn), lambda i, j: (i, j)),
        out_shape=jax.ShapeDtypeStruct((m, n), out_dtype),
        compiler_params=_params(("parallel", "parallel")),
        name=name,
    )(a, w)


def _norm_matmul_kernel(x_ref, g_ref, w_ref, o_ref, xn_scr):
    @pl.when(pl.program_id(1) == 0)
    def _():
        xn_scr[...] = _rms(x_ref[...], g_ref[...]).astype(BF16)

    o_ref[...] = _dot(xn_scr[...], w_ref[...]).astype(o_ref.dtype)


def _norm_matmul(x, gain, w, out_dtype, tm, tn, name):
    m, k = x.shape
    n = w.shape[1]
    return pl.pallas_call(
        _norm_matmul_kernel,
        grid=(m // tm, n // tn),
        in_specs=[pl.BlockSpec((tm, k), lambda i, j: (i, 0)), pl.BlockSpec((1, k), lambda i, j: (0, 0)),
                  pl.BlockSpec((k, tn), lambda i, j: (0, j))],
        out_specs=pl.BlockSpec((tm, tn), lambda i, j: (i, j)),
        out_shape=jax.ShapeDtypeStruct((m, n), out_dtype),
        scratch_shapes=[pltpu.VMEM((tm, k), BF16)],
        compiler_params=_params(("parallel", "arbitrary")),
        name=name,
    )(x, gain, w)


def _gla_levels(chunk):
    return [chunk >> (i + 1) for i in range(int(np.log2(chunk)))]


def _gla_exponent_matrix(chunk):
    c = chunk
    i = np.arange(c)[:, None]
    t = np.arange(c)[None, :]
    blocks = [(t <= i), (t > i)]
    for h in _gla_levels(c):
        r = (i // (2 * h)) * (2 * h) + h
        upper = i >= r
        blocks.append(np.where(upper, (t > r) & (t <= i), (t > i) & (t <= r)))
    return np.concatenate(blocks, axis=0).astype(np.float32)


def _gla_kernel(q_ref, k_ref, v_ref, sm_ref, r_ref, wlr_ref, blr_ref, gn_ref, mst_ref, o_ref, s_scr):
    c = GLA_CHUNK
    seq = q_ref.shape[0]
    s_scr[...] = jnp.zeros_like(s_scr)
    ri = lax.broadcasted_iota(jnp.int32, (c, c), 0)
    ci = lax.broadcasted_iota(jnp.int32, (c, c), 1)
    rowi = lax.broadcasted_iota(jnp.int32, (c, GLA_DK), 0)
    ones_b = jnp.ones((c, LANES), BF16)
    levels = _gla_levels(c)

    def body(n, carry):
        rows = pl.ds(pl.multiple_of(n * c, c), c)
        q = q_ref[rows, :] * (GLA_DK ** -0.5)
        k = k_ref[rows, :]
        v = v_ref[rows, :].astype(BF16)
        x = _dot(sm_ref[rows, :].astype(BF16), wlr_ref[...]) + blr_ref[...]
        lg = (jnp.minimum(x, 0.0) - jnp.log1p(jnp.exp(-jnp.abs(x)))) * (1.0 / GLA_TAU)
        hi, mid = _split2(lg)
        mst = mst_ref[...]
        e = jnp.exp(_dot(mst, hi) + _dot(mst, mid))
        qd = (q * e[0:c]).astype(BF16)
        kd = (k * e[c:2 * c]).astype(BF16)
        attn = jnp.where(ri == ci, _dot_nt(q.astype(BF16), k.astype(BF16)), 0.0)
        for lvl, h in enumerate(levels):
            f = e[(2 + lvl) * c:(3 + lvl) * c]
            upper = (rowi & h) != 0
            qt = jnp.where(upper, q * f, 0.0).astype(BF16)
            kt = jnp.where(upper, 0.0, k * f).astype(BF16)
            a = _dot_nt(qt, kt)
            sh = int(np.log2(2 * h))
            attn = attn + jnp.where((ri >> sh) == (ci >> sh), a, 0.0)
        s = s_scr[...]
        o = _dot(qd, s.astype(BF16)) + _dot(attn.astype(BF16), v)
        ccol = jnp.exp(_dot_tn(hi, ones_b) + _dot_tn(mid, ones_b))
        s_scr[...] = jnp.concatenate([ccol] * (GLA_DV // LANES), axis=1) * s + _dot_tn(kd, v)
        on = _rms(o, gn_ref[...]) * _silu(r_ref[rows, :].astype(F32))
        o_ref[rows, :] = on.astype(o_ref.dtype)
        return carry

    lax.fori_loop(0, seq // c, body, 0)


def _gla(pa, pb, ps, wlr, blr, gnorm, bsz, seq):
    t = bsz * seq
    mst = jnp.asarray(_gla_exponent_matrix(GLA_CHUNK), BF16)
    nq = GLA_QK // GLA_DK
    return pl.pallas_call(
        _gla_kernel,
        grid=(bsz, GLA_HEADS),
        in_specs=[
            pl.BlockSpec((seq, GLA_DK), lambda b, h: (b, h)),
            pl.BlockSpec((seq, GLA_DK), lambda b, h: (b, nq + h)),
            pl.BlockSpec((seq, GLA_DV), lambda b, h: (b, 2 * GLA_QK // GLA_DV + h)),
            pl.BlockSpec((seq, LANES), lambda b, h: (b, 0)),
            pl.BlockSpec((seq, GLA_DV), lambda b, h: (b, _PB_R // GLA_DV + h)),
            pl.BlockSpec((LANES, GLA_DK), lambda b, h: (0, h)),
            pl.BlockSpec((1, GLA_DK), lambda b, h: (0, h)),
            pl.BlockSpec((1, GLA_DV), lambda b, h: (0, 0)),
            pl.BlockSpec(mst.shape, lambda b, h: (0, 0)),
        ],
        out_specs=pl.BlockSpec((seq, GLA_DV), lambda b, h: (b, h)),
        out_shape=jax.ShapeDtypeStruct((t, GLA_V), BF16),
        scratch_shapes=[pltpu.VMEM((GLA_DK, GLA_DV), F32)],
        compiler_params=_params(("parallel", "parallel")),
        name="gla",
    )(pa, pa, pa, ps, pb, wlr, blr, gnorm, mst)


def _gdn_kernel(q_ref, k_ref, v_ref, wq_ref, wk_ref, wv_ref, sm_ref, par_ref, z_ref, gn_ref, o_ref,
                qs, ks, vs, us, ws, at, qds, kds, egl, s_scr):
    c = GDN_CHUNK
    seq = q_ref.shape[0]
    head = pl.program_id(1)

    ridx = lax.broadcasted_iota(jnp.int32, (seq, GDN_DK), 0)

    def conv(x_ref, w_ref):
        x = x_ref[...]
        w = w_ref[...]
        acc = x * w[CONV_W - 1:CONV_W, :]
        for sft in range(1, CONV_W):
            xs = jnp.where(ridx >= sft, pltpu.roll(x, sft, 0), 0.0)
            acc = acc + xs * w[CONV_W - 1 - sft:CONV_W - sft, :]
        return _silu(acc)

    def l2n(x):
        return x * lax.rsqrt(jnp.sum(x * x, axis=-1, keepdims=True) + EPS)

    qs[...] = l2n(conv(q_ref, wq_ref)) * (GDN_DK ** -0.5)
    ks[...] = l2n(conv(k_ref, wk_ref))
    vs[...] = conv(v_ref, wv_ref)

    ri = lax.broadcasted_iota(jnp.int32, (c, c), 0)
    ci = lax.broadcasted_iota(jnp.int32, (c, c), 1)
    lane = lax.broadcasted_iota(jnp.int32, (c, LANES), 1)
    tril = jnp.where(ri >= ci, 1.0, 0.0).astype(BF16)
    ones = jnp.ones((c, c), BF16)
    causal = ri >= ci
    n_sq = int(np.log2(c))

    def phase1(n, carry):
        rows = pl.ds(pl.multiple_of(n * c, c), c)
        q = qs[rows, :]
        k = ks[rows, :]
        v = vs[rows, :]
        sm = sm_ref[rows, :]
        par = par_ref[...]
        g_all = -jnp.exp(par[0:1, :]) * _softplus(sm + par[1:2, :])
        gcol = jnp.sum(jnp.where(lane == SM_A + head, g_all, 0.0), axis=-1, keepdims=True)
        beta = jnp.sum(jnp.where(lane == SM_B + head, jax.nn.sigmoid(sm), 0.0), axis=-1, keepdims=True)
        gb = jnp.broadcast_to(gcol, (c, LANES))
        cum_r = _dot01(tril, gb)
        cum_c = _dot01(ones, jnp.where(ri <= ci, gb, 0.0))
        dec = jnp.where(causal, jnp.exp(jnp.where(causal, cum_r - cum_c, 0.0)), 0.0)
        kb = k * beta
        kb16 = kb.astype(BF16)
        k16 = k.astype(BF16)
        low = jnp.where(ri > ci, _dot_nt(kb16, k16) * dec, 0.0)
        x = -low
        p = _mm(low, low)
        for it in range(n_sq - 1):
            x = x + p + _mm(x, p)
            if it < n_sq - 2:
                p = _mm(p, p)
        egc = jnp.exp(cum_r)
        rhs = jnp.concatenate([v * beta, kb * egc], axis=1)
        uw = rhs + _mm(x, rhs)
        us[rows, :] = uw[:, :GDN_DV]
        ws[rows, :] = uw[:, GDN_DV:].astype(BF16)
        at[rows, :] = jnp.where(causal, _dot_nt(q.astype(BF16), k16) * dec, 0.0).astype(BF16)
        qds[rows, :] = (q * egc).astype(BF16)
        glast = cum_r[c - 1:c, :]
        kds[rows, :] = (k * jnp.exp(glast - cum_r)).astype(BF16)
        egl[pl.ds(pl.multiple_of(n * 8, 8), 8), :] = jnp.broadcast_to(jnp.exp(glast), (8, LANES))
        return carry

    lax.fori_loop(0, seq // c, phase1, 0)

    s_scr[...] = jnp.zeros_like(s_scr)

    def phase2(n, carry):
        rows = pl.ds(pl.multiple_of(n * c, c), c)
        s = s_scr[...]
        s16 = s.astype(BF16)
        vnew = us[rows, :] - _dot(ws[rows, :], s16)
        vn16 = vnew.astype(BF16)
        o = _dot(qds[rows, :], s16) + _dot(at[rows, :], vn16)
        eg = egl[pl.ds(pl.multiple_of(n * 8, 8), 8), :][0:1, :]
        s_scr[...] = s * eg + _dot_tn(kds[rows, :], vn16)
        on = _rms(o, gn_ref[...]) * _silu(z_ref[rows, :].astype(F32))
        o_ref[rows, :] = on.astype(o_ref.dtype)
        return carry

    lax.fori_loop(0, seq // c, phase2, 0)


def _gdn(pa, pb, ps, conv_w, par, gnorm, bsz, seq):
    t = bsz * seq
    nb = seq // GDN_CHUNK
    q0 = _PA_GDN // GDN_DK
    nh = GDN_HEADS
    return pl.pallas_call(
        _gdn_kernel,
        grid=(bsz, GDN_HEADS),
        in_specs=[
            pl.BlockSpec((seq, GDN_DK), lambda b, h: (b, q0 + h)),
            pl.BlockSpec((seq, GDN_DK), lambda b, h: (b, q0 + nh + h)),
            pl.BlockSpec((seq, GDN_DV), lambda b, h: (b, q0 + 2 * nh + h)),
            pl.BlockSpec((CONV_W, GDN_DK), lambda b, h: (0, h)),
            pl.BlockSpec((CONV_W, GDN_DK), lambda b, h: (0, nh + h)),
            pl.BlockSpec((CONV_W, GDN_DV), lambda b, h: (0, 2 * nh + h)),
            pl.BlockSpec((seq, LANES), lambda b, h: (b, 0)),
            pl.BlockSpec((2, LANES), lambda b, h: (0, 0)),
            pl.BlockSpec((seq, GDN_DV), lambda b, h: (b, _PB_Z // GDN_DV + h)),
            pl.BlockSpec((1, GDN_DV), lambda b, h: (0, 0)),
        ],
        out_specs=pl.BlockSpec((seq, GDN_DV), lambda b, h: (b, h)),
        out_shape=jax.ShapeDtypeStruct((t, GDN_V), BF16),
        scratch_shapes=[
            pltpu.VMEM((seq, GDN_DK), F32), pltpu.VMEM((seq, GDN_DK), F32), pltpu.VMEM((seq, GDN_DV), F32),
            pltpu.VMEM((seq, GDN_DV), F32), pltpu.VMEM((seq, GDN_DK), BF16), pltpu.VMEM((seq, GDN_CHUNK), BF16),
            pltpu.VMEM((seq, GDN_DK), BF16), pltpu.VMEM((seq, GDN_DK), BF16), pltpu.VMEM((nb * 8, LANES), F32),
            pltpu.VMEM((GDN_DK, GDN_DV), F32),
        ],
        compiler_params=_params(("parallel", "parallel")),
        name="gdn",
    )(pa, pa, pa, conv_w, conv_w, conv_w, ps, par, pb, gnorm)


def _xa_kernel(q_ref, mk_ref, mv_ref, o_ref):
    s = _dot_nt(q_ref[...], mk_ref[...]) * (XA_DH ** -0.5)
    p = jnp.exp(s - jnp.max(s, axis=-1, keepdims=True))
    den = jnp.sum(p, axis=-1, keepdims=True)
    o_ref[...] = (_dot(p.astype(BF16), mv_ref[...]) / den).astype(o_ref.dtype)


def _xa(pb, mkv, bsz, seq, ts=1024):
    t = bsz * seq
    nt = seq // ts
    return pl.pallas_call(
        _xa_kernel,
        grid=(bsz, XA_HEADS, nt),
        in_specs=[
            pl.BlockSpec((ts, XA_DH), lambda b, h, i: (b * nt + i, _PB_XQ // XA_DH + h)),
            pl.BlockSpec((MEM_LEN, XA_DH), lambda b, h, i: (b, h)),
            pl.BlockSpec((MEM_LEN, XA_DH), lambda b, h, i: (b, XA_HEADS + h)),
        ],
        out_specs=pl.BlockSpec((ts, XA_DH), lambda b, h, i: (b * nt + i, h)),
        out_shape=jax.ShapeDtypeStruct((t, XA_W), BF16),
        compiler_params=_params(("parallel", "parallel", "parallel")),
        name="xattn",
    )(pb, mkv, mkv)


def _merge_kernel(h_ref, oa_ref, ob_ref, oc_ref, ga_ref, gb_ref, gc_ref, wa_ref, wb_ref, wc_ref, wo_ref,
                  gpost_ref, o_ref):
    mixed = jax.nn.sigmoid(ga_ref[...].astype(F32)) * _dot(oa_ref[...], wa_ref[...])
    mixed = mixed + jax.nn.sigmoid(gb_ref[...].astype(F32)) * _dot(ob_ref[...], wb_ref[...])
    mixed = mixed + jax.nn.sigmoid(gc_ref[...].astype(F32)) * _dot(oc_ref[...], wc_ref[...])
    m = _dot(mixed.astype(BF16), wo_ref[...])
    o_ref[...] = h_ref[...] + _rms(m, gpost_ref[...])


def _merge(h, oa, ob, oc, pb, wa, wb, wc, wo, gpost, tm=256):
    t, d = h.shape
    row = lambda i: (i, 0)
    const = lambda i: (0, 0)
    once = pl.Buffered(1)
    return pl.pallas_call(
        _merge_kernel,
        grid=(t // tm,),
        in_specs=[
            pl.BlockSpec((tm, d), row),
            pl.BlockSpec((tm, GLA_V), row), pl.BlockSpec((tm, GDN_V), row), pl.BlockSpec((tm, XA_W), row),
            pl.BlockSpec((tm, d), lambda i: (i, 0)), pl.BlockSpec((tm, d), lambda i: (i, 1)),
            pl.BlockSpec((tm, d), lambda i: (i, 2)),
            pl.BlockSpec((GLA_V, d), const, pipeline_mode=once), pl.BlockSpec((GDN_V, d), const, pipeline_mode=once),
            pl.BlockSpec((XA_W, d), const, pipeline_mode=once), pl.BlockSpec((d, d), const, pipeline_mode=once),
            pl.BlockSpec((1, d), const),
        ],
        out_specs=pl.BlockSpec((tm, d), row),
        out_shape=jax.ShapeDtypeStruct((t, d), F32),
        compiler_params=_params(("parallel",)),
        name="merge",
    )(h, oa, ob, oc, pb, pb, pb, wa, wb, wc, wo, gpost)


_PA_GDN = 2 * GLA_QK + GLA_V
_PA_COLS = _PA_GDN + 2 * GDN_QK + GDN_V
_PB_R = N_BRANCH * D_MODEL
_PB_Z = _PB_R + GLA_V
_PB_XQ = _PB_Z + GDN_V
_PB_COLS = _PB_XQ + XA_W


def _split_w_in(w_in):
    o = _IN_OFF
    cols = lambda i: w_in[:, o[i]:o[i + 1]]
    wa = jnp.concatenate([cols(0), cols(1), cols(2), cols(5)], axis=1).astype(BF16)
    wb = jnp.concatenate([cols(10), cols(4), cols(8), cols(9)], axis=1).astype(BF16)
    small = jnp.concatenate([cols(3), cols(6), cols(7)], axis=1)
    ws = jnp.pad(small, ((0, 0), (0, LANES - small.shape[1]))).astype(BF16)
    return wa, wb, ws


def kernel(x, mem, n_ffn1_pre, w_ffn1_gu, w_ffn1_down, n_ffn1_post, n_mix_pre, w_in, gla_w_lr2, gla_b_lr,
           gla_norm, gdn_conv, gdn_a_log, gdn_dt_bias, gdn_norm, mem_norm, w_mem_kv, w_up_gla, w_up_gdn,
           w_up_xa, w_out, n_mix_post, n_ffn2_pre, w_ffn2_gu, w_ffn2_down, n_ffn2_post):
    bsz, seq, d = x.shape
    t = bsz * seq
    h = x.reshape(t, d)
    for l in range(n_ffn1_pre.shape[0]):
        row = lambda a: a[l][None, :]
        h, u = _ffn(h, row(n_ffn1_pre), w_ffn1_gu[l].astype(BF16), w_ffn1_down[l].astype(BF16),
                    row(n_ffn1_post), row(n_mix_pre), emit_next=True)

        wa, wb, ws = _split_w_in(w_in[l])
        pa = _matmul(u, wa, F32, 2048, 512, "in_proj_a")
        pb = _matmul(u, wb, BF16, 2048, 1024, "in_proj_b")
        ps = _matmul(u, ws, F32, 2048, LANES, "in_proj_s")

        wlr = jnp.pad(gla_w_lr2[l], ((0, LANES - GLA_RANK), (0, 0))).astype(BF16)
        o_gla = _gla(pa, pb, ps, wlr, row(gla_b_lr), row(gla_norm), bsz, seq)

        par = jnp.zeros((2, LANES), F32)
        par = par.at[0, SM_A:SM_A + GDN_HEADS].set(gdn_a_log[l]).at[1, SM_A:SM_A + GDN_HEADS].set(gdn_dt_bias[l])
        o_gdn = _gdn(pa, pb, ps, gdn_conv[l], par, row(gdn_norm), bsz, seq)

        mkv = _norm_matmul(mem.reshape(bsz * MEM_LEN, d), row(mem_norm), w_mem_kv[l].astype(BF16), BF16,
                           bsz * MEM_LEN, 512, "mem_kv")
        o_xa = _xa(pb, mkv, bsz, seq)

        h = _merge(h, o_gla, o_gdn, o_xa, pb, w_up_gla[l].astype(BF16), w_up_gdn[l].astype(BF16),
                   w_up_xa[l].astype(BF16), w_out[l].astype(BF16), row(n_mix_post))
        h = _ffn(h, row(n_ffn2_pre), w_ffn2_gu[l].astype(BF16), w_ffn2_down[l].astype(BF16),
                 row(n_ffn2_post), row(n_ffn2_post), emit_next=False)
    return h.reshape(bsz, seq, d)
```

```python
import functools

import numpy as np
import jax
import jax.numpy as jnp
from jax import lax
from jax.experimental import pallas as pl
from jax.experimental.pallas import tpu as pltpu

F32 = jnp.float32
BF16 = jnp.bfloat16

D_MODEL = 2048
MEM_LEN = 256
EPS = 1e-6
GLA_HEADS = 4
GLA_DK = 128
GLA_DV = 256
GLA_QK = GLA_HEADS * GLA_DK
GLA_V = GLA_HEADS * GLA_DV
GLA_RANK = 16
GLA_TAU = 16.0
GDN_HEADS = 8
GDN_DK = 128
GDN_DV = 128
GDN_QK = GDN_HEADS * GDN_DK
GDN_V = GDN_HEADS * GDN_DV
CONV_W = 4
XA_HEADS = 4
XA_DH = 256
XA_W = XA_HEADS * XA_DH
N_BRANCH = 3
D_FF = 5632

LANES = 128
GLA_CHUNK = 64
GLA_GROUP = 4
GDN_CHUNK = 128
GDN_HPS = 2
GDN_GROUP = 2
VMEM_LIMIT = 56 * 1024 * 1024

_IN_SIZES = (GLA_QK, GLA_QK, GLA_V, GLA_RANK, GLA_V, 2 * GDN_QK + GDN_V, GDN_HEADS, GDN_HEADS, GDN_V,
             XA_W, N_BRANCH * D_MODEL)
_IN_OFF = tuple(int(v) for v in np.cumsum((0,) + _IN_SIZES))
SM_LR, SM_B, SM_A = 0, GLA_RANK, GLA_RANK + GDN_HEADS


def _dot(a, b):
    return jnp.dot(a, b, preferred_element_type=F32)


def _dot_nt(a, b):
    return lax.dot_general(a, b, (((1,), (1,)), ((), ())), preferred_element_type=F32)


def _dot_tn(a, b):
    return lax.dot_general(a, b, (((0,), (0,)), ((), ())), preferred_element_type=F32)


def _mm(a, b):
    return _dot(a.astype(BF16), b.astype(BF16))


def _split2(x):
    hi = x.astype(BF16)
    mid = (x - hi.astype(F32)).astype(BF16)
    return hi, mid


def _dot01(m01, x):
    hi, mid = _split2(x)
    return _dot(m01, hi) + _dot(m01, mid)


def _rms(x, gain):
    return x * lax.rsqrt(jnp.mean(x * x, axis=-1, keepdims=True) + EPS) * gain


def _silu(x):
    return x * jax.nn.sigmoid(x)


def _softplus(x):
    return jnp.maximum(x, 0.0) + jnp.log1p(jnp.exp(-jnp.abs(x)))


def _params(sem):
    return pltpu.CompilerParams(dimension_semantics=sem, vmem_limit_bytes=VMEM_LIMIT)


def _ffn_kernel(x_ref, gpre_ref, wg_ref, wu_ref, wd_ref, gpost_ref, gnext_ref, *out_and_scratch, emit_next):
    if emit_next:
        h_ref, un_ref, xn_scr, acc_scr = out_and_scratch
    else:
        h_ref, xn_scr, acc_scr = out_and_scratch
    j = pl.program_id(1)

    @pl.when(j == 0)
    def _():
        xn_scr[...] = _rms(x_ref[...], gpre_ref[...]).astype(BF16)
        acc_scr[...] = jnp.zeros_like(acc_scr)

    xn = xn_scr[...]
    g = _dot(xn, wg_ref[...])
    u = _dot(xn, wu_ref[...])
    hm = (_silu(g) * u).astype(BF16)
    acc_scr[...] += _dot(hm, wd_ref[...])

    @pl.when(j == pl.num_programs(1) - 1)
    def _():
        h = x_ref[...] + 0.5 * _rms(acc_scr[...], gpost_ref[...])
        h_ref[...] = h
        if emit_next:
            un_ref[...] = _rms(h, gnext_ref[...]).astype(BF16)


def _ffn(x, gpre, w_gu, w_down, gpost, gnext, emit_next, tm=512, tf=512):
    t, d = x.shape
    nf = D_FF // tf
    row = lambda i, j: (i, 0)
    const = lambda i, j: (0, 0)
    out_shape = [jax.ShapeDtypeStruct((t, d), F32)]
    out_specs = [pl.BlockSpec((tm, d), row)]
    if emit_next:
        out_shape.append(jax.ShapeDtypeStruct((t, d), BF16))
        out_specs.append(pl.BlockSpec((tm, d), row))
    res = pl.pallas_call(
        functools.partial(_ffn_kernel, emit_next=emit_next),
        grid=(t // tm, nf),
        in_specs=[
            pl.BlockSpec((tm, d), row),
            pl.BlockSpec((1, d), const),
            pl.BlockSpec((d, tf), lambda i, j: (0, j)),
            pl.BlockSpec((d, tf), lambda i, j: (0, nf + j)),
            pl.BlockSpec((tf, d), lambda i, j: (j, 0)),
            pl.BlockSpec((1, d), const),
            pl.BlockSpec((1, d), const),
        ],
        out_specs=out_specs,
        out_shape=out_shape,
        scratch_shapes=[pltpu.VMEM((tm, d), BF16), pltpu.VMEM((tm, d), F32)],
        compiler_params=_params(("parallel", "arbitrary")),
        name="ffn",
    )(x, gpre, w_gu, w_gu, w_down, gpost, gnext)
    return res if emit_next else res[0]


def _matmul_kernel(a_ref, w_ref, o_ref):
    o_ref[...] = _dot(a_ref[...], w_ref[...]).astype(o_ref.dtype)


def _matmul(a, w, out_dtype, tm, tn, name):
    m, k = a.shape
    n = w.shape[1]
    return pl.pallas_call(
        _matmul_kernel,
        grid=(m // tm, n // tn),
        in_specs=[pl.BlockSpec((tm, k), lambda i, j: (i, 0)), pl.BlockSpec((k, tn), lambda i, j: (0, j))],
        out_specs=pl.BlockSpec((tm, tn), lambda i, j: (i, j)),
        out_shape=jax.ShapeDtypeStruct((m, n), out_dtype),
        compiler_params=_params(("parallel", "parallel")),
        name=name,
    )(a, w)


def _norm_matmul_kernel(x_ref, g_ref, w_ref, o_ref, xn_scr):
    @pl.when(pl.program_id(1) == 0)
    def _():
        xn_scr[...] = _rms(x_ref[...], g_ref[...]).astype(BF16)

    o_ref[...] = _dot(xn_scr[...], w_ref[...]).astype(o_ref.dtype)


def _norm_matmul(x, gain, w, out_dtype, tm, tn, name):
    m, k = x.shape
    n = w.shape[1]
    return pl.pallas_call(
        _norm_matmul_kernel,
        grid=(m // tm, n // tn),
        in_specs=[pl.BlockSpec((tm, k), lambda i, j: (i, 0)), pl.BlockSpec((1, k), lambda i, j: (0, 0)),
                  pl.BlockSpec((k, tn), lambda i, j: (0, j))],
        out_specs=pl.BlockSpec((tm, tn), lambda i, j: (i, j)),
        out_shape=jax.ShapeDtypeStruct((m, n), out_dtype),
        scratch_shapes=[pltpu.VMEM((tm, k), BF16)],
        compiler_params=_params(("parallel", "arbitrary")),
        name=name,
    )(x, gain, w)


def _gla_levels(chunk):
    return [chunk >> (i + 1) for i in range(int(np.log2(chunk)))]


def _gla_exponent_matrix(chunk):
    c = chunk
    i = np.arange(c)[:, None]
    t = np.arange(c)[None, :]
    blocks = [(t <= i), (t > i)]
    for h in _gla_levels(c):
        r = (i // (2 * h)) * (2 * h) + h
        upper = i >= r
        blocks.append(np.where(upper, (t > r) & (t <= i), (t > i) & (t <= r)))
    m = np.concatenate(blocks, axis=0).astype(np.float32)
    z = np.zeros_like(m)
    return np.concatenate([np.concatenate([m, m, z, z], axis=1), np.concatenate([z, z, m, m], axis=1)], axis=0)


def _gla_kernel(q_ref, k_ref, v_ref, sm_ref, r_ref, wlr_ref, blr_ref, gn_ref, mst_ref, o_ref, s_scr):
    c = GLA_CHUNK
    seq = q_ref.shape[0]
    s_scr[...] = jnp.zeros_like(s_scr)
    ri = lax.broadcasted_iota(jnp.int32, (c, c), 0)
    ci = lax.broadcasted_iota(jnp.int32, (c, c), 1)
    rowi = lax.broadcasted_iota(jnp.int32, (c, GLA_DK), 0)
    levels = _gla_levels(c)
    assert GLA_GROUP == 4

    grp = GLA_GROUP
    gs = range(grp)

    def body(n, carry):
        row0 = pl.multiple_of(n * (grp * c), grp * c)
        rows = [pl.ds(row0 + g * c, c) for g in gs]
        allrows = pl.ds(row0, grp * c)
        x = _dot(sm_ref[allrows, :].astype(BF16), wlr_ref[...]) + blr_ref[...]
        lg = (jnp.minimum(x, 0.0) - jnp.log1p(jnp.exp(-jnp.abs(x)))) * (1.0 / GLA_TAU)
        hi, mid = _split2(lg)
        his = [hi[g * c:(g + 1) * c] for g in gs]
        mids = [mid[g * c:(g + 1) * c] for g in gs]
        hm = [jnp.concatenate([his[g], mids[g]], axis=0) for g in gs]
        rhs = jnp.concatenate([jnp.concatenate([hm[0], hm[1]], axis=1),
                               jnp.concatenate([hm[2], hm[3]], axis=1)], axis=0)
        ex = jnp.exp(_dot(mst_ref[...], rhs))
        nr = ex.shape[0] // 2
        e = [ex[0:nr, 0:GLA_DK], ex[0:nr, GLA_DK:], ex[nr:, 0:GLA_DK], ex[nr:, GLA_DK:]]
        q = [q_ref[rows[g], :] * (GLA_DK ** -0.5) for g in gs]
        k = [k_ref[rows[g], :] for g in gs]
        v = [v_ref[rows[g], :].astype(BF16) for g in gs]
        attn = [jnp.where(ri == ci, _dot_nt(q[g].astype(BF16), k[g].astype(BF16)), 0.0) for g in gs]
        for lvl, h in enumerate(levels):
            upper = (rowi & h) != 0
            sh = int(np.log2(2 * h))
            same = (ri >> sh) == (ci >> sh)
            for g in gs:
                f = e[g][(2 + lvl) * c:(3 + lvl) * c]
                qt = jnp.where(upper, q[g] * f, 0.0).astype(BF16)
                kt = jnp.where(upper, 0.0, k[g] * f).astype(BF16)
                attn[g] = attn[g] + jnp.where(same, _dot_nt(qt, kt), 0.0)
        vk = [_dot_tn(v[g], (k[g] * e[g][c:2 * c]).astype(BF16)) for g in gs]
        av = [_dot(attn[g].astype(BF16), v[g]) for g in gs]
        st = s_scr[...]
        for g in gs:
            o = _dot_nt((q[g] * e[g][0:c]).astype(BF16), st.astype(BF16)) + av[g]
            st = st * e[g][c - 1:c, :] + vk[g]
            on = _rms(o, gn_ref[...]) * _silu(r_ref[rows[g], :].astype(F32))
            o_ref[rows[g], :] = on.astype(o_ref.dtype)
        s_scr[...] = st
        return carry

    lax.fori_loop(0, seq // (grp * c), body, 0)


def _gla(pa, pb, ps, wlr, blr, gnorm, bsz, seq):
    t = bsz * seq
    mst = jnp.asarray(_gla_exponent_matrix(GLA_CHUNK), BF16)
    nq = GLA_QK // GLA_DK
    return pl.pallas_call(
        _gla_kernel,
        grid=(bsz, GLA_HEADS),
        in_specs=[
            pl.BlockSpec((seq, GLA_DK), lambda b, h: (b, h)),
            pl.BlockSpec((seq, GLA_DK), lambda b, h: (b, nq + h)),
            pl.BlockSpec((seq, GLA_DV), lambda b, h: (b, 2 * GLA_QK // GLA_DV + h)),
            pl.BlockSpec((seq, LANES), lambda b, h: (b, 0)),
            pl.BlockSpec((seq, GLA_DV), lambda b, h: (b, _PB_R // GLA_DV + h)),
            pl.BlockSpec((LANES, GLA_DK), lambda b, h: (0, h)),
            pl.BlockSpec((1, GLA_DK), lambda b, h: (0, h)),
            pl.BlockSpec((1, GLA_DV), lambda b, h: (0, 0)),
            pl.BlockSpec(mst.shape, lambda b, h: (0, 0)),
        ],
        out_specs=pl.BlockSpec((seq, GLA_DV), lambda b, h: (b, h)),
        out_shape=jax.ShapeDtypeStruct((t, GLA_V), BF16),
        scratch_shapes=[pltpu.VMEM((GLA_DV, GLA_DK), F32)],
        compiler_params=_params(("parallel", "parallel")),
        name="gla",
    )(pa, pa, pa, ps, pb, wlr, blr, gnorm, mst)


def _gdn_kernel(q_ref, k_ref, v_ref, wq_ref, wk_ref, wv_ref, sm_ref, par_ref, z_ref, gn_ref, o_ref,
                bms, ns, qms, os_, egl, s_scr):
    c = GDN_CHUNK
    seq = q_ref.shape[0]
    head0 = pl.program_id(1) * GDN_HPS

    ri = lax.broadcasted_iota(jnp.int32, (c, c), 0)
    ci = lax.broadcasted_iota(jnp.int32, (c, c), 1)
    r8 = lax.broadcasted_iota(jnp.int32, (8, GDN_HPS * GDN_DK), 0)
    lane = lax.broadcasted_iota(jnp.int32, (c, LANES), 1)
    tril = jnp.where(ri >= ci, 1.0, 0.0).astype(BF16)
    causal = ri >= ci
    n_sq = int(np.log2(c))

    def conv(x_ref, w_ref, n, rows):
        x = x_ref[rows, :]
        w = w_ref[...]
        prev = x_ref[pl.ds(pl.multiple_of(jnp.maximum(n * c - 8, 0), 8), 8), :]
        prev = jnp.where(n > 0, prev, 0.0)
        acc = x * w[CONV_W - 1:CONV_W, :]
        for sft in range(1, CONV_W):
            xs = pltpu.roll(x, sft, 0)
            top = jnp.where(r8 < sft, pltpu.roll(prev, sft, 0), xs[0:8])
            xs = jnp.concatenate([top, xs[8:]], axis=0)
            acc = acc + xs * w[CONV_W - 1 - sft:CONV_W - sft, :]
        return _silu(acc)

    def l2n(x):
        return x * lax.rsqrt(jnp.sum(x * x, axis=-1, keepdims=True) + EPS)

    grp = GDN_GROUP
    chains = [(g, j) for g in range(grp) for j in range(GDN_HPS)]

    def phase1(n, carry):
        par = par_ref[...]
        rows, q, k, kb, gb, rhs0 = {}, {}, {}, {}, {}, {}
        for g in range(grp):
            ng = n * grp + g
            rows[g] = pl.ds(pl.multiple_of(ng * c, c), c)
            qc = conv(q_ref, wq_ref, ng, rows[g])
            kc = conv(k_ref, wk_ref, ng, rows[g])
            vc = conv(v_ref, wv_ref, ng, rows[g])
            sm = sm_ref[rows[g], :]
            g_all = -jnp.exp(par[0:1, :]) * _softplus(sm + par[1:2, :])
            b_all = jax.nn.sigmoid(sm)
            for j in range(GDN_HPS):
                sl = slice(j * GDN_DK, (j + 1) * GDN_DK)
                q[g, j] = l2n(qc[:, sl]) * (GDN_DK ** -0.5)
                k[g, j] = l2n(kc[:, sl])
                gcol = jnp.sum(jnp.where(lane == SM_A + head0 + j, g_all, 0.0), axis=-1, keepdims=True)
                beta = jnp.sum(jnp.where(lane == SM_B + head0 + j, b_all, 0.0), axis=-1, keepdims=True)
                gb[g, j] = jnp.broadcast_to(gcol, (c, LANES))
                kb[g, j] = k[g, j] * beta
                rhs0[g, j] = vc[:, sl] * beta
        cum_r, cum_c, kt = {}, {}, {}
        for ch in chains:
            hi, mid = _split2(gb[ch])
            r = _dot(tril, jnp.concatenate([hi, mid], axis=1))
            cum_r[ch] = r[:, :LANES] + r[:, LANES:]
        for ch in chains:
            cum_c[ch] = cum_r[ch].T
            kt[ch] = k[ch].T
        kk = {}
        for ch in chains:
            k16 = k[ch].astype(BF16)
            kk[ch] = _dot_nt(jnp.concatenate([kb[ch].astype(BF16), q[ch].astype(BF16)], axis=0), k16)
        low, at16, x, p = {}, {}, {}, {}
        for ch in chains:
            dec = jnp.where(causal, jnp.exp(jnp.where(causal, cum_r[ch] - cum_c[ch], 0.0)), 0.0)
            low[ch] = jnp.where(ri > ci, kk[ch][:c] * dec, 0.0)
            at16[ch] = jnp.where(causal, kk[ch][c:] * dec, 0.0).astype(BF16)
        for ch in chains:
            x[ch] = -low[ch]
            p[ch] = _mm(low[ch], low[ch])
        for it in range(n_sq - 1):
            for ch in chains:
                if it < n_sq - 2:
                    r = _mm(jnp.concatenate([x[ch], p[ch]], axis=0), p[ch])
                    x[ch] = x[ch] + p[ch] + r[:c]
                    p[ch] = r[c:]
                else:
                    x[ch] = x[ch] + p[ch] + _mm(x[ch], p[ch])
        wu, egc = {}, {}
        for ch in chains:
            egc[ch] = jnp.exp(cum_r[ch])
            rhs = jnp.concatenate([kb[ch] * egc[ch], rhs0[ch]], axis=1)
            wu[ch] = (rhs + _mm(x[ch], rhs)).astype(BF16)
        for ch in chains:
            g, j = ch
            sl = slice(j * GDN_DK, (j + 1) * GDN_DK)
            glast = cum_r[ch][c - 1:c, :]
            kdt16 = (kt[ch] * jnp.exp(glast - cum_c[ch][0:1, :])).astype(BF16)
            r = _dot(jnp.concatenate([kdt16, at16[ch]], axis=0), wu[ch])
            bms[rows[g], sl] = (-r[:c, :GDN_DV]).astype(BF16)
            ns[rows[g], sl] = r[:c, GDN_DV:]
            qms[rows[g], sl] = (q[ch] * egc[ch] - r[c:, :GDN_DV]).astype(BF16)
            os_[rows[g], sl] = r[c:, GDN_DV:]
            egl[pl.ds(pl.multiple_of((n * grp + g) * 8, 8), 8), sl] = jnp.broadcast_to(jnp.exp(glast), (8, LANES))
        return carry

    lax.fori_loop(0, seq // (grp * c), phase1, 0)

    s_scr[...] = jnp.zeros_like(s_scr)

    def phase2(n, carry):
        rows = pl.ds(pl.multiple_of(n * c, c), c)
        for j in range(GDN_HPS):
            sl = slice(j * GDN_DK, (j + 1) * GDN_DK)
            s = s_scr[j]
            s16 = s.astype(BF16)
            eg = egl[pl.ds(pl.multiple_of(n * 8, 8), 8), sl][0:1, :]
            s_scr[j] = s * eg + _dot(bms[rows, sl], s16) + ns[rows, sl]
            o = _dot(qms[rows, sl], s16) + os_[rows, sl]
            on = _rms(o, gn_ref[...]) * _silu(z_ref[rows, sl].astype(F32))
            o_ref[rows, sl] = on.astype(o_ref.dtype)
        return carry

    lax.fori_loop(0, seq // c, phase2, 0)


def _gdn(pa, pb, ps, conv_w, par, gnorm, bsz, seq):
    t = bsz * seq
    nb = seq // GDN_CHUNK
    wd = GDN_HPS * GDN_DK
    q0 = _PA_GDN // wd
    ng = GDN_HEADS // GDN_HPS
    return pl.pallas_call(
        _gdn_kernel,
        grid=(bsz, ng),
        in_specs=[
            pl.BlockSpec((seq, wd), lambda b, h: (b, q0 + h)),
            pl.BlockSpec((seq, wd), lambda b, h: (b, q0 + ng + h)),
            pl.BlockSpec((seq, wd), lambda b, h: (b, q0 + 2 * ng + h)),
            pl.BlockSpec((CONV_W, wd), lambda b, h: (0, h)),
            pl.BlockSpec((CONV_W, wd), lambda b, h: (0, ng + h)),
            pl.BlockSpec((CONV_W, wd), lambda b, h: (0, 2 * ng + h)),
            pl.BlockSpec((seq, LANES), lambda b, h: (b, 0)),
            pl.BlockSpec((2, LANES), lambda b, h: (0, 0)),
            pl.BlockSpec((seq, wd), lambda b, h: (b, _PB_Z // wd + h)),
            pl.BlockSpec((1, GDN_DV), lambda b, h: (0, 0)),
        ],
        out_specs=pl.BlockSpec((seq, wd), lambda b, h: (b, h)),
        out_shape=jax.ShapeDtypeStruct((t, GDN_V), BF16),
        scratch_shapes=[
            pltpu.VMEM((seq, wd), BF16), pltpu.VMEM((seq, wd), F32), pltpu.VMEM((seq, wd), BF16),
            pltpu.VMEM((seq, wd), F32), pltpu.VMEM((nb * 8, wd), F32),
            pltpu.VMEM((GDN_HPS, GDN_DK, GDN_DV), F32),
        ],
        compiler_params=_params(("parallel", "parallel")),
        name="gdn",
    )(pa, pa, pa, conv_w, conv_w, conv_w, ps, par, pb, gnorm)


def _xa_kernel(q_ref, mk_ref, mv_ref, o_ref):
    s = _dot_nt(q_ref[...], mk_ref[...]) * (XA_DH ** -0.5)
    p = jnp.exp(s - jnp.max(s, axis=-1, keepdims=True))
    den = jnp.sum(p, axis=-1, keepdims=True)
    o_ref[...] = (_dot(p.astype(BF16), mv_ref[...]) / den).astype(o_ref.dtype)


def _xa(pb, mkv, bsz, seq, ts=1024):
    t = bsz * seq
    nt = seq // ts
    return pl.pallas_call(
        _xa_kernel,
        grid=(bsz, XA_HEADS, nt),
        in_specs=[
            pl.BlockSpec((ts, XA_DH), lambda b, h, i: (b * nt + i, _PB_XQ // XA_DH + h)),
            pl.BlockSpec((MEM_LEN, XA_DH), lambda b, h, i: (b, h)),
            pl.BlockSpec((MEM_LEN, XA_DH), lambda b, h, i: (b, XA_HEADS + h)),
        ],
        out_specs=pl.BlockSpec((ts, XA_DH), lambda b, h, i: (b * nt + i, h)),
        out_shape=jax.ShapeDtypeStruct((t, XA_W), BF16),
        compiler_params=_params(("parallel", "parallel", "parallel")),
        name="xattn",
    )(pb, mkv, mkv)


def _merge_kernel(h_ref, oa_ref, ob_ref, oc_ref, ga_ref, gb_ref, gc_ref, wa_ref, wb_ref, wc_ref, wo_ref,
                  gpost_ref, o_ref):
    mixed = jax.nn.sigmoid(ga_ref[...].astype(F32)) * _dot(oa_ref[...], wa_ref[...])
    mixed = mixed + jax.nn.sigmoid(gb_ref[...].astype(F32)) * _dot(ob_ref[...], wb_ref[...])
    mixed = mixed + jax.nn.sigmoid(gc_ref[...].astype(F32)) * _dot(oc_ref[...], wc_ref[...])
    m = _dot(mixed.astype(BF16), wo_ref[...])
    o_ref[...] = h_ref[...] + _rms(m, gpost_ref[...])


def _merge(h, oa, ob, oc, pb, wa, wb, wc, wo, gpost, tm=256):
    t, d = h.shape
    row = lambda i: (i, 0)
    const = lambda i: (0, 0)
    once = pl.Buffered(1)
    return pl.pallas_call(
        _merge_kernel,
        grid=(t // tm,),
        in_specs=[
            pl.BlockSpec((tm, d), row),
            pl.BlockSpec((tm, GLA_V), row), pl.BlockSpec((tm, GDN_V), row), pl.BlockSpec((tm, XA_W), row),
            pl.BlockSpec((tm, d), lambda i: (i, 0)), pl.BlockSpec((tm, d), lambda i: (i, 1)),
            pl.BlockSpec((tm, d), lambda i: (i, 2)),
            pl.BlockSpec((GLA_V, d), const, pipeline_mode=once), pl.BlockSpec((GDN_V, d), const, pipeline_mode=once),
            pl.BlockSpec((XA_W, d), const, pipeline_mode=once), pl.BlockSpec((d, d), const, pipeline_mode=once),
            pl.BlockSpec((1, d), const),
        ],
        out_specs=pl.BlockSpec((tm, d), row),
        out_shape=jax.ShapeDtypeStruct((t, d), F32),
        compiler_params=_params(("parallel",)),
        name="merge",
    )(h, oa, ob, oc, pb, pb, pb, wa, wb, wc, wo, gpost)


_PA_GDN = 2 * GLA_QK + GLA_V
_PA_COLS = _PA_GDN + 2 * GDN_QK + GDN_V
_PB_R = N_BRANCH * D_MODEL
_PB_Z = _PB_R + GLA_V
_PB_XQ = _PB_Z + GDN_V
_PB_COLS = _PB_XQ + XA_W


def _split_w_in(w_in):
    o = _IN_OFF
    cols = lambda i: w_in[:, o[i]:o[i + 1]]
    wa = jnp.concatenate([cols(0), cols(1), cols(2), cols(5)], axis=1).astype(BF16)
    wb = jnp.concatenate([cols(10), cols(4), cols(8), cols(9)], axis=1).astype(BF16)
    small = jnp.concatenate([cols(3), cols(6), cols(7)], axis=1)
    ws = jnp.pad(small, ((0, 0), (0, LANES - small.shape[1]))).astype(BF16)
    return wa, wb, ws


def kernel(x, mem, n_ffn1_pre, w_ffn1_gu, w_ffn1_down, n_ffn1_post, n_mix_pre, w_in, gla_w_lr2, gla_b_lr,
           gla_norm, gdn_conv, gdn_a_log, gdn_dt_bias, gdn_norm, mem_norm, w_mem_kv, w_up_gla, w_up_gdn,
           w_up_xa, w_out, n_mix_post, n_ffn2_pre, w_ffn2_gu, w_ffn2_down, n_ffn2_post):
    bsz, seq, d = x.shape
    t = bsz * seq
    h = x.reshape(t, d)
    for l in range(n_ffn1_pre.shape[0]):
        row = lambda a: a[l][None, :]
        h, u = _ffn(h, row(n_ffn1_pre), w_ffn1_gu[l].astype(BF16), w_ffn1_down[l].astype(BF16),
                    row(n_ffn1_post), row(n_mix_pre), emit_next=True)

        wa, wb, ws = _split_w_in(w_in[l])
        pa = _matmul(u, wa, F32, 2048, 512, "in_proj_a")
        pb = _matmul(u, wb, BF16, 2048, 1024, "in_proj_b")
        ps = _matmul(u, ws, F32, 2048, LANES, "in_proj_s")

        wlr = jnp.pad(gla_w_lr2[l], ((0, LANES - GLA_RANK), (0, 0))).astype(BF16)
        o_gla = _gla(pa, pb, ps, wlr, row(gla_b_lr), row(gla_norm), bsz, seq)

        par = jnp.zeros((2, LANES), F32)
        par = par.at[0, SM_A:SM_A + GDN_HEADS].set(gdn_a_log[l]).at[1, SM_A:SM_A + GDN_HEADS].set(gdn_dt_bias[l])
        o_gdn = _gdn(pa, pb, ps, gdn_conv[l], par, row(gdn_norm), bsz, seq)

        mkv = _norm_matmul(mem.reshape(bsz * MEM_LEN, d), row(mem_norm), w_mem_kv[l].astype(BF16), BF16,
                           bsz * MEM_LEN, 512, "mem_kv")
        o_xa = _xa(pb, mkv, bsz, seq)

        h = _merge(h, o_gla, o_gdn, o_xa, pb, w_up_gla[l].astype(BF16), w_up_gdn[l].astype(BF16),
                   w_up_xa[l].astype(BF16), w_out[l].astype(BF16), row(n_mix_post))
        h = _ffn(h, row(n_ffn2_pre), w_ffn2_gu[l].astype(BF16), w_ffn2_down[l].astype(BF16),
                 row(n_ffn2_post), row(n_ffn2_post), emit_next=False)
    return h.reshape(bsz, seq, d)
```

```python
import functools

import numpy as np
import jax
import jax.numpy as jnp
from jax import lax
from jax.experimental import pallas as pl
from jax.experimental.pallas import tpu as pltpu

F32 = jnp.float32
BF16 = jnp.bfloat16

D_MODEL = 2048
MEM_LEN = 256
EPS = 1e-6
GLA_HEADS = 4
GLA_DK = 128
GLA_DV = 256
GLA_QK = GLA_HEADS * GLA_DK
GLA_V = GLA_HEADS * GLA_DV
GLA_RANK = 16
GLA_TAU = 16.0
GDN_HEADS = 8
GDN_DK = 128
GDN_DV = 128
GDN_QK = GDN_HEADS * GDN_DK
GDN_V = GDN_HEADS * GDN_DV
CONV_W = 4
XA_HEADS = 4
XA_DH = 256
XA_W = XA_HEADS * XA_DH
N_BRANCH = 3
D_FF = 5632

LANES = 128
GLA_CHUNK = 64
GLA_GROUP = 4
GDN_CHUNK = 128
GDN_HPS = 2
GDN_GROUP = 2
VMEM_LIMIT = 56 * 1024 * 1024

_IN_SIZES = (GLA_QK, GLA_QK, GLA_V, GLA_RANK, GLA_V, 2 * GDN_QK + GDN_V, GDN_HEADS, GDN_HEADS, GDN_V,
             XA_W, N_BRANCH * D_MODEL)
_IN_OFF = tuple(int(v) for v in np.cumsum((0,) + _IN_SIZES))
SM_LR, SM_B, SM_A = 0, GLA_RANK, GLA_RANK + GDN_HEADS


def _dot(a, b):
    return jnp.dot(a, b, preferred_element_type=F32)


def _dot_nt(a, b):
    return lax.dot_general(a, b, (((1,), (1,)), ((), ())), preferred_element_type=F32)


def _dot_tn(a, b):
    return lax.dot_general(a, b, (((0,), (0,)), ((), ())), preferred_element_type=F32)


def _mm(a, b):
    return _dot(a.astype(BF16), b.astype(BF16))


def _split2(x):
    hi = x.astype(BF16)
    mid = (x - hi.astype(F32)).astype(BF16)
    return hi, mid


def _dot01(m01, x):
    hi, mid = _split2(x)
    return _dot(m01, hi) + _dot(m01, mid)


def _rms(x, gain):
    return x * lax.rsqrt(jnp.mean(x * x, axis=-1, keepdims=True) + EPS) * gain


def _silu(x):
    return x * jax.nn.sigmoid(x)


def _softplus(x):
    return jnp.maximum(x, 0.0) + jnp.log1p(jnp.exp(-jnp.abs(x)))


def _params(sem):
    return pltpu.CompilerParams(dimension_semantics=sem, vmem_limit_bytes=VMEM_LIMIT)


def _ffn_kernel(x_ref, gpre_ref, wg_ref, wu_ref, wd_ref, gpost_ref, gnext_ref, *out_and_scratch, emit_next):
    if emit_next:
        h_ref, un_ref, xn_scr, wg16, wu16, wd16 = out_and_scratch
    else:
        h_ref, xn_scr, wg16, wu16, wd16 = out_and_scratch
    j = pl.program_id(1)

    @pl.when(j == 0)
    def _():
        xn_scr[...] = _rms(x_ref[...], gpre_ref[...]).astype(BF16)
        h_ref[...] = jnp.zeros_like(h_ref)

    wg16[...] = wg_ref[...].astype(BF16)
    wu16[...] = wu_ref[...].astype(BF16)
    wd16[...] = wd_ref[...].astype(BF16)
    xn = xn_scr[...]
    g = _dot(xn, wg16[...])
    u = _dot(xn, wu16[...])
    hm = (_silu(g) * u).astype(BF16)
    h_ref[...] += _dot(hm, wd16[...])

    @pl.when(j == pl.num_programs(1) - 1)
    def _():
        h = x_ref[...] + 0.5 * _rms(h_ref[...], gpost_ref[...])
        h_ref[...] = h
        if emit_next:
            un_ref[...] = _rms(h, gnext_ref[...]).astype(BF16)


def _ffn(x, gpre, w_gu, w_down, gpost, gnext, emit_next, tm=1024, tf=256):
    t, d = x.shape
    nf = D_FF // tf
    row = lambda i, j: (i, 0)
    const = lambda i, j: (0, 0)
    once = pl.Buffered(1)
    out_shape = [jax.ShapeDtypeStruct((t, d), F32)]
    out_specs = [pl.BlockSpec((tm, d), row, pipeline_mode=once)]
    if emit_next:
        out_shape.append(jax.ShapeDtypeStruct((t, d), BF16))
        out_specs.append(pl.BlockSpec((tm, d), row, pipeline_mode=once))
    res = pl.pallas_call(
        functools.partial(_ffn_kernel, emit_next=emit_next),
        grid=(t // tm, nf),
        in_specs=[
            pl.BlockSpec((tm, d), row, pipeline_mode=once),
            pl.BlockSpec((1, d), const),
            pl.BlockSpec((d, tf), lambda i, j: (0, j)),
            pl.BlockSpec((d, tf), lambda i, j: (0, nf + j)),
            pl.BlockSpec((tf, d), lambda i, j: (j, 0)),
            pl.BlockSpec((1, d), const),
            pl.BlockSpec((1, d), const),
        ],
        out_specs=out_specs,
        out_shape=out_shape,
        scratch_shapes=[pltpu.VMEM((tm, d), BF16), pltpu.VMEM((d, tf), BF16), pltpu.VMEM((d, tf), BF16),
                        pltpu.VMEM((tf, d), BF16)],
        compiler_params=_params(("parallel", "arbitrary")),
        name="ffn",
    )(x, gpre, w_gu, w_gu, w_down, gpost, gnext)
    return res if emit_next else res[0]


def _matmul_kernel(a_ref, w_ref, o_ref):
    o_ref[...] = _dot(a_ref[...], w_ref[...]).astype(o_ref.dtype)


def _matmul(a, w, out_dtype, tm, tn, name):
    m, k = a.shape
    n = w.shape[1]
    return pl.pallas_call(
        _matmul_kernel,
        grid=(m // tm, n // tn),
        in_specs=[pl.BlockSpec((tm, k), lambda i, j: (i, 0)), pl.BlockSpec((k, tn), lambda i, j: (0, j))],
        out_specs=pl.BlockSpec((tm, tn), lambda i, j: (i, j)),
        out_shape=jax.ShapeDtypeStruct((m, n), out_dtype),
        compiler_params=_params(("parallel", "parallel")),
        name=name,
    )(a, w)


def _norm_matmul_kernel(x_ref, g_ref, w_ref, o_ref, xn_scr):
    @pl.when(pl.program_id(1) == 0)
    def _():
        xn_scr[...] = _rms(x_ref[...], g_ref[...]).astype(BF16)

    o_ref[...] = _dot(xn_scr[...], w_ref[...]).astype(o_ref.dtype)


def _norm_matmul(x, gain, w, out_dtype, tm, tn, name):
    m, k = x.shape
    n = w.shape[1]
    return pl.pallas_call(
        _norm_matmul_kernel,
        grid=(m // tm, n // tn),
        in_specs=[pl.BlockSpec((tm, k), lambda i, j: (i, 0)), pl.BlockSpec((1, k), lambda i, j: (0, 0)),
                  pl.BlockSpec((k, tn), lambda i, j: (0, j))],
        out_specs=pl.BlockSpec((tm, tn), lambda i, j: (i, j)),
        out_shape=jax.ShapeDtypeStruct((m, n), out_dtype),
        scratch_shapes=[pltpu.VMEM((tm, k), BF16)],
        compiler_params=_params(("parallel", "arbitrary")),
        name=name,
    )(x, gain, w)


def _gla_levels(chunk):
    return [chunk >> (i + 1) for i in range(int(np.log2(chunk)))]


def _gla_exponent_matrix(chunk):
    c = chunk
    i = np.arange(c)[:, None]
    t = np.arange(c)[None, :]
    blocks = [(t <= i), (t > i)]
    for h in _gla_levels(c):
        r = (i // (2 * h)) * (2 * h) + h
        upper = i >= r
        blocks.append(np.where(upper, (t > r) & (t <= i), (t > i) & (t <= r)))
    m = np.concatenate(blocks, axis=0).astype(np.float32)
    z = np.zeros_like(m)
    return np.concatenate([np.concatenate([m, m, z, z], axis=1), np.concatenate([z, z, m, m], axis=1)], axis=0)


def _gla_kernel(q_ref, k_ref, v_ref, sm_ref, r_ref, wlr_ref, blr_ref, gn_ref, mst_ref, o_ref, s_scr):
    c = GLA_CHUNK
    seq = q_ref.shape[0]
    s_scr[...] = jnp.zeros_like(s_scr)
    ri = lax.broadcasted_iota(jnp.int32, (c, c), 0)
    ci = lax.broadcasted_iota(jnp.int32, (c, c), 1)
    rowi = lax.broadcasted_iota(jnp.int32, (c, GLA_DK), 0)
    levels = _gla_levels(c)
    assert GLA_GROUP == 4

    grp = GLA_GROUP
    gs = range(grp)

    def body(n, carry):
        row0 = pl.multiple_of(n * (grp * c), grp * c)
        rows = [pl.ds(row0 + g * c, c) for g in gs]
        allrows = pl.ds(row0, grp * c)
        x = _dot(sm_ref[allrows, :].astype(BF16), wlr_ref[...]) + blr_ref[...]
        lg = (jnp.minimum(x, 0.0) - jnp.log1p(jnp.exp(-jnp.abs(x)))) * (1.0 / GLA_TAU)
        hi, mid = _split2(lg)
        his = [hi[g * c:(g + 1) * c] for g in gs]
        mids = [mid[g * c:(g + 1) * c] for g in gs]
        hm = [jnp.concatenate([his[g], mids[g]], axis=0) for g in gs]
        rhs = jnp.concatenate([jnp.concatenate([hm[0], hm[1]], axis=1),
                               jnp.concatenate([hm[2], hm[3]], axis=1)], axis=0)
        ex = jnp.exp(_dot(mst_ref[...], rhs))
        nr = ex.shape[0] // 2
        e = [ex[0:nr, 0:GLA_DK], ex[0:nr, GLA_DK:], ex[nr:, 0:GLA_DK], ex[nr:, GLA_DK:]]
        q = [q_ref[rows[g], :] * (GLA_DK ** -0.5) for g in gs]
        k = [k_ref[rows[g], :] for g in gs]
        v = [v_ref[rows[g], :].astype(BF16) for g in gs]
        attn = [jnp.where(ri == ci, _dot_nt(q[g].astype(BF16), k[g].astype(BF16)), 0.0) for g in gs]
        for lvl, h in enumerate(levels):
            upper = (rowi & h) != 0
            sh = int(np.log2(2 * h))
            same = (ri >> sh) == (ci >> sh)
            for g in gs:
                f = e[g][(2 + lvl) * c:(3 + lvl) * c]
                qt = jnp.where(upper, q[g] * f, 0.0).astype(BF16)
                kt = jnp.where(upper, 0.0, k[g] * f).astype(BF16)
                attn[g] = attn[g] + jnp.where(same, _dot_nt(qt, kt), 0.0)
        vk = [_dot_tn(v[g], (k[g] * e[g][c:2 * c]).astype(BF16)) for g in gs]
        av = [_dot(attn[g].astype(BF16), v[g]) for g in gs]
        st = s_scr[...]
        for g in gs:
            o = _dot_nt((q[g] * e[g][0:c]).astype(BF16), st.astype(BF16)) + av[g]
            st = st * e[g][c - 1:c, :] + vk[g]
            on = _rms(o, gn_ref[...]) * _silu(r_ref[rows[g], :].astype(F32))
            o_ref[rows[g], :] = on.astype(o_ref.dtype)
        s_scr[...] = st
        return carry

    lax.fori_loop(0, seq // (grp * c), body, 0)


def _gla(pa, pb, ps, wlr, blr, gnorm, bsz, seq):
    t = bsz * seq
    mst = jnp.asarray(_gla_exponent_matrix(GLA_CHUNK), BF16)
    nq = GLA_QK // GLA_DK
    return pl.pallas_call(
        _gla_kernel,
        grid=(bsz, GLA_HEADS),
        in_specs=[
            pl.BlockSpec((seq, GLA_DK), lambda b, h: (b, h)),
            pl.BlockSpec((seq, GLA_DK), lambda b, h: (b, nq + h)),
            pl.BlockSpec((seq, GLA_DV), lambda b, h: (b, 2 * GLA_QK // GLA_DV + h)),
            pl.BlockSpec((seq, LANES), lambda b, h: (b, 0)),
            pl.BlockSpec((seq, GLA_DV), lambda b, h: (b, _PB_R // GLA_DV + h)),
            pl.BlockSpec((LANES, GLA_DK), lambda b, h: (0, h)),
            pl.BlockSpec((1, GLA_DK), lambda b, h: (0, h)),
            pl.BlockSpec((1, GLA_DV), lambda b, h: (0, 0)),
            pl.BlockSpec(mst.shape, lambda b, h: (0, 0)),
        ],
        out_specs=pl.BlockSpec((seq, GLA_DV), lambda b, h: (b, h)),
        out_shape=jax.ShapeDtypeStruct((t, GLA_V), BF16),
        scratch_shapes=[pltpu.VMEM((GLA_DV, GLA_DK), F32)],
        compiler_params=_params(("parallel", "parallel")),
        name="gla",
    )(pa, pa, pa, ps, pb, wlr, blr, gnorm, mst)


def _gdn_kernel(q_ref, k_ref, v_ref, wq_ref, wk_ref, wv_ref, sm_ref, par_ref, z_ref, gn_ref, o_ref,
                bms, ns, qms, os_, egl, s_scr):
    c = GDN_CHUNK
    seq = q_ref.shape[0]
    head0 = pl.program_id(1) * GDN_HPS

    ri = lax.broadcasted_iota(jnp.int32, (c, c), 0)
    ci = lax.broadcasted_iota(jnp.int32, (c, c), 1)
    r8 = lax.broadcasted_iota(jnp.int32, (8, GDN_HPS * GDN_DK), 0)
    lane = lax.broadcasted_iota(jnp.int32, (c, LANES), 1)
    tril = jnp.where(ri >= ci, 1.0, 0.0).astype(BF16)
    causal = ri >= ci
    n_sq = int(np.log2(c))

    def conv(x_ref, w_ref, n, rows):
        x = x_ref[rows, :]
        w = w_ref[...]
        prev = x_ref[pl.ds(pl.multiple_of(jnp.maximum(n * c - 8, 0), 8), 8), :]
        prev = jnp.where(n > 0, prev, 0.0)
        acc = x * w[CONV_W - 1:CONV_W, :]
        for sft in range(1, CONV_W):
            xs = pltpu.roll(x, sft, 0)
            top = jnp.where(r8 < sft, pltpu.roll(prev, sft, 0), xs[0:8])
            xs = jnp.concatenate([top, xs[8:]], axis=0)
            acc = acc + xs * w[CONV_W - 1 - sft:CONV_W - sft, :]
        return _silu(acc)

    def l2n(x):
        return x * lax.rsqrt(jnp.sum(x * x, axis=-1, keepdims=True) + EPS)

    grp = GDN_GROUP
    chains = [(g, j) for g in range(grp) for j in range(GDN_HPS)]

    def phase1(n, carry):
        par = par_ref[...]
        rows, q, k, kb, gb, rhs0 = {}, {}, {}, {}, {}, {}
        for g in range(grp):
            ng = n * grp + g
            rows[g] = pl.ds(pl.multiple_of(ng * c, c), c)
            qc = conv(q_ref, wq_ref, ng, rows[g])
            kc = conv(k_ref, wk_ref, ng, rows[g])
            vc = conv(v_ref, wv_ref, ng, rows[g])
            sm = sm_ref[rows[g], :]
            g_all = -jnp.exp(par[0:1, :]) * _softplus(sm + par[1:2, :])
            b_all = jax.nn.sigmoid(sm)
            for j in range(GDN_HPS):
                sl = slice(j * GDN_DK, (j + 1) * GDN_DK)
                q[g, j] = l2n(qc[:, sl]) * (GDN_DK ** -0.5)
                k[g, j] = l2n(kc[:, sl])
                gcol = jnp.sum(jnp.where(lane == SM_A + head0 + j, g_all, 0.0), axis=-1, keepdims=True)
                beta = jnp.sum(jnp.where(lane == SM_B + head0 + j, b_all, 0.0), axis=-1, keepdims=True)
                gb[g, j] = jnp.broadcast_to(gcol, (c, LANES))
                kb[g, j] = k[g, j] * beta
                rhs0[g, j] = vc[:, sl] * beta
        cum_r, cum_c, kt = {}, {}, {}
        for ch in chains:
            hi, mid = _split2(gb[ch])
            r = _dot(tril, jnp.concatenate([hi, mid], axis=1))
            cum_r[ch] = r[:, :LANES] + r[:, LANES:]
        for ch in chains:
            cum_c[ch] = cum_r[ch].T
            kt[ch] = k[ch].T
        kk = {}
        for ch in chains:
            k16 = k[ch].astype(BF16)
            kk[ch] = _dot_nt(jnp.concatenate([kb[ch].astype(BF16), q[ch].astype(BF16)], axis=0), k16)
        low, at16, x, p = {}, {}, {}, {}
        for ch in chains:
            dec = jnp.where(causal, jnp.exp(jnp.where(causal, cum_r[ch] - cum_c[ch], 0.0)), 0.0)
            low[ch] = jnp.where(ri > ci, kk[ch][:c] * dec, 0.0)
            at16[ch] = jnp.where(causal, kk[ch][c:] * dec, 0.0).astype(BF16)
        for ch in chains:
            x[ch] = -low[ch]
            p[ch] = _mm(low[ch], low[ch])
        for it in range(n_sq - 1):
            for ch in chains:
                if it < n_sq - 2:
                    r = _mm(jnp.concatenate([x[ch], p[ch]], axis=0), p[ch])
                    x[ch] = x[ch] + p[ch] + r[:c]
                    p[ch] = r[c:]
                else:
                    x[ch] = x[ch] + p[ch] + _mm(x[ch], p[ch])
        wu, egc = {}, {}
        for ch in chains:
            egc[ch] = jnp.exp(cum_r[ch])
            rhs = jnp.concatenate([kb[ch] * egc[ch], rhs0[ch]], axis=1)
            wu[ch] = (rhs + _mm(x[ch], rhs)).astype(BF16)
        for ch in chains:
            g, j = ch
            sl = slice(j * GDN_DK, (j + 1) * GDN_DK)
            glast = cum_r[ch][c - 1:c, :]
            kdt16 = (kt[ch] * jnp.exp(glast - cum_c[ch][0:1, :])).astype(BF16)
            r = _dot(jnp.concatenate([kdt16, at16[ch]], axis=0), wu[ch])
            bms[rows[g], sl] = (-r[:c, :GDN_DV]).astype(BF16)
            ns[rows[g], sl] = r[:c, GDN_DV:]
            qms[rows[g], sl] = (q[ch] * egc[ch] - r[c:, :GDN_DV]).astype(BF16)
            os_[rows[g], sl] = r[c:, GDN_DV:]
            egl[pl.ds(pl.multiple_of((n * grp + g) * 8, 8), 8), sl] = jnp.broadcast_to(jnp.exp(glast), (8, LANES))
        return carry

    lax.fori_loop(0, seq // (grp * c), phase1, 0)

    s_scr[...] = jnp.zeros_like(s_scr)

    def phase2(n, carry):
        rows = pl.ds(pl.multiple_of(n * c, c), c)
        for j in range(GDN_HPS):
            sl = slice(j * GDN_DK, (j + 1) * GDN_DK)
            s = s_scr[j]
            s16 = s.astype(BF16)
            eg = egl[pl.ds(pl.multiple_of(n * 8, 8), 8), sl][0:1, :]
            s_scr[j] = s * eg + _dot(bms[rows, sl], s16) + ns[rows, sl]
            o = _dot(qms[rows, sl], s16) + os_[rows, sl]
            on = _rms(o, gn_ref[...]) * _silu(z_ref[rows, sl].astype(F32))
            o_ref[rows, sl] = on.astype(o_ref.dtype)
        return carry

    lax.fori_loop(0, seq // c, phase2, 0)


def _gdn(pa, pb, ps, conv_w, par, gnorm, bsz, seq):
    t = bsz * seq
    nb = seq // GDN_CHUNK
    wd = GDN_HPS * GDN_DK
    q0 = _PA_GDN // wd
    ng = GDN_HEADS // GDN_HPS
    return pl.pallas_call(
        _gdn_kernel,
        grid=(bsz, ng),
        in_specs=[
            pl.BlockSpec((seq, wd), lambda b, h: (b, q0 + h)),
            pl.BlockSpec((seq, wd), lambda b, h: (b, q0 + ng + h)),
            pl.BlockSpec((seq, wd), lambda b, h: (b, q0 + 2 * ng + h)),
            pl.BlockSpec((CONV_W, wd), lambda b, h: (0, h)),
            pl.BlockSpec((CONV_W, wd), lambda b, h: (0, ng + h)),
            pl.BlockSpec((CONV_W, wd), lambda b, h: (0, 2 * ng + h)),
            pl.BlockSpec((seq, LANES), lambda b, h: (b, 0)),
            pl.BlockSpec((2, LANES), lambda b, h: (0, 0)),
            pl.BlockSpec((seq, wd), lambda b, h: (b, _PB_Z // wd + h)),
            pl.BlockSpec((1, GDN_DV), lambda b, h: (0, 0)),
        ],
        out_specs=pl.BlockSpec((seq, wd), lambda b, h: (b, h)),
        out_shape=jax.ShapeDtypeStruct((t, GDN_V), BF16),
        scratch_shapes=[
            pltpu.VMEM((seq, wd), BF16), pltpu.VMEM((seq, wd), F32), pltpu.VMEM((seq, wd), BF16),
            pltpu.VMEM((seq, wd), F32), pltpu.VMEM((nb * 8, wd), F32),
            pltpu.VMEM((GDN_HPS, GDN_DK, GDN_DV), F32),
        ],
        compiler_params=_params(("parallel", "parallel")),
        name="gdn",
    )(pa, pa, pa, conv_w, conv_w, conv_w, ps, par, pb, gnorm)


def _xa_kernel(q_ref, mk_ref, mv_ref, o_ref):
    s = _dot_nt(q_ref[...], mk_ref[...]) * (XA_DH ** -0.5)
    p = jnp.exp(s - jnp.max(s, axis=-1, keepdims=True))
    den = jnp.sum(p, axis=-1, keepdims=True)
    o_ref[...] = (_dot(p.astype(BF16), mv_ref[...]) / den).astype(o_ref.dtype)


def _xa(pb, mkv, bsz, seq, ts=1024):
    t = bsz * seq
    nt = seq // ts
    return pl.pallas_call(
        _xa_kernel,
        grid=(bsz, XA_HEADS, nt),
        in_specs=[
            pl.BlockSpec((ts, XA_DH), lambda b, h, i: (b * nt + i, _PB_XQ // XA_DH + h)),
            pl.BlockSpec((MEM_LEN, XA_DH), lambda b, h, i: (b, h)),
            pl.BlockSpec((MEM_LEN, XA_DH), lambda b, h, i: (b, XA_HEADS + h)),
        ],
        out_specs=pl.BlockSpec((ts, XA_DH), lambda b, h, i: (b * nt + i, h)),
        out_shape=jax.ShapeDtypeStruct((t, XA_W), BF16),
        compiler_params=_params(("parallel", "parallel", "parallel")),
        name="xattn",
    )(pb, mkv, mkv)


def _merge_kernel(h_ref, oa_ref, ob_ref, oc_ref, ga_ref, gb_ref, gc_ref, wa_ref, wb_ref, wc_ref, wo_ref,
                  gpost_ref, o_ref):
    mixed = jax.nn.sigmoid(ga_ref[...].astype(F32)) * _dot(oa_ref[...], wa_ref[...])
    mixed = mixed + jax.nn.sigmoid(gb_ref[...].astype(F32)) * _dot(ob_ref[...], wb_ref[...])
    mixed = mixed + jax.nn.sigmoid(gc_ref[...].astype(F32)) * _dot(oc_ref[...], wc_ref[...])
    m = _dot(mixed.astype(BF16), wo_ref[...])
    o_ref[...] = h_ref[...] + _rms(m, gpost_ref[...])


def _merge(h, oa, ob, oc, pb, wa, wb, wc, wo, gpost, tm=256):
    t, d = h.shape
    row = lambda i: (i, 0)
    const = lambda i: (0, 0)
    once = pl.Buffered(1)
    return pl.pallas_call(
        _merge_kernel,
        grid=(t // tm,),
        in_specs=[
            pl.BlockSpec((tm, d), row),
            pl.BlockSpec((tm, GLA_V), row), pl.BlockSpec((tm, GDN_V), row), pl.BlockSpec((tm, XA_W), row),
            pl.BlockSpec((tm, d), lambda i: (i, 0)), pl.BlockSpec((tm, d), lambda i: (i, 1)),
            pl.BlockSpec((tm, d), lambda i: (i, 2)),
            pl.BlockSpec((GLA_V, d), const, pipeline_mode=once), pl.BlockSpec((GDN_V, d), const, pipeline_mode=once),
            pl.BlockSpec((XA_W, d), const, pipeline_mode=once), pl.BlockSpec((d, d), const, pipeline_mode=once),
            pl.BlockSpec((1, d), const),
        ],
        out_specs=pl.BlockSpec((tm, d), row),
        out_shape=jax.ShapeDtypeStruct((t, d), F32),
        compiler_params=_params(("parallel",)),
        name="merge",
    )(h, oa, ob, oc, pb, pb, pb, wa, wb, wc, wo, gpost)


_PA_GDN = 2 * GLA_QK + GLA_V
_PA_COLS = _PA_GDN + 2 * GDN_QK + GDN_V
_PB_R = N_BRANCH * D_MODEL
_PB_Z = _PB_R + GLA_V
_PB_XQ = _PB_Z + GDN_V
_PB_COLS = _PB_XQ + XA_W


def _split_w_in(w_in):
    o = _IN_OFF
    cols = lambda i: w_in[:, o[i]:o[i + 1]].astype(BF16)
    wa = jnp.concatenate([cols(0), cols(1), cols(2), cols(5)], axis=1)
    wb = jnp.concatenate([cols(10), cols(4), cols(8), cols(9)], axis=1)
    small = jnp.concatenate([cols(3), cols(6), cols(7)], axis=1)
    ws = jnp.pad(small, ((0, 0), (0, LANES - small.shape[1])))
    return wa, wb, ws


def kernel(x, mem, n_ffn1_pre, w_ffn1_gu, w_ffn1_down, n_ffn1_post, n_mix_pre, w_in, gla_w_lr2, gla_b_lr,
           gla_norm, gdn_conv, gdn_a_log, gdn_dt_bias, gdn_norm, mem_norm, w_mem_kv, w_up_gla, w_up_gdn,
           w_up_xa, w_out, n_mix_post, n_ffn2_pre, w_ffn2_gu, w_ffn2_down, n_ffn2_post):
    bsz, seq, d = x.shape
    t = bsz * seq
    h = x.reshape(t, d)
    for l in range(n_ffn1_pre.shape[0]):
        row = lambda a: a[l][None, :]
        h, u = _ffn(h, row(n_ffn1_pre), w_ffn1_gu[l], w_ffn1_down[l],
                    row(n_ffn1_post), row(n_mix_pre), emit_next=True)

        wa, wb, ws = _split_w_in(w_in[l])
        pa = _matmul(u, wa, F32, 2048, 512, "in_proj_a")
        pb = _matmul(u, wb, BF16, 2048, 1024, "in_proj_b")
        ps = _matmul(u, ws, F32, 2048, LANES, "in_proj_s")

        wlr = jnp.pad(gla_w_lr2[l], ((0, LANES - GLA_RANK), (0, 0))).astype(BF16)
        o_gla = _gla(pa, pb, ps, wlr, row(gla_b_lr), row(gla_norm), bsz, seq)

        par = jnp.zeros((2, LANES), F32)
        par = par.at[0, SM_A:SM_A + GDN_HEADS].set(gdn_a_log[l]).at[1, SM_A:SM_A + GDN_HEADS].set(gdn_dt_bias[l])
        o_gdn = _gdn(pa, pb, ps, gdn_conv[l], par, row(gdn_norm), bsz, seq)

        mkv = _norm_matmul(mem.reshape(bsz * MEM_LEN, d), row(mem_norm), w_mem_kv[l].astype(BF16), BF16,
                           bsz * MEM_LEN, 512, "mem_kv")
        o_xa = _xa(pb, mkv, bsz, seq)

        h = _merge(h, o_gla, o_gdn, o_xa, pb, w_up_gla[l].astype(BF16), w_up_gdn[l].astype(BF16),
                   w_up_xa[l].astype(BF16), w_out[l].astype(BF16), row(n_mix_post))
        h = _ffn(h, row(n_ffn2_pre), w_ffn2_gu[l], w_ffn2_down[l],
                 row(n_ffn2_post), row(n_ffn2_post), emit_next=False)
    return h.reshape(bsz, seq, d)
```

```python
import functools

import numpy as np
import jax
import jax.numpy as jnp
from jax import lax
from jax.experimental import pallas as pl
from jax.experimental.pallas import tpu as pltpu

F32 = jnp.float32
BF16 = jnp.bfloat16

D_MODEL = 2048
MEM_LEN = 256
EPS = 1e-6
GLA_HEADS = 4
GLA_DK = 128
GLA_DV = 256
GLA_QK = GLA_HEADS * GLA_DK
GLA_V = GLA_HEADS * GLA_DV
GLA_RANK = 16
GLA_TAU = 16.0
GDN_HEADS = 8
GDN_DK = 128
GDN_DV = 128
GDN_QK = GDN_HEADS * GDN_DK
GDN_V = GDN_HEADS * GDN_DV
CONV_W = 4
XA_HEADS = 4
XA_DH = 256
XA_W = XA_HEADS * XA_DH
N_BRANCH = 3
D_FF = 5632

LANES = 128
GLA_CHUNK = 64
GLA_GROUP = 4
GDN_CHUNK = 128
GDN_HPS = 2
GDN_GROUP = 4
VMEM_LIMIT = 56 * 1024 * 1024

_IN_SIZES = (GLA_QK, GLA_QK, GLA_V, GLA_RANK, GLA_V, 2 * GDN_QK + GDN_V, GDN_HEADS, GDN_HEADS, GDN_V,
             XA_W, N_BRANCH * D_MODEL)
_IN_OFF = tuple(int(v) for v in np.cumsum((0,) + _IN_SIZES))
SM_LR, SM_B, SM_A = 0, GLA_RANK, GLA_RANK + GDN_HEADS


def _dot(a, b):
    return jnp.dot(a, b, preferred_element_type=F32)


def _dot_nt(a, b):
    return lax.dot_general(a, b, (((1,), (1,)), ((), ())), preferred_element_type=F32)


def _dot_tn(a, b):
    return lax.dot_general(a, b, (((0,), (0,)), ((), ())), preferred_element_type=F32)


def _mm(a, b):
    return _dot(a.astype(BF16), b.astype(BF16))


def _split2(x):
    hi = x.astype(BF16)
    mid = (x - hi.astype(F32)).astype(BF16)
    return hi, mid


def _dot01(m01, x):
    hi, mid = _split2(x)
    return _dot(m01, hi) + _dot(m01, mid)


def _rms(x, gain):
    return x * lax.rsqrt(jnp.mean(x * x, axis=-1, keepdims=True) + EPS) * gain


def _silu(x):
    return x * jax.nn.sigmoid(x)


def _softplus(x):
    return jnp.maximum(x, 0.0) + jnp.log1p(jnp.exp(-jnp.abs(x)))


def _params(sem):
    return pltpu.CompilerParams(dimension_semantics=sem, vmem_limit_bytes=VMEM_LIMIT)


def _ffn_kernel(x_ref, gpre_ref, wg_ref, wu_ref, wd_ref, gpost_ref, gnext_ref, *out_and_scratch, emit_next):
    if emit_next:
        h_ref, un_ref, xn_scr, wg16, wu16, wd16 = out_and_scratch
    else:
        h_ref, xn_scr, wg16, wu16, wd16 = out_and_scratch
    j = pl.program_id(1)

    @pl.when(j == 0)
    def _():
        xn_scr[...] = _rms(x_ref[...], gpre_ref[...]).astype(BF16)
        h_ref[...] = jnp.zeros_like(h_ref)

    wg16[...] = wg_ref[...].astype(BF16)
    wu16[...] = wu_ref[...].astype(BF16)
    wd16[...] = wd_ref[...].astype(BF16)
    xn = xn_scr[...]
    g = _dot(xn, wg16[...])
    u = _dot(xn, wu16[...])
    hm = (_silu(g) * u).astype(BF16)
    h_ref[...] += _dot(hm, wd16[...])

    @pl.when(j == pl.num_programs(1) - 1)
    def _():
        h = x_ref[...] + 0.5 * _rms(h_ref[...], gpost_ref[...])
        h_ref[...] = h
        if emit_next:
            un_ref[...] = _rms(h, gnext_ref[...]).astype(BF16)


def _ffn(x, gpre, w_gu, w_down, gpost, gnext, emit_next, tm=1024, tf=256):
    t, d = x.shape
    nf = D_FF // tf
    row = lambda i, j: (i, 0)
    const = lambda i, j: (0, 0)
    once = pl.Buffered(1)
    out_shape = [jax.ShapeDtypeStruct((t, d), F32)]
    out_specs = [pl.BlockSpec((tm, d), row, pipeline_mode=once)]
    if emit_next:
        out_shape.append(jax.ShapeDtypeStruct((t, d), BF16))
        out_specs.append(pl.BlockSpec((tm, d), row, pipeline_mode=once))
    res = pl.pallas_call(
        functools.partial(_ffn_kernel, emit_next=emit_next),
        grid=(t // tm, nf),
        in_specs=[
            pl.BlockSpec((tm, d), row, pipeline_mode=once),
            pl.BlockSpec((1, d), const),
            pl.BlockSpec((d, tf), lambda i, j: (0, j)),
            pl.BlockSpec((d, tf), lambda i, j: (0, nf + j)),
            pl.BlockSpec((tf, d), lambda i, j: (j, 0)),
            pl.BlockSpec((1, d), const),
            pl.BlockSpec((1, d), const),
        ],
        out_specs=out_specs,
        out_shape=out_shape,
        scratch_shapes=[pltpu.VMEM((tm, d), BF16), pltpu.VMEM((d, tf), BF16), pltpu.VMEM((d, tf), BF16),
                        pltpu.VMEM((tf, d), BF16)],
        compiler_params=_params(("parallel", "arbitrary")),
        name="ffn",
    )(x, gpre, w_gu, w_gu, w_down, gpost, gnext)
    return res if emit_next else res[0]


def _matmul_kernel(a_ref, w_ref, o_ref):
    o_ref[...] = _dot(a_ref[...], w_ref[...]).astype(o_ref.dtype)


def _proj_conv_kernel(a_ref, w_ref, cw_ref, o_ref):
    tm, tn = o_ref.shape
    rb = 256
    kind = pl.program_id(1) // (GDN_QK // tn)
    scale = jnp.where(kind == 0, GDN_DK ** -0.5, 1.0)
    w = cw_ref[...]
    r8 = lax.broadcasted_iota(jnp.int32, (8, tn), 0)
    prev = jnp.zeros((8, tn), F32)
    for r in range(0, tm, rb):
        y = _dot(a_ref[r:r + rb, :], w_ref[...])
        acc = y * w[CONV_W - 1:CONV_W, :]
        for sft in range(1, CONV_W):
            xs = pltpu.roll(y, sft, 0)
            top = jnp.where(r8 < sft, pltpu.roll(prev, sft, 0), xs[0:8])
            acc = acc + jnp.concatenate([top, xs[8:]], axis=0) * w[CONV_W - 1 - sft:CONV_W - sft, :]
        prev = y[rb - 8:rb]
        c = _silu(acc)
        for hh in range(tn // GDN_DK):
            sl = slice(hh * GDN_DK, (hh + 1) * GDN_DK)
            blk = c[:, sl]
            f = lax.rsqrt(jnp.sum(blk * blk, axis=-1, keepdims=True) + EPS) * scale
            o_ref[r:r + rb, sl] = blk * jnp.where(kind == 2, 1.0, f)


def _proj_s_kernel(a_ref, w_ref, par_ref, o_ref):
    y = _dot(a_ref[...], w_ref[...])
    par = par_ref[...]
    lane = lax.broadcasted_iota(jnp.int32, y.shape, 1)
    g = -jnp.exp(par[0:1, :]) * _softplus(y + par[1:2, :])
    is_g = (lane >= SM_A) & (lane < SM_A + GDN_HEADS)
    is_b = (lane >= SM_B) & (lane < SM_B + GDN_HEADS)
    o_ref[...] = jnp.where(is_g, g, jnp.where(is_b, jax.nn.sigmoid(y), y))


def _matmul(a, w, col0, n, out_dtype, tm, tn, name, body=_matmul_kernel, extra=(), extra_specs=()):
    m, k = a.shape
    c0 = col0 // tn
    return pl.pallas_call(
        body,
        grid=(m // tm, n // tn),
        in_specs=[pl.BlockSpec((tm, k), lambda i, j: (i, 0)), pl.BlockSpec((k, tn), lambda i, j: (0, c0 + j)),
                  *extra_specs],
        out_specs=pl.BlockSpec((tm, tn), lambda i, j: (i, j)),
        out_shape=jax.ShapeDtypeStruct((m, n), out_dtype),
        compiler_params=_params(("parallel", "parallel")),
        name=name,
    )(a, w, *extra)


def _norm_matmul_kernel(x_ref, g_ref, w_ref, o_ref, xn_scr):
    @pl.when(pl.program_id(1) == 0)
    def _():
        xn_scr[...] = _rms(x_ref[...], g_ref[...]).astype(BF16)

    o_ref[...] = _dot(xn_scr[...], w_ref[...]).astype(o_ref.dtype)


def _norm_matmul(x, gain, w, out_dtype, tm, tn, name):
    m, k = x.shape
    n = w.shape[1]
    return pl.pallas_call(
        _norm_matmul_kernel,
        grid=(m // tm, n // tn),
        in_specs=[pl.BlockSpec((tm, k), lambda i, j: (i, 0)), pl.BlockSpec((1, k), lambda i, j: (0, 0)),
                  pl.BlockSpec((k, tn), lambda i, j: (0, j))],
        out_specs=pl.BlockSpec((tm, tn), lambda i, j: (i, j)),
        out_shape=jax.ShapeDtypeStruct((m, n), out_dtype),
        scratch_shapes=[pltpu.VMEM((tm, k), BF16)],
        compiler_params=_params(("parallel", "arbitrary")),
        name=name,
    )(x, gain, w)


def _gla_levels(chunk):
    return [chunk >> (i + 1) for i in range(int(np.log2(chunk)))]


def _gla_exponent_matrix(chunk):
    c = chunk
    i = np.arange(c)[:, None]
    t = np.arange(c)[None, :]
    blocks = [(t <= i), (t > i)]
    for h in _gla_levels(c):
        r = (i // (2 * h)) * (2 * h) + h
        upper = i >= r
        blocks.append(np.where(upper, (t > r) & (t <= i), (t > i) & (t <= r)))
    m = np.concatenate(blocks, axis=0).astype(np.float32)
    z = np.zeros_like(m)
    return np.concatenate([np.concatenate([m, m, z, z], axis=1), np.concatenate([z, z, m, m], axis=1)], axis=0)


def _gla_kernel(q_ref, k_ref, v_ref, sm_ref, r_ref, wlr_ref, blr_ref, gn_ref, mst_ref, o_ref, s_scr):
    c = GLA_CHUNK
    seq = q_ref.shape[0]
    s_scr[...] = jnp.zeros_like(s_scr)
    ri = lax.broadcasted_iota(jnp.int32, (c, c), 0)
    ci = lax.broadcasted_iota(jnp.int32, (c, c), 1)
    rowi = lax.broadcasted_iota(jnp.int32, (c, GLA_DK), 0)
    levels = _gla_levels(c)
    assert GLA_GROUP == 4

    grp = GLA_GROUP
    gs = range(grp)

    def body(n, carry):
        row0 = pl.multiple_of(n * (grp * c), grp * c)
        rows = [pl.ds(row0 + g * c, c) for g in gs]
        allrows = pl.ds(row0, grp * c)
        x = _dot(sm_ref[allrows, :].astype(BF16), wlr_ref[...]) + blr_ref[...]
        lg = (jnp.minimum(x, 0.0) - jnp.log1p(jnp.exp(-jnp.abs(x)))) * (1.0 / GLA_TAU)
        hi, mid = _split2(lg)
        his = [hi[g * c:(g + 1) * c] for g in gs]
        mids = [mid[g * c:(g + 1) * c] for g in gs]
        hm = [jnp.concatenate([his[g], mids[g]], axis=0) for g in gs]
        rhs = jnp.concatenate([jnp.concatenate([hm[0], hm[1]], axis=1),
                               jnp.concatenate([hm[2], hm[3]], axis=1)], axis=0)
        ex = jnp.exp(_dot(mst_ref[...], rhs))
        nr = ex.shape[0] // 2
        e = [ex[0:nr, 0:GLA_DK], ex[0:nr, GLA_DK:], ex[nr:, 0:GLA_DK], ex[nr:, GLA_DK:]]
        q = [q_ref[rows[g], :] * (GLA_DK ** -0.5) for g in gs]
        k = [k_ref[rows[g], :] for g in gs]
        v = [v_ref[rows[g], :].astype(BF16) for g in gs]
        attn = [jnp.where(ri == ci, _dot_nt(q[g].astype(BF16), k[g].astype(BF16)), 0.0) for g in gs]
        for lvl, h in enumerate(levels):
            upper = (rowi & h) != 0
            sh = int(np.log2(2 * h))
            same = (ri >> sh) == (ci >> sh)
            for g in gs:
                f = e[g][(2 + lvl) * c:(3 + lvl) * c]
                qt = jnp.where(upper, q[g] * f, 0.0).astype(BF16)
                kt = jnp.where(upper, 0.0, k[g] * f).astype(BF16)
                attn[g] = attn[g] + jnp.where(same, _dot_nt(qt, kt), 0.0)
        vk = [_dot_tn(v[g], (k[g] * e[g][c:2 * c]).astype(BF16)) for g in gs]
        av = [_dot(attn[g].astype(BF16), v[g]) for g in gs]
        st = s_scr[...]
        for g in gs:
            o = _dot_nt((q[g] * e[g][0:c]).astype(BF16), st.astype(BF16)) + av[g]
            st = st * e[g][c - 1:c, :] + vk[g]
            on = _rms(o, gn_ref[...]) * _silu(r_ref[rows[g], :].astype(F32))
            o_ref[rows[g], :] = on.astype(o_ref.dtype)
        s_scr[...] = st
        return carry

    lax.fori_loop(0, seq // (grp * c), body, 0)


def _gla(pa, pb, ps, wlr, blr, gnorm, bsz, seq):
    t = bsz * seq
    mst = jnp.asarray(_gla_exponent_matrix(GLA_CHUNK), BF16)
    nq = GLA_QK // GLA_DK
    return pl.pallas_call(
        _gla_kernel,
        grid=(bsz, GLA_HEADS),
        in_specs=[
            pl.BlockSpec((seq, GLA_DK), lambda b, h: (b, h)),
            pl.BlockSpec((seq, GLA_DK), lambda b, h: (b, nq + h)),
            pl.BlockSpec((seq, GLA_DV), lambda b, h: (b, 2 * GLA_QK // GLA_DV + h)),
            pl.BlockSpec((seq, LANES), lambda b, h: (b, 0)),
            pl.BlockSpec((seq, GLA_DV), lambda b, h: (b, _PB_R // GLA_DV + h)),
            pl.BlockSpec((LANES, GLA_DK), lambda b, h: (0, h)),
            pl.BlockSpec((1, GLA_DK), lambda b, h: (0, h)),
            pl.BlockSpec((1, GLA_DV), lambda b, h: (0, 0)),
            pl.BlockSpec(mst.shape, lambda b, h: (0, 0)),
        ],
        out_specs=pl.BlockSpec((seq, GLA_DV), lambda b, h: (b, h)),
        out_shape=jax.ShapeDtypeStruct((t, GLA_V), BF16),
        scratch_shapes=[pltpu.VMEM((GLA_DV, GLA_DK), F32)],
        compiler_params=_params(("parallel", "parallel")),
        name="gla",
    )(pa, pa, pa, ps, pb, wlr, blr, gnorm, mst)


def _gdn_kernel(q_ref, k_ref, v_ref, sm_ref, z_ref, gn_ref, o_ref, bms, ns, qms, os_, egl, s_scr):
    c = GDN_CHUNK
    seq = q_ref.shape[0]
    head0 = pl.program_id(1) * GDN_HPS

    ri = lax.broadcasted_iota(jnp.int32, (c, c), 0)
    ci = lax.broadcasted_iota(jnp.int32, (c, c), 1)
    lane = lax.broadcasted_iota(jnp.int32, (c, LANES), 1)
    tril = jnp.where(ri >= ci, 1.0, 0.0).astype(BF16)
    causal = ri >= ci
    n_sq = int(np.log2(c))
    lower_left = [(((ri ^ ci) >> (l + 1)) == 0) & ((ri & (1 << l)) != 0) & ((ci & (1 << l)) == 0) for l in range(n_sq)]

    grp = GDN_GROUP
    chains = [(g, j) for g in range(grp) for j in range(GDN_HPS)]

    def phase1(n, carry):
        rows, q, k, kb, gb, rhs0 = {}, {}, {}, {}, {}, {}
        for g in range(grp):
            rows[g] = pl.ds(pl.multiple_of((n * grp + g) * c, c), c)
            sm = sm_ref[rows[g], :]
            for j in range(GDN_HPS):
                sl = slice(j * GDN_DK, (j + 1) * GDN_DK)
                q[g, j] = q_ref[rows[g], sl]
                k[g, j] = k_ref[rows[g], sl]
                gcol = jnp.sum(jnp.where(lane == SM_A + head0 + j, sm, 0.0), axis=-1, keepdims=True)
                beta = jnp.sum(jnp.where(lane == SM_B + head0 + j, sm, 0.0), axis=-1, keepdims=True)
                gb[g, j] = jnp.broadcast_to(gcol, (c, LANES))
                kb[g, j] = k[g, j] * beta
                rhs0[g, j] = v_ref[rows[g], sl] * beta
        cum_r, cum_c, kt = {}, {}, {}
        for ch in chains:
            hi, mid = _split2(gb[ch])
            r = _dot(tril, jnp.concatenate([hi, mid], axis=1))
            cum_r[ch] = r[:, :LANES] + r[:, LANES:]
        for ch in chains:
            cum_c[ch] = cum_r[ch].T
            kt[ch] = k[ch].T
        kk = {}
        for ch in chains:
            k16 = k[ch].astype(BF16)
            kk[ch] = _dot_nt(jnp.concatenate([kb[ch].astype(BF16), q[ch].astype(BF16)], axis=0), k16)
        low, at16, x, p = {}, {}, {}, {}
        for ch in chains:
            dec = jnp.where(causal, jnp.exp(jnp.where(causal, cum_r[ch] - cum_c[ch], 0.0)), 0.0)
            low[ch] = jnp.where(ri > ci, kk[ch][:c] * dec, 0.0)
            at16[ch] = jnp.where(causal, kk[ch][c:] * dec, 0.0).astype(BF16)
        for ch in chains:
            x[ch] = -jnp.where(lower_left[0], low[ch], 0.0)
        for lvl in range(1, n_sq):
            e = {}
            for ch in chains:
                cm = jnp.where(lower_left[lvl], low[ch], 0.0)
                e[ch] = cm + _mm(x[ch], cm)
            for ch in chains:
                x[ch] = x[ch] - e[ch] - _mm(e[ch], x[ch])
        wu, egc = {}, {}
        for ch in chains:
            egc[ch] = jnp.exp(cum_r[ch])
            rhs = jnp.concatenate([kb[ch] * egc[ch], rhs0[ch]], axis=1)
            wu[ch] = (rhs + _mm(x[ch], rhs)).astype(BF16)
        for ch in chains:
            g, j = ch
            sl = slice(j * GDN_DK, (j + 1) * GDN_DK)
            glast = cum_r[ch][c - 1:c, :]
            kdt16 = (kt[ch] * jnp.exp(glast - cum_c[ch][0:1, :])).astype(BF16)
            r = _dot(jnp.concatenate([kdt16, at16[ch]], axis=0), wu[ch])
            bms[rows[g], sl] = (-r[:c, :GDN_DV]).astype(BF16)
            ns[rows[g], sl] = r[:c, GDN_DV:]
            qms[rows[g], sl] = (q[ch] * egc[ch] - r[c:, :GDN_DV]).astype(BF16)
            os_[rows[g], sl] = r[c:, GDN_DV:]
            egl[pl.ds(pl.multiple_of((n * grp + g) * 8, 8), 8), sl] = jnp.broadcast_to(jnp.exp(glast), (8, LANES))
        return carry

    lax.fori_loop(0, seq // (grp * c), phase1, 0)

    s_scr[...] = jnp.zeros_like(s_scr)

    def phase2(n, carry):
        rows = pl.ds(pl.multiple_of(n * c, c), c)
        for j in range(GDN_HPS):
            sl = slice(j * GDN_DK, (j + 1) * GDN_DK)
            s = s_scr[j]
            s16 = s.astype(BF16)
            eg = egl[pl.ds(pl.multiple_of(n * 8, 8), 8), sl][0:1, :]
            s_scr[j] = s * eg + _dot(bms[rows, sl], s16) + ns[rows, sl]
            o = _dot(qms[rows, sl], s16) + os_[rows, sl]
            on = _rms(o, gn_ref[...]) * _silu(z_ref[rows, sl].astype(F32))
            o_ref[rows, sl] = on.astype(o_ref.dtype)
        return carry

    lax.fori_loop(0, seq // c, phase2, 0)


def _gdn(pg, pb, ps, gnorm, bsz, seq):
    t = bsz * seq
    nb = seq // GDN_CHUNK
    wd = GDN_HPS * GDN_DK
    ng = GDN_HEADS // GDN_HPS
    return pl.pallas_call(
        _gdn_kernel,
        grid=(bsz, ng),
        in_specs=[
            pl.BlockSpec((seq, wd), lambda b, h: (b, h)),
            pl.BlockSpec((seq, wd), lambda b, h: (b, ng + h)),
            pl.BlockSpec((seq, wd), lambda b, h: (b, 2 * ng + h)),
            pl.BlockSpec((seq, LANES), lambda b, h: (b, 0)),
            pl.BlockSpec((seq, wd), lambda b, h: (b, _PB_Z // wd + h)),
            pl.BlockSpec((1, GDN_DV), lambda b, h: (0, 0)),
        ],
        out_specs=pl.BlockSpec((seq, wd), lambda b, h: (b, h)),
        out_shape=jax.ShapeDtypeStruct((t, GDN_V), BF16),
        scratch_shapes=[
            pltpu.VMEM((seq, wd), BF16), pltpu.VMEM((seq, wd), F32), pltpu.VMEM((seq, wd), BF16),
            pltpu.VMEM((seq, wd), F32), pltpu.VMEM((nb * 8, wd), F32),
            pltpu.VMEM((GDN_HPS, GDN_DK, GDN_DV), F32),
        ],
        compiler_params=_params(("parallel", "parallel")),
        name="gdn",
    )(pg, pg, pg, ps, pb, gnorm)


def _xa_kernel(q_ref, mk_ref, mv_ref, o_ref):
    s = _dot_nt(q_ref[...], mk_ref[...]) * (XA_DH ** -0.5)
    p = jnp.exp(s - jnp.max(s, axis=-1, keepdims=True))
    den = jnp.sum(p, axis=-1, keepdims=True)
    o_ref[...] = (_dot(p.astype(BF16), mv_ref[...]) / den).astype(o_ref.dtype)


def _xa(pb, mkv, bsz, seq, ts=1024):
    t = bsz * seq
    nt = seq // ts
    return pl.pallas_call(
        _xa_kernel,
        grid=(bsz, XA_HEADS, nt),
        in_specs=[
            pl.BlockSpec((ts, XA_DH), lambda b, h, i: (b * nt + i, _PB_XQ // XA_DH + h)),
            pl.BlockSpec((MEM_LEN, XA_DH), lambda b, h, i: (b, h)),
            pl.BlockSpec((MEM_LEN, XA_DH), lambda b, h, i: (b, XA_HEADS + h)),
        ],
        out_specs=pl.BlockSpec((ts, XA_DH), lambda b, h, i: (b * nt + i, h)),
        out_shape=jax.ShapeDtypeStruct((t, XA_W), BF16),
        compiler_params=_params(("parallel", "parallel", "parallel")),
        name="xattn",
    )(pb, mkv, mkv)


def _merge_kernel(h_ref, oa_ref, ob_ref, oc_ref, ga_ref, gb_ref, gc_ref, wa_ref, wb_ref, wc_ref, wo_ref,
                  gpost_ref, o_ref):
    mixed = jax.nn.sigmoid(ga_ref[...].astype(F32)) * _dot(oa_ref[...], wa_ref[...])
    mixed = mixed + jax.nn.sigmoid(gb_ref[...].astype(F32)) * _dot(ob_ref[...], wb_ref[...])
    mixed = mixed + jax.nn.sigmoid(gc_ref[...].astype(F32)) * _dot(oc_ref[...], wc_ref[...])
    m = _dot(mixed.astype(BF16), wo_ref[...])
    o_ref[...] = h_ref[...] + _rms(m, gpost_ref[...])


def _merge(h, oa, ob, oc, pb, wa, wb, wc, wo, gpost, tm=256):
    t, d = h.shape
    row = lambda i: (i, 0)
    const = lambda i: (0, 0)
    once = pl.Buffered(1)
    return pl.pallas_call(
        _merge_kernel,
        grid=(t // tm,),
        in_specs=[
            pl.BlockSpec((tm, d), row),
            pl.BlockSpec((tm, GLA_V), row), pl.BlockSpec((tm, GDN_V), row), pl.BlockSpec((tm, XA_W), row),
            pl.BlockSpec((tm, d), lambda i: (i, 0)), pl.BlockSpec((tm, d), lambda i: (i, 1)),
            pl.BlockSpec((tm, d), lambda i: (i, 2)),
            pl.BlockSpec((GLA_V, d), const, pipeline_mode=once), pl.BlockSpec((GDN_V, d), const, pipeline_mode=once),
            pl.BlockSpec((XA_W, d), const, pipeline_mode=once), pl.BlockSpec((d, d), const, pipeline_mode=once),
            pl.BlockSpec((1, d), const),
        ],
        out_specs=pl.BlockSpec((tm, d), row),
        out_shape=jax.ShapeDtypeStruct((t, d), F32),
        compiler_params=_params(("parallel",)),
        name="merge",
    )(h, oa, ob, oc, pb, pb, pb, wa, wb, wc, wo, gpost)


_PA_GDN = 2 * GLA_QK + GLA_V
_PA_COLS = _PA_GDN + 2 * GDN_QK + GDN_V
_PB_R = N_BRANCH * D_MODEL
_PB_Z = _PB_R + GLA_V
_PB_XQ = _PB_Z + GDN_V
_PB_COLS = _PB_XQ + XA_W


def _split_w_in(w_in):
    o = _IN_OFF
    cols = lambda i: w_in[:, o[i]:o[i + 1]].astype(BF16)
    n_small = GLA_RANK + 2 * GDN_HEADS
    pad = jnp.zeros((w_in.shape[0], LANES - n_small), BF16)
    return jnp.concatenate([cols(0), cols(1), cols(2), cols(5), cols(10), cols(4), cols(8), cols(9),
                            cols(3), cols(6), cols(7), pad], axis=1)


def kernel(x, mem, n_ffn1_pre, w_ffn1_gu, w_ffn1_down, n_ffn1_post, n_mix_pre, w_in, gla_w_lr2, gla_b_lr,
           gla_norm, gdn_conv, gdn_a_log, gdn_dt_bias, gdn_norm, mem_norm, w_mem_kv, w_up_gla, w_up_gdn,
           w_up_xa, w_out, n_mix_post, n_ffn2_pre, w_ffn2_gu, w_ffn2_down, n_ffn2_post):
    bsz, seq, d = x.shape
    t = bsz * seq
    h = x.reshape(t, d)
    for l in range(n_ffn1_pre.shape[0]):
        row = lambda a: a[l][None, :]
        h, u = _ffn(h, row(n_ffn1_pre), w_ffn1_gu[l], w_ffn1_down[l],
                    row(n_ffn1_post), row(n_mix_pre), emit_next=True)

        w_all = _split_w_in(w_in[l])
        par = jnp.zeros((2, LANES), F32)
        par = par.at[0, SM_A:SM_A + GDN_HEADS].set(gdn_a_log[l]).at[1, SM_A:SM_A + GDN_HEADS].set(gdn_dt_bias[l])
        tn_g = 512
        pa = _matmul(u, w_all, 0, _PA_GDN, F32, 2048, 512, "in_proj_a")
        pg = _matmul(u, w_all, _PA_GDN, _PA_COLS - _PA_GDN, F32, seq, tn_g, "in_proj_g", body=_proj_conv_kernel,
                     extra=(gdn_conv[l],), extra_specs=(pl.BlockSpec((CONV_W, tn_g), lambda i, j: (0, j)),))
        pb = _matmul(u, w_all, _PA_COLS, _PB_COLS, BF16, 2048, 1024, "in_proj_b")
        ps = _matmul(u, w_all, _PA_COLS + _PB_COLS, LANES, F32, 2048, LANES, "in_proj_s", body=_proj_s_kernel,
                     extra=(par,), extra_specs=(pl.BlockSpec((2, LANES), lambda i, j: (0, 0)),))

        wlr = jnp.pad(gla_w_lr2[l], ((0, LANES - GLA_RANK), (0, 0))).astype(BF16)
        o_gla = _gla(pa, pb, ps, wlr, row(gla_b_lr), row(gla_norm), bsz, seq)
        o_gdn = _gdn(pg, pb, ps, row(gdn_norm), bsz, seq)

        mkv = _norm_matmul(mem.reshape(bsz * MEM_LEN, d), row(mem_norm), w_mem_kv[l].astype(BF16), BF16,
                           bsz * MEM_LEN, 512, "mem_kv")
        o_xa = _xa(pb, mkv, bsz, seq)

        h = _merge(h, o_gla, o_gdn, o_xa, pb, w_up_gla[l].astype(BF16), w_up_gdn[l].astype(BF16),
                   w_up_xa[l].astype(BF16), w_out[l].astype(BF16), row(n_mix_post))
        h = _ffn(h, row(n_ffn2_pre), w_ffn2_gu[l], w_ffn2_down[l],
                 row(n_ffn2_post), row(n_ffn2_post), emit_next=False)
    return h.reshape(bsz, seq, d)
```

```python
import functools

import numpy as np
import jax
import jax.numpy as jnp
from jax import lax
from jax.experimental import pallas as pl
from jax.experimental.pallas import tpu as pltpu

F32 = jnp.float32
BF16 = jnp.bfloat16

D_MODEL = 2048
MEM_LEN = 256
EPS = 1e-6
GLA_HEADS = 4
GLA_DK = 128
GLA_DV = 256
GLA_QK = GLA_HEADS * GLA_DK
GLA_V = GLA_HEADS * GLA_DV
GLA_RANK = 16
GLA_TAU = 16.0
GDN_HEADS = 8
GDN_DK = 128
GDN_DV = 128
GDN_QK = GDN_HEADS * GDN_DK
GDN_V = GDN_HEADS * GDN_DV
CONV_W = 4
XA_HEADS = 4
XA_DH = 256
XA_W = XA_HEADS * XA_DH
N_BRANCH = 3
D_FF = 5632

LANES = 128
GLA_CHUNK = 64
GLA_GROUP = 4
GDN_CHUNK = 128
GDN_HPS = 2
GDN_GROUP = 4
VMEM_LIMIT = 56 * 1024 * 1024

_IN_SIZES = (GLA_QK, GLA_QK, GLA_V, GLA_RANK, GLA_V, 2 * GDN_QK + GDN_V, GDN_HEADS, GDN_HEADS, GDN_V,
             XA_W, N_BRANCH * D_MODEL)
_IN_OFF = tuple(int(v) for v in np.cumsum((0,) + _IN_SIZES))
SM_LR, SM_B, SM_A = 0, GLA_RANK, GLA_RANK + GDN_HEADS


def _dot(a, b):
    return jnp.dot(a, b, preferred_element_type=F32)


def _dot_nt(a, b):
    return lax.dot_general(a, b, (((1,), (1,)), ((), ())), preferred_element_type=F32)


def _dot_tn(a, b):
    return lax.dot_general(a, b, (((0,), (0,)), ((), ())), preferred_element_type=F32)


def _mm(a, b):
    return _dot(a.astype(BF16), b.astype(BF16))


def _split2(x):
    hi = x.astype(BF16)
    mid = (x - hi.astype(F32)).astype(BF16)
    return hi, mid


def _dot01(m01, x):
    hi, mid = _split2(x)
    return _dot(m01, hi) + _dot(m01, mid)


def _rms(x, gain):
    return x * lax.rsqrt(jnp.mean(x * x, axis=-1, keepdims=True) + EPS) * gain


def _silu(x):
    return x * jax.nn.sigmoid(x)


def _softplus(x):
    return jnp.maximum(x, 0.0) + jnp.log1p(jnp.exp(-jnp.abs(x)))


def _params(sem):
    return pltpu.CompilerParams(dimension_semantics=sem, vmem_limit_bytes=VMEM_LIMIT)


def _ffn_kernel(x_ref, gpre_ref, wg_ref, wu_ref, wd_ref, gpost_ref, gnext_ref, *out_and_scratch, emit_next):
    if emit_next:
        h_ref, un_ref, xn_scr, wg16, wu16, wd16 = out_and_scratch
    else:
        h_ref, xn_scr, wg16, wu16, wd16 = out_and_scratch
    j = pl.program_id(1)

    @pl.when(j == 0)
    def _():
        xn_scr[...] = _rms(x_ref[...], gpre_ref[...]).astype(BF16)
        h_ref[...] = jnp.zeros_like(h_ref)

    wg16[...] = wg_ref[...].astype(BF16)
    wu16[...] = wu_ref[...].astype(BF16)
    wd16[...] = wd_ref[...].astype(BF16)
    xn = xn_scr[...]
    g = _dot(xn, wg16[...])
    u = _dot(xn, wu16[...])
    hm = (_silu(g) * u).astype(BF16)
    h_ref[...] += _dot(hm, wd16[...])

    @pl.when(j == pl.num_programs(1) - 1)
    def _():
        h = x_ref[...] + 0.5 * _rms(h_ref[...], gpost_ref[...])
        h_ref[...] = h
        if emit_next:
            un_ref[...] = _rms(h, gnext_ref[...]).astype(BF16)


def _ffn(x, gpre, w_gu, w_down, gpost, gnext, emit_next, tm=1024, tf=256):
    t, d = x.shape
    nf = D_FF // tf
    row = lambda i, j: (i, 0)
    const = lambda i, j: (0, 0)
    once = pl.Buffered(1)
    out_shape = [jax.ShapeDtypeStruct((t, d), F32)]
    out_specs = [pl.BlockSpec((tm, d), row, pipeline_mode=once)]
    if emit_next:
        out_shape.append(jax.ShapeDtypeStruct((t, d), BF16))
        out_specs.append(pl.BlockSpec((tm, d), row, pipeline_mode=once))
    res = pl.pallas_call(
        functools.partial(_ffn_kernel, emit_next=emit_next),
        grid=(t // tm, nf),
        in_specs=[
            pl.BlockSpec((tm, d), row, pipeline_mode=once),
            pl.BlockSpec((1, d), const),
            pl.BlockSpec((d, tf), lambda i, j: (0, j)),
            pl.BlockSpec((d, tf), lambda i, j: (0, nf + j)),
            pl.BlockSpec((tf, d), lambda i, j: (j, 0)),
            pl.BlockSpec((1, d), const),
            pl.BlockSpec((1, d), const),
        ],
        out_specs=out_specs,
        out_shape=out_shape,
        scratch_shapes=[pltpu.VMEM((tm, d), BF16), pltpu.VMEM((d, tf), BF16), pltpu.VMEM((d, tf), BF16),
                        pltpu.VMEM((tf, d), BF16)],
        compiler_params=_params(("parallel", "arbitrary")),
        name="ffn",
    )(x, gpre, w_gu, w_gu, w_down, gpost, gnext)
    return res if emit_next else res[0]


def _matmul_kernel(a_ref, w_ref, o_ref):
    o_ref[...] = _dot(a_ref[...], w_ref[...]).astype(o_ref.dtype)


def _proj_conv_kernel(a_ref, w_ref, cw_ref, o_ref):
    tm, tn = o_ref.shape
    rb = 256
    kind = pl.program_id(1) // (GDN_QK // tn)
    scale = jnp.where(kind == 0, GDN_DK ** -0.5, 1.0)
    w = cw_ref[...]
    r8 = lax.broadcasted_iota(jnp.int32, (8, tn), 0)
    prev = jnp.zeros((8, tn), F32)
    for r in range(0, tm, rb):
        y = _dot(a_ref[r:r + rb, :], w_ref[...])
        acc = y * w[CONV_W - 1:CONV_W, :]
        for sft in range(1, CONV_W):
            xs = pltpu.roll(y, sft, 0)
            top = jnp.where(r8 < sft, pltpu.roll(prev, sft, 0), xs[0:8])
            acc = acc + jnp.concatenate([top, xs[8:]], axis=0) * w[CONV_W - 1 - sft:CONV_W - sft, :]
        prev = y[rb - 8:rb]
        c = _silu(acc)
        for hh in range(tn // GDN_DK):
            sl = slice(hh * GDN_DK, (hh + 1) * GDN_DK)
            blk = c[:, sl]
            f = lax.rsqrt(jnp.sum(blk * blk, axis=-1, keepdims=True) + EPS) * scale
            o_ref[r:r + rb, sl] = blk * jnp.where(kind == 2, 1.0, f)


def _proj_s_kernel(a_ref, w_ref, par_ref, o_ref):
    y = _dot(a_ref[...], w_ref[...])
    par = par_ref[...]
    lane = lax.broadcasted_iota(jnp.int32, y.shape, 1)
    g = -jnp.exp(par[0:1, :]) * _softplus(y + par[1:2, :])
    is_g = (lane >= SM_A) & (lane < SM_A + GDN_HEADS)
    is_b = (lane >= SM_B) & (lane < SM_B + GDN_HEADS)
    o_ref[...] = jnp.where(is_g, g, jnp.where(is_b, jax.nn.sigmoid(y), y))


def _matmul(a, w, col0, n, out_dtype, tm, tn, name, body=_matmul_kernel, extra=(), extra_specs=()):
    m, k = a.shape
    c0 = col0 // tn
    return pl.pallas_call(
        body,
        grid=(m // tm, n // tn),
        in_specs=[pl.BlockSpec((tm, k), lambda i, j: (i, 0)), pl.BlockSpec((k, tn), lambda i, j: (0, c0 + j)),
                  *extra_specs],
        out_specs=pl.BlockSpec((tm, tn), lambda i, j: (i, j)),
        out_shape=jax.ShapeDtypeStruct((m, n), out_dtype),
        compiler_params=_params(("parallel", "parallel")),
        name=name,
    )(a, w, *extra)


def _norm_matmul_kernel(x_ref, g_ref, w_ref, o_ref, xn_scr):
    @pl.when(pl.program_id(1) == 0)
    def _():
        xn_scr[...] = _rms(x_ref[...], g_ref[...]).astype(BF16)

    o_ref[...] = _dot(xn_scr[...], w_ref[...].astype(BF16)).astype(o_ref.dtype)


def _norm_matmul(x, gain, w, out_dtype, tm, tn, name):
    m, k = x.shape
    n = w.shape[1]
    return pl.pallas_call(
        _norm_matmul_kernel,
        grid=(m // tm, n // tn),
        in_specs=[pl.BlockSpec((tm, k), lambda i, j: (i, 0)), pl.BlockSpec((1, k), lambda i, j: (0, 0)),
                  pl.BlockSpec((k, tn), lambda i, j: (0, j))],
        out_specs=pl.BlockSpec((tm, tn), lambda i, j: (i, j)),
        out_shape=jax.ShapeDtypeStruct((m, n), out_dtype),
        scratch_shapes=[pltpu.VMEM((tm, k), BF16)],
        compiler_params=_params(("parallel", "arbitrary")),
        name=name,
    )(x, gain, w)


def _gla_levels(chunk):
    return [chunk >> (i + 1) for i in range(int(np.log2(chunk)))]


def _gla_exponent_matrix(chunk):
    c = chunk
    i = np.arange(c)[:, None]
    t = np.arange(c)[None, :]
    blocks = [(t <= i), (t > i)]
    for h in _gla_levels(c):
        r = (i // (2 * h)) * (2 * h) + h
        upper = i >= r
        blocks.append(np.where(upper, (t > r) & (t <= i), (t > i) & (t <= r)))
    m = np.concatenate(blocks, axis=0).astype(np.float32)
    z = np.zeros_like(m)
    return np.concatenate([np.concatenate([m, m, z, z], axis=1), np.concatenate([z, z, m, m], axis=1)], axis=0)


def _gla_kernel(q_ref, k_ref, v_ref, sm_ref, r_ref, wlr_ref, blr_ref, gn_ref, mst_ref, o_ref, s_scr):
    c = GLA_CHUNK
    seq = q_ref.shape[0]
    s_scr[...] = jnp.zeros_like(s_scr)
    ri = lax.broadcasted_iota(jnp.int32, (c, c), 0)
    ci = lax.broadcasted_iota(jnp.int32, (c, c), 1)
    rowi = lax.broadcasted_iota(jnp.int32, (c, GLA_DK), 0)
    levels = _gla_levels(c)
    assert GLA_GROUP == 4

    grp = GLA_GROUP
    gs = range(grp)

    def body(n, carry):
        row0 = pl.multiple_of(n * (grp * c), grp * c)
        rows = [pl.ds(row0 + g * c, c) for g in gs]
        allrows = pl.ds(row0, grp * c)
        x = _dot(sm_ref[allrows, :].astype(BF16), wlr_ref[...]) + blr_ref[...]
        lg = (jnp.minimum(x, 0.0) - jnp.log1p(jnp.exp(-jnp.abs(x)))) * (1.0 / GLA_TAU)
        hi, mid = _split2(lg)
        his = [hi[g * c:(g + 1) * c] for g in gs]
        mids = [mid[g * c:(g + 1) * c] for g in gs]
        hm = [jnp.concatenate([his[g], mids[g]], axis=0) for g in gs]
        rhs = jnp.concatenate([jnp.concatenate([hm[0], hm[1]], axis=1),
                               jnp.concatenate([hm[2], hm[3]], axis=1)], axis=0)
        ex = jnp.exp(_dot(mst_ref[...], rhs))
        nr = ex.shape[0] // 2
        e = [ex[0:nr, 0:GLA_DK], ex[0:nr, GLA_DK:], ex[nr:, 0:GLA_DK], ex[nr:, GLA_DK:]]
        q = [q_ref[rows[g], :] * (GLA_DK ** -0.5) for g in gs]
        k = [k_ref[rows[g], :] for g in gs]
        v = [v_ref[rows[g], :].astype(BF16) for g in gs]
        attn = [jnp.where(ri == ci, _dot_nt(q[g].astype(BF16), k[g].astype(BF16)), 0.0) for g in gs]
        for lvl, h in enumerate(levels):
            upper = (rowi & h) != 0
            sh = int(np.log2(2 * h))
            same = (ri >> sh) == (ci >> sh)
            for g in gs:
                f = e[g][(2 + lvl) * c:(3 + lvl) * c]
                qt = jnp.where(upper, q[g] * f, 0.0).astype(BF16)
                kt = jnp.where(upper, 0.0, k[g] * f).astype(BF16)
                attn[g] = attn[g] + jnp.where(same, _dot_nt(qt, kt), 0.0)
        vk = [_dot_tn(v[g], (k[g] * e[g][c:2 * c]).astype(BF16)) for g in gs]
        av = [_dot(attn[g].astype(BF16), v[g]) for g in gs]
        st = s_scr[...]
        for g in gs:
            o = _dot_nt((q[g] * e[g][0:c]).astype(BF16), st.astype(BF16)) + av[g]
            st = st * e[g][c - 1:c, :] + vk[g]
            on = _rms(o, gn_ref[...]) * _silu(r_ref[rows[g], :].astype(F32))
            o_ref[rows[g], :] = on.astype(o_ref.dtype)
        s_scr[...] = st
        return carry

    lax.fori_loop(0, seq // (grp * c), body, 0)


def _gla(pa, pb, ps, wlr, blr, gnorm, bsz, seq):
    t = bsz * seq
    mst = jnp.asarray(_gla_exponent_matrix(GLA_CHUNK), BF16)
    nq = GLA_QK // GLA_DK
    return pl.pallas_call(
        _gla_kernel,
        grid=(bsz, GLA_HEADS),
        in_specs=[
            pl.BlockSpec((seq, GLA_DK), lambda b, h: (b, h)),
            pl.BlockSpec((seq, GLA_DK), lambda b, h: (b, nq + h)),
            pl.BlockSpec((seq, GLA_DV), lambda b, h: (b, 2 * GLA_QK // GLA_DV + h)),
            pl.BlockSpec((seq, LANES), lambda b, h: (b, 0)),
            pl.BlockSpec((seq, GLA_DV), lambda b, h: (b, _PB_R // GLA_DV + h)),
            pl.BlockSpec((LANES, GLA_DK), lambda b, h: (0, h)),
            pl.BlockSpec((1, GLA_DK), lambda b, h: (0, h)),
            pl.BlockSpec((1, GLA_DV), lambda b, h: (0, 0)),
            pl.BlockSpec(mst.shape, lambda b, h: (0, 0)),
        ],
        out_specs=pl.BlockSpec((seq, GLA_DV), lambda b, h: (b, h)),
        out_shape=jax.ShapeDtypeStruct((t, GLA_V), BF16),
        scratch_shapes=[pltpu.VMEM((GLA_DV, GLA_DK), F32)],
        compiler_params=_params(("parallel", "parallel")),
        name="gla",
    )(pa, pa, pa, ps, pb, wlr, blr, gnorm, mst)


def _gdn_kernel(q_ref, k_ref, v_ref, sm_ref, z_ref, gn_ref, o_ref, bms, ns, qms, os_, egl, s_scr):
    c = GDN_CHUNK
    seq = q_ref.shape[0]
    head0 = pl.program_id(1) * GDN_HPS

    ri = lax.broadcasted_iota(jnp.int32, (c, c), 0)
    ci = lax.broadcasted_iota(jnp.int32, (c, c), 1)
    lane = lax.broadcasted_iota(jnp.int32, (c, LANES), 1)
    tril = jnp.where(ri >= ci, 1.0, 0.0).astype(BF16)
    causal = ri >= ci
    n_sq = int(np.log2(c))
    lower_left = [(((ri ^ ci) >> (l + 1)) == 0) & ((ri & (1 << l)) != 0) & ((ci & (1 << l)) == 0) for l in range(n_sq)]

    grp = GDN_GROUP
    chains = [(g, j) for g in range(grp) for j in range(GDN_HPS)]

    def phase1(n, carry):
        rows, q, k, kb, gb, rhs0 = {}, {}, {}, {}, {}, {}
        for g in range(grp):
            rows[g] = pl.ds(pl.multiple_of((n * grp + g) * c, c), c)
            sm = sm_ref[rows[g], :]
            for j in range(GDN_HPS):
                sl = slice(j * GDN_DK, (j + 1) * GDN_DK)
                q[g, j] = q_ref[rows[g], sl]
                k[g, j] = k_ref[rows[g], sl]
                gcol = jnp.sum(jnp.where(lane == SM_A + head0 + j, sm, 0.0), axis=-1, keepdims=True)
                beta = jnp.sum(jnp.where(lane == SM_B + head0 + j, sm, 0.0), axis=-1, keepdims=True)
                gb[g, j] = jnp.broadcast_to(gcol, (c, LANES))
                kb[g, j] = k[g, j] * beta
                rhs0[g, j] = v_ref[rows[g], sl] * beta
        cum_r, cum_c, kt = {}, {}, {}
        for ch in chains:
            hi, mid = _split2(gb[ch])
            r = _dot(tril, jnp.concatenate([hi, mid], axis=1))
            cum_r[ch] = r[:, :LANES] + r[:, LANES:]
        for ch in chains:
            cum_c[ch] = cum_r[ch].T
            kt[ch] = k[ch].T
        kk = {}
        for ch in chains:
            k16 = k[ch].astype(BF16)
            kk[ch] = _dot_nt(jnp.concatenate([kb[ch].astype(BF16), q[ch].astype(BF16)], axis=0), k16)
        low, at16, x, p = {}, {}, {}, {}
        for ch in chains:
            dec = jnp.where(causal, jnp.exp(jnp.where(causal, cum_r[ch] - cum_c[ch], 0.0)), 0.0)
            low[ch] = jnp.where(ri > ci, kk[ch][:c] * dec, 0.0)
            at16[ch] = jnp.where(causal, kk[ch][c:] * dec, 0.0).astype(BF16)
        for ch in chains:
            x[ch] = -jnp.where(lower_left[0], low[ch], 0.0)
        for lvl in range(1, n_sq):
            e = {}
            for ch in chains:
                cm = jnp.where(lower_left[lvl], low[ch], 0.0)
                e[ch] = cm + _mm(x[ch], cm)
            for ch in chains:
                x[ch] = x[ch] - e[ch] - _mm(e[ch], x[ch])
        wu, egc = {}, {}
        for ch in chains:
            egc[ch] = jnp.exp(cum_r[ch])
            rhs = jnp.concatenate([kb[ch] * egc[ch], rhs0[ch]], axis=1)
            wu[ch] = (rhs + _mm(x[ch], rhs)).astype(BF16)
        for ch in chains:
            g, j = ch
            sl = slice(j * GDN_DK, (j + 1) * GDN_DK)
            glast = cum_r[ch][c - 1:c, :]
            kdt16 = (kt[ch] * jnp.exp(glast - cum_c[ch][0:1, :])).astype(BF16)
            r = _dot(jnp.concatenate([kdt16, at16[ch]], axis=0), wu[ch])
            bms[rows[g], sl] = (-r[:c, :GDN_DV]).astype(BF16)
            ns[rows[g], sl] = r[:c, GDN_DV:]
            qms[rows[g], sl] = (q[ch] * egc[ch] - r[c:, :GDN_DV]).astype(BF16)
            os_[rows[g], sl] = r[c:, GDN_DV:]
            egl[pl.ds(pl.multiple_of((n * grp + g) * 8, 8), 8), sl] = jnp.broadcast_to(jnp.exp(glast), (8, LANES))
        return carry

    lax.fori_loop(0, seq // (grp * c), phase1, 0)

    s_scr[...] = jnp.zeros_like(s_scr)

    def phase2(n, carry):
        rows = pl.ds(pl.multiple_of(n * c, c), c)
        for j in range(GDN_HPS):
            sl = slice(j * GDN_DK, (j + 1) * GDN_DK)
            s = s_scr[j]
            s16 = s.astype(BF16)
            eg = egl[pl.ds(pl.multiple_of(n * 8, 8), 8), sl][0:1, :]
            s_scr[j] = s * eg + _dot(bms[rows, sl], s16) + ns[rows, sl]
            o = _dot(qms[rows, sl], s16) + os_[rows, sl]
            on = _rms(o, gn_ref[...]) * _silu(z_ref[rows, sl].astype(F32))
            o_ref[rows, sl] = on.astype(o_ref.dtype)
        return carry

    lax.fori_loop(0, seq // c, phase2, 0)


def _gdn(pg, pb, ps, gnorm, bsz, seq):
    t = bsz * seq
    nb = seq // GDN_CHUNK
    wd = GDN_HPS * GDN_DK
    ng = GDN_HEADS // GDN_HPS
    return pl.pallas_call(
        _gdn_kernel,
        grid=(bsz, ng),
        in_specs=[
            pl.BlockSpec((seq, wd), lambda b, h: (b, h)),
            pl.BlockSpec((seq, wd), lambda b, h: (b, ng + h)),
            pl.BlockSpec((seq, wd), lambda b, h: (b, 2 * ng + h)),
            pl.BlockSpec((seq, LANES), lambda b, h: (b, 0)),
            pl.BlockSpec((seq, wd), lambda b, h: (b, _PB_Z // wd + h)),
            pl.BlockSpec((1, GDN_DV), lambda b, h: (0, 0)),
        ],
        out_specs=pl.BlockSpec((seq, wd), lambda b, h: (b, h)),
        out_shape=jax.ShapeDtypeStruct((t, GDN_V), BF16),
        scratch_shapes=[
            pltpu.VMEM((seq, wd), BF16), pltpu.VMEM((seq, wd), F32), pltpu.VMEM((seq, wd), BF16),
            pltpu.VMEM((seq, wd), F32), pltpu.VMEM((nb * 8, wd), F32),
            pltpu.VMEM((GDN_HPS, GDN_DK, GDN_DV), F32),
        ],
        compiler_params=_params(("parallel", "parallel")),
        name="gdn",
    )(pg, pg, pg, ps, pb, gnorm)


def _xa_kernel(q_ref, mk_ref, mv_ref, o_ref):
    s = _dot_nt(q_ref[...], mk_ref[...]) * (XA_DH ** -0.5)
    p = jnp.exp(s - jnp.max(s, axis=-1, keepdims=True))
    den = jnp.sum(p, axis=-1, keepdims=True)
    o_ref[...] = (_dot(p.astype(BF16), mv_ref[...]) / den).astype(o_ref.dtype)


def _xa(pb, mkv, bsz, seq, ts=1024):
    t = bsz * seq
    nt = seq // ts
    return pl.pallas_call(
        _xa_kernel,
        grid=(bsz, XA_HEADS, nt),
        in_specs=[
            pl.BlockSpec((ts, XA_DH), lambda b, h, i: (b * nt + i, _PB_XQ // XA_DH + h)),
            pl.BlockSpec((MEM_LEN, XA_DH), lambda b, h, i: (b, h)),
            pl.BlockSpec((MEM_LEN, XA_DH), lambda b, h, i: (b, XA_HEADS + h)),
        ],
        out_specs=pl.BlockSpec((ts, XA_DH), lambda b, h, i: (b * nt + i, h)),
        out_shape=jax.ShapeDtypeStruct((t, XA_W), BF16),
        compiler_params=_params(("parallel", "parallel", "parallel")),
        name="xattn",
    )(pb, mkv, mkv)


def _merge_kernel(h_ref, oa_ref, ob_ref, oc_ref, ga_ref, gb_ref, gc_ref, wa_ref, wb_ref, wc_ref, wo_ref,
                  gpost_ref, o_ref):
    mixed = jax.nn.sigmoid(ga_ref[...].astype(F32)) * _dot(oa_ref[...], wa_ref[...])
    mixed = mixed + jax.nn.sigmoid(gb_ref[...].astype(F32)) * _dot(ob_ref[...], wb_ref[...])
    mixed = mixed + jax.nn.sigmoid(gc_ref[...].astype(F32)) * _dot(oc_ref[...], wc_ref[...])
    m = _dot(mixed.astype(BF16), wo_ref[...])
    o_ref[...] = h_ref[...] + _rms(m, gpost_ref[...])


def _merge(h, oa, ob, oc, pb, wa, wb, wc, wo, gpost, tm=256):
    t, d = h.shape
    row = lambda i: (i, 0)
    const = lambda i: (0, 0)
    once = pl.Buffered(1)
    return pl.pallas_call(
        _merge_kernel,
        grid=(t // tm,),
        in_specs=[
            pl.BlockSpec((tm, d), row),
            pl.BlockSpec((tm, GLA_V), row), pl.BlockSpec((tm, GDN_V), row), pl.BlockSpec((tm, XA_W), row),
            pl.BlockSpec((tm, d), lambda i: (i, 0)), pl.BlockSpec((tm, d), lambda i: (i, 1)),
            pl.BlockSpec((tm, d), lambda i: (i, 2)),
            pl.BlockSpec((GLA_V, d), const, pipeline_mode=once), pl.BlockSpec((GDN_V, d), const, pipeline_mode=once),
            pl.BlockSpec((XA_W, d), const, pipeline_mode=once), pl.BlockSpec((d, d), const, pipeline_mode=once),
            pl.BlockSpec((1, d), const),
        ],
        out_specs=pl.BlockSpec((tm, d), row),
        out_shape=jax.ShapeDtypeStruct((t, d), F32),
        compiler_params=_params(("parallel",)),
        name="merge",
    )(h, oa, ob, oc, pb, pb, pb, wa, wb, wc, wo, gpost)


_PA_GDN = 2 * GLA_QK + GLA_V
_PA_COLS = _PA_GDN + 2 * GDN_QK + GDN_V
_PB_R = N_BRANCH * D_MODEL
_PB_Z = _PB_R + GLA_V
_PB_XQ = _PB_Z + GDN_V
_PB_COLS = _PB_XQ + XA_W


_W_ORDER = (0, 1, 2, 5, 10, 4, 8, 9, 3, 6, 7)
_W_ALL_COLS = _PA_COLS + _PB_COLS + LANES


def _relayout_kernel(w_ref, o_ref):
    dst = 0
    for i in _W_ORDER:
        n = _IN_OFF[i + 1] - _IN_OFF[i]
        o_ref[:, dst:dst + n] = w_ref[:, _IN_OFF[i]:_IN_OFF[i + 1]].astype(BF16)
        dst += n
    o_ref[:, dst:] = jnp.zeros((o_ref.shape[0], _W_ALL_COLS - dst), BF16)


def _split_w_in(w_in, rb=64):
    k, n = w_in.shape
    return pl.pallas_call(
        _relayout_kernel,
        grid=(k // rb,),
        in_specs=[pl.BlockSpec((rb, n), lambda i: (i, 0))],
        out_specs=pl.BlockSpec((rb, _W_ALL_COLS), lambda i: (i, 0)),
        out_shape=jax.ShapeDtypeStruct((k, _W_ALL_COLS), BF16),
        compiler_params=_params(("parallel",)),
        name="w_in_relayout",
    )(w_in)


def kernel(x, mem, n_ffn1_pre, w_ffn1_gu, w_ffn1_down, n_ffn1_post, n_mix_pre, w_in, gla_w_lr2, gla_b_lr,
           gla_norm, gdn_conv, gdn_a_log, gdn_dt_bias, gdn_norm, mem_norm, w_mem_kv, w_up_gla, w_up_gdn,
           w_up_xa, w_out, n_mix_post, n_ffn2_pre, w_ffn2_gu, w_ffn2_down, n_ffn2_post):
    bsz, seq, d = x.shape
    t = bsz * seq
    h = x.reshape(t, d)
    for l in range(n_ffn1_pre.shape[0]):
        row = lambda a: a[l][None, :]
        h, u = _ffn(h, row(n_ffn1_pre), w_ffn1_gu[l], w_ffn1_down[l],
                    row(n_ffn1_post), row(n_mix_pre), emit_next=True)

        w_all = _split_w_in(w_in[l])
        par = jnp.zeros((2, LANES), F32)
        par = par.at[0, SM_A:SM_A + GDN_HEADS].set(gdn_a_log[l]).at[1, SM_A:SM_A + GDN_HEADS].set(gdn_dt_bias[l])
        tn_g = 512
        pa = _matmul(u, w_all, 0, _PA_GDN, F32, 2048, 512, "in_proj_a")
        pg = _matmul(u, w_all, _PA_GDN, _PA_COLS - _PA_GDN, F32, seq, tn_g, "in_proj_g", body=_proj_conv_kernel,
                     extra=(gdn_conv[l],), extra_specs=(pl.BlockSpec((CONV_W, tn_g), lambda i, j: (0, j)),))
        pb = _matmul(u, w_all, _PA_COLS, _PB_COLS, BF16, 2048, 1024, "in_proj_b")
        ps = _matmul(u, w_all, _PA_COLS + _PB_COLS, LANES, F32, 2048, LANES, "in_proj_s", body=_proj_s_kernel,
                     extra=(par,), extra_specs=(pl.BlockSpec((2, LANES), lambda i, j: (0, 0)),))

        wlr = jnp.pad(gla_w_lr2[l], ((0, LANES - GLA_RANK), (0, 0))).astype(BF16)
        o_gla = _gla(pa, pb, ps, wlr, row(gla_b_lr), row(gla_norm), bsz, seq)
        o_gdn = _gdn(pg, pb, ps, row(gdn_norm), bsz, seq)

        mkv = _norm_matmul(mem.reshape(bsz * MEM_LEN, d), row(mem_norm), w_mem_kv[l], BF16,
                           bsz * MEM_LEN, 512, "mem_kv")
        o_xa = _xa(pb, mkv, bsz, seq)

        h = _merge(h, o_gla, o_gdn, o_xa, pb, w_up_gla[l].astype(BF16), w_up_gdn[l].astype(BF16),
                   w_up_xa[l].astype(BF16), w_out[l].astype(BF16), row(n_mix_post))
        h = _ffn(h, row(n_ffn2_pre), w_ffn2_gu[l], w_ffn2_down[l],
                 row(n_ffn2_post), row(n_ffn2_post), emit_next=False)
    return h.reshape(bsz, seq, d)
```

```python
import functools

import numpy as np
import jax
import jax.numpy as jnp
from jax import lax
from jax.experimental import pallas as pl
from jax.experimental.pallas import tpu as pltpu

F32 = jnp.float32
BF16 = jnp.bfloat16

D_MODEL = 2048
MEM_LEN = 256
EPS = 1e-6
GLA_HEADS = 4
GLA_DK = 128
GLA_DV = 256
GLA_QK = GLA_HEADS * GLA_DK
GLA_V = GLA_HEADS * GLA_DV
GLA_RANK = 16
GLA_TAU = 16.0
GDN_HEADS = 8
GDN_DK = 128
GDN_DV = 128
GDN_QK = GDN_HEADS * GDN_DK
GDN_V = GDN_HEADS * GDN_DV
CONV_W = 4
XA_HEADS = 4
XA_DH = 256
XA_W = XA_HEADS * XA_DH
N_BRANCH = 3
D_FF = 5632

LANES = 128
GLA_CHUNK = 64
GLA_GROUP = 4
GDN_CHUNK = 128
GDN_HPS = 2
GDN_GROUP = 4
VMEM_LIMIT = 56 * 1024 * 1024

_IN_SIZES = (GLA_QK, GLA_QK, GLA_V, GLA_RANK, GLA_V, 2 * GDN_QK + GDN_V, GDN_HEADS, GDN_HEADS, GDN_V,
             XA_W, N_BRANCH * D_MODEL)
_IN_OFF = tuple(int(v) for v in np.cumsum((0,) + _IN_SIZES))
SM_LR, SM_B, SM_A = 0, GLA_RANK, GLA_RANK + GDN_HEADS


def _dot(a, b):
    return jnp.dot(a, b, preferred_element_type=F32)


def _dot_nt(a, b):
    return lax.dot_general(a, b, (((1,), (1,)), ((), ())), preferred_element_type=F32)


def _dot_tn(a, b):
    return lax.dot_general(a, b, (((0,), (0,)), ((), ())), preferred_element_type=F32)


def _mm(a, b):
    return _dot(a.astype(BF16), b.astype(BF16))


def _split2(x):
    hi = x.astype(BF16)
    mid = (x - hi.astype(F32)).astype(BF16)
    return hi, mid


def _dot01(m01, x):
    hi, mid = _split2(x)
    return _dot(m01, hi) + _dot(m01, mid)


def _rms(x, gain):
    return x * lax.rsqrt(jnp.mean(x * x, axis=-1, keepdims=True) + EPS) * gain


def _silu(x):
    return x * jax.nn.sigmoid(x)


def _softplus(x):
    return jnp.maximum(x, 0.0) + jnp.log1p(jnp.exp(-jnp.abs(x)))


def _params(sem):
    return pltpu.CompilerParams(dimension_semantics=sem, vmem_limit_bytes=VMEM_LIMIT)


def _ffn_kernel(x_ref, gpre_ref, wg_ref, wu_ref, wd_ref, gpost_ref, gnext_ref, *out_and_scratch, emit_next):
    if emit_next:
        h_ref, un_ref, xn_scr, wg16, wu16, wd16 = out_and_scratch
    else:
        h_ref, xn_scr, wg16, wu16, wd16 = out_and_scratch
    j = pl.program_id(1)

    @pl.when(j == 0)
    def _():
        xn_scr[...] = _rms(x_ref[...], gpre_ref[...]).astype(BF16)
        h_ref[...] = jnp.zeros_like(h_ref)

    wg16[...] = wg_ref[...].astype(BF16)
    wu16[...] = wu_ref[...].astype(BF16)
    wd16[...] = wd_ref[...].astype(BF16)
    xn = xn_scr[...]
    g = _dot(xn, wg16[...])
    u = _dot(xn, wu16[...])
    hm = (_silu(g) * u).astype(BF16)
    h_ref[...] += _dot(hm, wd16[...])

    @pl.when(j == pl.num_programs(1) - 1)
    def _():
        h = x_ref[...] + 0.5 * _rms(h_ref[...], gpost_ref[...])
        h_ref[...] = h
        if emit_next:
            un_ref[...] = _rms(h, gnext_ref[...]).astype(BF16)


def _ffn(x, gpre, w_gu, w_down, gpost, gnext, emit_next, tm=1024, tf=256):
    t, d = x.shape
    nf = D_FF // tf
    row = lambda i, j: (i, 0)
    const = lambda i, j: (0, 0)
    once = pl.Buffered(1)
    out_shape = [jax.ShapeDtypeStruct((t, d), F32)]
    out_specs = [pl.BlockSpec((tm, d), row, pipeline_mode=once)]
    if emit_next:
        out_shape.append(jax.ShapeDtypeStruct((t, d), BF16))
        out_specs.append(pl.BlockSpec((tm, d), row, pipeline_mode=once))
    res = pl.pallas_call(
        functools.partial(_ffn_kernel, emit_next=emit_next),
        grid=(t // tm, nf),
        in_specs=[
            pl.BlockSpec((tm, d), row, pipeline_mode=once),
            pl.BlockSpec((1, d), const),
            pl.BlockSpec((d, tf), lambda i, j: (0, j)),
            pl.BlockSpec((d, tf), lambda i, j: (0, nf + j)),
            pl.BlockSpec((tf, d), lambda i, j: (j, 0)),
            pl.BlockSpec((1, d), const),
            pl.BlockSpec((1, d), const),
        ],
        out_specs=out_specs,
        out_shape=out_shape,
        scratch_shapes=[pltpu.VMEM((tm, d), BF16), pltpu.VMEM((d, tf), BF16), pltpu.VMEM((d, tf), BF16),
                        pltpu.VMEM((tf, d), BF16)],
        compiler_params=_params(("parallel", "arbitrary")),
        name="ffn",
    )(x, gpre, w_gu, w_gu, w_down, gpost, gnext)
    return res if emit_next else res[0]


def _matmul_kernel(a_ref, w_ref, o_ref):
    o_ref[...] = _dot_nt(a_ref[...], w_ref[...]).astype(o_ref.dtype)


def _proj_conv_kernel(a_ref, w_ref, cw_ref, o_ref):
    tm, tn = o_ref.shape
    rb = 256
    kind = pl.program_id(1) // (GDN_QK // tn)
    scale = jnp.where(kind == 0, GDN_DK ** -0.5, 1.0)
    w = cw_ref[...]
    r8 = lax.broadcasted_iota(jnp.int32, (8, tn), 0)
    prev = jnp.zeros((8, tn), F32)
    for r in range(0, tm, rb):
        y = _dot_nt(a_ref[r:r + rb, :], w_ref[...])
        acc = y * w[CONV_W - 1:CONV_W, :]
        for sft in range(1, CONV_W):
            xs = pltpu.roll(y, sft, 0)
            top = jnp.where(r8 < sft, pltpu.roll(prev, sft, 0), xs[0:8])
            acc = acc + jnp.concatenate([top, xs[8:]], axis=0) * w[CONV_W - 1 - sft:CONV_W - sft, :]
        prev = y[rb - 8:rb]
        c = _silu(acc)
        for hh in range(tn // GDN_DK):
            sl = slice(hh * GDN_DK, (hh + 1) * GDN_DK)
            blk = c[:, sl]
            f = lax.rsqrt(jnp.sum(blk * blk, axis=-1, keepdims=True) + EPS) * scale
            o_ref[r:r + rb, sl] = blk * jnp.where(kind == 2, 1.0, f)


def _proj_s_kernel(a_ref, w_ref, par_ref, o_ref):
    y = _dot_nt(a_ref[...], w_ref[...])
    par = par_ref[...]
    lane = lax.broadcasted_iota(jnp.int32, y.shape, 1)
    g = -jnp.exp(par[0:1, :]) * _softplus(y + par[1:2, :])
    is_g = (lane >= SM_A) & (lane < SM_A + GDN_HEADS)
    is_b = (lane >= SM_B) & (lane < SM_B + GDN_HEADS)
    o_ref[...] = jnp.where(is_g, g, jnp.where(is_b, jax.nn.sigmoid(y), y))


def _matmul(a, w, col0, n, out_dtype, tm, tn, name, body=_matmul_kernel, extra=(), extra_specs=()):
    m, k = a.shape
    c0 = col0 // tn
    return pl.pallas_call(
        body,
        grid=(m // tm, n // tn),
        in_specs=[pl.BlockSpec((tm, k), lambda i, j: (i, 0)), pl.BlockSpec((tn, k), lambda i, j: (c0 + j, 0)),
                  *extra_specs],
        out_specs=pl.BlockSpec((tm, tn), lambda i, j: (i, j)),
        out_shape=jax.ShapeDtypeStruct((m, n), out_dtype),
        compiler_params=_params(("parallel", "parallel")),
        name=name,
    )(a, w, *extra)


def _norm_matmul_kernel(x_ref, g_ref, w_ref, o_ref, xn_scr):
    @pl.when(pl.program_id(1) == 0)
    def _():
        xn_scr[...] = _rms(x_ref[...], g_ref[...]).astype(BF16)

    o_ref[...] = _dot(xn_scr[...], w_ref[...].astype(BF16)).astype(o_ref.dtype)


def _norm_matmul(x, gain, w, out_dtype, tm, tn, name):
    m, k = x.shape
    n = w.shape[1]
    return pl.pallas_call(
        _norm_matmul_kernel,
        grid=(m // tm, n // tn),
        in_specs=[pl.BlockSpec((tm, k), lambda i, j: (i, 0)), pl.BlockSpec((1, k), lambda i, j: (0, 0)),
                  pl.BlockSpec((k, tn), lambda i, j: (0, j))],
        out_specs=pl.BlockSpec((tm, tn), lambda i, j: (i, j)),
        out_shape=jax.ShapeDtypeStruct((m, n), out_dtype),
        scratch_shapes=[pltpu.VMEM((tm, k), BF16)],
        compiler_params=_params(("parallel", "arbitrary")),
        name=name,
    )(x, gain, w)


def _gla_levels(chunk):
    return [chunk >> (i + 1) for i in range(int(np.log2(chunk)))]


def _gla_exponent_matrix(chunk):
    c = chunk
    i = np.arange(c)[:, None]
    t = np.arange(c)[None, :]
    blocks = [(t <= i), (t > i)]
    for h in _gla_levels(c):
        r = (i // (2 * h)) * (2 * h) + h
        upper = i >= r
        blocks.append(np.where(upper, (t > r) & (t <= i), (t > i) & (t <= r)))
    m = np.concatenate(blocks, axis=0).astype(np.float32)
    z = np.zeros_like(m)
    return np.concatenate([np.concatenate([m, m, z, z], axis=1), np.concatenate([z, z, m, m], axis=1)], axis=0)


def _gla_kernel(q_ref, k_ref, v_ref, sm_ref, r_ref, wlr_ref, blr_ref, gn_ref, mst_ref, o_ref, s_scr):
    c = GLA_CHUNK
    seq = q_ref.shape[0]
    s_scr[...] = jnp.zeros_like(s_scr)
    ri = lax.broadcasted_iota(jnp.int32, (c, c), 0)
    ci = lax.broadcasted_iota(jnp.int32, (c, c), 1)
    rowi = lax.broadcasted_iota(jnp.int32, (c, GLA_DK), 0)
    levels = _gla_levels(c)
    assert GLA_GROUP == 4

    grp = GLA_GROUP
    gs = range(grp)

    def body(n, carry):
        row0 = pl.multiple_of(n * (grp * c), grp * c)
        rows = [pl.ds(row0 + g * c, c) for g in gs]
        allrows = pl.ds(row0, grp * c)
        x = _dot(sm_ref[allrows, :].astype(BF16), wlr_ref[...]) + blr_ref[...]
        lg = (jnp.minimum(x, 0.0) - jnp.log1p(jnp.exp(-jnp.abs(x)))) * (1.0 / GLA_TAU)
        hi, mid = _split2(lg)
        his = [hi[g * c:(g + 1) * c] for g in gs]
        mids = [mid[g * c:(g + 1) * c] for g in gs]
        hm = [jnp.concatenate([his[g], mids[g]], axis=0) for g in gs]
        rhs = jnp.concatenate([jnp.concatenate([hm[0], hm[1]], axis=1),
                               jnp.concatenate([hm[2], hm[3]], axis=1)], axis=0)
        ex = jnp.exp(_dot(mst_ref[...], rhs))
        nr = ex.shape[0] // 2
        e = [ex[0:nr, 0:GLA_DK], ex[0:nr, GLA_DK:], ex[nr:, 0:GLA_DK], ex[nr:, GLA_DK:]]
        q = [q_ref[rows[g], :] * (GLA_DK ** -0.5) for g in gs]
        k = [k_ref[rows[g], :] for g in gs]
        v = [v_ref[rows[g], :].astype(BF16) for g in gs]
        attn = [jnp.where(ri == ci, _dot_nt(q[g].astype(BF16), k[g].astype(BF16)), 0.0) for g in gs]
        for lvl, h in enumerate(levels):
            upper = (rowi & h) != 0
            sh = int(np.log2(2 * h))
            same = (ri >> sh) == (ci >> sh)
            for g in gs:
                f = e[g][(2 + lvl) * c:(3 + lvl) * c]
                qt = jnp.where(upper, q[g] * f, 0.0).astype(BF16)
                kt = jnp.where(upper, 0.0, k[g] * f).astype(BF16)
                attn[g] = attn[g] + jnp.where(same, _dot_nt(qt, kt), 0.0)
        vk = [_dot_tn(v[g], (k[g] * e[g][c:2 * c]).astype(BF16)) for g in gs]
        av = [_dot(attn[g].astype(BF16), v[g]) for g in gs]
        st = s_scr[...]
        for g in gs:
            o = _dot_nt((q[g] * e[g][0:c]).astype(BF16), st.astype(BF16)) + av[g]
            st = st * e[g][c - 1:c, :] + vk[g]
            on = _rms(o, gn_ref[...]) * _silu(r_ref[rows[g], :].astype(F32))
            o_ref[rows[g], :] = on.astype(o_ref.dtype)
        s_scr[...] = st
        return carry

    lax.fori_loop(0, seq // (grp * c), body, 0)


def _gla(pa, pb, ps, wlr, blr, gnorm, bsz, seq):
    t = bsz * seq
    mst = jnp.asarray(_gla_exponent_matrix(GLA_CHUNK), BF16)
    nq = GLA_QK // GLA_DK
    return pl.pallas_call(
        _gla_kernel,
        grid=(bsz, GLA_HEADS),
        in_specs=[
            pl.BlockSpec((seq, GLA_DK), lambda b, h: (b, h)),
            pl.BlockSpec((seq, GLA_DK), lambda b, h: (b, nq + h)),
            pl.BlockSpec((seq, GLA_DV), lambda b, h: (b, 2 * GLA_QK // GLA_DV + h)),
            pl.BlockSpec((seq, LANES), lambda b, h: (b, 0)),
            pl.BlockSpec((seq, GLA_DV), lambda b, h: (b, _PB_R // GLA_DV + h)),
            pl.BlockSpec((LANES, GLA_DK), lambda b, h: (0, h)),
            pl.BlockSpec((1, GLA_DK), lambda b, h: (0, h)),
            pl.BlockSpec((1, GLA_DV), lambda b, h: (0, 0)),
            pl.BlockSpec(mst.shape, lambda b, h: (0, 0)),
        ],
        out_specs=pl.BlockSpec((seq, GLA_DV), lambda b, h: (b, h)),
        out_shape=jax.ShapeDtypeStruct((t, GLA_V), BF16),
        scratch_shapes=[pltpu.VMEM((GLA_DV, GLA_DK), F32)],
        compiler_params=_params(("parallel", "parallel")),
        name="gla",
    )(pa, pa, pa, ps, pb, wlr, blr, gnorm, mst)


def _gdn_kernel(q_ref, k_ref, v_ref, sm_ref, z_ref, gn_ref, o_ref, bms, ns, qms, os_, egl, s_scr):
    c = GDN_CHUNK
    seq = q_ref.shape[0]
    head0 = pl.program_id(1) * GDN_HPS

    ri = lax.broadcasted_iota(jnp.int32, (c, c), 0)
    ci = lax.broadcasted_iota(jnp.int32, (c, c), 1)
    lane = lax.broadcasted_iota(jnp.int32, (c, LANES), 1)
    tril = jnp.where(ri >= ci, 1.0, 0.0).astype(BF16)
    causal = ri >= ci
    n_sq = int(np.log2(c))
    lower_left = [(((ri ^ ci) >> (l + 1)) == 0) & ((ri & (1 << l)) != 0) & ((ci & (1 << l)) == 0) for l in range(n_sq)]

    grp = GDN_GROUP
    chains = [(g, j) for g in range(grp) for j in range(GDN_HPS)]

    def phase1(n, carry):
        rows, q, k, kb, gb, rhs0 = {}, {}, {}, {}, {}, {}
        for g in range(grp):
            rows[g] = pl.ds(pl.multiple_of((n * grp + g) * c, c), c)
            sm = sm_ref[rows[g], :]
            for j in range(GDN_HPS):
                sl = slice(j * GDN_DK, (j + 1) * GDN_DK)
                q[g, j] = q_ref[rows[g], sl]
                k[g, j] = k_ref[rows[g], sl]
                gcol = jnp.sum(jnp.where(lane == SM_A + head0 + j, sm, 0.0), axis=-1, keepdims=True)
                beta = jnp.sum(jnp.where(lane == SM_B + head0 + j, sm, 0.0), axis=-1, keepdims=True)
                gb[g, j] = jnp.broadcast_to(gcol, (c, LANES))
                kb[g, j] = k[g, j] * beta
                rhs0[g, j] = v_ref[rows[g], sl] * beta
        cum_r, cum_c, kt = {}, {}, {}
        for ch in chains:
            hi, mid = _split2(gb[ch])
            r = _dot(tril, jnp.concatenate([hi, mid], axis=1))
            cum_r[ch] = r[:, :LANES] + r[:, LANES:]
        for ch in chains:
            cum_c[ch] = cum_r[ch].T
            kt[ch] = k[ch].T
        kk = {}
        for ch in chains:
            k16 = k[ch].astype(BF16)
            kk[ch] = _dot_nt(jnp.concatenate([kb[ch].astype(BF16), q[ch].astype(BF16)], axis=0), k16)
        low, at16, x, p = {}, {}, {}, {}
        for ch in chains:
            dec = jnp.where(causal, jnp.exp(jnp.where(causal, cum_r[ch] - cum_c[ch], 0.0)), 0.0)
            low[ch] = jnp.where(ri > ci, kk[ch][:c] * dec, 0.0)
            at16[ch] = jnp.where(causal, kk[ch][c:] * dec, 0.0).astype(BF16)
        for ch in chains:
            x[ch] = -jnp.where(lower_left[0], low[ch], 0.0)
        for lvl in range(1, n_sq):
            e = {}
            for ch in chains:
                cm = jnp.where(lower_left[lvl], low[ch], 0.0)
                e[ch] = cm + _mm(x[ch], cm)
            for ch in chains:
                x[ch] = x[ch] - e[ch] - _mm(e[ch], x[ch])
        wu, egc = {}, {}
        for ch in chains:
            egc[ch] = jnp.exp(cum_r[ch])
            rhs = jnp.concatenate([kb[ch] * egc[ch], rhs0[ch]], axis=1)
            wu[ch] = (rhs + _mm(x[ch], rhs)).astype(BF16)
        for ch in chains:
            g, j = ch
            sl = slice(j * GDN_DK, (j + 1) * GDN_DK)
            glast = cum_r[ch][c - 1:c, :]
            kdt16 = (kt[ch] * jnp.exp(glast - cum_c[ch][0:1, :])).astype(BF16)
            r = _dot(jnp.concatenate([kdt16, at16[ch]], axis=0), wu[ch])
            bms[rows[g], sl] = (-r[:c, :GDN_DV]).astype(BF16)
            ns[rows[g], sl] = r[:c, GDN_DV:]
            qms[rows[g], sl] = (q[ch] * egc[ch] - r[c:, :GDN_DV]).astype(BF16)
            os_[rows[g], sl] = r[c:, GDN_DV:]
            egl[pl.ds(pl.multiple_of((n * grp + g) * 8, 8), 8), sl] = jnp.broadcast_to(jnp.exp(glast), (8, LANES))
        return carry

    lax.fori_loop(0, seq // (grp * c), phase1, 0)

    s_scr[...] = jnp.zeros_like(s_scr)

    def phase2(n, carry):
        rows = pl.ds(pl.multiple_of(n * c, c), c)
        for j in range(GDN_HPS):
            sl = slice(j * GDN_DK, (j + 1) * GDN_DK)
            s = s_scr[j]
            s16 = s.astype(BF16)
            eg = egl[pl.ds(pl.multiple_of(n * 8, 8), 8), sl][0:1, :]
            s_scr[j] = s * eg + _dot(bms[rows, sl], s16) + ns[rows, sl]
            o = _dot(qms[rows, sl], s16) + os_[rows, sl]
            on = _rms(o, gn_ref[...]) * _silu(z_ref[rows, sl].astype(F32))
            o_ref[rows, sl] = on.astype(o_ref.dtype)
        return carry

    lax.fori_loop(0, seq // c, phase2, 0)


def _gdn(pg, pb, ps, gnorm, bsz, seq):
    t = bsz * seq
    nb = seq // GDN_CHUNK
    wd = GDN_HPS * GDN_DK
    ng = GDN_HEADS // GDN_HPS
    return pl.pallas_call(
        _gdn_kernel,
        grid=(bsz, ng),
        in_specs=[
            pl.BlockSpec((seq, wd), lambda b, h: (b, h)),
            pl.BlockSpec((seq, wd), lambda b, h: (b, ng + h)),
            pl.BlockSpec((seq, wd), lambda b, h: (b, 2 * ng + h)),
            pl.BlockSpec((seq, LANES), lambda b, h: (b, 0)),
            pl.BlockSpec((seq, wd), lambda b, h: (b, _PB_Z // wd + h)),
            pl.BlockSpec((1, GDN_DV), lambda b, h: (0, 0)),
        ],
        out_specs=pl.BlockSpec((seq, wd), lambda b, h: (b, h)),
        out_shape=jax.ShapeDtypeStruct((t, GDN_V), BF16),
        scratch_shapes=[
            pltpu.VMEM((seq, wd), BF16), pltpu.VMEM((seq, wd), F32), pltpu.VMEM((seq, wd), BF16),
            pltpu.VMEM((seq, wd), F32), pltpu.VMEM((nb * 8, wd), F32),
            pltpu.VMEM((GDN_HPS, GDN_DK, GDN_DV), F32),
        ],
        compiler_params=_params(("parallel", "parallel")),
        name="gdn",
    )(pg, pg, pg, ps, pb, gnorm)


def _xa_kernel(q_ref, mk_ref, mv_ref, o_ref):
    s = _dot_nt(q_ref[...], mk_ref[...]) * (XA_DH ** -0.5)
    p = jnp.exp(s - jnp.max(s, axis=-1, keepdims=True))
    den = jnp.sum(p, axis=-1, keepdims=True)
    o_ref[...] = (_dot(p.astype(BF16), mv_ref[...]) / den).astype(o_ref.dtype)


def _xa(pb, mkv, bsz, seq, ts=1024):
    t = bsz * seq
    nt = seq // ts
    return pl.pallas_call(
        _xa_kernel,
        grid=(bsz, XA_HEADS, nt),
        in_specs=[
            pl.BlockSpec((ts, XA_DH), lambda b, h, i: (b * nt + i, _PB_XQ // XA_DH + h)),
            pl.BlockSpec((MEM_LEN, XA_DH), lambda b, h, i: (b, h)),
            pl.BlockSpec((MEM_LEN, XA_DH), lambda b, h, i: (b, XA_HEADS + h)),
        ],
        out_specs=pl.BlockSpec((ts, XA_DH), lambda b, h, i: (b * nt + i, h)),
        out_shape=jax.ShapeDtypeStruct((t, XA_W), BF16),
        compiler_params=_params(("parallel", "parallel", "parallel")),
        name="xattn",
    )(pb, mkv, mkv)


def _merge_kernel(h_ref, oa_ref, ob_ref, oc_ref, ga_ref, gb_ref, gc_ref, wa_ref, wb_ref, wc_ref, wo_ref,
                  gpost_ref, o_ref):
    mixed = jax.nn.sigmoid(ga_ref[...].astype(F32)) * _dot(oa_ref[...], wa_ref[...])
    mixed = mixed + jax.nn.sigmoid(gb_ref[...].astype(F32)) * _dot(ob_ref[...], wb_ref[...])
    mixed = mixed + jax.nn.sigmoid(gc_ref[...].astype(F32)) * _dot(oc_ref[...], wc_ref[...])
    m = _dot(mixed.astype(BF16), wo_ref[...])
    o_ref[...] = h_ref[...] + _rms(m, gpost_ref[...])


def _merge(h, oa, ob, oc, pb, wa, wb, wc, wo, gpost, tm=256):
    t, d = h.shape
    row = lambda i: (i, 0)
    const = lambda i: (0, 0)
    once = pl.Buffered(1)
    return pl.pallas_call(
        _merge_kernel,
        grid=(t // tm,),
        in_specs=[
            pl.BlockSpec((tm, d), row),
            pl.BlockSpec((tm, GLA_V), row), pl.BlockSpec((tm, GDN_V), row), pl.BlockSpec((tm, XA_W), row),
            pl.BlockSpec((tm, d), lambda i: (i, 0)), pl.BlockSpec((tm, d), lambda i: (i, 1)),
            pl.BlockSpec((tm, d), lambda i: (i, 2)),
            pl.BlockSpec((GLA_V, d), const, pipeline_mode=once), pl.BlockSpec((GDN_V, d), const, pipeline_mode=once),
            pl.BlockSpec((XA_W, d), const, pipeline_mode=once), pl.BlockSpec((d, d), const, pipeline_mode=once),
            pl.BlockSpec((1, d), const),
        ],
        out_specs=pl.BlockSpec((tm, d), row),
        out_shape=jax.ShapeDtypeStruct((t, d), F32),
        compiler_params=_params(("parallel",)),
        name="merge",
    )(h, oa, ob, oc, pb, pb, pb, wa, wb, wc, wo, gpost)


_PA_GDN = 2 * GLA_QK + GLA_V
_PA_COLS = _PA_GDN + 2 * GDN_QK + GDN_V
_PB_R = N_BRANCH * D_MODEL
_PB_Z = _PB_R + GLA_V
_PB_XQ = _PB_Z + GDN_V
_PB_COLS = _PB_XQ + XA_W


_W_ORDER = (0, 1, 2, 5, 10, 4, 8, 9)
_W_SMALL = ((_IN_OFF[3], GLA_RANK), (_IN_OFF[6], 2 * GDN_HEADS))
_W_ALL_COLS = _PA_COLS + _PB_COLS + LANES
_W_RB = 512


def _w_block_rows():
    tab = []
    for i in _W_ORDER:
        assert (_IN_OFF[i + 1] - _IN_OFF[i]) % _W_RB == 0
        tab.extend(range(_IN_OFF[i], _IN_OFF[i + 1], _W_RB))
    return np.asarray(tab, np.int32)


def _relayout_kernel(tab_ref, wt_hbm, o_ref, buf, sem):
    j = pl.program_id(0)
    n_main = pl.num_programs(0) - 1

    @pl.when(j < n_main)
    def _():
        start = pl.multiple_of(tab_ref[j], 16)
        cp = pltpu.make_async_copy(wt_hbm.at[pl.ds(start, _W_RB), :], buf, sem.at[0])
        cp.start()
        cp.wait()
        o_ref[...] = buf[...].astype(BF16)

    @pl.when(j == n_main)
    def _():
        dst = 0
        copies = []
        for n, (row0, rows) in enumerate(_W_SMALL):
            copies.append(pltpu.make_async_copy(wt_hbm.at[pl.ds(row0, rows), :], buf.at[pl.ds(dst, rows), :],
                                                sem.at[n]))
            dst += rows
        for cp in copies:
            cp.start()
        for cp in copies:
            cp.wait()
        o_ref[0:dst, :] = buf[0:dst, :].astype(BF16)
        o_ref[dst:, :] = jnp.zeros((_W_RB - dst, o_ref.shape[1]), BF16)


def _split_w_in(w_in_t):
    n, k = w_in_t.shape
    tab = _w_block_rows()
    return pl.pallas_call(
        _relayout_kernel,
        grid_spec=pltpu.PrefetchScalarGridSpec(
            num_scalar_prefetch=1,
            grid=(len(tab) + 1,),
            in_specs=[pl.BlockSpec(memory_space=pl.ANY)],
            out_specs=pl.BlockSpec((_W_RB, k), lambda i, tab: (i, 0)),
            scratch_shapes=[pltpu.VMEM((_W_RB, k), F32), pltpu.SemaphoreType.DMA((len(_W_SMALL),))],
        ),
        out_shape=jax.ShapeDtypeStruct((_W_ALL_COLS, k), BF16),
        compiler_params=_params(("arbitrary",)),
        name="w_in_relayout",
    )(jnp.asarray(tab), w_in_t)


def kernel(x, mem, n_ffn1_pre, w_ffn1_gu, w_ffn1_down, n_ffn1_post, n_mix_pre, w_in, gla_w_lr2, gla_b_lr,
           gla_norm, gdn_conv, gdn_a_log, gdn_dt_bias, gdn_norm, mem_norm, w_mem_kv, w_up_gla, w_up_gdn,
           w_up_xa, w_out, n_mix_post, n_ffn2_pre, w_ffn2_gu, w_ffn2_down, n_ffn2_post):
    bsz, seq, d = x.shape
    t = bsz * seq
    h = x.reshape(t, d)
    for l in range(n_ffn1_pre.shape[0]):
        row = lambda a: a[l][None, :]
        h, u = _ffn(h, row(n_ffn1_pre), w_ffn1_gu[l], w_ffn1_down[l],
                    row(n_ffn1_post), row(n_mix_pre), emit_next=True)

        w_all = _split_w_in(w_in[l].T)
        par = jnp.zeros((2, LANES), F32)
        par = par.at[0, SM_A:SM_A + GDN_HEADS].set(gdn_a_log[l]).at[1, SM_A:SM_A + GDN_HEADS].set(gdn_dt_bias[l])
        tn_g = 512
        pa = _matmul(u, w_all, 0, _PA_GDN, F32, 2048, 512, "in_proj_a")
        pg = _matmul(u, w_all, _PA_GDN, _PA_COLS - _PA_GDN, F32, seq, tn_g, "in_proj_g", body=_proj_conv_kernel,
                     extra=(gdn_conv[l],), extra_specs=(pl.BlockSpec((CONV_W, tn_g), lambda i, j: (0, j)),))
        pb = _matmul(u, w_all, _PA_COLS, _PB_COLS, BF16, 2048, 1024, "in_proj_b")
        ps = _matmul(u, w_all, _PA_COLS + _PB_COLS, LANES, F32, 2048, LANES, "in_proj_s", body=_proj_s_kernel,
                     extra=(par,), extra_specs=(pl.BlockSpec((2, LANES), lambda i, j: (0, 0)),))

        wlr = jnp.pad(gla_w_lr2[l], ((0, LANES - GLA_RANK), (0, 0))).astype(BF16)
        o_gla = _gla(pa, pb, ps, wlr, row(gla_b_lr), row(gla_norm), bsz, seq)
        o_gdn = _gdn(pg, pb, ps, row(gdn_norm), bsz, seq)

        mkv = _norm_matmul(mem.reshape(bsz * MEM_LEN, d), row(mem_norm), w_mem_kv[l], BF16,
                           bsz * MEM_LEN, 512, "mem_kv")
        o_xa = _xa(pb, mkv, bsz, seq)

        h = _merge(h, o_gla, o_gdn, o_xa, pb, w_up_gla[l].astype(BF16), w_up_gdn[l].astype(BF16),
                   w_up_xa[l].astype(BF16), w_out[l].astype(BF16), row(n_mix_post))
        h = _ffn(h, row(n_ffn2_pre), w_ffn2_gu[l], w_ffn2_down[l],
                 row(n_ffn2_post), row(n_ffn2_post), emit_next=False)
    return h.reshape(bsz, seq, d)
```

```python
import functools

import numpy as np
import jax
import jax.numpy as jnp
from jax import lax
from jax.experimental import pallas as pl
from jax.experimental.pallas import tpu as pltpu

F32 = jnp.float32
BF16 = jnp.bfloat16

D_MODEL = 2048
MEM_LEN = 256
EPS = 1e-6
GLA_HEADS = 4
GLA_DK = 128
GLA_DV = 256
GLA_QK = GLA_HEADS * GLA_DK
GLA_V = GLA_HEADS * GLA_DV
GLA_RANK = 16
GLA_TAU = 16.0
GDN_HEADS = 8
GDN_DK = 128
GDN_DV = 128
GDN_QK = GDN_HEADS * GDN_DK
GDN_V = GDN_HEADS * GDN_DV
CONV_W = 4
XA_HEADS = 4
XA_DH = 256
XA_W = XA_HEADS * XA_DH
N_BRANCH = 3
D_FF = 5632

LANES = 128
GLA_CHUNK = 64
GLA_GROUP = 4
GDN_CHUNK = 128
GDN_HPS = 2
GDN_GROUP = 4
VMEM_LIMIT = 56 * 1024 * 1024

_IN_SIZES = (GLA_QK, GLA_QK, GLA_V, GLA_RANK, GLA_V, 2 * GDN_QK + GDN_V, GDN_HEADS, GDN_HEADS, GDN_V,
             XA_W, N_BRANCH * D_MODEL)
_IN_OFF = tuple(int(v) for v in np.cumsum((0,) + _IN_SIZES))
SM_LR, SM_B, SM_A = 0, GLA_RANK, GLA_RANK + GDN_HEADS


def _dot(a, b):
    return jnp.dot(a, b, preferred_element_type=F32)


def _dot_nt(a, b):
    return lax.dot_general(a, b, (((1,), (1,)), ((), ())), preferred_element_type=F32)


def _dot_tn(a, b):
    return lax.dot_general(a, b, (((0,), (0,)), ((), ())), preferred_element_type=F32)


def _mm(a, b):
    return _dot(a.astype(BF16), b.astype(BF16))


def _split2(x):
    hi = x.astype(BF16)
    mid = (x - hi.astype(F32)).astype(BF16)
    return hi, mid


def _dot01(m01, x):
    hi, mid = _split2(x)
    return _dot(m01, hi) + _dot(m01, mid)


def _rms(x, gain):
    return x * lax.rsqrt(jnp.mean(x * x, axis=-1, keepdims=True) + EPS) * gain


def _silu(x):
    return x * jax.nn.sigmoid(x)


def _softplus(x):
    return jnp.maximum(x, 0.0) + jnp.log1p(jnp.exp(-jnp.abs(x)))


def _params(sem):
    return pltpu.CompilerParams(dimension_semantics=sem, vmem_limit_bytes=VMEM_LIMIT)


def _ffn_kernel(x_ref, gpre_ref, wg_ref, wu_ref, wd_ref, gpost_ref, gnext_ref, *out_and_scratch, emit_next):
    if emit_next:
        h_ref, un_ref, xn_scr, wg16, wu16, wd16 = out_and_scratch
    else:
        h_ref, xn_scr, wg16, wu16, wd16 = out_and_scratch
    j = pl.program_id(1)

    @pl.when(j == 0)
    def _():
        xn_scr[...] = _rms(x_ref[...], gpre_ref[...]).astype(BF16)
        h_ref[...] = jnp.zeros_like(h_ref)

    wg16[...] = wg_ref[...].astype(BF16)
    wu16[...] = wu_ref[...].astype(BF16)
    wd16[...] = wd_ref[...].astype(BF16)
    xn = xn_scr[...]
    g = _dot(xn, wg16[...])
    u = _dot(xn, wu16[...])
    hm = (_silu(g) * u).astype(BF16)
    h_ref[...] += _dot(hm, wd16[...])

    @pl.when(j == pl.num_programs(1) - 1)
    def _():
        h = x_ref[...] + 0.5 * _rms(h_ref[...], gpost_ref[...])
        h_ref[...] = h
        if emit_next:
            un_ref[...] = _rms(h, gnext_ref[...]).astype(BF16)


def _ffn(x, gpre, w_gu, w_down, gpost, gnext, emit_next, tm=1024, tf=256):
    t, d = x.shape
    nf = D_FF // tf
    row = lambda i, j: (i, 0)
    const = lambda i, j: (0, 0)
    once = pl.Buffered(1)
    out_shape = [jax.ShapeDtypeStruct((t, d), F32)]
    out_specs = [pl.BlockSpec((tm, d), row, pipeline_mode=once)]
    if emit_next:
        out_shape.append(jax.ShapeDtypeStruct((t, d), BF16))
        out_specs.append(pl.BlockSpec((tm, d), row, pipeline_mode=once))
    res = pl.pallas_call(
        functools.partial(_ffn_kernel, emit_next=emit_next),
        grid=(t // tm, nf),
        in_specs=[
            pl.BlockSpec((tm, d), row, pipeline_mode=once),
            pl.BlockSpec((1, d), const),
            pl.BlockSpec((d, tf), lambda i, j: (0, j)),
            pl.BlockSpec((d, tf), lambda i, j: (0, nf + j)),
            pl.BlockSpec((tf, d), lambda i, j: (j, 0)),
            pl.BlockSpec((1, d), const),
            pl.BlockSpec((1, d), const),
        ],
        out_specs=out_specs,
        out_shape=out_shape,
        scratch_shapes=[pltpu.VMEM((tm, d), BF16), pltpu.VMEM((d, tf), BF16), pltpu.VMEM((d, tf), BF16),
                        pltpu.VMEM((tf, d), BF16)],
        compiler_params=_params(("parallel", "arbitrary")),
        name="ffn",
    )(x, gpre, w_gu, w_gu, w_down, gpost, gnext)
    return res if emit_next else res[0]


def _matmul_kernel(a_ref, w_ref, o_ref):
    o_ref[...] = _dot_nt(a_ref[...], w_ref[...]).astype(o_ref.dtype)


def _proj_conv_kernel(a_ref, w_ref, cw_ref, o_ref):
    tm, tn = o_ref.shape
    rb = 256
    kind = pl.program_id(1) // (GDN_QK // tn)
    scale = jnp.where(kind == 0, GDN_DK ** -0.5, 1.0)
    w = cw_ref[...]
    r8 = lax.broadcasted_iota(jnp.int32, (8, tn), 0)
    prev = jnp.zeros((8, tn), F32)
    for r in range(0, tm, rb):
        y = _dot_nt(a_ref[r:r + rb, :], w_ref[...])
        acc = y * w[CONV_W - 1:CONV_W, :]
        for sft in range(1, CONV_W):
            xs = pltpu.roll(y, sft, 0)
            top = jnp.where(r8 < sft, pltpu.roll(prev, sft, 0), xs[0:8])
            acc = acc + jnp.concatenate([top, xs[8:]], axis=0) * w[CONV_W - 1 - sft:CONV_W - sft, :]
        prev = y[rb - 8:rb]
        c = _silu(acc)
        for hh in range(tn // GDN_DK):
            sl = slice(hh * GDN_DK, (hh + 1) * GDN_DK)
            blk = c[:, sl]
            f = lax.rsqrt(jnp.sum(blk * blk, axis=-1, keepdims=True) + EPS) * scale
            o_ref[r:r + rb, sl] = blk * jnp.where(kind == 2, 1.0, f)


def _proj_s_kernel(a_ref, w_ref, par_ref, o_ref):
    y = _dot_nt(a_ref[...], w_ref[...])
    par = par_ref[...]
    lane = lax.broadcasted_iota(jnp.int32, y.shape, 1)
    g = -jnp.exp(par[0:1, :]) * _softplus(y + par[1:2, :])
    is_g = (lane >= SM_A) & (lane < SM_A + GDN_HEADS)
    is_b = (lane >= SM_B) & (lane < SM_B + GDN_HEADS)
    o_ref[...] = jnp.where(is_g, g, jnp.where(is_b, jax.nn.sigmoid(y), y))


def _matmul(a, w, col0, n, out_dtype, tm, tn, name, body=_matmul_kernel, extra=(), extra_specs=()):
    m, k = a.shape
    c0 = col0 // tn
    return pl.pallas_call(
        body,
        grid=(m // tm, n // tn),
        in_specs=[pl.BlockSpec((tm, k), lambda i, j: (i, 0)), pl.BlockSpec((tn, k), lambda i, j: (c0 + j, 0)),
                  *extra_specs],
        out_specs=pl.BlockSpec((tm, tn), lambda i, j: (i, j)),
        out_shape=jax.ShapeDtypeStruct((m, n), out_dtype),
        compiler_params=_params(("parallel", "parallel")),
        name=name,
    )(a, w, *extra)


def _norm_matmul_kernel(x_ref, g_ref, w_ref, o_ref, xn_scr):
    @pl.when(pl.program_id(1) == 0)
    def _():
        xn_scr[...] = _rms(x_ref[...], g_ref[...]).astype(BF16)

    o_ref[...] = _dot(xn_scr[...], w_ref[...].astype(BF16)).astype(o_ref.dtype)


def _norm_matmul(x, gain, w, out_dtype, tm, tn, name):
    m, k = x.shape
    n = w.shape[1]
    return pl.pallas_call(
        _norm_matmul_kernel,
        grid=(m // tm, n // tn),
        in_specs=[pl.BlockSpec((tm, k), lambda i, j: (i, 0)), pl.BlockSpec((1, k), lambda i, j: (0, 0)),
                  pl.BlockSpec((k, tn), lambda i, j: (0, j))],
        out_specs=pl.BlockSpec((tm, tn), lambda i, j: (i, j)),
        out_shape=jax.ShapeDtypeStruct((m, n), out_dtype),
        scratch_shapes=[pltpu.VMEM((tm, k), BF16)],
        compiler_params=_params(("parallel", "arbitrary")),
        name=name,
    )(x, gain, w)


def _gla_levels(chunk):
    return [chunk >> (i + 1) for i in range(int(np.log2(chunk)))]


def _gla_exponent_matrix(chunk):
    c = chunk
    i = np.arange(c)[:, None]
    t = np.arange(c)[None, :]
    blocks = [(t <= i), (t > i)]
    for h in _gla_levels(c):
        r = (i // (2 * h)) * (2 * h) + h
        upper = i >= r
        blocks.append(np.where(upper, (t > r) & (t <= i), (t > i) & (t <= r)))
    m = np.concatenate(blocks, axis=0).astype(np.float32)
    z = np.zeros_like(m)
    return np.concatenate([np.concatenate([m, m, z, z], axis=1), np.concatenate([z, z, m, m], axis=1)], axis=0)


def _gla_kernel(q_ref, k_ref, v_ref, sm_ref, r_ref, wlr_ref, blr_ref, gn_ref, mst_ref, o_ref, s_scr, e0_scr,
                e1_scr):
    c = GLA_CHUNK
    seq = q_ref.shape[0]
    s_scr[...] = jnp.zeros_like(s_scr)
    ri = lax.broadcasted_iota(jnp.int32, (c, c), 0)
    ci = lax.broadcasted_iota(jnp.int32, (c, c), 1)
    rowi = lax.broadcasted_iota(jnp.int32, (c, GLA_DK), 0)
    levels = _gla_levels(c)
    assert GLA_GROUP == 4

    grp = GLA_GROUP
    gs = range(grp)

    n_trips = seq // (grp * c)

    def decays(n, e_ref):
        allrows = pl.ds(pl.multiple_of(n * (grp * c), grp * c), grp * c)
        x = _dot(sm_ref[allrows, :].astype(BF16), wlr_ref[...]) + blr_ref[...]
        lg = (jnp.minimum(x, 0.0) - jnp.log1p(jnp.exp(-jnp.abs(x)))) * (1.0 / GLA_TAU)
        hi, mid = _split2(lg)
        hm = [jnp.concatenate([hi[g * c:(g + 1) * c], mid[g * c:(g + 1) * c]], axis=0) for g in gs]
        rhs = jnp.concatenate([jnp.concatenate([hm[0], hm[1]], axis=1),
                               jnp.concatenate([hm[2], hm[3]], axis=1)], axis=0)
        e_ref[...] = jnp.exp(_dot(mst_ref[...], rhs))

    def trip(n, e_ref):
        row0 = pl.multiple_of(n * (grp * c), grp * c)
        rows = [pl.ds(row0 + g * c, c) for g in gs]
        ex = e_ref[...]
        nr = ex.shape[0] // 2
        e = [ex[0:nr, 0:GLA_DK], ex[0:nr, GLA_DK:], ex[nr:, 0:GLA_DK], ex[nr:, GLA_DK:]]
        q = [q_ref[rows[g], :] * (GLA_DK ** -0.5) for g in gs]
        k = [k_ref[rows[g], :] for g in gs]
        v = [v_ref[rows[g], :].astype(BF16) for g in gs]
        attn = [jnp.where(ri == ci, _dot_nt(q[g].astype(BF16), k[g].astype(BF16)), 0.0) for g in gs]
        for lvl, h in enumerate(levels):
            upper = (rowi & h) != 0
            sh = int(np.log2(2 * h))
            same = (ri >> sh) == (ci >> sh)
            for g in gs:
                f = e[g][(2 + lvl) * c:(3 + lvl) * c]
                qt = jnp.where(upper, q[g] * f, 0.0).astype(BF16)
                kt = jnp.where(upper, 0.0, k[g] * f).astype(BF16)
                attn[g] = attn[g] + jnp.where(same, _dot_nt(qt, kt), 0.0)
        vk = [_dot_tn(v[g], (k[g] * e[g][c:2 * c]).astype(BF16)) for g in gs]
        av = [_dot(attn[g].astype(BF16), v[g]) for g in gs]
        st = s_scr[...]
        for g in gs:
            o = _dot_nt((q[g] * e[g][0:c]).astype(BF16), st.astype(BF16)) + av[g]
            st = st * e[g][c - 1:c, :] + vk[g]
            on = _rms(o, gn_ref[...]) * _silu(r_ref[rows[g], :].astype(F32))
            o_ref[rows[g], :] = on.astype(o_ref.dtype)
        s_scr[...] = st

    assert n_trips % 2 == 0
    decays(0, e0_scr)

    def body(m, carry):
        decays(2 * m + 1, e1_scr)
        trip(2 * m, e0_scr)
        decays(jnp.minimum(2 * m + 2, n_trips - 1), e0_scr)
        trip(2 * m + 1, e1_scr)
        return carry

    lax.fori_loop(0, n_trips // 2, body, 0)


def _gla(pa, pb, ps, wlr, blr, gnorm, bsz, seq):
    t = bsz * seq
    mst = jnp.asarray(_gla_exponent_matrix(GLA_CHUNK), BF16)
    nq = GLA_QK // GLA_DK
    return pl.pallas_call(
        _gla_kernel,
        grid=(bsz, GLA_HEADS),
        in_specs=[
            pl.BlockSpec((seq, GLA_DK), lambda b, h: (b, h)),
            pl.BlockSpec((seq, GLA_DK), lambda b, h: (b, nq + h)),
            pl.BlockSpec((seq, GLA_DV), lambda b, h: (b, 2 * GLA_QK // GLA_DV + h)),
            pl.BlockSpec((seq, LANES), lambda b, h: (b, 0)),
            pl.BlockSpec((seq, GLA_DV), lambda b, h: (b, _PB_R // GLA_DV + h)),
            pl.BlockSpec((LANES, GLA_DK), lambda b, h: (0, h)),
            pl.BlockSpec((1, GLA_DK), lambda b, h: (0, h)),
            pl.BlockSpec((1, GLA_DV), lambda b, h: (0, 0)),
            pl.BlockSpec(mst.shape, lambda b, h: (0, 0)),
        ],
        out_specs=pl.BlockSpec((seq, GLA_DV), lambda b, h: (b, h)),
        out_shape=jax.ShapeDtypeStruct((t, GLA_V), BF16),
        scratch_shapes=[pltpu.VMEM((GLA_DV, GLA_DK), F32), pltpu.VMEM((mst.shape[0], 2 * GLA_DK), F32),
                        pltpu.VMEM((mst.shape[0], 2 * GLA_DK), F32)],
        compiler_params=_params(("parallel", "parallel")),
        name="gla",
    )(pa, pa, pa, ps, pb, wlr, blr, gnorm, mst)


def _gdn_kernel(q_ref, k_ref, v_ref, sm_ref, z_ref, gn_ref, o_ref, bms, ns, qms, os_, egl, s_scr):
    c = GDN_CHUNK
    seq = q_ref.shape[0]
    head0 = pl.program_id(1) * GDN_HPS

    ri = lax.broadcasted_iota(jnp.int32, (c, c), 0)
    ci = lax.broadcasted_iota(jnp.int32, (c, c), 1)
    lane = lax.broadcasted_iota(jnp.int32, (c, LANES), 1)
    tril = jnp.where(ri >= ci, 1.0, 0.0).astype(BF16)
    causal = ri >= ci
    n_sq = int(np.log2(c))
    lower_left = [(((ri ^ ci) >> (l + 1)) == 0) & ((ri & (1 << l)) != 0) & ((ci & (1 << l)) == 0) for l in range(n_sq)]

    grp = GDN_GROUP
    chains = [(g, j) for g in range(grp) for j in range(GDN_HPS)]

    def phase1(n, carry):
        rows, q, k, kb, gb, rhs0 = {}, {}, {}, {}, {}, {}
        for g in range(grp):
            rows[g] = pl.ds(pl.multiple_of((n * grp + g) * c, c), c)
            sm = sm_ref[rows[g], :]
            for j in range(GDN_HPS):
                sl = slice(j * GDN_DK, (j + 1) * GDN_DK)
                q[g, j] = q_ref[rows[g], sl]
                k[g, j] = k_ref[rows[g], sl]
                gcol = jnp.sum(jnp.where(lane == SM_A + head0 + j, sm, 0.0), axis=-1, keepdims=True)
                beta = jnp.sum(jnp.where(lane == SM_B + head0 + j, sm, 0.0), axis=-1, keepdims=True)
                gb[g, j] = jnp.broadcast_to(gcol, (c, LANES))
                kb[g, j] = k[g, j] * beta
                rhs0[g, j] = v_ref[rows[g], sl] * beta
        cum_r, cum_c, kt = {}, {}, {}
        for ch in chains:
            hi, mid = _split2(gb[ch])
            r = _dot(tril, jnp.concatenate([hi, mid], axis=1))
            cum_r[ch] = r[:, :LANES] + r[:, LANES:]
        for ch in chains:
            cum_c[ch] = cum_r[ch].T
            kt[ch] = k[ch].T
        kk = {}
        for ch in chains:
            k16 = k[ch].astype(BF16)
            kk[ch] = _dot_nt(jnp.concatenate([kb[ch].astype(BF16), q[ch].astype(BF16)], axis=0), k16)
        low, at16, x, p = {}, {}, {}, {}
        for ch in chains:
            dec = jnp.where(causal, jnp.exp(jnp.where(causal, cum_r[ch] - cum_c[ch], 0.0)), 0.0)
            low[ch] = jnp.where(ri > ci, kk[ch][:c] * dec, 0.0)
            at16[ch] = jnp.where(causal, kk[ch][c:] * dec, 0.0).astype(BF16)
        for ch in chains:
            x[ch] = -jnp.where(lower_left[0], low[ch], 0.0)
        for lvl in range(1, n_sq):
            e = {}
            for ch in chains:
                cm = jnp.where(lower_left[lvl], low[ch], 0.0)
                e[ch] = cm + _mm(x[ch], cm)
            for ch in chains:
                x[ch] = x[ch] - e[ch] - _mm(e[ch], x[ch])
        wu, egc = {}, {}
        for ch in chains:
            egc[ch] = jnp.exp(cum_r[ch])
            rhs = jnp.concatenate([kb[ch] * egc[ch], rhs0[ch]], axis=1)
            wu[ch] = (rhs + _mm(x[ch], rhs)).astype(BF16)
        for ch in chains:
            g, j = ch
            sl = slice(j * GDN_DK, (j + 1) * GDN_DK)
            glast = cum_r[ch][c - 1:c, :]
            kdt16 = (kt[ch] * jnp.exp(glast - cum_c[ch][0:1, :])).astype(BF16)
            r = _dot(jnp.concatenate([kdt16, at16[ch]], axis=0), wu[ch])
            bms[rows[g], sl] = (-r[:c, :GDN_DV]).astype(BF16)
            ns[rows[g], sl] = r[:c, GDN_DV:]
            qms[rows[g], sl] = (q[ch] * egc[ch] - r[c:, :GDN_DV]).astype(BF16)
            os_[rows[g], sl] = r[c:, GDN_DV:]
            egl[pl.ds(pl.multiple_of((n * grp + g) * 8, 8), 8), sl] = jnp.broadcast_to(jnp.exp(glast), (8, LANES))
        return carry

    lax.fori_loop(0, seq // (grp * c), phase1, 0)

    s_scr[...] = jnp.zeros_like(s_scr)

    def phase2(n, carry):
        rows = pl.ds(pl.multiple_of(n * c, c), c)
        for j in range(GDN_HPS):
            sl = slice(j * GDN_DK, (j + 1) * GDN_DK)
            s = s_scr[j]
            s16 = s.astype(BF16)
            eg = egl[pl.ds(pl.multiple_of(n * 8, 8), 8), sl][0:1, :]
            s_scr[j] = s * eg + _dot(bms[rows, sl], s16) + ns[rows, sl]
            o = _dot(qms[rows, sl], s16) + os_[rows, sl]
            on = _rms(o, gn_ref[...]) * _silu(z_ref[rows, sl].astype(F32))
            o_ref[rows, sl] = on.astype(o_ref.dtype)
        return carry

    lax.fori_loop(0, seq // c, phase2, 0)


def _gdn(pg, pb, ps, gnorm, bsz, seq):
    t = bsz * seq
    nb = seq // GDN_CHUNK
    wd = GDN_HPS * GDN_DK
    ng = GDN_HEADS // GDN_HPS
    return pl.pallas_call(
        _gdn_kernel,
        grid=(bsz, ng),
        in_specs=[
            pl.BlockSpec((seq, wd), lambda b, h: (b, h)),
            pl.BlockSpec((seq, wd), lambda b, h: (b, ng + h)),
            pl.BlockSpec((seq, wd), lambda b, h: (b, 2 * ng + h)),
            pl.BlockSpec((seq, LANES), lambda b, h: (b, 0)),
            pl.BlockSpec((seq, wd), lambda b, h: (b, _PB_Z // wd + h)),
            pl.BlockSpec((1, GDN_DV), lambda b, h: (0, 0)),
        ],
        out_specs=pl.BlockSpec((seq, wd), lambda b, h: (b, h)),
        out_shape=jax.ShapeDtypeStruct((t, GDN_V), BF16),
        scratch_shapes=[
            pltpu.VMEM((seq, wd), BF16), pltpu.VMEM((seq, wd), F32), pltpu.VMEM((seq, wd), BF16),
            pltpu.VMEM((seq, wd), F32), pltpu.VMEM((nb * 8, wd), F32),
            pltpu.VMEM((GDN_HPS, GDN_DK, GDN_DV), F32),
        ],
        compiler_params=_params(("parallel", "parallel")),
        name="gdn",
    )(pg, pg, pg, ps, pb, gnorm)


def _xa_kernel(q_ref, mk_ref, mv_ref, o_ref):
    s = _dot_nt(q_ref[...], mk_ref[...]) * (XA_DH ** -0.5)
    p = jnp.exp(s - jnp.max(s, axis=-1, keepdims=True))
    den = jnp.sum(p, axis=-1, keepdims=True)
    o_ref[...] = (_dot(p.astype(BF16), mv_ref[...]) / den).astype(o_ref.dtype)


def _xa(pb, mkv, bsz, seq, ts=1024):
    t = bsz * seq
    nt = seq // ts
    return pl.pallas_call(
        _xa_kernel,
        grid=(bsz, XA_HEADS, nt),
        in_specs=[
            pl.BlockSpec((ts, XA_DH), lambda b, h, i: (b * nt + i, _PB_XQ // XA_DH + h)),
            pl.BlockSpec((MEM_LEN, XA_DH), lambda b, h, i: (b, h)),
            pl.BlockSpec((MEM_LEN, XA_DH), lambda b, h, i: (b, XA_HEADS + h)),
        ],
        out_specs=pl.BlockSpec((ts, XA_DH), lambda b, h, i: (b * nt + i, h)),
        out_shape=jax.ShapeDtypeStruct((t, XA_W), BF16),
        compiler_params=_params(("parallel", "parallel", "parallel")),
        name="xattn",
    )(pb, mkv, mkv)


def _merge_kernel(h_ref, oa_ref, ob_ref, oc_ref, ga_ref, gb_ref, gc_ref, wa_ref, wb_ref, wc_ref, wo_ref,
                  gpost_ref, o_ref):
    mixed = jax.nn.sigmoid(ga_ref[...].astype(F32)) * _dot(oa_ref[...], wa_ref[...])
    mixed = mixed + jax.nn.sigmoid(gb_ref[...].astype(F32)) * _dot(ob_ref[...], wb_ref[...])
    mixed = mixed + jax.nn.sigmoid(gc_ref[...].astype(F32)) * _dot(oc_ref[...], wc_ref[...])
    m = _dot(mixed.astype(BF16), wo_ref[...])
    o_ref[...] = h_ref[...] + _rms(m, gpost_ref[...])


def _merge(h, oa, ob, oc, pb, wa, wb, wc, wo, gpost, tm=256):
    t, d = h.shape
    row = lambda i: (i, 0)
    const = lambda i: (0, 0)
    once = pl.Buffered(1)
    return pl.pallas_call(
        _merge_kernel,
        grid=(t // tm,),
        in_specs=[
            pl.BlockSpec((tm, d), row),
            pl.BlockSpec((tm, GLA_V), row), pl.BlockSpec((tm, GDN_V), row), pl.BlockSpec((tm, XA_W), row),
            pl.BlockSpec((tm, d), lambda i: (i, 0)), pl.BlockSpec((tm, d), lambda i: (i, 1)),
            pl.BlockSpec((tm, d), lambda i: (i, 2)),
            pl.BlockSpec((GLA_V, d), const, pipeline_mode=once), pl.BlockSpec((GDN_V, d), const, pipeline_mode=once),
            pl.BlockSpec((XA_W, d), const, pipeline_mode=once), pl.BlockSpec((d, d), const, pipeline_mode=once),
            pl.BlockSpec((1, d), const),
        ],
        out_specs=pl.BlockSpec((tm, d), row),
        out_shape=jax.ShapeDtypeStruct((t, d), F32),
        compiler_params=_params(("parallel",)),
        name="merge",
    )(h, oa, ob, oc, pb, pb, pb, wa, wb, wc, wo, gpost)


_PA_GDN = 2 * GLA_QK + GLA_V
_PA_COLS = _PA_GDN + 2 * GDN_QK + GDN_V
_PB_R = N_BRANCH * D_MODEL
_PB_Z = _PB_R + GLA_V
_PB_XQ = _PB_Z + GDN_V
_PB_COLS = _PB_XQ + XA_W


_W_ORDER = (0, 1, 2, 5, 10, 4, 8, 9)
_W_SMALL = ((_IN_OFF[3], GLA_RANK), (_IN_OFF[6], 2 * GDN_HEADS))
_W_ALL_COLS = _PA_COLS + _PB_COLS + LANES
_W_RB = 512


def _w_block_rows():
    tab = []
    for i in _W_ORDER:
        assert (_IN_OFF[i + 1] - _IN_OFF[i]) % _W_RB == 0
        tab.extend(range(_IN_OFF[i], _IN_OFF[i + 1], _W_RB))
    return np.asarray(tab, np.int32)


def _relayout_kernel(tab_ref, wt_hbm, o_ref, buf, sem):
    j = pl.program_id(0)
    n_main = pl.num_programs(0) - 1

    def block_copy(blk):
        start = pl.multiple_of(tab_ref[blk], 16)
        return pltpu.make_async_copy(wt_hbm.at[pl.ds(start, _W_RB), :], buf.at[blk % 2], sem.at[blk % 2])

    @pl.when(j == 0)
    def _():
        block_copy(j).start()

    @pl.when(j + 1 < n_main)
    def _():
        block_copy(j + 1).start()

    @pl.when(j < n_main)
    def _():
        block_copy(j).wait()
        o_ref[...] = buf[j % 2].astype(BF16)

    @pl.when(j == n_main)
    def _():
        dst = 0
        copies = []
        for n, (row0, rows) in enumerate(_W_SMALL):
            copies.append(pltpu.make_async_copy(wt_hbm.at[pl.ds(row0, rows), :], buf.at[0, pl.ds(dst, rows), :],
                                                sem.at[n]))
            dst += rows
        for cp in copies:
            cp.start()
        for cp in copies:
            cp.wait()
        o_ref[0:dst, :] = buf[0, 0:dst, :].astype(BF16)
        o_ref[dst:, :] = jnp.zeros((_W_RB - dst, o_ref.shape[1]), BF16)


def _split_w_in(w_in_t):
    n, k = w_in_t.shape
    tab = _w_block_rows()
    return pl.pallas_call(
        _relayout_kernel,
        grid_spec=pltpu.PrefetchScalarGridSpec(
            num_scalar_prefetch=1,
            grid=(len(tab) + 1,),
            in_specs=[pl.BlockSpec(memory_space=pl.ANY)],
            out_specs=pl.BlockSpec((_W_RB, k), lambda i, tab: (i, 0)),
            scratch_shapes=[pltpu.VMEM((2, _W_RB, k), F32), pltpu.SemaphoreType.DMA((2,))],
        ),
        out_shape=jax.ShapeDtypeStruct((_W_ALL_COLS, k), BF16),
        compiler_params=_params(("arbitrary",)),
        name="w_in_relayout",
    )(jnp.asarray(tab), w_in_t)


def kernel(x, mem, n_ffn1_pre, w_ffn1_gu, w_ffn1_down, n_ffn1_post, n_mix_pre, w_in, gla_w_lr2, gla_b_lr,
           gla_norm, gdn_conv, gdn_a_log, gdn_dt_bias, gdn_norm, mem_norm, w_mem_kv, w_up_gla, w_up_gdn,
           w_up_xa, w_out, n_mix_post, n_ffn2_pre, w_ffn2_gu, w_ffn2_down, n_ffn2_post):
    bsz, seq, d = x.shape
    t = bsz * seq
    h = x.reshape(t, d)
    for l in range(n_ffn1_pre.shape[0]):
        row = lambda a: a[l][None, :]
        h, u = _ffn(h, row(n_ffn1_pre), w_ffn1_gu[l], w_ffn1_down[l],
                    row(n_ffn1_post), row(n_mix_pre), emit_next=True)

        w_all = _split_w_in(w_in[l].T)
        par = jnp.zeros((2, LANES), F32)
        par = par.at[0, SM_A:SM_A + GDN_HEADS].set(gdn_a_log[l]).at[1, SM_A:SM_A + GDN_HEADS].set(gdn_dt_bias[l])
        tn_g = 512
        pa = _matmul(u, w_all, 0, _PA_GDN, F32, 2048, 512, "in_proj_a")
        pg = _matmul(u, w_all, _PA_GDN, _PA_COLS - _PA_GDN, F32, seq, tn_g, "in_proj_g", body=_proj_conv_kernel,
                     extra=(gdn_conv[l],), extra_specs=(pl.BlockSpec((CONV_W, tn_g), lambda i, j: (0, j)),))
        pb = _matmul(u, w_all, _PA_COLS, _PB_COLS, BF16, 2048, 1024, "in_proj_b")
        ps = _matmul(u, w_all, _PA_COLS + _PB_COLS, LANES, F32, 2048, LANES, "in_proj_s", body=_proj_s_kernel,
                     extra=(par,), extra_specs=(pl.BlockSpec((2, LANES), lambda i, j: (0, 0)),))

        wlr = jnp.pad(gla_w_lr2[l], ((0, LANES - GLA_RANK), (0, 0))).astype(BF16)
        o_gla = _gla(pa, pb, ps, wlr, row(gla_b_lr), row(gla_norm), bsz, seq)
        o_gdn = _gdn(pg, pb, ps, row(gdn_norm), bsz, seq)

        mkv = _norm_matmul(mem.reshape(bsz * MEM_LEN, d), row(mem_norm), w_mem_kv[l], BF16,
                           bsz * MEM_LEN, 512, "mem_kv")
        o_xa = _xa(pb, mkv, bsz, seq)

        h = _merge(h, o_gla, o_gdn, o_xa, pb, w_up_gla[l].astype(BF16), w_up_gdn[l].astype(BF16),
                   w_up_xa[l].astype(BF16), w_out[l].astype(BF16), row(n_mix_post))
        h = _ffn(h, row(n_ffn2_pre), w_ffn2_gu[l], w_ffn2_down[l],
                 row(n_ffn2_post), row(n_ffn2_post), emit_next=False)
    return h.reshape(bsz, seq, d)
```

```python
import functools

import numpy as np
import jax
import jax.numpy as jnp
from jax import lax
from jax.experimental import pallas as pl
from jax.experimental.pallas import tpu as pltpu

F32 = jnp.float32
BF16 = jnp.bfloat16

D_MODEL = 2048
MEM_LEN = 256
EPS = 1e-6
GLA_HEADS = 4
GLA_DK = 128
GLA_DV = 256
GLA_QK = GLA_HEADS * GLA_DK
GLA_V = GLA_HEADS * GLA_DV
GLA_RANK = 16
GLA_TAU = 16.0
GDN_HEADS = 8
GDN_DK = 128
GDN_DV = 128
GDN_QK = GDN_HEADS * GDN_DK
GDN_V = GDN_HEADS * GDN_DV
CONV_W = 4
XA_HEADS = 4
XA_DH = 256
XA_W = XA_HEADS * XA_DH
N_BRANCH = 3
D_FF = 5632

LANES = 128
GLA_CHUNK = 64
GLA_GROUP = 4
GDN_CHUNK = 128
GDN_HPS = 4
GDN_GROUP = 2
VMEM_LIMIT = 56 * 1024 * 1024

_IN_SIZES = (GLA_QK, GLA_QK, GLA_V, GLA_RANK, GLA_V, 2 * GDN_QK + GDN_V, GDN_HEADS, GDN_HEADS, GDN_V,
             XA_W, N_BRANCH * D_MODEL)
_IN_OFF = tuple(int(v) for v in np.cumsum((0,) + _IN_SIZES))
SM_LR, SM_B, SM_A = 0, GLA_RANK, GLA_RANK + GDN_HEADS


def _dot(a, b):
    return jnp.dot(a, b, preferred_element_type=F32)


def _dot_nt(a, b):
    return lax.dot_general(a, b, (((1,), (1,)), ((), ())), preferred_element_type=F32)


def _dot_tn(a, b):
    return lax.dot_general(a, b, (((0,), (0,)), ((), ())), preferred_element_type=F32)


def _mm(a, b):
    return _dot(a.astype(BF16), b.astype(BF16))


def _split2(x):
    hi = x.astype(BF16)
    mid = (x - hi.astype(F32)).astype(BF16)
    return hi, mid


def _dot01(m01, x):
    hi, mid = _split2(x)
    return _dot(m01, hi) + _dot(m01, mid)


def _rms(x, gain):
    return x * lax.rsqrt(jnp.mean(x * x, axis=-1, keepdims=True) + EPS) * gain


def _silu(x):
    return x * jax.nn.sigmoid(x)


def _softplus(x):
    return jnp.maximum(x, 0.0) + jnp.log1p(jnp.exp(-jnp.abs(x)))


def _params(sem):
    return pltpu.CompilerParams(dimension_semantics=sem, vmem_limit_bytes=VMEM_LIMIT)


def _ffn_kernel(x_ref, gpre_ref, wg_ref, wu_ref, wd_ref, gpost_ref, gnext_ref, *out_and_scratch, emit_next):
    if emit_next:
        h_ref, un_ref, xn_scr, wg16, wu16, wd16 = out_and_scratch
    else:
        h_ref, xn_scr, wg16, wu16, wd16 = out_and_scratch
    j = pl.program_id(1)

    @pl.when(j == 0)
    def _():
        xn_scr[...] = _rms(x_ref[...], gpre_ref[...]).astype(BF16)
        h_ref[...] = jnp.zeros_like(h_ref)

    wg16[...] = wg_ref[...].astype(BF16)
    wu16[...] = wu_ref[...].astype(BF16)
    wd16[...] = wd_ref[...].astype(BF16)
    xn = xn_scr[...]
    g = _dot(xn, wg16[...])
    u = _dot(xn, wu16[...])
    hm = (_silu(g) * u).astype(BF16)
    h_ref[...] += _dot(hm, wd16[...])

    @pl.when(j == pl.num_programs(1) - 1)
    def _():
        h = x_ref[...] + 0.5 * _rms(h_ref[...], gpost_ref[...])
        h_ref[...] = h
        if emit_next:
            un_ref[...] = _rms(h, gnext_ref[...]).astype(BF16)


def _ffn(x, gpre, w_gu, w_down, gpost, gnext, emit_next, tm=1024, tf=256):
    t, d = x.shape
    nf = D_FF // tf
    row = lambda i, j: (i, 0)
    const = lambda i, j: (0, 0)
    once = pl.Buffered(1)
    out_shape = [jax.ShapeDtypeStruct((t, d), F32)]
    out_specs = [pl.BlockSpec((tm, d), row)]
    if emit_next:
        out_shape.append(jax.ShapeDtypeStruct((t, d), BF16))
        out_specs.append(pl.BlockSpec((tm, d), row, pipeline_mode=once))
    res = pl.pallas_call(
        functools.partial(_ffn_kernel, emit_next=emit_next),
        grid=(t // tm, nf),
        in_specs=[
            pl.BlockSpec((tm, d), row, pipeline_mode=once),
            pl.BlockSpec((1, d), const),
            pl.BlockSpec((d, tf), lambda i, j: (0, j)),
            pl.BlockSpec((d, tf), lambda i, j: (0, nf + j)),
            pl.BlockSpec((tf, d), lambda i, j: (j, 0)),
            pl.BlockSpec((1, d), const),
            pl.BlockSpec((1, d), const),
        ],
        out_specs=out_specs,
        out_shape=out_shape,
        scratch_shapes=[pltpu.VMEM((tm, d), BF16), pltpu.VMEM((d, tf), BF16), pltpu.VMEM((d, tf), BF16),
                        pltpu.VMEM((tf, d), BF16)],
        compiler_params=_params(("parallel", "arbitrary")),
        name="ffn",
    )(x, gpre, w_gu, w_gu, w_down, gpost, gnext)
    return res if emit_next else res[0]


def _matmul_kernel(a_ref, w_ref, o_ref):
    o_ref[...] = _dot_nt(a_ref[...], w_ref[...]).astype(o_ref.dtype)


def _proj_conv_kernel(a_ref, w_ref, cw_ref, o_ref):
    tm, tn = o_ref.shape
    rb = 256
    kind = pl.program_id(1) // (GDN_QK // tn)
    scale = jnp.where(kind == 0, GDN_DK ** -0.5, 1.0)
    w = cw_ref[...]
    r8 = lax.broadcasted_iota(jnp.int32, (8, tn), 0)

    sb = 32

    def finish(r, before):
        for s0 in range(0, rb, sb):
            y = o_ref[r + s0:r + s0 + sb, :]
            acc = y * w[CONV_W - 1:CONV_W, :]
            for sft in range(1, CONV_W):
                xs = pltpu.roll(y, sft, 0)
                top = jnp.where(r8 < sft, pltpu.roll(before, sft, 0), xs[0:8])
                acc = acc + jnp.concatenate([top, xs[8:]], axis=0) * w[CONV_W - 1 - sft:CONV_W - sft, :]
            before = y[sb - 8:sb]
            c = _silu(acc)
            for hh in range(tn // GDN_DK):
                sl = slice(hh * GDN_DK, (hh + 1) * GDN_DK)
                blk = c[:, sl]
                f = lax.rsqrt(jnp.sum(blk * blk, axis=-1, keepdims=True) + EPS) * scale
                o_ref[r + s0:r + s0 + sb, sl] = blk * jnp.where(kind == 2, 1.0, f)
        return before

    o_ref[0:rb, :] = _dot_nt(a_ref[0:rb, :], w_ref[...])
    before = jnp.zeros((8, tn), F32)
    for r in range(rb, tm + rb, rb):
        if r < tm:
            o_ref[r:r + rb, :] = _dot_nt(a_ref[r:r + rb, :], w_ref[...])
        before = finish(r - rb, before)


def _proj_s_kernel(a_ref, w_ref, par_ref, o_ref):
    y = _dot_nt(a_ref[...], w_ref[...])
    par = par_ref[...]
    lane = lax.broadcasted_iota(jnp.int32, y.shape, 1)
    g = -jnp.exp(par[0:1, :]) * _softplus(y + par[1:2, :])
    is_g = (lane >= SM_A) & (lane < SM_A + GDN_HEADS)
    is_b = (lane >= SM_B) & (lane < SM_B + GDN_HEADS)
    o_ref[...] = jnp.where(is_g, g, jnp.where(is_b, jax.nn.sigmoid(y), y))


def _matmul(a, w, col0, n, out_dtype, tm, tn, name, body=_matmul_kernel, extra=(), extra_specs=()):
    m, k = a.shape
    c0 = col0 // tn
    return pl.pallas_call(
        body,
        grid=(m // tm, n // tn),
        in_specs=[pl.BlockSpec((tm, k), lambda i, j: (i, 0)), pl.BlockSpec((tn, k), lambda i, j: (c0 + j, 0)),
                  *extra_specs],
        out_specs=pl.BlockSpec((tm, tn), lambda i, j: (i, j)),
        out_shape=jax.ShapeDtypeStruct((m, n), out_dtype),
        compiler_params=_params(("parallel", "parallel")),
        name=name,
    )(a, w, *extra)


def _norm_matmul_kernel(x_ref, g_ref, w_ref, o_ref, xn_scr):
    @pl.when(pl.program_id(1) == 0)
    def _():
        xn_scr[...] = _rms(x_ref[...], g_ref[...]).astype(BF16)

    o_ref[...] = _dot(xn_scr[...], w_ref[...].astype(BF16)).astype(o_ref.dtype)


def _norm_matmul(x, gain, w, out_dtype, tm, tn, name):
    m, k = x.shape
    n = w.shape[1]
    return pl.pallas_call(
        _norm_matmul_kernel,
        grid=(m // tm, n // tn),
        in_specs=[pl.BlockSpec((tm, k), lambda i, j: (i, 0)), pl.BlockSpec((1, k), lambda i, j: (0, 0)),
                  pl.BlockSpec((k, tn), lambda i, j: (0, j))],
        out_specs=pl.BlockSpec((tm, tn), lambda i, j: (i, j)),
        out_shape=jax.ShapeDtypeStruct((m, n), out_dtype),
        scratch_shapes=[pltpu.VMEM((tm, k), BF16)],
        compiler_params=_params(("parallel", "arbitrary")),
        name=name,
    )(x, gain, w)


def _gla_levels(chunk):
    return [chunk >> (i + 1) for i in range(int(np.log2(chunk)))]


def _gla_exponent_matrix(chunk):
    c = chunk
    i = np.arange(c)[:, None]
    t = np.arange(c)[None, :]
    blocks = [(t <= i), (t > i)]
    for h in _gla_levels(c):
        r = (i // (2 * h)) * (2 * h) + h
        upper = i >= r
        blocks.append(np.where(upper, (t > r) & (t <= i), (t > i) & (t <= r)))
    m = np.concatenate(blocks, axis=0).astype(np.float32)
    z = np.zeros_like(m)
    return np.concatenate([np.concatenate([m, m, z, z], axis=1), np.concatenate([z, z, m, m], axis=1)], axis=0)


def _gla_kernel(q_ref, k_ref, v_ref, sm_ref, r_ref, wlr_ref, blr_ref, gn_ref, mst_ref, o_ref, s_scr, e0_scr,
                e1_scr):
    c = GLA_CHUNK
    seq = q_ref.shape[0]
    s_scr[...] = jnp.zeros_like(s_scr)
    ri = lax.broadcasted_iota(jnp.int32, (c, c), 0)
    ci = lax.broadcasted_iota(jnp.int32, (c, c), 1)
    rowi = lax.broadcasted_iota(jnp.int32, (c, GLA_DK), 0)
    levels = _gla_levels(c)
    assert GLA_GROUP == 4

    grp = GLA_GROUP
    gs = range(grp)

    n_trips = seq // (grp * c)

    def decays(n, e_ref):
        allrows = pl.ds(pl.multiple_of(n * (grp * c), grp * c), grp * c)
        x = _dot(sm_ref[allrows, :].astype(BF16), wlr_ref[...]) + blr_ref[...]
        lg = (jnp.minimum(x, 0.0) - jnp.log1p(jnp.exp(-jnp.abs(x)))) * (1.0 / GLA_TAU)
        hi, mid = _split2(lg)
        hm = [jnp.concatenate([hi[g * c:(g + 1) * c], mid[g * c:(g + 1) * c]], axis=0) for g in gs]
        rhs = jnp.concatenate([jnp.concatenate([hm[0], hm[1]], axis=1),
                               jnp.concatenate([hm[2], hm[3]], axis=1)], axis=0)
        e_ref[...] = jnp.exp(_dot(mst_ref[...], rhs))

    def trip(n, e_ref):
        row0 = pl.multiple_of(n * (grp * c), grp * c)
        rows = [pl.ds(row0 + g * c, c) for g in gs]
        ex = e_ref[...]
        nr = ex.shape[0] // 2
        e = [ex[0:nr, 0:GLA_DK], ex[0:nr, GLA_DK:], ex[nr:, 0:GLA_DK], ex[nr:, GLA_DK:]]
        q = [q_ref[rows[g], :] * (GLA_DK ** -0.5) for g in gs]
        k = [k_ref[rows[g], :] for g in gs]
        v = [v_ref[rows[g], :].astype(BF16) for g in gs]
        attn = [jnp.where(ri == ci, _dot_nt(q[g].astype(BF16), k[g].astype(BF16)), 0.0) for g in gs]
        for lvl, h in enumerate(levels):
            upper = (rowi & h) != 0
            sh = int(np.log2(2 * h))
            same = (ri >> sh) == (ci >> sh)
            for g in gs:
                f = e[g][(2 + lvl) * c:(3 + lvl) * c]
                qt = jnp.where(upper, q[g] * f, 0.0).astype(BF16)
                kt = jnp.where(upper, 0.0, k[g] * f).astype(BF16)
                attn[g] = attn[g] + jnp.where(same, _dot_nt(qt, kt), 0.0)
        vk = [_dot_tn(v[g], (k[g] * e[g][c:2 * c]).astype(BF16)) for g in gs]
        av = [_dot(attn[g].astype(BF16), v[g]) for g in gs]
        st = s_scr[...]
        for g in gs:
            o = _dot_nt((q[g] * e[g][0:c]).astype(BF16), st.astype(BF16)) + av[g]
            st = st * e[g][c - 1:c, :] + vk[g]
            on = _rms(o, gn_ref[...]) * _silu(r_ref[rows[g], :].astype(F32))
            o_ref[rows[g], :] = on.astype(o_ref.dtype)
        s_scr[...] = st

    assert n_trips % 2 == 0
    decays(0, e0_scr)

    def body(m, carry):
        decays(2 * m + 1, e1_scr)
        trip(2 * m, e0_scr)
        decays(jnp.minimum(2 * m + 2, n_trips - 1), e0_scr)
        trip(2 * m + 1, e1_scr)
        return carry

    lax.fori_loop(0, n_trips // 2, body, 0)


def _gla(pa, pb, ps, wlr, blr, gnorm, bsz, seq):
    t = bsz * seq
    mst = jnp.asarray(_gla_exponent_matrix(GLA_CHUNK), BF16)
    nq = GLA_QK // GLA_DK
    return pl.pallas_call(
        _gla_kernel,
        grid=(bsz, GLA_HEADS),
        in_specs=[
            pl.BlockSpec((seq, GLA_DK), lambda b, h: (b, h)),
            pl.BlockSpec((seq, GLA_DK), lambda b, h: (b, nq + h)),
            pl.BlockSpec((seq, GLA_DV), lambda b, h: (b, 2 * GLA_QK // GLA_DV + h)),
            pl.BlockSpec((seq, LANES), lambda b, h: (b, 0)),
            pl.BlockSpec((seq, GLA_DV), lambda b, h: (b, _PB_R // GLA_DV + h)),
            pl.BlockSpec((LANES, GLA_DK), lambda b, h: (0, h)),
            pl.BlockSpec((1, GLA_DK), lambda b, h: (0, h)),
            pl.BlockSpec((1, GLA_DV), lambda b, h: (0, 0)),
            pl.BlockSpec(mst.shape, lambda b, h: (0, 0)),
        ],
        out_specs=pl.BlockSpec((seq, GLA_DV), lambda b, h: (b, h)),
        out_shape=jax.ShapeDtypeStruct((t, GLA_V), BF16),
        scratch_shapes=[pltpu.VMEM((GLA_DV, GLA_DK), F32), pltpu.VMEM((mst.shape[0], 2 * GLA_DK), F32),
                        pltpu.VMEM((mst.shape[0], 2 * GLA_DK), F32)],
        compiler_params=_params(("parallel", "parallel")),
        name="gla",
    )(pa, pa, pa, ps, pb, wlr, blr, gnorm, mst)


def _gdn_kernel(q_ref, k_ref, v_ref, sm_ref, z_ref, gn_ref, o_ref, bms, ns, qms, os_, egl, s_scr):
    c = GDN_CHUNK
    seq = q_ref.shape[0]
    head0 = pl.program_id(1) * GDN_HPS

    ri = lax.broadcasted_iota(jnp.int32, (c, c), 0)
    ci = lax.broadcasted_iota(jnp.int32, (c, c), 1)
    lane = lax.broadcasted_iota(jnp.int32, (c, LANES), 1)
    tril = jnp.where(ri >= ci, 1.0, 0.0).astype(BF16)
    causal = ri >= ci
    n_sq = int(np.log2(c))
    lower_left = [(((ri ^ ci) >> (l + 1)) == 0) & ((ri & (1 << l)) != 0) & ((ci & (1 << l)) == 0) for l in range(n_sq)]

    grp = GDN_GROUP
    chains = [(g, j) for g in range(grp) for j in range(GDN_HPS)]

    def phase1(n, carry):
        rows, q, k, kb, gb, rhs0 = {}, {}, {}, {}, {}, {}
        for g in range(grp):
            rows[g] = pl.ds(pl.multiple_of((n * grp + g) * c, c), c)
            sm = sm_ref[rows[g], :]
            for j in range(GDN_HPS):
                sl = slice(j * GDN_DK, (j + 1) * GDN_DK)
                q[g, j] = q_ref[rows[g], sl]
                k[g, j] = k_ref[rows[g], sl]
                gcol = jnp.sum(jnp.where(lane == SM_A + head0 + j, sm, 0.0), axis=-1, keepdims=True)
                beta = jnp.sum(jnp.where(lane == SM_B + head0 + j, sm, 0.0), axis=-1, keepdims=True)
                gb[g, j] = jnp.broadcast_to(gcol, (c, LANES))
                kb[g, j] = k[g, j] * beta
                rhs0[g, j] = v_ref[rows[g], sl] * beta
        cum_r, cum_c, kt = {}, {}, {}
        for ch in chains:
            hi, mid = _split2(gb[ch])
            r = _dot(tril, jnp.concatenate([hi, mid], axis=1))
            cum_r[ch] = r[:, :LANES] + r[:, LANES:]
        for ch in chains:
            cum_c[ch] = cum_r[ch].T
            kt[ch] = k[ch].T
        kk = {}
        for ch in chains:
            k16 = k[ch].astype(BF16)
            kk[ch] = _dot_nt(jnp.concatenate([kb[ch].astype(BF16), q[ch].astype(BF16)], axis=0), k16)
        low, at16 = {}, {}
        for ch in chains:
            dec = jnp.where(causal, jnp.exp(jnp.where(causal, cum_r[ch] - cum_c[ch], 0.0)), 0.0)
            low[ch] = jnp.where(ri > ci, kk[ch][:c] * dec, 0.0)
            at16[ch] = jnp.where(causal, kk[ch][c:] * dec, 0.0).astype(BF16)
        x = {ch: -jnp.where(lower_left[0], low[ch], 0.0) for ch in chains}
        for lvl in range(1, n_sq):
            e = {}
            for ch in chains:
                cm = jnp.where(lower_left[lvl], low[ch], 0.0)
                e[ch] = cm + _mm(x[ch], cm)
            for ch in chains:
                x[ch] = x[ch] - e[ch] - _mm(e[ch], x[ch])
        wu, egc = {}, {}
        for ch in chains:
            egc[ch] = jnp.exp(cum_r[ch])
            rhs = jnp.concatenate([kb[ch] * egc[ch], rhs0[ch]], axis=1)
            wu[ch] = (rhs + _mm(x[ch], rhs)).astype(BF16)
        for ch in chains:
            g, j = ch
            sl = slice(j * GDN_DK, (j + 1) * GDN_DK)
            glast = cum_r[ch][c - 1:c, :]
            kdt16 = (kt[ch] * jnp.exp(glast - cum_c[ch][0:1, :])).astype(BF16)
            r = _dot(jnp.concatenate([kdt16, at16[ch]], axis=0), wu[ch])
            bms[rows[g], sl] = (-r[:c, :GDN_DV]).astype(BF16)
            ns[rows[g], sl] = r[:c, GDN_DV:]
            qms[rows[g], sl] = (q[ch] * egc[ch] - r[c:, :GDN_DV]).astype(BF16)
            os_[rows[g], sl] = r[c:, GDN_DV:]
            egl[pl.ds(pl.multiple_of((n * grp + g) * 8, 8), 8), sl] = jnp.broadcast_to(jnp.exp(glast), (8, LANES))
        return carry

    lax.fori_loop(0, seq // (grp * c), phase1, 0)

    s_scr[...] = jnp.zeros_like(s_scr)

    def phase2(n, carry):
        rows = pl.ds(pl.multiple_of(n * c, c), c)
        sls = [slice(j * GDN_DK, (j + 1) * GDN_DK) for j in range(GDN_HPS)]
        s = [s_scr[j] for j in range(GDN_HPS)]
        s16 = [sj.astype(BF16) for sj in s]
        r = [_dot(jnp.concatenate([bms[rows, sl], qms[rows, sl]], axis=0), s16[j]) for j, sl in enumerate(sls)]
        for j, sl in enumerate(sls):
            eg = egl[pl.ds(pl.multiple_of(n * 8, 8), 8), sl][0:1, :]
            s_scr[j] = s[j] * eg + r[j][:c] + ns[rows, sl]
            o = r[j][c:] + os_[rows, sl]
            on = _rms(o, gn_ref[...]) * _silu(z_ref[rows, sl].astype(F32))
            o_ref[rows, sl] = on.astype(o_ref.dtype)
        return carry

    lax.fori_loop(0, seq // c, phase2, 0)


def _gdn(pg, pb, ps, gnorm, bsz, seq):
    t = bsz * seq
    nb = seq // GDN_CHUNK
    wd = GDN_HPS * GDN_DK
    ng = GDN_HEADS // GDN_HPS
    return pl.pallas_call(
        _gdn_kernel,
        grid=(bsz, ng),
        in_specs=[
            pl.BlockSpec((seq, wd), lambda b, h: (b, h)),
            pl.BlockSpec((seq, wd), lambda b, h: (b, ng + h)),
            pl.BlockSpec((seq, wd), lambda b, h: (b, 2 * ng + h)),
            pl.BlockSpec((seq, LANES), lambda b, h: (b, 0)),
            pl.BlockSpec((seq, wd), lambda b, h: (b, _PB_Z // wd + h)),
            pl.BlockSpec((1, GDN_DV), lambda b, h: (0, 0)),
        ],
        out_specs=pl.BlockSpec((seq, wd), lambda b, h: (b, h)),
        out_shape=jax.ShapeDtypeStruct((t, GDN_V), BF16),
        scratch_shapes=[
            pltpu.VMEM((seq, wd), BF16), pltpu.VMEM((seq, wd), F32), pltpu.VMEM((seq, wd), BF16),
            pltpu.VMEM((seq, wd), F32), pltpu.VMEM((nb * 8, wd), F32),
            pltpu.VMEM((GDN_HPS, GDN_DK, GDN_DV), F32),
        ],
        compiler_params=_params(("parallel", "parallel")),
        name="gdn",
    )(pg, pg, pg, ps, pb, gnorm)


def _xa_kernel(q_ref, mk_ref, mv_ref, o_ref):
    s = _dot_nt(q_ref[...], mk_ref[...]) * (XA_DH ** -0.5)
    p = jnp.exp(s - jnp.max(s, axis=-1, keepdims=True))
    den = jnp.sum(p, axis=-1, keepdims=True)
    o_ref[...] = (_dot(p.astype(BF16), mv_ref[...]) / den).astype(o_ref.dtype)


def _xa(pb, mkv, bsz, seq, ts=1024):
    t = bsz * seq
    nt = seq // ts
    return pl.pallas_call(
        _xa_kernel,
        grid=(bsz, XA_HEADS, nt),
        in_specs=[
            pl.BlockSpec((ts, XA_DH), lambda b, h, i: (b * nt + i, _PB_XQ // XA_DH + h)),
            pl.BlockSpec((MEM_LEN, XA_DH), lambda b, h, i: (b, h)),
            pl.BlockSpec((MEM_LEN, XA_DH), lambda b, h, i: (b, XA_HEADS + h)),
        ],
        out_specs=pl.BlockSpec((ts, XA_DH), lambda b, h, i: (b * nt + i, h)),
        out_shape=jax.ShapeDtypeStruct((t, XA_W), BF16),
        compiler_params=_params(("parallel", "parallel", "parallel")),
        name="xattn",
    )(pb, mkv, mkv)


def _merge_kernel(h_ref, oa_ref, ob_ref, oc_ref, ga_ref, gb_ref, gc_ref, wa_ref, wb_ref, wc_ref, wo_ref,
                  gpost_ref, o_ref):
    mixed = jax.nn.sigmoid(ga_ref[...].astype(F32)) * _dot(oa_ref[...], wa_ref[...])
    mixed = mixed + jax.nn.sigmoid(gb_ref[...].astype(F32)) * _dot(ob_ref[...], wb_ref[...])
    mixed = mixed + jax.nn.sigmoid(gc_ref[...].astype(F32)) * _dot(oc_ref[...], wc_ref[...])
    m = _dot(mixed.astype(BF16), wo_ref[...])
    o_ref[...] = h_ref[...] + _rms(m, gpost_ref[...])


def _merge(h, oa, ob, oc, pb, wa, wb, wc, wo, gpost, tm=256):
    t, d = h.shape
    row = lambda i: (i, 0)
    const = lambda i: (0, 0)
    once = pl.Buffered(1)
    return pl.pallas_call(
        _merge_kernel,
        grid=(t // tm,),
        in_specs=[
            pl.BlockSpec((tm, d), row),
            pl.BlockSpec((tm, GLA_V), row), pl.BlockSpec((tm, GDN_V), row), pl.BlockSpec((tm, XA_W), row),
            pl.BlockSpec((tm, d), lambda i: (i, 0)), pl.BlockSpec((tm, d), lambda i: (i, 1)),
            pl.BlockSpec((tm, d), lambda i: (i, 2)),
            pl.BlockSpec((GLA_V, d), const, pipeline_mode=once), pl.BlockSpec((GDN_V, d), const, pipeline_mode=once),
            pl.BlockSpec((XA_W, d), const, pipeline_mode=once), pl.BlockSpec((d, d), const, pipeline_mode=once),
            pl.BlockSpec((1, d), const),
        ],
        out_specs=pl.BlockSpec((tm, d), row),
        out_shape=jax.ShapeDtypeStruct((t, d), F32),
        compiler_params=_params(("parallel",)),
        name="merge",
    )(h, oa, ob, oc, pb, pb, pb, wa, wb, wc, wo, gpost)


_PA_GDN = 2 * GLA_QK + GLA_V
_PA_COLS = _PA_GDN + 2 * GDN_QK + GDN_V
_PB_R = N_BRANCH * D_MODEL
_PB_Z = _PB_R + GLA_V
_PB_XQ = _PB_Z + GDN_V
_PB_COLS = _PB_XQ + XA_W


_W_ORDER = (0, 1, 2, 5, 10, 4, 8, 9)
_W_SMALL = ((_IN_OFF[3], GLA_RANK), (_IN_OFF[6], 2 * GDN_HEADS))
_W_ALL_COLS = _PA_COLS + _PB_COLS + LANES
_W_RB = 512


def _w_block_rows():
    tab = []
    for i in _W_ORDER:
        assert (_IN_OFF[i + 1] - _IN_OFF[i]) % _W_RB == 0
        tab.extend(range(_IN_OFF[i], _IN_OFF[i + 1], _W_RB))
    return np.asarray(tab, np.int32)


def _relayout_kernel(tab_ref, wt_hbm, o_ref, buf, sem):
    j = pl.program_id(0)
    n_main = pl.num_programs(0) - 1

    def block_copy(blk):
        start = pl.multiple_of(tab_ref[blk], 16)
        return pltpu.make_async_copy(wt_hbm.at[pl.ds(start, _W_RB), :], buf.at[blk % 2], sem.at[blk % 2])

    @pl.when(j == 0)
    def _():
        block_copy(j).start()

    @pl.when(j + 1 < n_main)
    def _():
        block_copy(j + 1).start()

    @pl.when(j < n_main)
    def _():
        block_copy(j).wait()
        o_ref[...] = buf[j % 2].astype(BF16)

    @pl.when(j == n_main)
    def _():
        dst = 0
        copies = []
        for n, (row0, rows) in enumerate(_W_SMALL):
            copies.append(pltpu.make_async_copy(wt_hbm.at[pl.ds(row0, rows), :], buf.at[0, pl.ds(dst, rows), :],
                                                sem.at[n]))
            dst += rows
        for cp in copies:
            cp.start()
        for cp in copies:
            cp.wait()
        o_ref[0:dst, :] = buf[0, 0:dst, :].astype(BF16)
        o_ref[dst:, :] = jnp.zeros((_W_RB - dst, o_ref.shape[1]), BF16)


def _split_w_in(w_in_t):
    n, k = w_in_t.shape
    tab = _w_block_rows()
    return pl.pallas_call(
        _relayout_kernel,
        grid_spec=pltpu.PrefetchScalarGridSpec(
            num_scalar_prefetch=1,
            grid=(len(tab) + 1,),
            in_specs=[pl.BlockSpec(memory_space=pl.ANY)],
            out_specs=pl.BlockSpec((_W_RB, k), lambda i, tab: (i, 0)),
            scratch_shapes=[pltpu.VMEM((2, _W_RB, k), F32), pltpu.SemaphoreType.DMA((2,))],
        ),
        out_shape=jax.ShapeDtypeStruct((_W_ALL_COLS, k), BF16),
        compiler_params=_params(("arbitrary",)),
        name="w_in_relayout",
    )(jnp.asarray(tab), w_in_t)


def kernel(x, mem, n_ffn1_pre, w_ffn1_gu, w_ffn1_down, n_ffn1_post, n_mix_pre, w_in, gla_w_lr2, gla_b_lr,
           gla_norm, gdn_conv, gdn_a_log, gdn_dt_bias, gdn_norm, mem_norm, w_mem_kv, w_up_gla, w_up_gdn,
           w_up_xa, w_out, n_mix_post, n_ffn2_pre, w_ffn2_gu, w_ffn2_down, n_ffn2_post):
    bsz, seq, d = x.shape
    t = bsz * seq
    h = x.reshape(t, d)
    for l in range(n_ffn1_pre.shape[0]):
        row = lambda a: a[l][None, :]
        h, u = _ffn(h, row(n_ffn1_pre), w_ffn1_gu[l], w_ffn1_down[l],
                    row(n_ffn1_post), row(n_mix_pre), emit_next=True)

        w_all = _split_w_in(w_in[l].T)
        par = jnp.zeros((2, LANES), F32)
        par = par.at[0, SM_A:SM_A + GDN_HEADS].set(gdn_a_log[l]).at[1, SM_A:SM_A + GDN_HEADS].set(gdn_dt_bias[l])
        tn_g = 512
        pa = _matmul(u, w_all, 0, _PA_GDN, F32, 2048, 512, "in_proj_a")
        pg = _matmul(u, w_all, _PA_GDN, _PA_COLS - _PA_GDN, F32, seq, tn_g, "in_proj_g", body=_proj_conv_kernel,
                     extra=(gdn_conv[l],), extra_specs=(pl.BlockSpec((CONV_W, tn_g), lambda i, j: (0, j)),))
        pb = _matmul(u, w_all, _PA_COLS, _PB_COLS, BF16, 2048, 1024, "in_proj_b")
        ps = _matmul(u, w_all, _PA_COLS + _PB_COLS, LANES, F32, 2048, LANES, "in_proj_s", body=_proj_s_kernel,
                     extra=(par,), extra_specs=(pl.BlockSpec((2, LANES), lambda i, j: (0, 0)),))

        wlr = jnp.pad(gla_w_lr2[l], ((0, LANES - GLA_RANK), (0, 0))).astype(BF16)
        o_gla = _gla(pa, pb, ps, wlr, row(gla_b_lr), row(gla_norm), bsz, seq)
        o_gdn = _gdn(pg, pb, ps, row(gdn_norm), bsz, seq)

        mkv = _norm_matmul(mem.reshape(bsz * MEM_LEN, d), row(mem_norm), w_mem_kv[l], BF16,
                           bsz * MEM_LEN, 512, "mem_kv")
        o_xa = _xa(pb, mkv, bsz, seq)

        h = _merge(h, o_gla, o_gdn, o_xa, pb, w_up_gla[l].astype(BF16), w_up_gdn[l].astype(BF16),
                   w_up_xa[l].astype(BF16), w_out[l].astype(BF16), row(n_mix_post))
        h = _ffn(h, row(n_ffn2_pre), w_ffn2_gu[l], w_ffn2_down[l],
                 row(n_ffn2_post), row(n_ffn2_post), emit_next=False)
    return h.reshape(bsz, seq, d)
```

```python
import functools

import numpy as np
import jax
import jax.numpy as jnp
from jax import lax
from jax.experimental import pallas as pl
from jax.experimental.pallas import tpu as pltpu

F32 = jnp.float32
BF16 = jnp.bfloat16

D_MODEL = 2048
MEM_LEN = 256
EPS = 1e-6
GLA_HEADS = 4
GLA_DK = 128
GLA_DV = 256
GLA_QK = GLA_HEADS * GLA_DK
GLA_V = GLA_HEADS * GLA_DV
GLA_RANK = 16
GLA_TAU = 16.0
GDN_HEADS = 8
GDN_DK = 128
GDN_DV = 128
GDN_QK = GDN_HEADS * GDN_DK
GDN_V = GDN_HEADS * GDN_DV
CONV_W = 4
XA_HEADS = 4
XA_DH = 256
XA_W = XA_HEADS * XA_DH
N_BRANCH = 3
D_FF = 5632

LANES = 128
GLA_CHUNK = 64
GLA_GROUP = 4
GDN_CHUNK = 128
GDN_HPS = 4
GDN_GROUP = 2
VMEM_LIMIT = 56 * 1024 * 1024

_IN_SIZES = (GLA_QK, GLA_QK, GLA_V, GLA_RANK, GLA_V, 2 * GDN_QK + GDN_V, GDN_HEADS, GDN_HEADS, GDN_V,
             XA_W, N_BRANCH * D_MODEL)
_IN_OFF = tuple(int(v) for v in np.cumsum((0,) + _IN_SIZES))
SM_LR, SM_B, SM_A = 0, GLA_RANK, GLA_RANK + GDN_HEADS


def _dot(a, b):
    return jnp.dot(a, b, preferred_element_type=F32)


def _dot_nt(a, b):
    return lax.dot_general(a, b, (((1,), (1,)), ((), ())), preferred_element_type=F32)


def _dot_tn(a, b):
    return lax.dot_general(a, b, (((0,), (0,)), ((), ())), preferred_element_type=F32)


def _mm(a, b):
    return _dot(a.astype(BF16), b.astype(BF16))


def _split2(x):
    hi = x.astype(BF16)
    mid = (x - hi.astype(F32)).astype(BF16)
    return hi, mid


def _dot01(m01, x):
    hi, mid = _split2(x)
    return _dot(m01, hi) + _dot(m01, mid)


def _rms(x, gain):
    return x * lax.rsqrt(jnp.mean(x * x, axis=-1, keepdims=True) + EPS) * gain


def _silu(x):
    return x * jax.nn.sigmoid(x)


def _softplus(x):
    return jnp.maximum(x, 0.0) + jnp.log1p(jnp.exp(-jnp.abs(x)))


def _params(sem):
    return pltpu.CompilerParams(dimension_semantics=sem, vmem_limit_bytes=VMEM_LIMIT)


def _ffn_kernel(x_ref, gpre_ref, wgu_hbm, wd_hbm, gpost_ref, gnext_ref, *out_and_scratch, emit_next, tf):
    if emit_next:
        h_ref, un_ref, xn_scr, wg_buf, wu_buf, wd_buf, wg16, wu16, wd16, sem = out_and_scratch
    else:
        h_ref, xn_scr, wg_buf, wu_buf, wd_buf, wg16, wu16, wd16, sem = out_and_scratch
    i = pl.program_id(0)
    nf = D_FF // tf
    assert nf % 2 == 0

    def tile_copies(jj, slot):
        col = pl.multiple_of(jj * tf, tf)
        return (pltpu.make_async_copy(wgu_hbm.at[:, pl.ds(col, tf)], wg_buf.at[slot], sem.at[0, slot]),
                pltpu.make_async_copy(wgu_hbm.at[:, pl.ds(D_FF + col, tf)], wu_buf.at[slot], sem.at[1, slot]),
                pltpu.make_async_copy(wd_hbm.at[pl.ds(col, tf), :], wd_buf.at[slot], sem.at[2, slot]))

    @pl.when(i == 0)
    def _():
        for cp in tile_copies(0, 0):
            cp.start()

    xn_scr[...] = _rms(x_ref[...], gpre_ref[...]).astype(BF16)
    h_ref[...] = jnp.zeros_like(h_ref)

    def body(jj, carry):
        slot = jj % 2
        @pl.when(jj + 1 < nf)
        def _():
            for cp in tile_copies(jj + 1, 1 - slot):
                cp.start()

        @pl.when((jj + 1 == nf) & (i + 1 < pl.num_programs(0)))
        def _():
            for cp in tile_copies(0, 0):
                cp.start()

        for cp in tile_copies(jj, slot):
            cp.wait()
        wg16[...] = wg_buf[slot].astype(BF16)
        wu16[...] = wu_buf[slot].astype(BF16)
        wd16[...] = wd_buf[slot].astype(BF16)
        xn = xn_scr[...]
        g = _dot(xn, wg16[...])
        u = _dot(xn, wu16[...])
        hm = (_silu(g) * u).astype(BF16)
        h_ref[...] += _dot(hm, wd16[...])
        return carry

    lax.fori_loop(0, nf, body, 0)

    h = x_ref[...] + 0.5 * _rms(h_ref[...], gpost_ref[...])
    h_ref[...] = h
    if emit_next:
        un_ref[...] = _rms(h, gnext_ref[...]).astype(BF16)


def _ffn(x, gpre, w_gu, w_down, gpost, gnext, emit_next, tm=1024, tf=256):
    t, d = x.shape
    row = lambda i: (i, 0)
    const = lambda i: (0, 0)
    once = pl.Buffered(1)
    out_shape = [jax.ShapeDtypeStruct((t, d), F32)]
    out_specs = [pl.BlockSpec((tm, d), row)]
    if emit_next:
        out_shape.append(jax.ShapeDtypeStruct((t, d), BF16))
        out_specs.append(pl.BlockSpec((tm, d), row, pipeline_mode=once))
    res = pl.pallas_call(
        functools.partial(_ffn_kernel, emit_next=emit_next, tf=tf),
        grid=(t // tm,),
        in_specs=[
            pl.BlockSpec((tm, d), row, pipeline_mode=once),
            pl.BlockSpec((1, d), const),
            pl.BlockSpec(memory_space=pl.ANY),
            pl.BlockSpec(memory_space=pl.ANY),
            pl.BlockSpec((1, d), const),
            pl.BlockSpec((1, d), const),
        ],
        out_specs=out_specs,
        out_shape=out_shape,
        scratch_shapes=[pltpu.VMEM((tm, d), BF16),
                        pltpu.VMEM((2, d, tf), F32), pltpu.VMEM((2, d, tf), F32), pltpu.VMEM((2, tf, d), F32),
                        pltpu.VMEM((d, tf), BF16), pltpu.VMEM((d, tf), BF16), pltpu.VMEM((tf, d), BF16),
                        pltpu.SemaphoreType.DMA((3, 2))],
        compiler_params=_params(("arbitrary",)),
        name="ffn",
    )(x, gpre, w_gu, w_down, gpost, gnext)
    return res if emit_next else res[0]


def _matmul_kernel(a_ref, w_ref, o_ref):
    o_ref[...] = _dot_nt(a_ref[...], w_ref[...]).astype(o_ref.dtype)


def _proj_conv_kernel(a_ref, w_ref, cw_ref, o_ref):
    tm, tn = o_ref.shape
    rb = 256
    kind = pl.program_id(1) // (GDN_QK // tn)
    scale = jnp.where(kind == 0, GDN_DK ** -0.5, 1.0)
    w = cw_ref[...]
    r8 = lax.broadcasted_iota(jnp.int32, (8, tn), 0)

    sb = 32

    def finish(r, before):
        for s0 in range(0, rb, sb):
            y = o_ref[r + s0:r + s0 + sb, :]
            acc = y * w[CONV_W - 1:CONV_W, :]
            for sft in range(1, CONV_W):
                xs = pltpu.roll(y, sft, 0)
                top = jnp.where(r8 < sft, pltpu.roll(before, sft, 0), xs[0:8])
                acc = acc + jnp.concatenate([top, xs[8:]], axis=0) * w[CONV_W - 1 - sft:CONV_W - sft, :]
            before = y[sb - 8:sb]
            c = _silu(acc)
            for hh in range(tn // GDN_DK):
                sl = slice(hh * GDN_DK, (hh + 1) * GDN_DK)
                blk = c[:, sl]
                f = lax.rsqrt(jnp.sum(blk * blk, axis=-1, keepdims=True) + EPS) * scale
                o_ref[r + s0:r + s0 + sb, sl] = blk * jnp.where(kind == 2, 1.0, f)
        return before

    o_ref[0:rb, :] = _dot_nt(a_ref[0:rb, :], w_ref[...])
    before = jnp.zeros((8, tn), F32)
    for r in range(rb, tm + rb, rb):
        if r < tm:
            o_ref[r:r + rb, :] = _dot_nt(a_ref[r:r + rb, :], w_ref[...])
        before = finish(r - rb, before)


def _proj_s_kernel(a_ref, w_ref, par_ref, o_ref):
    y = _dot_nt(a_ref[...], w_ref[...])
    par = par_ref[...]
    lane = lax.broadcasted_iota(jnp.int32, y.shape, 1)
    g = -jnp.exp(par[0:1, :]) * _softplus(y + par[1:2, :])
    is_g = (lane >= SM_A) & (lane < SM_A + GDN_HEADS)
    is_b = (lane >= SM_B) & (lane < SM_B + GDN_HEADS)
    o_ref[...] = jnp.where(is_g, g, jnp.where(is_b, jax.nn.sigmoid(y), y))


def _matmul(a, w, col0, n, out_dtype, tm, tn, name, body=_matmul_kernel, extra=(), extra_specs=()):
    m, k = a.shape
    c0 = col0 // tn
    return pl.pallas_call(
        body,
        grid=(m // tm, n // tn),
        in_specs=[pl.BlockSpec((tm, k), lambda i, j: (i, 0)), pl.BlockSpec((tn, k), lambda i, j: (c0 + j, 0)),
                  *extra_specs],
        out_specs=pl.BlockSpec((tm, tn), lambda i, j: (i, j)),
        out_shape=jax.ShapeDtypeStruct((m, n), out_dtype),
        compiler_params=_params(("parallel", "parallel")),
        name=name,
    )(a, w, *extra)


def _norm_matmul_kernel(x_ref, g_ref, w_ref, o_ref, xn_scr):
    @pl.when(pl.program_id(1) == 0)
    def _():
        xn_scr[...] = _rms(x_ref[...], g_ref[...]).astype(BF16)

    o_ref[...] = _dot(xn_scr[...], w_ref[...].astype(BF16)).astype(o_ref.dtype)


def _norm_matmul(x, gain, w, out_dtype, tm, tn, name):
    m, k = x.shape
    n = w.shape[1]
    return pl.pallas_call(
        _norm_matmul_kernel,
        grid=(m // tm, n // tn),
        in_specs=[pl.BlockSpec((tm, k), lambda i, j: (i, 0)), pl.BlockSpec((1, k), lambda i, j: (0, 0)),
                  pl.BlockSpec((k, tn), lambda i, j: (0, j))],
        out_specs=pl.BlockSpec((tm, tn), lambda i, j: (i, j)),
        out_shape=jax.ShapeDtypeStruct((m, n), out_dtype),
        scratch_shapes=[pltpu.VMEM((tm, k), BF16)],
        compiler_params=_params(("parallel", "arbitrary")),
        name=name,
    )(x, gain, w)


def _gla_levels(chunk):
    return [chunk >> (i + 1) for i in range(int(np.log2(chunk)))]


def _gla_exponent_matrix(chunk):
    c = chunk
    i = np.arange(c)[:, None]
    t = np.arange(c)[None, :]
    blocks = [(t <= i), (t > i)]
    for h in _gla_levels(c):
        r = (i // (2 * h)) * (2 * h) + h
        upper = i >= r
        blocks.append(np.where(upper, (t > r) & (t <= i), (t > i) & (t <= r)))
    m = np.concatenate(blocks, axis=0).astype(np.float32)
    z = np.zeros_like(m)
    return np.concatenate([np.concatenate([m, m, z, z], axis=1), np.concatenate([z, z, m, m], axis=1)], axis=0)


def _gla_kernel(q_ref, k_ref, v_ref, sm_ref, r_ref, wlr_ref, blr_ref, gn_ref, mst_ref, o_ref, s_scr, e0_scr,
                e1_scr):
    c = GLA_CHUNK
    seq = q_ref.shape[0]
    s_scr[...] = jnp.zeros_like(s_scr)
    ri = lax.broadcasted_iota(jnp.int32, (c, c), 0)
    ci = lax.broadcasted_iota(jnp.int32, (c, c), 1)
    rowi = lax.broadcasted_iota(jnp.int32, (c, GLA_DK), 0)
    levels = _gla_levels(c)
    assert GLA_GROUP == 4

    grp = GLA_GROUP
    gs = range(grp)

    n_trips = seq // (grp * c)

    def decays(n, e_ref):
        allrows = pl.ds(pl.multiple_of(n * (grp * c), grp * c), grp * c)
        x = _dot(sm_ref[allrows, :].astype(BF16), wlr_ref[...]) + blr_ref[...]
        lg = (jnp.minimum(x, 0.0) - jnp.log1p(jnp.exp(-jnp.abs(x)))) * (1.0 / GLA_TAU)
        hi, mid = _split2(lg)
        hm = [jnp.concatenate([hi[g * c:(g + 1) * c], mid[g * c:(g + 1) * c]], axis=0) for g in gs]
        rhs = jnp.concatenate([jnp.concatenate([hm[0], hm[1]], axis=1),
                               jnp.concatenate([hm[2], hm[3]], axis=1)], axis=0)
        e_ref[...] = jnp.exp(_dot(mst_ref[...], rhs))

    def trip(n, e_ref):
        row0 = pl.multiple_of(n * (grp * c), grp * c)
        rows = [pl.ds(row0 + g * c, c) for g in gs]
        ex = e_ref[...]
        nr = ex.shape[0] // 2
        e = [ex[0:nr, 0:GLA_DK], ex[0:nr, GLA_DK:], ex[nr:, 0:GLA_DK], ex[nr:, GLA_DK:]]
        q = [q_ref[rows[g], :] * (GLA_DK ** -0.5) for g in gs]
        k = [k_ref[rows[g], :] for g in gs]
        v = [v_ref[rows[g], :].astype(BF16) for g in gs]
        attn = [jnp.where(ri == ci, _dot_nt(q[g].astype(BF16), k[g].astype(BF16)), 0.0) for g in gs]
        for lvl, h in enumerate(levels):
            upper = (rowi & h) != 0
            sh = int(np.log2(2 * h))
            same = (ri >> sh) == (ci >> sh)
            for g in gs:
                f = e[g][(2 + lvl) * c:(3 + lvl) * c]
                qt = jnp.where(upper, q[g] * f, 0.0).astype(BF16)
                kt = jnp.where(upper, 0.0, k[g] * f).astype(BF16)
                attn[g] = attn[g] + jnp.where(same, _dot_nt(qt, kt), 0.0)
        vk = [_dot_tn(v[g], (k[g] * e[g][c:2 * c]).astype(BF16)) for g in gs]
        av = [_dot(attn[g].astype(BF16), v[g]) for g in gs]
        st = s_scr[...]
        for g in gs:
            o = _dot_nt((q[g] * e[g][0:c]).astype(BF16), st.astype(BF16)) + av[g]
            st = st * e[g][c - 1:c, :] + vk[g]
            on = _rms(o, gn_ref[...]) * _silu(r_ref[rows[g], :].astype(F32))
            o_ref[rows[g], :] = on.astype(o_ref.dtype)
        s_scr[...] = st

    assert n_trips % 2 == 0
    decays(0, e0_scr)

    def body(m, carry):
        decays(2 * m + 1, e1_scr)
        trip(2 * m, e0_scr)
        decays(jnp.minimum(2 * m + 2, n_trips - 1), e0_scr)
        trip(2 * m + 1, e1_scr)
        return carry

    lax.fori_loop(0, n_trips // 2, body, 0)


def _gla(pa, pb, ps, wlr, blr, gnorm, bsz, seq):
    t = bsz * seq
    mst = jnp.asarray(_gla_exponent_matrix(GLA_CHUNK), BF16)
    nq = GLA_QK // GLA_DK
    return pl.pallas_call(
        _gla_kernel,
        grid=(bsz, GLA_HEADS),
        in_specs=[
            pl.BlockSpec((seq, GLA_DK), lambda b, h: (b, h)),
            pl.BlockSpec((seq, GLA_DK), lambda b, h: (b, nq + h)),
            pl.BlockSpec((seq, GLA_DV), lambda b, h: (b, 2 * GLA_QK // GLA_DV + h)),
            pl.BlockSpec((seq, LANES), lambda b, h: (b, 0)),
            pl.BlockSpec((seq, GLA_DV), lambda b, h: (b, _PB_R // GLA_DV + h)),
            pl.BlockSpec((LANES, GLA_DK), lambda b, h: (0, h)),
            pl.BlockSpec((1, GLA_DK), lambda b, h: (0, h)),
            pl.BlockSpec((1, GLA_DV), lambda b, h: (0, 0)),
            pl.BlockSpec(mst.shape, lambda b, h: (0, 0)),
        ],
        out_specs=pl.BlockSpec((seq, GLA_DV), lambda b, h: (b, h)),
        out_shape=jax.ShapeDtypeStruct((t, GLA_V), BF16),
        scratch_shapes=[pltpu.VMEM((GLA_DV, GLA_DK), F32), pltpu.VMEM((mst.shape[0], 2 * GLA_DK), F32),
                        pltpu.VMEM((mst.shape[0], 2 * GLA_DK), F32)],
        compiler_params=_params(("parallel", "parallel")),
        name="gla",
    )(pa, pa, pa, ps, pb, wlr, blr, gnorm, mst)


def _gdn_kernel(q_ref, k_ref, v_ref, sm_ref, z_ref, gn_ref, o_ref, bms, ns, qms, os_, egl, s_scr):
    c = GDN_CHUNK
    seq = q_ref.shape[0]
    head0 = pl.program_id(1) * GDN_HPS

    ri = lax.broadcasted_iota(jnp.int32, (c, c), 0)
    ci = lax.broadcasted_iota(jnp.int32, (c, c), 1)
    lane = lax.broadcasted_iota(jnp.int32, (c, LANES), 1)
    tril = jnp.where(ri >= ci, 1.0, 0.0).astype(BF16)
    causal = ri >= ci
    n_sq = int(np.log2(c))
    lower_left = [(((ri ^ ci) >> (l + 1)) == 0) & ((ri & (1 << l)) != 0) & ((ci & (1 << l)) == 0) for l in range(n_sq)]

    grp = GDN_GROUP
    chains = [(g, j) for g in range(grp) for j in range(GDN_HPS)]

    def phase1(n, carry):
        rows, q, k, kb, gb, rhs0 = {}, {}, {}, {}, {}, {}
        for g in range(grp):
            rows[g] = pl.ds(pl.multiple_of((n * grp + g) * c, c), c)
            sm = sm_ref[rows[g], :]
            for j in range(GDN_HPS):
                sl = slice(j * GDN_DK, (j + 1) * GDN_DK)
                q[g, j] = q_ref[rows[g], sl]
                k[g, j] = k_ref[rows[g], sl]
                gcol = jnp.sum(jnp.where(lane == SM_A + head0 + j, sm, 0.0), axis=-1, keepdims=True)
                beta = jnp.sum(jnp.where(lane == SM_B + head0 + j, sm, 0.0), axis=-1, keepdims=True)
                gb[g, j] = jnp.broadcast_to(gcol, (c, LANES))
                kb[g, j] = k[g, j] * beta
                rhs0[g, j] = v_ref[rows[g], sl] * beta
        cum_r, cum_c, kt = {}, {}, {}
        for ch in chains:
            hi, mid = _split2(gb[ch])
            r = _dot(tril, jnp.concatenate([hi, mid], axis=1))
            cum_r[ch] = r[:, :LANES] + r[:, LANES:]
        for ch in chains:
            cum_c[ch] = cum_r[ch].T
            kt[ch] = k[ch].T
        kk = {}
        for ch in chains:
            k16 = k[ch].astype(BF16)
            kk[ch] = _dot_nt(jnp.concatenate([kb[ch].astype(BF16), q[ch].astype(BF16)], axis=0), k16)
        low, at16 = {}, {}
        for ch in chains:
            dec = jnp.where(causal, jnp.exp(jnp.where(causal, cum_r[ch] - cum_c[ch], 0.0)), 0.0)
            low[ch] = jnp.where(ri > ci, kk[ch][:c] * dec, 0.0)
            at16[ch] = jnp.where(causal, kk[ch][c:] * dec, 0.0).astype(BF16)
        x = {ch: -jnp.where(lower_left[0], low[ch], 0.0) for ch in chains}
        for lvl in range(1, n_sq):
            e = {}
            for ch in chains:
                cm = jnp.where(lower_left[lvl], low[ch], 0.0)
                e[ch] = cm + _mm(x[ch], cm)
            for ch in chains:
                x[ch] = x[ch] - e[ch] - _mm(e[ch], x[ch])
        wu, egc = {}, {}
        for ch in chains:
            egc[ch] = jnp.exp(cum_r[ch])
            rhs = jnp.concatenate([kb[ch] * egc[ch], rhs0[ch]], axis=1)
            wu[ch] = (rhs + _mm(x[ch], rhs)).astype(BF16)
        for ch in chains:
            g, j = ch
            sl = slice(j * GDN_DK, (j + 1) * GDN_DK)
            glast = cum_r[ch][c - 1:c, :]
            kdt16 = (kt[ch] * jnp.exp(glast - cum_c[ch][0:1, :])).astype(BF16)
            r = _dot(jnp.concatenate([kdt16, at16[ch]], axis=0), wu[ch])
            bms[rows[g], sl] = (-r[:c, :GDN_DV]).astype(BF16)
            ns[rows[g], sl] = r[:c, GDN_DV:]
            qms[rows[g], sl] = (q[ch] * egc[ch] - r[c:, :GDN_DV]).astype(BF16)
            os_[rows[g], sl] = r[c:, GDN_DV:]
            egl[pl.ds(pl.multiple_of((n * grp + g) * 8, 8), 8), sl] = jnp.broadcast_to(jnp.exp(glast), (8, LANES))
        return carry

    lax.fori_loop(0, seq // (grp * c), phase1, 0)

    s_scr[...] = jnp.zeros_like(s_scr)

    def phase2(n, carry):
        rows = pl.ds(pl.multiple_of(n * c, c), c)
        sls = [slice(j * GDN_DK, (j + 1) * GDN_DK) for j in range(GDN_HPS)]
        s = [s_scr[j] for j in range(GDN_HPS)]
        s16 = [sj.astype(BF16) for sj in s]
        r = [_dot(jnp.concatenate([bms[rows, sl], qms[rows, sl]], axis=0), s16[j]) for j, sl in enumerate(sls)]
        for j, sl in enumerate(sls):
            eg = egl[pl.ds(pl.multiple_of(n * 8, 8), 8), sl][0:1, :]
            s_scr[j] = s[j] * eg + r[j][:c] + ns[rows, sl]
            o = r[j][c:] + os_[rows, sl]
            on = _rms(o, gn_ref[...]) * _silu(z_ref[rows, sl].astype(F32))
            o_ref[rows, sl] = on.astype(o_ref.dtype)
        return carry

    lax.fori_loop(0, seq // c, phase2, 0)


def _gdn(pg, pb, ps, gnorm, bsz, seq):
    t = bsz * seq
    nb = seq // GDN_CHUNK
    wd = GDN_HPS * GDN_DK
    ng = GDN_HEADS // GDN_HPS
    return pl.pallas_call(
        _gdn_kernel,
        grid=(bsz, ng),
        in_specs=[
            pl.BlockSpec((seq, wd), lambda b, h: (b, h)),
            pl.BlockSpec((seq, wd), lambda b, h: (b, ng + h)),
            pl.BlockSpec((seq, wd), lambda b, h: (b, 2 * ng + h)),
            pl.BlockSpec((seq, LANES), lambda b, h: (b, 0)),
            pl.BlockSpec((seq, wd), lambda b, h: (b, _PB_Z // wd + h)),
            pl.BlockSpec((1, GDN_DV), lambda b, h: (0, 0)),
        ],
        out_specs=pl.BlockSpec((seq, wd), lambda b, h: (b, h)),
        out_shape=jax.ShapeDtypeStruct((t, GDN_V), BF16),
        scratch_shapes=[
            pltpu.VMEM((seq, wd), BF16), pltpu.VMEM((seq, wd), F32), pltpu.VMEM((seq, wd), BF16),
            pltpu.VMEM((seq, wd), F32), pltpu.VMEM((nb * 8, wd), F32),
            pltpu.VMEM((GDN_HPS, GDN_DK, GDN_DV), F32),
        ],
        compiler_params=_params(("parallel", "parallel")),
        name="gdn",
    )(pg, pg, pg, ps, pb, gnorm)


def _xa_kernel(q_ref, mk_ref, mv_ref, o_ref):
    s = _dot_nt(q_ref[...], mk_ref[...]) * (XA_DH ** -0.5)
    p = jnp.exp(s - jnp.max(s, axis=-1, keepdims=True))
    den = jnp.sum(p, axis=-1, keepdims=True)
    o_ref[...] = (_dot(p.astype(BF16), mv_ref[...]) / den).astype(o_ref.dtype)


def _xa(pb, mkv, bsz, seq, ts=1024):
    t = bsz * seq
    nt = seq // ts
    return pl.pallas_call(
        _xa_kernel,
        grid=(bsz, XA_HEADS, nt),
        in_specs=[
            pl.BlockSpec((ts, XA_DH), lambda b, h, i: (b * nt + i, _PB_XQ // XA_DH + h)),
            pl.BlockSpec((MEM_LEN, XA_DH), lambda b, h, i: (b, h)),
            pl.BlockSpec((MEM_LEN, XA_DH), lambda b, h, i: (b, XA_HEADS + h)),
        ],
        out_specs=pl.BlockSpec((ts, XA_DH), lambda b, h, i: (b * nt + i, h)),
        out_shape=jax.ShapeDtypeStruct((t, XA_W), BF16),
        compiler_params=_params(("parallel", "parallel", "parallel")),
        name="xattn",
    )(pb, mkv, mkv)


def _merge_kernel(h_ref, oa_ref, ob_ref, oc_ref, ga_ref, gb_ref, gc_ref, wa_ref, wb_ref, wc_ref, wo_ref,
                  gpost_ref, o_ref):
    mixed = jax.nn.sigmoid(ga_ref[...].astype(F32)) * _dot(oa_ref[...], wa_ref[...])
    mixed = mixed + jax.nn.sigmoid(gb_ref[...].astype(F32)) * _dot(ob_ref[...], wb_ref[...])
    mixed = mixed + jax.nn.sigmoid(gc_ref[...].astype(F32)) * _dot(oc_ref[...], wc_ref[...])
    m = _dot(mixed.astype(BF16), wo_ref[...])
    o_ref[...] = h_ref[...] + _rms(m, gpost_ref[...])


def _merge(h, oa, ob, oc, pb, wa, wb, wc, wo, gpost, tm=256):
    t, d = h.shape
    row = lambda i: (i, 0)
    const = lambda i: (0, 0)
    once = pl.Buffered(1)
    return pl.pallas_call(
        _merge_kernel,
        grid=(t // tm,),
        in_specs=[
            pl.BlockSpec((tm, d), row),
            pl.BlockSpec((tm, GLA_V), row), pl.BlockSpec((tm, GDN_V), row), pl.BlockSpec((tm, XA_W), row),
            pl.BlockSpec((tm, d), lambda i: (i, 0)), pl.BlockSpec((tm, d), lambda i: (i, 1)),
            pl.BlockSpec((tm, d), lambda i: (i, 2)),
            pl.BlockSpec((GLA_V, d), const, pipeline_mode=once), pl.BlockSpec((GDN_V, d), const, pipeline_mode=once),
            pl.BlockSpec((XA_W, d), const, pipeline_mode=once), pl.BlockSpec((d, d), const, pipeline_mode=once),
            pl.BlockSpec((1, d), const),
        ],
        out_specs=pl.BlockSpec((tm, d), row),
        out_shape=jax.ShapeDtypeStruct((t, d), F32),
        compiler_params=_params(("parallel",)),
        name="merge",
    )(h, oa, ob, oc, pb, pb, pb, wa, wb, wc, wo, gpost)


_PA_GDN = 2 * GLA_QK + GLA_V
_PA_COLS = _PA_GDN + 2 * GDN_QK + GDN_V
_PB_R = N_BRANCH * D_MODEL
_PB_Z = _PB_R + GLA_V
_PB_XQ = _PB_Z + GDN_V
_PB_COLS = _PB_XQ + XA_W


_W_ORDER = (0, 1, 2, 5, 10, 4, 8, 9)
_W_SMALL = ((_IN_OFF[3], GLA_RANK), (_IN_OFF[6], 2 * GDN_HEADS))
_W_ALL_COLS = _PA_COLS + _PB_COLS + LANES
_W_RB = 512


def _w_block_rows():
    tab = []
    for i in _W_ORDER:
        assert (_IN_OFF[i + 1] - _IN_OFF[i]) % _W_RB == 0
        tab.extend(range(_IN_OFF[i], _IN_OFF[i + 1], _W_RB))
    return np.asarray(tab, np.int32)


def _relayout_kernel(tab_ref, wt_hbm, o_ref, buf, sem):
    j = pl.program_id(0)
    n_main = pl.num_programs(0) - 1

    def block_copy(blk):
        start = pl.multiple_of(tab_ref[blk], 16)
        return pltpu.make_async_copy(wt_hbm.at[pl.ds(start, _W_RB), :], buf.at[blk % 2], sem.at[blk % 2])

    @pl.when(j == 0)
    def _():
        block_copy(j).start()

    @pl.when(j + 1 < n_main)
    def _():
        block_copy(j + 1).start()

    @pl.when(j < n_main)
    def _():
        block_copy(j).wait()
        o_ref[...] = buf[j % 2].astype(BF16)

    @pl.when(j == n_main)
    def _():
        dst = 0
        copies = []
        for n, (row0, rows) in enumerate(_W_SMALL):
            copies.append(pltpu.make_async_copy(wt_hbm.at[pl.ds(row0, rows), :], buf.at[0, pl.ds(dst, rows), :],
                                                sem.at[n]))
            dst += rows
        for cp in copies:
            cp.start()
        for cp in copies:
            cp.wait()
        o_ref[0:dst, :] = buf[0, 0:dst, :].astype(BF16)
        o_ref[dst:, :] = jnp.zeros((_W_RB - dst, o_ref.shape[1]), BF16)


def _split_w_in(w_in_t):
    n, k = w_in_t.shape
    tab = _w_block_rows()
    return pl.pallas_call(
        _relayout_kernel,
        grid_spec=pltpu.PrefetchScalarGridSpec(
            num_scalar_prefetch=1,
            grid=(len(tab) + 1,),
            in_specs=[pl.BlockSpec(memory_space=pl.ANY)],
            out_specs=pl.BlockSpec((_W_RB, k), lambda i, tab: (i, 0)),
            scratch_shapes=[pltpu.VMEM((2, _W_RB, k), F32), pltpu.SemaphoreType.DMA((2,))],
        ),
        out_shape=jax.ShapeDtypeStruct((_W_ALL_COLS, k), BF16),
        compiler_params=_params(("arbitrary",)),
        name="w_in_relayout",
    )(jnp.asarray(tab), w_in_t)


def kernel(x, mem, n_ffn1_pre, w_ffn1_gu, w_ffn1_down, n_ffn1_post, n_mix_pre, w_in, gla_w_lr2, gla_b_lr,
           gla_norm, gdn_conv, gdn_a_log, gdn_dt_bias, gdn_norm, mem_norm, w_mem_kv, w_up_gla, w_up_gdn,
           w_up_xa, w_out, n_mix_post, n_ffn2_pre, w_ffn2_gu, w_ffn2_down, n_ffn2_post):
    bsz, seq, d = x.shape
    t = bsz * seq
    h = x.reshape(t, d)
    for l in range(n_ffn1_pre.shape[0]):
        row = lambda a: a[l][None, :]
        h, u = _ffn(h, row(n_ffn1_pre), w_ffn1_gu[l], w_ffn1_down[l],
                    row(n_ffn1_post), row(n_mix_pre), emit_next=True)

        w_all = _split_w_in(w_in[l].T)
        par = jnp.zeros((2, LANES), F32)
        par = par.at[0, SM_A:SM_A + GDN_HEADS].set(gdn_a_log[l]).at[1, SM_A:SM_A + GDN_HEADS].set(gdn_dt_bias[l])
        tn_g = 512
        pa = _matmul(u, w_all, 0, _PA_GDN, F32, 2048, 512, "in_proj_a")
        pg = _matmul(u, w_all, _PA_GDN, _PA_COLS - _PA_GDN, F32, seq, tn_g, "in_proj_g", body=_proj_conv_kernel,
                     extra=(gdn_conv[l],), extra_specs=(pl.BlockSpec((CONV_W, tn_g), lambda i, j: (0, j)),))
        pb = _matmul(u, w_all, _PA_COLS, _PB_COLS, BF16, 2048, 1024, "in_proj_b")
        ps = _matmul(u, w_all, _PA_COLS + _PB_COLS, LANES, F32, 2048, LANES, "in_proj_s", body=_proj_s_kernel,
                     extra=(par,), extra_specs=(pl.BlockSpec((2, LANES), lambda i, j: (0, 0)),))

        wlr = jnp.pad(gla_w_lr2[l], ((0, LANES - GLA_RANK), (0, 0))).astype(BF16)
        o_gla = _gla(pa, pb, ps, wlr, row(gla_b_lr), row(gla_norm), bsz, seq)
        o_gdn = _gdn(pg, pb, ps, row(gdn_norm), bsz, seq)

        mkv = _norm_matmul(mem.reshape(bsz * MEM_LEN, d), row(mem_norm), w_mem_kv[l], BF16,
                           bsz * MEM_LEN, 512, "mem_kv")
        o_xa = _xa(pb, mkv, bsz, seq)

        h = _merge(h, o_gla, o_gdn, o_xa, pb, w_up_gla[l].astype(BF16), w_up_gdn[l].astype(BF16),
                   w_up_xa[l].astype(BF16), w_out[l].astype(BF16), row(n_mix_post))
        h = _ffn(h, row(n_ffn2_pre), w_ffn2_gu[l], w_ffn2_down[l],
                 row(n_ffn2_post), row(n_ffn2_post), emit_next=False)
    return h.reshape(bsz, seq, d)
```

```python
import functools

import numpy as np
import jax
import jax.numpy as jnp
from jax import lax
from jax.experimental import pallas as pl
from jax.experimental.pallas import tpu as pltpu

F32 = jnp.float32
BF16 = jnp.bfloat16

D_MODEL = 2048
MEM_LEN = 256
EPS = 1e-6
GLA_HEADS = 4
GLA_DK = 128
GLA_DV = 256
GLA_QK = GLA_HEADS * GLA_DK
GLA_V = GLA_HEADS * GLA_DV
GLA_RANK = 16
GLA_TAU = 16.0
GDN_HEADS = 8
GDN_DK = 128
GDN_DV = 128
GDN_QK = GDN_HEADS * GDN_DK
GDN_V = GDN_HEADS * GDN_DV
CONV_W = 4
XA_HEADS = 4
XA_DH = 256
XA_W = XA_HEADS * XA_DH
N_BRANCH = 3
D_FF = 5632

LANES = 128
GLA_CHUNK = 64
GLA_GROUP = 4
GDN_CHUNK = 128
GDN_HPS = 4
GDN_GROUP = 2
VMEM_LIMIT = 56 * 1024 * 1024

_IN_SIZES = (GLA_QK, GLA_QK, GLA_V, GLA_RANK, GLA_V, 2 * GDN_QK + GDN_V, GDN_HEADS, GDN_HEADS, GDN_V,
             XA_W, N_BRANCH * D_MODEL)
_IN_OFF = tuple(int(v) for v in np.cumsum((0,) + _IN_SIZES))
SM_LR, SM_B, SM_A = 0, GLA_RANK, GLA_RANK + GDN_HEADS


def _dot(a, b):
    return jnp.dot(a, b, preferred_element_type=F32)


def _dot_nt(a, b):
    return lax.dot_general(a, b, (((1,), (1,)), ((), ())), preferred_element_type=F32)


def _dot_tn(a, b):
    return lax.dot_general(a, b, (((0,), (0,)), ((), ())), preferred_element_type=F32)


def _mm(a, b):
    return _dot(a.astype(BF16), b.astype(BF16))


def _split2(x):
    hi = x.astype(BF16)
    mid = (x - hi.astype(F32)).astype(BF16)
    return hi, mid


def _dot01(m01, x):
    hi, mid = _split2(x)
    return _dot(m01, hi) + _dot(m01, mid)


def _rms(x, gain):
    return x * lax.rsqrt(jnp.mean(x * x, axis=-1, keepdims=True) + EPS) * gain


def _silu(x):
    return x * jax.nn.sigmoid(x)


def _softplus(x):
    return jnp.maximum(x, 0.0) + jnp.log1p(jnp.exp(-jnp.abs(x)))


def _params(sem):
    return pltpu.CompilerParams(dimension_semantics=sem, vmem_limit_bytes=VMEM_LIMIT)


def _ffn_kernel(x_ref, gpre_ref, wgu_hbm, wd_hbm, gpost_ref, gnext_ref, *out_and_scratch, emit_next, tf):
    if emit_next:
        h_ref, un_ref, xn_scr, wg_buf, wu_buf, wd_buf, wg16, wu16, wd16, sem = out_and_scratch
    else:
        h_ref, xn_scr, wg_buf, wu_buf, wd_buf, wg16, wu16, wd16, sem = out_and_scratch
    i = pl.program_id(0)
    nf = D_FF // tf
    assert nf % 2 == 0

    def tile_copies(jj, slot):
        col = pl.multiple_of(jj * tf, tf)
        return (pltpu.make_async_copy(wgu_hbm.at[:, pl.ds(col, tf)], wg_buf.at[slot], sem.at[0, slot]),
                pltpu.make_async_copy(wgu_hbm.at[:, pl.ds(D_FF + col, tf)], wu_buf.at[slot], sem.at[1, slot]),
                pltpu.make_async_copy(wd_hbm.at[pl.ds(col, tf), :], wd_buf.at[slot], sem.at[2, slot]))

    @pl.when(i == 0)
    def _():
        for cp in tile_copies(0, 0):
            cp.start()

    xn_scr[...] = _rms(x_ref[...], gpre_ref[...]).astype(BF16)
    h_ref[...] = jnp.zeros_like(h_ref)

    def body(jj, carry):
        slot = jj % 2
        @pl.when(jj + 1 < nf)
        def _():
            for cp in tile_copies(jj + 1, 1 - slot):
                cp.start()

        @pl.when((jj + 1 == nf) & (i + 1 < pl.num_programs(0)))
        def _():
            for cp in tile_copies(0, 0):
                cp.start()

        for cp in tile_copies(jj, slot):
            cp.wait()
        wg16[...] = wg_buf[slot].astype(BF16)
        wu16[...] = wu_buf[slot].astype(BF16)
        wd16[...] = wd_buf[slot].astype(BF16)
        xn = xn_scr[...]
        g = _dot(xn, wg16[...])
        u = _dot(xn, wu16[...])
        hm = (_silu(g) * u).astype(BF16)
        h_ref[...] += _dot(hm, wd16[...])
        return carry

    lax.fori_loop(0, nf, body, 0)

    h = x_ref[...] + 0.5 * _rms(h_ref[...], gpost_ref[...])
    h_ref[...] = h
    if emit_next:
        un_ref[...] = _rms(h, gnext_ref[...]).astype(BF16)


def _ffn(x, gpre, w_gu, w_down, gpost, gnext, emit_next, tm=1024, tf=256):
    t, d = x.shape
    row = lambda i: (i, 0)
    const = lambda i: (0, 0)
    once = pl.Buffered(1)
    out_shape = [jax.ShapeDtypeStruct((t, d), F32)]
    out_specs = [pl.BlockSpec((tm, d), row, pipeline_mode=once)]
    if emit_next:
        out_shape.append(jax.ShapeDtypeStruct((t, d), BF16))
        out_specs.append(pl.BlockSpec((tm, d), row, pipeline_mode=once))
    res = pl.pallas_call(
        functools.partial(_ffn_kernel, emit_next=emit_next, tf=tf),
        grid=(t // tm,),
        in_specs=[
            pl.BlockSpec((tm, d), row),
            pl.BlockSpec((1, d), const),
            pl.BlockSpec(memory_space=pl.ANY),
            pl.BlockSpec(memory_space=pl.ANY),
            pl.BlockSpec((1, d), const),
            pl.BlockSpec((1, d), const),
        ],
        out_specs=out_specs,
        out_shape=out_shape,
        scratch_shapes=[pltpu.VMEM((tm, d), BF16),
                        pltpu.VMEM((2, d, tf), F32), pltpu.VMEM((2, d, tf), F32), pltpu.VMEM((2, tf, d), F32),
                        pltpu.VMEM((d, tf), BF16), pltpu.VMEM((d, tf), BF16), pltpu.VMEM((tf, d), BF16),
                        pltpu.SemaphoreType.DMA((3, 2))],
        compiler_params=_params(("arbitrary",)),
        name="ffn",
    )(x, gpre, w_gu, w_down, gpost, gnext)
    return res if emit_next else res[0]


def _matmul_kernel(a_ref, w_ref, o_ref):
    o_ref[...] = _dot_nt(a_ref[...], w_ref[...]).astype(o_ref.dtype)


def _proj_conv_kernel(a_ref, w_ref, cw_ref, o_ref):
    tm, tn = o_ref.shape
    rb = 256
    kind = pl.program_id(1) // (GDN_QK // tn)
    scale = jnp.where(kind == 0, GDN_DK ** -0.5, 1.0)
    w = cw_ref[...]
    r8 = lax.broadcasted_iota(jnp.int32, (8, tn), 0)

    sb = 32

    def finish(r, before):
        for s0 in range(0, rb, sb):
            y = o_ref[r + s0:r + s0 + sb, :]
            acc = y * w[CONV_W - 1:CONV_W, :]
            for sft in range(1, CONV_W):
                xs = pltpu.roll(y, sft, 0)
                top = jnp.where(r8 < sft, pltpu.roll(before, sft, 0), xs[0:8])
                acc = acc + jnp.concatenate([top, xs[8:]], axis=0) * w[CONV_W - 1 - sft:CONV_W - sft, :]
            before = y[sb - 8:sb]
            c = _silu(acc)
            for hh in range(tn // GDN_DK):
                sl = slice(hh * GDN_DK, (hh + 1) * GDN_DK)
                blk = c[:, sl]
                f = lax.rsqrt(jnp.sum(blk * blk, axis=-1, keepdims=True) + EPS) * scale
                o_ref[r + s0:r + s0 + sb, sl] = blk * jnp.where(kind == 2, 1.0, f)
        return before

    o_ref[0:rb, :] = _dot_nt(a_ref[0:rb, :], w_ref[...])
    before = jnp.zeros((8, tn), F32)
    for r in range(rb, tm + rb, rb):
        if r < tm:
            o_ref[r:r + rb, :] = _dot_nt(a_ref[r:r + rb, :], w_ref[...])
        before = finish(r - rb, before)


def _proj_s_kernel(a_ref, w_ref, par_ref, o_ref):
    y = _dot_nt(a_ref[...], w_ref[...])
    par = par_ref[...]
    lane = lax.broadcasted_iota(jnp.int32, y.shape, 1)
    g = -jnp.exp(par[0:1, :]) * _softplus(y + par[1:2, :])
    is_g = (lane >= SM_A) & (lane < SM_A + GDN_HEADS)
    is_b = (lane >= SM_B) & (lane < SM_B + GDN_HEADS)
    o_ref[...] = jnp.where(is_g, g, jnp.where(is_b, jax.nn.sigmoid(y), y))


def _matmul(a, w, col0, n, out_dtype, tm, tn, name, body=_matmul_kernel, extra=(), extra_specs=()):
    m, k = a.shape
    c0 = col0 // tn
    return pl.pallas_call(
        body,
        grid=(m // tm, n // tn),
        in_specs=[pl.BlockSpec((tm, k), lambda i, j: (i, 0)), pl.BlockSpec((tn, k), lambda i, j: (c0 + j, 0)),
                  *extra_specs],
        out_specs=pl.BlockSpec((tm, tn), lambda i, j: (i, j)),
        out_shape=jax.ShapeDtypeStruct((m, n), out_dtype),
        compiler_params=_params(("parallel", "parallel")),
        name=name,
    )(a, w, *extra)


def _norm_matmul_kernel(x_ref, g_ref, w_ref, o_ref, xn_scr):
    @pl.when(pl.program_id(1) == 0)
    def _():
        xn_scr[...] = _rms(x_ref[...], g_ref[...]).astype(BF16)

    o_ref[...] = _dot(xn_scr[...], w_ref[...].astype(BF16)).astype(o_ref.dtype)


def _norm_matmul(x, gain, w, out_dtype, tm, tn, name):
    m, k = x.shape
    n = w.shape[1]
    return pl.pallas_call(
        _norm_matmul_kernel,
        grid=(m // tm, n // tn),
        in_specs=[pl.BlockSpec((tm, k), lambda i, j: (i, 0)), pl.BlockSpec((1, k), lambda i, j: (0, 0)),
                  pl.BlockSpec((k, tn), lambda i, j: (0, j))],
        out_specs=pl.BlockSpec((tm, tn), lambda i, j: (i, j)),
        out_shape=jax.ShapeDtypeStruct((m, n), out_dtype),
        scratch_shapes=[pltpu.VMEM((tm, k), BF16)],
        compiler_params=_params(("parallel", "arbitrary")),
        name=name,
    )(x, gain, w)


def _gla_levels(chunk):
    return [chunk >> (i + 1) for i in range(int(np.log2(chunk)))]


def _gla_exponent_matrix(chunk):
    c = chunk
    i = np.arange(c)[:, None]
    t = np.arange(c)[None, :]
    blocks = [(t <= i), (t > i)]
    for h in _gla_levels(c):
        r = (i // (2 * h)) * (2 * h) + h
        upper = i >= r
        blocks.append(np.where(upper, (t > r) & (t <= i), (t > i) & (t <= r)))
    m = np.concatenate(blocks, axis=0).astype(np.float32)
    z = np.zeros_like(m)
    return np.concatenate([np.concatenate([m, m, z, z], axis=1), np.concatenate([z, z, m, m], axis=1)], axis=0)


def _gla_kernel(q_ref, k_ref, v_ref, sm_ref, r_ref, wlr_ref, blr_ref, gn_ref, mst_ref, o_ref, s_scr, e0_scr,
                e1_scr):
    c = GLA_CHUNK
    seq = q_ref.shape[0]
    s_scr[...] = jnp.zeros_like(s_scr)
    ri = lax.broadcasted_iota(jnp.int32, (c, c), 0)
    ci = lax.broadcasted_iota(jnp.int32, (c, c), 1)
    rowi = lax.broadcasted_iota(jnp.int32, (c, GLA_DK), 0)
    levels = _gla_levels(c)
    assert GLA_GROUP == 4

    grp = GLA_GROUP
    gs = range(grp)

    n_trips = seq // (grp * c)

    def decays(n, e_ref):
        allrows = pl.ds(pl.multiple_of(n * (grp * c), grp * c), grp * c)
        x = _dot(sm_ref[allrows, :].astype(BF16), wlr_ref[...]) + blr_ref[...]
        lg = (jnp.minimum(x, 0.0) - jnp.log1p(jnp.exp(-jnp.abs(x)))) * (1.0 / GLA_TAU)
        hi, mid = _split2(lg)
        hm = [jnp.concatenate([hi[g * c:(g + 1) * c], mid[g * c:(g + 1) * c]], axis=0) for g in gs]
        rhs = jnp.concatenate([jnp.concatenate([hm[0], hm[1]], axis=1),
                               jnp.concatenate([hm[2], hm[3]], axis=1)], axis=0)
        e_ref[...] = jnp.exp(_dot(mst_ref[...], rhs))

    def trip(n, e_ref):
        row0 = pl.multiple_of(n * (grp * c), grp * c)
        rows = [pl.ds(row0 + g * c, c) for g in gs]
        ex = e_ref[...]
        nr = ex.shape[0] // 2
        e = [ex[0:nr, 0:GLA_DK], ex[0:nr, GLA_DK:], ex[nr:, 0:GLA_DK], ex[nr:, GLA_DK:]]
        q = [q_ref[rows[g], :] * (GLA_DK ** -0.5) for g in gs]
        k = [k_ref[rows[g], :] for g in gs]
        v = [v_ref[rows[g], :].astype(BF16) for g in gs]
        attn = [jnp.where(ri == ci, _dot_nt(q[g].astype(BF16), k[g].astype(BF16)), 0.0) for g in gs]
        for lvl, h in enumerate(levels):
            upper = (rowi & h) != 0
            sh = int(np.log2(2 * h))
            same = (ri >> sh) == (ci >> sh)
            for g in gs:
                f = e[g][(2 + lvl) * c:(3 + lvl) * c]
                qt = jnp.where(upper, q[g] * f, 0.0).astype(BF16)
                kt = jnp.where(upper, 0.0, k[g] * f).astype(BF16)
                attn[g] = attn[g] + jnp.where(same, _dot_nt(qt, kt), 0.0)
        vk = [_dot_tn(v[g], (k[g] * e[g][c:2 * c]).astype(BF16)) for g in gs]
        av = [_dot(attn[g].astype(BF16), v[g]) for g in gs]
        st = s_scr[...]
        for g in gs:
            o = _dot_nt((q[g] * e[g][0:c]).astype(BF16), st.astype(BF16)) + av[g]
            st = st * e[g][c - 1:c, :] + vk[g]
            on = _rms(o, gn_ref[...]) * _silu(r_ref[rows[g], :].astype(F32))
            o_ref[rows[g], :] = on.astype(o_ref.dtype)
        s_scr[...] = st

    assert n_trips % 2 == 0
    decays(0, e0_scr)

    def body(m, carry):
        decays(2 * m + 1, e1_scr)
        trip(2 * m, e0_scr)
        decays(jnp.minimum(2 * m + 2, n_trips - 1), e0_scr)
        trip(2 * m + 1, e1_scr)
        return carry

    lax.fori_loop(0, n_trips // 2, body, 0)


def _gla(pa, pb, ps, wlr, blr, gnorm, bsz, seq):
    t = bsz * seq
    mst = jnp.asarray(_gla_exponent_matrix(GLA_CHUNK), BF16)
    nq = GLA_QK // GLA_DK
    return pl.pallas_call(
        _gla_kernel,
        grid=(bsz, GLA_HEADS),
        in_specs=[
            pl.BlockSpec((seq, GLA_DK), lambda b, h: (b, h)),
            pl.BlockSpec((seq, GLA_DK), lambda b, h: (b, nq + h)),
            pl.BlockSpec((seq, GLA_DV), lambda b, h: (b, 2 * GLA_QK // GLA_DV + h)),
            pl.BlockSpec((seq, LANES), lambda b, h: (b, 0)),
            pl.BlockSpec((seq, GLA_DV), lambda b, h: (b, _PB_R // GLA_DV + h)),
            pl.BlockSpec((LANES, GLA_DK), lambda b, h: (0, h)),
            pl.BlockSpec((1, GLA_DK), lambda b, h: (0, h)),
            pl.BlockSpec((1, GLA_DV), lambda b, h: (0, 0)),
            pl.BlockSpec(mst.shape, lambda b, h: (0, 0)),
        ],
        out_specs=pl.BlockSpec((seq, GLA_DV), lambda b, h: (b, h)),
        out_shape=jax.ShapeDtypeStruct((t, GLA_V), BF16),
        scratch_shapes=[pltpu.VMEM((GLA_DV, GLA_DK), F32), pltpu.VMEM((mst.shape[0], 2 * GLA_DK), F32),
                        pltpu.VMEM((mst.shape[0], 2 * GLA_DK), F32)],
        compiler_params=_params(("parallel", "parallel")),
        name="gla",
    )(pa, pa, pa, ps, pb, wlr, blr, gnorm, mst)


def _gdn_kernel(q_ref, k_ref, v_ref, sm_ref, z_ref, gn_ref, o_ref, bms, ns, qms, os_, egl, s_scr):
    c = GDN_CHUNK
    seq = q_ref.shape[0]
    head0 = pl.program_id(1) * GDN_HPS

    ri = lax.broadcasted_iota(jnp.int32, (c, c), 0)
    ci = lax.broadcasted_iota(jnp.int32, (c, c), 1)
    lane = lax.broadcasted_iota(jnp.int32, (c, LANES), 1)
    tril = jnp.where(ri >= ci, 1.0, 0.0).astype(BF16)
    causal = ri >= ci
    n_sq = int(np.log2(c))
    lower_left = [(((ri ^ ci) >> (l + 1)) == 0) & ((ri & (1 << l)) != 0) & ((ci & (1 << l)) == 0) for l in range(n_sq)]

    grp = GDN_GROUP
    chains = [(g, j) for g in range(grp) for j in range(GDN_HPS)]

    def phase1(n, carry):
        rows, q, k, kb, gb, rhs0 = {}, {}, {}, {}, {}, {}
        for g in range(grp):
            rows[g] = pl.ds(pl.multiple_of((n * grp + g) * c, c), c)
            sm = sm_ref[rows[g], :]
            for j in range(GDN_HPS):
                sl = slice(j * GDN_DK, (j + 1) * GDN_DK)
                q[g, j] = q_ref[rows[g], sl]
                k[g, j] = k_ref[rows[g], sl]
                gcol = jnp.sum(jnp.where(lane == SM_A + head0 + j, sm, 0.0), axis=-1, keepdims=True)
                beta = jnp.sum(jnp.where(lane == SM_B + head0 + j, sm, 0.0), axis=-1, keepdims=True)
                gb[g, j] = jnp.broadcast_to(gcol, (c, LANES))
                kb[g, j] = k[g, j] * beta
                rhs0[g, j] = v_ref[rows[g], sl] * beta
        cum_r, cum_c, kt = {}, {}, {}
        for ch in chains:
            hi, mid = _split2(gb[ch])
            r = _dot(tril, jnp.concatenate([hi, mid], axis=1))
            cum_r[ch] = r[:, :LANES] + r[:, LANES:]
        for ch in chains:
            cum_c[ch] = cum_r[ch].T
            kt[ch] = k[ch].T
        kk = {}
        for ch in chains:
            k16 = k[ch].astype(BF16)
            kk[ch] = _dot_nt(jnp.concatenate([kb[ch].astype(BF16), q[ch].astype(BF16)], axis=0), k16)
        low, at16 = {}, {}
        for ch in chains:
            dec = jnp.where(causal, jnp.exp(jnp.where(causal, cum_r[ch] - cum_c[ch], 0.0)), 0.0)
            low[ch] = jnp.where(ri > ci, kk[ch][:c] * dec, 0.0)
            at16[ch] = jnp.where(causal, kk[ch][c:] * dec, 0.0).astype(BF16)
        x = {ch: -jnp.where(lower_left[0], low[ch], 0.0) for ch in chains}
        for lvl in range(1, n_sq):
            e = {}
            for ch in chains:
                cm = jnp.where(lower_left[lvl], low[ch], 0.0)
                e[ch] = cm + _mm(x[ch], cm)
            for ch in chains:
                x[ch] = x[ch] - e[ch] - _mm(e[ch], x[ch])
        wu, egc = {}, {}
        for ch in chains:
            egc[ch] = jnp.exp(cum_r[ch])
            rhs = jnp.concatenate([kb[ch] * egc[ch], rhs0[ch]], axis=1)
            wu[ch] = (rhs + _mm(x[ch], rhs)).astype(BF16)
        for ch in chains:
            g, j = ch
            sl = slice(j * GDN_DK, (j + 1) * GDN_DK)
            glast = cum_r[ch][c - 1:c, :]
            kdt16 = (kt[ch] * jnp.exp(glast - cum_c[ch][0:1, :])).astype(BF16)
            r = _dot(jnp.concatenate([kdt16, at16[ch]], axis=0), wu[ch])
            bms[rows[g], sl] = (-r[:c, :GDN_DV]).astype(BF16)
            ns[rows[g], sl] = r[:c, GDN_DV:]
            qms[rows[g], sl] = (q[ch] * egc[ch] - r[c:, :GDN_DV]).astype(BF16)
            os_[rows[g], sl] = r[c:, GDN_DV:]
            egl[pl.ds(pl.multiple_of((n * grp + g) * 8, 8), 8), sl] = jnp.broadcast_to(jnp.exp(glast), (8, LANES))
        return carry

    lax.fori_loop(0, seq // (grp * c), phase1, 0)

    s_scr[...] = jnp.zeros_like(s_scr)

    def phase2(n, carry):
        rows = pl.ds(pl.multiple_of(n * c, c), c)
        sls = [slice(j * GDN_DK, (j + 1) * GDN_DK) for j in range(GDN_HPS)]
        s = [s_scr[j] for j in range(GDN_HPS)]
        s16 = [sj.astype(BF16) for sj in s]
        r = [_dot(jnp.concatenate([bms[rows, sl], qms[rows, sl]], axis=0), s16[j]) for j, sl in enumerate(sls)]
        for j, sl in enumerate(sls):
            eg = egl[pl.ds(pl.multiple_of(n * 8, 8), 8), sl][0:1, :]
            s_scr[j] = s[j] * eg + r[j][:c] + ns[rows, sl]
            o = r[j][c:] + os_[rows, sl]
            on = _rms(o, gn_ref[...]) * _silu(z_ref[rows, sl].astype(F32))
            o_ref[rows, sl] = on.astype(o_ref.dtype)
        return carry

    lax.fori_loop(0, seq // c, phase2, 0)


def _gdn(pg, pb, ps, gnorm, bsz, seq):
    t = bsz * seq
    nb = seq // GDN_CHUNK
    wd = GDN_HPS * GDN_DK
    ng = GDN_HEADS // GDN_HPS
    return pl.pallas_call(
        _gdn_kernel,
        grid=(bsz, ng),
        in_specs=[
            pl.BlockSpec((seq, wd), lambda b, h: (b, h)),
            pl.BlockSpec((seq, wd), lambda b, h: (b, ng + h)),
            pl.BlockSpec((seq, wd), lambda b, h: (b, 2 * ng + h)),
            pl.BlockSpec((seq, LANES), lambda b, h: (b, 0)),
            pl.BlockSpec((seq, wd), lambda b, h: (b, _PB_Z // wd + h)),
            pl.BlockSpec((1, GDN_DV), lambda b, h: (0, 0)),
        ],
        out_specs=pl.BlockSpec((seq, wd), lambda b, h: (b, h)),
        out_shape=jax.ShapeDtypeStruct((t, GDN_V), BF16),
        scratch_shapes=[
            pltpu.VMEM((seq, wd), BF16), pltpu.VMEM((seq, wd), F32), pltpu.VMEM((seq, wd), BF16),
            pltpu.VMEM((seq, wd), F32), pltpu.VMEM((nb * 8, wd), F32),
            pltpu.VMEM((GDN_HPS, GDN_DK, GDN_DV), F32),
        ],
        compiler_params=_params(("parallel", "parallel")),
        name="gdn",
    )(pg, pg, pg, ps, pb, gnorm)


def _xa_kernel(q_ref, mk_ref, mv_ref, o_ref):
    s = _dot_nt(q_ref[...], mk_ref[...]) * (XA_DH ** -0.5)
    p = jnp.exp(s - jnp.max(s, axis=-1, keepdims=True))
    den = jnp.sum(p, axis=-1, keepdims=True)
    o_ref[...] = (_dot(p.astype(BF16), mv_ref[...]) / den).astype(o_ref.dtype)


def _xa(pb, mkv, bsz, seq, ts=1024):
    t = bsz * seq
    nt = seq // ts
    return pl.pallas_call(
        _xa_kernel,
        grid=(bsz, XA_HEADS, nt),
        in_specs=[
            pl.BlockSpec((ts, XA_DH), lambda b, h, i: (b * nt + i, _PB_XQ // XA_DH + h)),
            pl.BlockSpec((MEM_LEN, XA_DH), lambda b, h, i: (b, h)),
            pl.BlockSpec((MEM_LEN, XA_DH), lambda b, h, i: (b, XA_HEADS + h)),
        ],
        out_specs=pl.BlockSpec((ts, XA_DH), lambda b, h, i: (b * nt + i, h)),
        out_shape=jax.ShapeDtypeStruct((t, XA_W), BF16),
        compiler_params=_params(("parallel", "parallel", "parallel")),
        name="xattn",
    )(pb, mkv, mkv)


def _merge_kernel(h_ref, oa_ref, ob_ref, oc_ref, ga_ref, gb_ref, gc_ref, wa_ref, wb_ref, wc_ref, wo_ref,
                  gpost_ref, o_ref):
    mixed = jax.nn.sigmoid(ga_ref[...].astype(F32)) * _dot(oa_ref[...], wa_ref[...])
    mixed = mixed + jax.nn.sigmoid(gb_ref[...].astype(F32)) * _dot(ob_ref[...], wb_ref[...])
    mixed = mixed + jax.nn.sigmoid(gc_ref[...].astype(F32)) * _dot(oc_ref[...], wc_ref[...])
    m = _dot(mixed.astype(BF16), wo_ref[...])
    o_ref[...] = h_ref[...] + _rms(m, gpost_ref[...])


def _merge(h, oa, ob, oc, pb, wa, wb, wc, wo, gpost, tm=256):
    t, d = h.shape
    row = lambda i: (i, 0)
    const = lambda i: (0, 0)
    once = pl.Buffered(1)
    return pl.pallas_call(
        _merge_kernel,
        grid=(t // tm,),
        in_specs=[
            pl.BlockSpec((tm, d), row),
            pl.BlockSpec((tm, GLA_V), row), pl.BlockSpec((tm, GDN_V), row), pl.BlockSpec((tm, XA_W), row),
            pl.BlockSpec((tm, d), lambda i: (i, 0)), pl.BlockSpec((tm, d), lambda i: (i, 1)),
            pl.BlockSpec((tm, d), lambda i: (i, 2)),
            pl.BlockSpec((GLA_V, d), const, pipeline_mode=once), pl.BlockSpec((GDN_V, d), const, pipeline_mode=once),
            pl.BlockSpec((XA_W, d), const, pipeline_mode=once), pl.BlockSpec((d, d), const, pipeline_mode=once),
            pl.BlockSpec((1, d), const),
        ],
        out_specs=pl.BlockSpec((tm, d), row),
        out_shape=jax.ShapeDtypeStruct((t, d), F32),
        compiler_params=_params(("parallel",)),
        name="merge",
    )(h, oa, ob, oc, pb, pb, pb, wa, wb, wc, wo, gpost)


_PA_GDN = 2 * GLA_QK + GLA_V
_PA_COLS = _PA_GDN + 2 * GDN_QK + GDN_V
_PB_R = N_BRANCH * D_MODEL
_PB_Z = _PB_R + GLA_V
_PB_XQ = _PB_Z + GDN_V
_PB_COLS = _PB_XQ + XA_W


_W_ORDER = (0, 1, 2, 5, 10, 4, 8, 9)
_W_SMALL = ((_IN_OFF[3], GLA_RANK), (_IN_OFF[6], 2 * GDN_HEADS))
_W_ALL_COLS = _PA_COLS + _PB_COLS + LANES
_W_RB = 512


def _w_block_rows():
    tab = []
    for i in _W_ORDER:
        assert (_IN_OFF[i + 1] - _IN_OFF[i]) % _W_RB == 0
        tab.extend(range(_IN_OFF[i], _IN_OFF[i + 1], _W_RB))
    return np.asarray(tab, np.int32)


def _relayout_kernel(tab_ref, wt_hbm, o_ref, buf, sem):
    j = pl.program_id(0)
    n_main = pl.num_programs(0) - 1

    def block_copy(blk):
        start = pl.multiple_of(tab_ref[blk], 16)
        return pltpu.make_async_copy(wt_hbm.at[pl.ds(start, _W_RB), :], buf.at[blk % 2], sem.at[blk % 2])

    @pl.when(j == 0)
    def _():
        block_copy(j).start()

    @pl.when(j + 1 < n_main)
    def _():
        block_copy(j + 1).start()

    @pl.when(j < n_main)
    def _():
        block_copy(j).wait()
        o_ref[...] = buf[j % 2].astype(BF16)

    @pl.when(j == n_main)
    def _():
        dst = 0
        copies = []
        for n, (row0, rows) in enumerate(_W_SMALL):
            copies.append(pltpu.make_async_copy(wt_hbm.at[pl.ds(row0, rows), :], buf.at[0, pl.ds(dst, rows), :],
                                                sem.at[n]))
            dst += rows
        for cp in copies:
            cp.start()
        for cp in copies:
            cp.wait()
        o_ref[0:dst, :] = buf[0, 0:dst, :].astype(BF16)
        o_ref[dst:, :] = jnp.zeros((_W_RB - dst, o_ref.shape[1]), BF16)


def _split_w_in(w_in_t):
    n, k = w_in_t.shape
    tab = _w_block_rows()
    return pl.pallas_call(
        _relayout_kernel,
        grid_spec=pltpu.PrefetchScalarGridSpec(
            num_scalar_prefetch=1,
            grid=(len(tab) + 1,),
            in_specs=[pl.BlockSpec(memory_space=pl.ANY)],
            out_specs=pl.BlockSpec((_W_RB, k), lambda i, tab: (i, 0)),
            scratch_shapes=[pltpu.VMEM((2, _W_RB, k), F32), pltpu.SemaphoreType.DMA((2,))],
        ),
        out_shape=jax.ShapeDtypeStruct((_W_ALL_COLS, k), BF16),
        compiler_params=_params(("arbitrary",)),
        name="w_in_relayout",
    )(jnp.asarray(tab), w_in_t)


def kernel(x, mem, n_ffn1_pre, w_ffn1_gu, w_ffn1_down, n_ffn1_post, n_mix_pre, w_in, gla_w_lr2, gla_b_lr,
           gla_norm, gdn_conv, gdn_a_log, gdn_dt_bias, gdn_norm, mem_norm, w_mem_kv, w_up_gla, w_up_gdn,
           w_up_xa, w_out, n_mix_post, n_ffn2_pre, w_ffn2_gu, w_ffn2_down, n_ffn2_post):
    bsz, seq, d = x.shape
    t = bsz * seq
    h = x.reshape(t, d)
    for l in range(n_ffn1_pre.shape[0]):
        row = lambda a: a[l][None, :]
        h, u = _ffn(h, row(n_ffn1_pre), w_ffn1_gu[l], w_ffn1_down[l],
                    row(n_ffn1_post), row(n_mix_pre), emit_next=True)

        w_all = _split_w_in(w_in[l].T)
        par = jnp.zeros((2, LANES), F32)
        par = par.at[0, SM_A:SM_A + GDN_HEADS].set(gdn_a_log[l]).at[1, SM_A:SM_A + GDN_HEADS].set(gdn_dt_bias[l])
        tn_g = 512
        pa = _matmul(u, w_all, 0, _PA_GDN, F32, 2048, 512, "in_proj_a")
        pg = _matmul(u, w_all, _PA_GDN, _PA_COLS - _PA_GDN, F32, seq, tn_g, "in_proj_g", body=_proj_conv_kernel,
                     extra=(gdn_conv[l],), extra_specs=(pl.BlockSpec((CONV_W, tn_g), lambda i, j: (0, j)),))
        pb = _matmul(u, w_all, _PA_COLS, _PB_COLS, BF16, 2048, 1024, "in_proj_b")
        ps = _matmul(u, w_all, _PA_COLS + _PB_COLS, LANES, F32, 2048, LANES, "in_proj_s", body=_proj_s_kernel,
                     extra=(par,), extra_specs=(pl.BlockSpec((2, LANES), lambda i, j: (0, 0)),))

        wlr = jnp.pad(gla_w_lr2[l], ((0, LANES - GLA_RANK), (0, 0))).astype(BF16)
        o_gla = _gla(pa, pb, ps, wlr, row(gla_b_lr), row(gla_norm), bsz, seq)
        o_gdn = _gdn(pg, pb, ps, row(gdn_norm), bsz, seq)

        mkv = _norm_matmul(mem.reshape(bsz * MEM_LEN, d), row(mem_norm), w_mem_kv[l], BF16,
                           bsz * MEM_LEN, 512, "mem_kv")
        o_xa = _xa(pb, mkv, bsz, seq)

        h = _merge(h, o_gla, o_gdn, o_xa, pb, w_up_gla[l].astype(BF16), w_up_gdn[l].astype(BF16),
                   w_up_xa[l].astype(BF16), w_out[l].astype(BF16), row(n_mix_post))
        h = _ffn(h, row(n_ffn2_pre), w_ffn2_gu[l], w_ffn2_down[l],
                 row(n_ffn2_post), row(n_ffn2_post), emit_next=False)
    return h.reshape(bsz, seq, d)
```

```python
import functools

import numpy as np
import jax
import jax.numpy as jnp
from jax import lax
from jax.experimental import pallas as pl
from jax.experimental.pallas import tpu as pltpu

F32 = jnp.float32
BF16 = jnp.bfloat16

D_MODEL = 2048
MEM_LEN = 256
EPS = 1e-6
GLA_HEADS = 4
GLA_DK = 128
GLA_DV = 256
GLA_QK = GLA_HEADS * GLA_DK
GLA_V = GLA_HEADS * GLA_DV
GLA_RANK = 16
GLA_TAU = 16.0
GDN_HEADS = 8
GDN_DK = 128
GDN_DV = 128
GDN_QK = GDN_HEADS * GDN_DK
GDN_V = GDN_HEADS * GDN_DV
CONV_W = 4
XA_HEADS = 4
XA_DH = 256
XA_W = XA_HEADS * XA_DH
N_BRANCH = 3
D_FF = 5632

LANES = 128
GLA_CHUNK = 64
GLA_GROUP = 4
GDN_CHUNK = 128
GDN_HPS = 4
GDN_GROUP = 2
VMEM_LIMIT = 56 * 1024 * 1024

_IN_SIZES = (GLA_QK, GLA_QK, GLA_V, GLA_RANK, GLA_V, 2 * GDN_QK + GDN_V, GDN_HEADS, GDN_HEADS, GDN_V,
             XA_W, N_BRANCH * D_MODEL)
_IN_OFF = tuple(int(v) for v in np.cumsum((0,) + _IN_SIZES))
SM_LR, SM_B, SM_A = 0, GLA_RANK, GLA_RANK + GDN_HEADS


def _dot(a, b):
    return jnp.dot(a, b, preferred_element_type=F32)


def _dot_nt(a, b):
    return lax.dot_general(a, b, (((1,), (1,)), ((), ())), preferred_element_type=F32)


def _dot_tn(a, b):
    return lax.dot_general(a, b, (((0,), (0,)), ((), ())), preferred_element_type=F32)


def _mm(a, b):
    return _dot(a.astype(BF16), b.astype(BF16))


def _split2(x):
    hi = x.astype(BF16)
    mid = (x - hi.astype(F32)).astype(BF16)
    return hi, mid


def _dot01(m01, x):
    hi, mid = _split2(x)
    return _dot(m01, hi) + _dot(m01, mid)


def _rms(x, gain):
    return x * lax.rsqrt(jnp.mean(x * x, axis=-1, keepdims=True) + EPS) * gain


def _silu(x):
    return x * jax.nn.sigmoid(x)


def _softplus(x):
    return jnp.maximum(x, 0.0) + jnp.log1p(jnp.exp(-jnp.abs(x)))


def _params(sem):
    return pltpu.CompilerParams(dimension_semantics=sem, vmem_limit_bytes=VMEM_LIMIT)


def _ffn_kernel(x_ref, gpre_ref, wgu_hbm, wd_hbm, gpost_ref, gnext_ref, *out_and_scratch, emit_next, tf):
    if emit_next:
        h_ref, un_ref, xn_scr, wg_buf, wu_buf, wd_buf, wg16, wu16, wd16, sem = out_and_scratch
    else:
        h_ref, xn_scr, wg_buf, wu_buf, wd_buf, wg16, wu16, wd16, sem = out_and_scratch
    i = pl.program_id(0)
    nf = D_FF // tf
    assert nf % 2 == 0

    def tile_copies(jj, slot):
        col = pl.multiple_of(jj * tf, tf)
        return (pltpu.make_async_copy(wgu_hbm.at[:, pl.ds(col, tf)], wg_buf.at[slot], sem.at[0, slot]),
                pltpu.make_async_copy(wgu_hbm.at[:, pl.ds(D_FF + col, tf)], wu_buf.at[slot], sem.at[1, slot]),
                pltpu.make_async_copy(wd_hbm.at[pl.ds(col, tf), :], wd_buf.at[slot], sem.at[2, slot]))

    @pl.when(i == 0)
    def _():
        for cp in tile_copies(0, 0):
            cp.start()

    xn_scr[...] = _rms(x_ref[...], gpre_ref[...]).astype(BF16)
    h_ref[...] = jnp.zeros_like(h_ref)

    def body(jj, carry):
        slot = jj % 2
        @pl.when(jj + 1 < nf)
        def _():
            for cp in tile_copies(jj + 1, 1 - slot):
                cp.start()

        @pl.when((jj + 1 == nf) & (i + 1 < pl.num_programs(0)))
        def _():
            for cp in tile_copies(0, 0):
                cp.start()

        for cp in tile_copies(jj, slot):
            cp.wait()
        wg16[...] = wg_buf[slot].astype(BF16)
        wu16[...] = wu_buf[slot].astype(BF16)
        wd16[...] = wd_buf[slot].astype(BF16)
        xn = xn_scr[...]
        g = _dot(xn, wg16[...])
        u = _dot(xn, wu16[...])
        hm = (_silu(g) * u).astype(BF16)
        h_ref[...] += _dot(hm, wd16[...])
        return carry

    lax.fori_loop(0, nf, body, 0)

    h = x_ref[...] + 0.5 * _rms(h_ref[...], gpost_ref[...])
    h_ref[...] = h
    if emit_next:
        un_ref[...] = _rms(h, gnext_ref[...]).astype(BF16)


def _ffn(x, gpre, w_gu, w_down, gpost, gnext, emit_next, tm=1024, tf=256):
    t, d = x.shape
    row = lambda i: (i, 0)
    const = lambda i: (0, 0)
    once = pl.Buffered(1)
    out_shape = [jax.ShapeDtypeStruct((t, d), F32)]
    out_specs = [pl.BlockSpec((tm, d), row, pipeline_mode=once)]
    if emit_next:
        out_shape.append(jax.ShapeDtypeStruct((t, d), BF16))
        out_specs.append(pl.BlockSpec((tm, d), row, pipeline_mode=once))
    res = pl.pallas_call(
        functools.partial(_ffn_kernel, emit_next=emit_next, tf=tf),
        grid=(t // tm,),
        in_specs=[
            pl.BlockSpec((tm, d), row),
            pl.BlockSpec((1, d), const),
            pl.BlockSpec(memory_space=pl.ANY),
            pl.BlockSpec(memory_space=pl.ANY),
            pl.BlockSpec((1, d), const),
            pl.BlockSpec((1, d), const),
        ],
        out_specs=out_specs,
        out_shape=out_shape,
        scratch_shapes=[pltpu.VMEM((tm, d), BF16),
                        pltpu.VMEM((2, d, tf), F32), pltpu.VMEM((2, d, tf), F32), pltpu.VMEM((2, tf, d), F32),
                        pltpu.VMEM((d, tf), BF16), pltpu.VMEM((d, tf), BF16), pltpu.VMEM((tf, d), BF16),
                        pltpu.SemaphoreType.DMA((3, 2))],
        compiler_params=_params(("arbitrary",)),
        name="ffn",
    )(x, gpre, w_gu, w_down, gpost, gnext)
    return res if emit_next else res[0]


def _matmul_kernel(a_ref, w_ref, o_ref):
    o_ref[...] = _dot_nt(a_ref[...], w_ref[...]).astype(o_ref.dtype)


def _proj_conv_kernel(a_ref, w_ref, cw_ref, o_ref):
    tm, tn = o_ref.shape
    rb = 256
    kind = pl.program_id(1) // (GDN_QK // tn)
    scale = jnp.where(kind == 0, GDN_DK ** -0.5, 1.0)
    w = cw_ref[...]
    r8 = lax.broadcasted_iota(jnp.int32, (8, tn), 0)

    sb = 32

    def finish(r, before):
        for s0 in range(0, rb, sb):
            y = o_ref[r + s0:r + s0 + sb, :]
            acc = y * w[CONV_W - 1:CONV_W, :]
            for sft in range(1, CONV_W):
                xs = pltpu.roll(y, sft, 0)
                top = jnp.where(r8 < sft, pltpu.roll(before, sft, 0), xs[0:8])
                acc = acc + jnp.concatenate([top, xs[8:]], axis=0) * w[CONV_W - 1 - sft:CONV_W - sft, :]
            before = y[sb - 8:sb]
            c = _silu(acc)
            for hh in range(tn // GDN_DK):
                sl = slice(hh * GDN_DK, (hh + 1) * GDN_DK)
                blk = c[:, sl]
                f = lax.rsqrt(jnp.sum(blk * blk, axis=-1, keepdims=True) + EPS) * scale
                o_ref[r + s0:r + s0 + sb, sl] = blk * jnp.where(kind == 2, 1.0, f)
        return before

    o_ref[0:rb, :] = _dot_nt(a_ref[0:rb, :], w_ref[...])
    before = jnp.zeros((8, tn), F32)
    for r in range(rb, tm + rb, rb):
        if r < tm:
            o_ref[r:r + rb, :] = _dot_nt(a_ref[r:r + rb, :], w_ref[...])
        before = finish(r - rb, before)


def _proj_s_kernel(a_ref, w_ref, par_ref, o_ref):
    y = _dot_nt(a_ref[...], w_ref[...])
    par = par_ref[...]
    lane = lax.broadcasted_iota(jnp.int32, y.shape, 1)
    g = -jnp.exp(par[0:1, :]) * _softplus(y + par[1:2, :])
    is_g = (lane >= SM_A) & (lane < SM_A + GDN_HEADS)
    is_b = (lane >= SM_B) & (lane < SM_B + GDN_HEADS)
    o_ref[...] = jnp.where(is_g, g, jnp.where(is_b, jax.nn.sigmoid(y), y))


def _matmul(a, w, col0, n, out_dtype, tm, tn, name, body=_matmul_kernel, extra=(), extra_specs=()):
    m, k = a.shape
    c0 = col0 // tn
    return pl.pallas_call(
        body,
        grid=(m // tm, n // tn),
        in_specs=[pl.BlockSpec((tm, k), lambda i, j: (i, 0)), pl.BlockSpec((tn, k), lambda i, j: (c0 + j, 0)),
                  *extra_specs],
        out_specs=pl.BlockSpec((tm, tn), lambda i, j: (i, j)),
        out_shape=jax.ShapeDtypeStruct((m, n), out_dtype),
        compiler_params=_params(("parallel", "parallel")),
        name=name,
    )(a, w, *extra)


def _norm_matmul_kernel(x_ref, g_ref, w_ref, o_ref, xn_scr):
    @pl.when(pl.program_id(1) == 0)
    def _():
        xn_scr[...] = _rms(x_ref[...], g_ref[...]).astype(BF16)

    o_ref[...] = _dot(xn_scr[...], w_ref[...].astype(BF16)).astype(o_ref.dtype)


def _norm_matmul(x, gain, w, out_dtype, tm, tn, name):
    m, k = x.shape
    n = w.shape[1]
    return pl.pallas_call(
        _norm_matmul_kernel,
        grid=(m // tm, n // tn),
        in_specs=[pl.BlockSpec((tm, k), lambda i, j: (i, 0)), pl.BlockSpec((1, k), lambda i, j: (0, 0)),
                  pl.BlockSpec((k, tn), lambda i, j: (0, j))],
        out_specs=pl.BlockSpec((tm, tn), lambda i, j: (i, j)),
        out_shape=jax.ShapeDtypeStruct((m, n), out_dtype),
        scratch_shapes=[pltpu.VMEM((tm, k), BF16)],
        compiler_params=_params(("parallel", "arbitrary")),
        name=name,
    )(x, gain, w)


def _gla_levels(chunk):
    return [chunk >> (i + 1) for i in range(int(np.log2(chunk)))]


def _gla_exponent_matrix(chunk):
    c = chunk
    i = np.arange(c)[:, None]
    t = np.arange(c)[None, :]
    blocks = [(t <= i), (t > i)]
    for h in _gla_levels(c):
        r = (i // (2 * h)) * (2 * h) + h
        upper = i >= r
        blocks.append(np.where(upper, (t > r) & (t <= i), (t > i) & (t <= r)))
    m = np.concatenate(blocks, axis=0).astype(np.float32)
    z = np.zeros_like(m)
    return np.concatenate([np.concatenate([m, m, z, z], axis=1), np.concatenate([z, z, m, m], axis=1)], axis=0)


def _gla_kernel(q_ref, k_ref, v_ref, sm_ref, r_ref, wlr_ref, blr_ref, gn_ref, mst_ref, o_ref, s_scr, e0_scr,
                e1_scr):
    c = GLA_CHUNK
    seq = q_ref.shape[0]
    s_scr[...] = jnp.zeros_like(s_scr)
    ri = lax.broadcasted_iota(jnp.int32, (c, c), 0)
    ci = lax.broadcasted_iota(jnp.int32, (c, c), 1)
    rowi = lax.broadcasted_iota(jnp.int32, (c, GLA_DK), 0)
    levels = _gla_levels(c)
    assert GLA_GROUP == 4

    grp = GLA_GROUP
    gs = range(grp)

    n_trips = seq // (grp * c)

    def decays(n, e_ref):
        allrows = pl.ds(pl.multiple_of(n * (grp * c), grp * c), grp * c)
        x = _dot(sm_ref[allrows, :].astype(BF16), wlr_ref[...]) + blr_ref[...]
        lg = (jnp.minimum(x, 0.0) - jnp.log1p(jnp.exp(-jnp.abs(x)))) * (1.0 / GLA_TAU)
        hi, mid = _split2(lg)
        hm = [jnp.concatenate([hi[g * c:(g + 1) * c], mid[g * c:(g + 1) * c]], axis=0) for g in gs]
        rhs = jnp.concatenate([jnp.concatenate([hm[0], hm[1]], axis=1),
                               jnp.concatenate([hm[2], hm[3]], axis=1)], axis=0)
        e_ref[...] = jnp.exp(_dot(mst_ref[...], rhs))

    def trip(n, e_ref):
        row0 = pl.multiple_of(n * (grp * c), grp * c)
        rows = [pl.ds(row0 + g * c, c) for g in gs]
        ex = e_ref[...]
        nr = ex.shape[0] // 2
        e = [ex[0:nr, 0:GLA_DK], ex[0:nr, GLA_DK:], ex[nr:, 0:GLA_DK], ex[nr:, GLA_DK:]]
        q = [q_ref[rows[g], :] * (GLA_DK ** -0.5) for g in gs]
        k = [k_ref[rows[g], :] for g in gs]
        v = [v_ref[rows[g], :].astype(BF16) for g in gs]
        attn = [jnp.where(ri == ci, _dot_nt(q[g].astype(BF16), k[g].astype(BF16)), 0.0) for g in gs]
        for lvl, h in enumerate(levels):
            upper = (rowi & h) != 0
            sh = int(np.log2(2 * h))
            same = (ri >> sh) == (ci >> sh)
            for g in gs:
                f = e[g][(2 + lvl) * c:(3 + lvl) * c]
                qt = jnp.where(upper, q[g] * f, 0.0).astype(BF16)
                kt = jnp.where(upper, 0.0, k[g] * f).astype(BF16)
                attn[g] = attn[g] + jnp.where(same, _dot_nt(qt, kt), 0.0)
        vk = [_dot_tn(v[g], (k[g] * e[g][c:2 * c]).astype(BF16)) for g in gs]
        av = [_dot(attn[g].astype(BF16), v[g]) for g in gs]
        st = s_scr[...]
        for g in gs:
            o = _dot_nt((q[g] * e[g][0:c]).astype(BF16), st.astype(BF16)) + av[g]
            st = st * e[g][c - 1:c, :] + vk[g]
            on = _rms(o, gn_ref[...]) * _silu(r_ref[rows[g], :].astype(F32))
            o_ref[rows[g], :] = on.astype(o_ref.dtype)
        s_scr[...] = st

    assert n_trips % 2 == 0
    decays(0, e0_scr)

    def body(m, carry):
        decays(2 * m + 1, e1_scr)
        trip(2 * m, e0_scr)
        decays(jnp.minimum(2 * m + 2, n_trips - 1), e0_scr)
        trip(2 * m + 1, e1_scr)
        return carry

    lax.fori_loop(0, n_trips // 2, body, 0)


def _gla(pa, pb, ps, wlr, blr, gnorm, bsz, seq):
    t = bsz * seq
    mst = jnp.asarray(_gla_exponent_matrix(GLA_CHUNK), BF16)
    nq = GLA_QK // GLA_DK
    return pl.pallas_call(
        _gla_kernel,
        grid=(bsz, GLA_HEADS),
        in_specs=[
            pl.BlockSpec((seq, GLA_DK), lambda b, h: (b, h)),
            pl.BlockSpec((seq, GLA_DK), lambda b, h: (b, nq + h)),
            pl.BlockSpec((seq, GLA_DV), lambda b, h: (b, 2 * GLA_QK // GLA_DV + h)),
            pl.BlockSpec((seq, LANES), lambda b, h: (b, 0)),
            pl.BlockSpec((seq, GLA_DV), lambda b, h: (b, _PB_R // GLA_DV + h)),
            pl.BlockSpec((LANES, GLA_DK), lambda b, h: (0, h)),
            pl.BlockSpec((1, GLA_DK), lambda b, h: (0, h)),
            pl.BlockSpec((1, GLA_DV), lambda b, h: (0, 0)),
            pl.BlockSpec(mst.shape, lambda b, h: (0, 0)),
        ],
        out_specs=pl.BlockSpec((seq, GLA_DV), lambda b, h: (b, h)),
        out_shape=jax.ShapeDtypeStruct((t, GLA_V), BF16),
        scratch_shapes=[pltpu.VMEM((GLA_DV, GLA_DK), F32), pltpu.VMEM((mst.shape[0], 2 * GLA_DK), F32),
                        pltpu.VMEM((mst.shape[0], 2 * GLA_DK), F32)],
        compiler_params=_params(("parallel", "parallel")),
        name="gla",
    )(pa, pa, pa, ps, pb, wlr, blr, gnorm, mst)


def _gdn_kernel(q_ref, k_ref, v_ref, sm_ref, z_ref, gn_ref, o_ref, bms, ns, qms, os_, egl, s_scr):
    c = GDN_CHUNK
    seq = q_ref.shape[0]
    head0 = pl.program_id(1) * GDN_HPS

    ri = lax.broadcasted_iota(jnp.int32, (c, c), 0)
    ci = lax.broadcasted_iota(jnp.int32, (c, c), 1)
    lane = lax.broadcasted_iota(jnp.int32, (c, LANES), 1)
    tril = jnp.where(ri >= ci, 1.0, 0.0).astype(BF16)
    causal = ri >= ci
    n_sq = int(np.log2(c))
    lower_left = [(((ri ^ ci) >> (l + 1)) == 0) & ((ri & (1 << l)) != 0) & ((ci & (1 << l)) == 0) for l in range(n_sq)]

    grp = GDN_GROUP
    chains = [(g, j) for g in range(grp) for j in range(GDN_HPS)]

    def phase1(n, hooks):
        rows, q, k, kb, gb, rhs0 = {}, {}, {}, {}, {}, {}
        for g in range(grp):
            rows[g] = pl.ds(pl.multiple_of((n * grp + g) * c, c), c)
            sm = sm_ref[rows[g], :]
            for j in range(GDN_HPS):
                sl = slice(j * GDN_DK, (j + 1) * GDN_DK)
                q[g, j] = q_ref[rows[g], sl]
                k[g, j] = k_ref[rows[g], sl]
                gcol = jnp.sum(jnp.where(lane == SM_A + head0 + j, sm, 0.0), axis=-1, keepdims=True)
                beta = jnp.sum(jnp.where(lane == SM_B + head0 + j, sm, 0.0), axis=-1, keepdims=True)
                gb[g, j] = jnp.broadcast_to(gcol, (c, LANES))
                kb[g, j] = k[g, j] * beta
                rhs0[g, j] = v_ref[rows[g], sl] * beta
        cum_r, cum_c, kt = {}, {}, {}
        for ch in chains:
            hi, mid = _split2(gb[ch])
            r = _dot(tril, jnp.concatenate([hi, mid], axis=1))
            cum_r[ch] = r[:, :LANES] + r[:, LANES:]
        for ch in chains:
            cum_c[ch] = cum_r[ch].T
            kt[ch] = k[ch].T
        kk = {}
        for ch in chains:
            k16 = k[ch].astype(BF16)
            kk[ch] = _dot_nt(jnp.concatenate([kb[ch].astype(BF16), q[ch].astype(BF16)], axis=0), k16)
        low, at16 = {}, {}
        for ch in chains:
            dec = jnp.where(causal, jnp.exp(jnp.where(causal, cum_r[ch] - cum_c[ch], 0.0)), 0.0)
            low[ch] = jnp.where(ri > ci, kk[ch][:c] * dec, 0.0)
            at16[ch] = jnp.where(causal, kk[ch][c:] * dec, 0.0).astype(BF16)
        x = {ch: -jnp.where(lower_left[0], low[ch], 0.0) for ch in chains}
        for lvl in range(1, n_sq):
            e = {}
            for ch in chains:
                cm = jnp.where(lower_left[lvl], low[ch], 0.0)
                e[ch] = cm + _mm(x[ch], cm)
            for ch in chains:
                x[ch] = x[ch] - e[ch] - _mm(e[ch], x[ch])
            if lvl in hooks:
                hooks[lvl]()
        wu, egc = {}, {}
        for ch in chains:
            egc[ch] = jnp.exp(cum_r[ch])
            rhs = jnp.concatenate([kb[ch] * egc[ch], rhs0[ch]], axis=1)
            wu[ch] = (rhs + _mm(x[ch], rhs)).astype(BF16)
        for ch in chains:
            g, j = ch
            sl = slice(j * GDN_DK, (j + 1) * GDN_DK)
            glast = cum_r[ch][c - 1:c, :]
            kdt16 = (kt[ch] * jnp.exp(glast - cum_c[ch][0:1, :])).astype(BF16)
            r = _dot(jnp.concatenate([kdt16, at16[ch]], axis=0), wu[ch])
            bms[rows[g], sl] = (-r[:c, :GDN_DV]).astype(BF16)
            ns[rows[g], sl] = r[:c, GDN_DV:]
            qms[rows[g], sl] = (q[ch] * egc[ch] - r[c:, :GDN_DV]).astype(BF16)
            os_[rows[g], sl] = r[c:, GDN_DV:]
            egl[pl.ds(pl.multiple_of((n * grp + g) * 8, 8), 8), sl] = jnp.broadcast_to(jnp.exp(glast), (8, LANES))

    s_scr[...] = jnp.zeros_like(s_scr)

    def phase2(n):
        rows = pl.ds(pl.multiple_of(n * c, c), c)
        sls = [slice(j * GDN_DK, (j + 1) * GDN_DK) for j in range(GDN_HPS)]
        s = [s_scr[j] for j in range(GDN_HPS)]
        s16 = [sj.astype(BF16) for sj in s]
        r = [_dot(jnp.concatenate([bms[rows, sl], qms[rows, sl]], axis=0), s16[j]) for j, sl in enumerate(sls)]
        for j, sl in enumerate(sls):
            eg = egl[pl.ds(pl.multiple_of(n * 8, 8), 8), sl][0:1, :]
            s_scr[j] = s[j] * eg + r[j][:c] + ns[rows, sl]
            o = r[j][c:] + os_[rows, sl]
            on = _rms(o, gn_ref[...]) * _silu(z_ref[rows, sl].astype(F32))
            o_ref[rows, sl] = on.astype(o_ref.dtype)

    assert grp == 2 and n_sq >= 5
    n_trips = seq // (grp * c)
    phase1(0, {})

    def body(n, carry):
        first = (n - 1) * grp
        phase1(n, {1: lambda: phase2(first), n_sq - 2: lambda: phase2(first + 1)})
        return carry

    lax.fori_loop(1, n_trips, body, 0)
    for g in range(grp):
        phase2((n_trips - 1) * grp + g)


def _gdn(pg, pb, ps, gnorm, bsz, seq):
    t = bsz * seq
    nb = seq // GDN_CHUNK
    wd = GDN_HPS * GDN_DK
    ng = GDN_HEADS // GDN_HPS
    return pl.pallas_call(
        _gdn_kernel,
        grid=(bsz, ng),
        in_specs=[
            pl.BlockSpec((seq, wd), lambda b, h: (b, h)),
            pl.BlockSpec((seq, wd), lambda b, h: (b, ng + h)),
            pl.BlockSpec((seq, wd), lambda b, h: (b, 2 * ng + h)),
            pl.BlockSpec((seq, LANES), lambda b, h: (b, 0)),
            pl.BlockSpec((seq, wd), lambda b, h: (b, _PB_Z // wd + h)),
            pl.BlockSpec((1, GDN_DV), lambda b, h: (0, 0)),
        ],
        out_specs=pl.BlockSpec((seq, wd), lambda b, h: (b, h)),
        out_shape=jax.ShapeDtypeStruct((t, GDN_V), BF16),
        scratch_shapes=[
            pltpu.VMEM((seq, wd), BF16), pltpu.VMEM((seq, wd), F32), pltpu.VMEM((seq, wd), BF16),
            pltpu.VMEM((seq, wd), F32), pltpu.VMEM((nb * 8, wd), F32),
            pltpu.VMEM((GDN_HPS, GDN_DK, GDN_DV), F32),
        ],
        compiler_params=_params(("parallel", "parallel")),
        name="gdn",
    )(pg, pg, pg, ps, pb, gnorm)


def _xa_kernel(q_ref, mk_ref, mv_ref, o_ref):
    s = _dot_nt(q_ref[...], mk_ref[...]) * (XA_DH ** -0.5)
    p = jnp.exp(s - jnp.max(s, axis=-1, keepdims=True))
    den = jnp.sum(p, axis=-1, keepdims=True)
    o_ref[...] = (_dot(p.astype(BF16), mv_ref[...]) / den).astype(o_ref.dtype)


def _xa(pb, mkv, bsz, seq, ts=1024):
    t = bsz * seq
    nt = seq // ts
    return pl.pallas_call(
        _xa_kernel,
        grid=(bsz, XA_HEADS, nt),
        in_specs=[
            pl.BlockSpec((ts, XA_DH), lambda b, h, i: (b * nt + i, _PB_XQ // XA_DH + h)),
            pl.BlockSpec((MEM_LEN, XA_DH), lambda b, h, i: (b, h)),
            pl.BlockSpec((MEM_LEN, XA_DH), lambda b, h, i: (b, XA_HEADS + h)),
        ],
        out_specs=pl.BlockSpec((ts, XA_DH), lambda b, h, i: (b * nt + i, h)),
        out_shape=jax.ShapeDtypeStruct((t, XA_W), BF16),
        compiler_params=_params(("parallel", "parallel", "parallel")),
        name="xattn",
    )(pb, mkv, mkv)


def _merge_kernel(h_ref, oa_ref, ob_ref, oc_ref, ga_ref, gb_ref, gc_ref, wa_ref, wb_ref, wc_ref, wo_ref,
                  gpost_ref, o_ref):
    mixed = jax.nn.sigmoid(ga_ref[...].astype(F32)) * _dot(oa_ref[...], wa_ref[...])
    mixed = mixed + jax.nn.sigmoid(gb_ref[...].astype(F32)) * _dot(ob_ref[...], wb_ref[...])
    mixed = mixed + jax.nn.sigmoid(gc_ref[...].astype(F32)) * _dot(oc_ref[...], wc_ref[...])
    m = _dot(mixed.astype(BF16), wo_ref[...])
    o_ref[...] = h_ref[...] + _rms(m, gpost_ref[...])


def _merge(h, oa, ob, oc, pb, wa, wb, wc, wo, gpost, tm=256):
    t, d = h.shape
    row = lambda i: (i, 0)
    const = lambda i: (0, 0)
    once = pl.Buffered(1)
    return pl.pallas_call(
        _merge_kernel,
        grid=(t // tm,),
        in_specs=[
            pl.BlockSpec((tm, d), row),
            pl.BlockSpec((tm, GLA_V), row), pl.BlockSpec((tm, GDN_V), row), pl.BlockSpec((tm, XA_W), row),
            pl.BlockSpec((tm, d), lambda i: (i, 0)), pl.BlockSpec((tm, d), lambda i: (i, 1)),
            pl.BlockSpec((tm, d), lambda i: (i, 2)),
            pl.BlockSpec((GLA_V, d), const, pipeline_mode=once), pl.BlockSpec((GDN_V, d), const, pipeline_mode=once),
            pl.BlockSpec((XA_W, d), const, pipeline_mode=once), pl.BlockSpec((d, d), const, pipeline_mode=once),
            pl.BlockSpec((1, d), const),
        ],
        out_specs=pl.BlockSpec((tm, d), row),
        out_shape=jax.ShapeDtypeStruct((t, d), F32),
        compiler_params=_params(("parallel",)),
        name="merge",
    )(h, oa, ob, oc, pb, pb, pb, wa, wb, wc, wo, gpost)


_PA_GDN = 2 * GLA_QK + GLA_V
_PA_COLS = _PA_GDN + 2 * GDN_QK + GDN_V
_PB_R = N_BRANCH * D_MODEL
_PB_Z = _PB_R + GLA_V
_PB_XQ = _PB_Z + GDN_V
_PB_COLS = _PB_XQ + XA_W


_W_ORDER = (0, 1, 2, 5, 10, 4, 8, 9)
_W_SMALL = ((_IN_OFF[3], GLA_RANK), (_IN_OFF[6], 2 * GDN_HEADS))
_W_ALL_COLS = _PA_COLS + _PB_COLS + LANES
_W_RB = 512


def _w_block_rows():
    tab = []
    for i in _W_ORDER:
        assert (_IN_OFF[i + 1] - _IN_OFF[i]) % _W_RB == 0
        tab.extend(range(_IN_OFF[i], _IN_OFF[i + 1], _W_RB))
    return np.asarray(tab, np.int32)


def _relayout_kernel(tab_ref, wt_hbm, o_ref, buf, sem):
    j = pl.program_id(0)
    n_main = pl.num_programs(0) - 1

    def block_copy(blk):
        start = pl.multiple_of(tab_ref[blk], 16)
        return pltpu.make_async_copy(wt_hbm.at[pl.ds(start, _W_RB), :], buf.at[blk % 2], sem.at[blk % 2])

    @pl.when(j == 0)
    def _():
        block_copy(j).start()

    @pl.when(j + 1 < n_main)
    def _():
        block_copy(j + 1).start()

    @pl.when(j < n_main)
    def _():
        block_copy(j).wait()
        o_ref[...] = buf[j % 2].astype(BF16)

    @pl.when(j == n_main)
    def _():
        dst = 0
        copies = []
        for n, (row0, rows) in enumerate(_W_SMALL):
            copies.append(pltpu.make_async_copy(wt_hbm.at[pl.ds(row0, rows), :], buf.at[0, pl.ds(dst, rows), :],
                                                sem.at[n]))
            dst += rows
        for cp in copies:
            cp.start()
        for cp in copies:
            cp.wait()
        o_ref[0:dst, :] = buf[0, 0:dst, :].astype(BF16)
        o_ref[dst:, :] = jnp.zeros((_W_RB - dst, o_ref.shape[1]), BF16)


def _split_w_in(w_in_t):
    n, k = w_in_t.shape
    tab = _w_block_rows()
    return pl.pallas_call(
        _relayout_kernel,
        grid_spec=pltpu.PrefetchScalarGridSpec(
            num_scalar_prefetch=1,
            grid=(len(tab) + 1,),
            in_specs=[pl.BlockSpec(memory_space=pl.ANY)],
            out_specs=pl.BlockSpec((_W_RB, k), lambda i, tab: (i, 0)),
            scratch_shapes=[pltpu.VMEM((2, _W_RB, k), F32), pltpu.SemaphoreType.DMA((2,))],
        ),
        out_shape=jax.ShapeDtypeStruct((_W_ALL_COLS, k), BF16),
        compiler_params=_params(("arbitrary",)),
        name="w_in_relayout",
    )(jnp.asarray(tab), w_in_t)


def kernel(x, mem, n_ffn1_pre, w_ffn1_gu, w_ffn1_down, n_ffn1_post, n_mix_pre, w_in, gla_w_lr2, gla_b_lr,
           gla_norm, gdn_conv, gdn_a_log, gdn_dt_bias, gdn_norm, mem_norm, w_mem_kv, w_up_gla, w_up_gdn,
           w_up_xa, w_out, n_mix_post, n_ffn2_pre, w_ffn2_gu, w_ffn2_down, n_ffn2_post):
    bsz, seq, d = x.shape
    t = bsz * seq
    h = x.reshape(t, d)
    for l in range(n_ffn1_pre.shape[0]):
        row = lambda a: a[l][None, :]
        h, u = _ffn(h, row(n_ffn1_pre), w_ffn1_gu[l], w_ffn1_down[l],
                    row(n_ffn1_post), row(n_mix_pre), emit_next=True)

        w_all = _split_w_in(w_in[l].T)
        par = jnp.zeros((2, LANES), F32)
        par = par.at[0, SM_A:SM_A + GDN_HEADS].set(gdn_a_log[l]).at[1, SM_A:SM_A + GDN_HEADS].set(gdn_dt_bias[l])
        tn_g = 512
        pa = _matmul(u, w_all, 0, _PA_GDN, F32, 2048, 512, "in_proj_a")
        pg = _matmul(u, w_all, _PA_GDN, _PA_COLS - _PA_GDN, F32, seq, tn_g, "in_proj_g", body=_proj_conv_kernel,
                     extra=(gdn_conv[l],), extra_specs=(pl.BlockSpec((CONV_W, tn_g), lambda i, j: (0, j)),))
        pb = _matmul(u, w_all, _PA_COLS, _PB_COLS, BF16, 2048, 1024, "in_proj_b")
        ps = _matmul(u, w_all, _PA_COLS + _PB_COLS, LANES, F32, 2048, LANES, "in_proj_s", body=_proj_s_kernel,
                     extra=(par,), extra_specs=(pl.BlockSpec((2, LANES), lambda i, j: (0, 0)),))

        wlr = jnp.pad(gla_w_lr2[l], ((0, LANES - GLA_RANK), (0, 0))).astype(BF16)
        o_gla = _gla(pa, pb, ps, wlr, row(gla_b_lr), row(gla_norm), bsz, seq)
        o_gdn = _gdn(pg, pb, ps, row(gdn_norm), bsz, seq)

        mkv = _norm_matmul(mem.reshape(bsz * MEM_LEN, d), row(mem_norm), w_mem_kv[l], BF16,
                           bsz * MEM_LEN, 512, "mem_kv")
        o_xa = _xa(pb, mkv, bsz, seq)

        h = _merge(h, o_gla, o_gdn, o_xa, pb, w_up_gla[l].astype(BF16), w_up_gdn[l].astype(BF16),
                   w_up_xa[l].astype(BF16), w_out[l].astype(BF16), row(n_mix_post))
        h = _ffn(h, row(n_ffn2_pre), w_ffn2_gu[l], w_ffn2_down[l],
                 row(n_ffn2_post), row(n_ffn2_post), emit_next=False)
    return h.reshape(bsz, seq, d)
```

```python
import functools

import numpy as np
import jax
import jax.numpy as jnp
from jax import lax
from jax.experimental import pallas as pl
from jax.experimental.pallas import tpu as pltpu

F32 = jnp.float32
BF16 = jnp.bfloat16

D_MODEL = 2048
MEM_LEN = 256
EPS = 1e-6
GLA_HEADS = 4
GLA_DK = 128
GLA_DV = 256
GLA_QK = GLA_HEADS * GLA_DK
GLA_V = GLA_HEADS * GLA_DV
GLA_RANK = 16
GLA_TAU = 16.0
GDN_HEADS = 8
GDN_DK = 128
GDN_DV = 128
GDN_QK = GDN_HEADS * GDN_DK
GDN_V = GDN_HEADS * GDN_DV
CONV_W = 4
XA_HEADS = 4
XA_DH = 256
XA_W = XA_HEADS * XA_DH
N_BRANCH = 3
D_FF = 5632

LANES = 128
GLA_CHUNK = 64
GLA_GROUP = 4
GDN_CHUNK = 128
GDN_HPS = 4
GDN_GROUP = 2
VMEM_LIMIT = 56 * 1024 * 1024

_IN_SIZES = (GLA_QK, GLA_QK, GLA_V, GLA_RANK, GLA_V, 2 * GDN_QK + GDN_V, GDN_HEADS, GDN_HEADS, GDN_V,
             XA_W, N_BRANCH * D_MODEL)
_IN_OFF = tuple(int(v) for v in np.cumsum((0,) + _IN_SIZES))
SM_LR, SM_B, SM_A = 0, GLA_RANK, GLA_RANK + GDN_HEADS


def _dot(a, b):
    return jnp.dot(a, b, preferred_element_type=F32)


def _dot_nt(a, b):
    return lax.dot_general(a, b, (((1,), (1,)), ((), ())), preferred_element_type=F32)


def _dot_tn(a, b):
    return lax.dot_general(a, b, (((0,), (0,)), ((), ())), preferred_element_type=F32)


def _mm(a, b):
    return _dot(a.astype(BF16), b.astype(BF16))


def _split2(x):
    hi = x.astype(BF16)
    mid = (x - hi.astype(F32)).astype(BF16)
    return hi, mid


def _dot01(m01, x):
    hi, mid = _split2(x)
    return _dot(m01, hi) + _dot(m01, mid)


def _rms(x, gain):
    return x * lax.rsqrt(jnp.mean(x * x, axis=-1, keepdims=True) + EPS) * gain


def _silu(x):
    return x * jax.nn.sigmoid(x)


def _softplus(x):
    return jnp.maximum(x, 0.0) + jnp.log1p(jnp.exp(-jnp.abs(x)))


def _params(sem):
    return pltpu.CompilerParams(dimension_semantics=sem, vmem_limit_bytes=VMEM_LIMIT)


def _ffn_kernel(x_ref, gpre_ref, wgu_hbm, wd_hbm, gpost_ref, gnext_ref, *out_and_scratch, emit_next, tf):
    if emit_next:
        h_ref, un_ref, xn_scr, wg_buf, wu_buf, wd_buf, wg16, wu16, wd16, sem = out_and_scratch
    else:
        h_ref, xn_scr, wg_buf, wu_buf, wd_buf, wg16, wu16, wd16, sem = out_and_scratch
    i = pl.program_id(0)
    nf = D_FF // tf
    assert nf % 2 == 0

    def tile_copies(jj, slot):
        col = pl.multiple_of(jj * tf, tf)
        return (pltpu.make_async_copy(wgu_hbm.at[:, pl.ds(col, tf)], wg_buf.at[slot], sem.at[0, slot]),
                pltpu.make_async_copy(wgu_hbm.at[:, pl.ds(D_FF + col, tf)], wu_buf.at[slot], sem.at[1, slot]),
                pltpu.make_async_copy(wd_hbm.at[pl.ds(col, tf), :], wd_buf.at[slot], sem.at[2, slot]))

    @pl.when(i == 0)
    def _():
        for cp in tile_copies(0, 0):
            cp.start()

    xn_scr[...] = _rms(x_ref[...], gpre_ref[...]).astype(BF16)
    h_ref[...] = jnp.zeros_like(h_ref)

    def body(jj, carry):
        slot = jj % 2
        @pl.when(jj + 1 < nf)
        def _():
            for cp in tile_copies(jj + 1, 1 - slot):
                cp.start()

        @pl.when((jj + 1 == nf) & (i + 1 < pl.num_programs(0)))
        def _():
            for cp in tile_copies(0, 0):
                cp.start()

        for cp in tile_copies(jj, slot):
            cp.wait()
        wg16[...] = wg_buf[slot].astype(BF16)
        wu16[...] = wu_buf[slot].astype(BF16)
        wd16[...] = wd_buf[slot].astype(BF16)
        xn = xn_scr[...]
        g = _dot(xn, wg16[...])
        u = _dot(xn, wu16[...])
        hm = (_silu(g) * u).astype(BF16)
        h_ref[...] += _dot(hm, wd16[...])
        return carry

    lax.fori_loop(0, nf, body, 0)

    h = x_ref[...] + 0.5 * _rms(h_ref[...], gpost_ref[...])
    h_ref[...] = h
    if emit_next:
        un_ref[...] = _rms(h, gnext_ref[...]).astype(BF16)


def _ffn(x, gpre, w_gu, w_down, gpost, gnext, emit_next, tm=1024, tf=256):
    t, d = x.shape
    row = lambda i: (i, 0)
    const = lambda i: (0, 0)
    once = pl.Buffered(1)
    out_shape = [jax.ShapeDtypeStruct((t, d), F32)]
    out_specs = [pl.BlockSpec((tm, d), row, pipeline_mode=once)]
    if emit_next:
        out_shape.append(jax.ShapeDtypeStruct((t, d), BF16))
        out_specs.append(pl.BlockSpec((tm, d), row, pipeline_mode=once))
    res = pl.pallas_call(
        functools.partial(_ffn_kernel, emit_next=emit_next, tf=tf),
        grid=(t // tm,),
        in_specs=[
            pl.BlockSpec((tm, d), row),
            pl.BlockSpec((1, d), const),
            pl.BlockSpec(memory_space=pl.ANY),
            pl.BlockSpec(memory_space=pl.ANY),
            pl.BlockSpec((1, d), const),
            pl.BlockSpec((1, d), const),
        ],
        out_specs=out_specs,
        out_shape=out_shape,
        scratch_shapes=[pltpu.VMEM((tm, d), BF16),
                        pltpu.VMEM((2, d, tf), F32), pltpu.VMEM((2, d, tf), F32), pltpu.VMEM((2, tf, d), F32),
                        pltpu.VMEM((d, tf), BF16), pltpu.VMEM((d, tf), BF16), pltpu.VMEM((tf, d), BF16),
                        pltpu.SemaphoreType.DMA((3, 2))],
        compiler_params=_params(("arbitrary",)),
        name="ffn",
    )(x, gpre, w_gu, w_down, gpost, gnext)
    return res if emit_next else res[0]


def _matmul_kernel(a_ref, w_ref, o_ref):
    o_ref[...] = _dot_nt(a_ref[...], w_ref[...]).astype(o_ref.dtype)


def _proj_conv_kernel(a_ref, w_ref, cw_ref, o_ref):
    tm, tn = o_ref.shape
    rb = 256
    kind = pl.program_id(1) // (GDN_QK // tn)
    scale = jnp.where(kind == 0, GDN_DK ** -0.5, 1.0)
    w = cw_ref[...]
    r8 = lax.broadcasted_iota(jnp.int32, (8, tn), 0)

    sb = 32

    def finish(r, before):
        for s0 in range(0, rb, sb):
            y = o_ref[r + s0:r + s0 + sb, :]
            acc = y * w[CONV_W - 1:CONV_W, :]
            for sft in range(1, CONV_W):
                xs = pltpu.roll(y, sft, 0)
                top = jnp.where(r8 < sft, pltpu.roll(before, sft, 0), xs[0:8])
                acc = acc + jnp.concatenate([top, xs[8:]], axis=0) * w[CONV_W - 1 - sft:CONV_W - sft, :]
            before = y[sb - 8:sb]
            c = _silu(acc)
            for hh in range(tn // GDN_DK):
                sl = slice(hh * GDN_DK, (hh + 1) * GDN_DK)
                blk = c[:, sl]
                f = lax.rsqrt(jnp.sum(blk * blk, axis=-1, keepdims=True) + EPS) * scale
                o_ref[r + s0:r + s0 + sb, sl] = blk * jnp.where(kind == 2, 1.0, f)
        return before

    o_ref[0:rb, :] = _dot_nt(a_ref[0:rb, :], w_ref[...])
    before = jnp.zeros((8, tn), F32)
    for r in range(rb, tm + rb, rb):
        if r < tm:
            o_ref[r:r + rb, :] = _dot_nt(a_ref[r:r + rb, :], w_ref[...])
        before = finish(r - rb, before)


def _proj_s_kernel(a_ref, w_ref, par_ref, o_ref):
    y = _dot_nt(a_ref[...], w_ref[...])
    par = par_ref[...]
    lane = lax.broadcasted_iota(jnp.int32, y.shape, 1)
    g = -jnp.exp(par[0:1, :]) * _softplus(y + par[1:2, :])
    is_g = (lane >= SM_A) & (lane < SM_A + GDN_HEADS)
    is_b = (lane >= SM_B) & (lane < SM_B + GDN_HEADS)
    o_ref[...] = jnp.where(is_g, g, jnp.where(is_b, jax.nn.sigmoid(y), y))


def _matmul(a, w, col0, n, out_dtype, tm, tn, name, body=_matmul_kernel, extra=(), extra_specs=()):
    m, k = a.shape
    c0 = col0 // tn
    return pl.pallas_call(
        body,
        grid=(m // tm, n // tn),
        in_specs=[pl.BlockSpec((tm, k), lambda i, j: (i, 0)), pl.BlockSpec((tn, k), lambda i, j: (c0 + j, 0)),
                  *extra_specs],
        out_specs=pl.BlockSpec((tm, tn), lambda i, j: (i, j)),
        out_shape=jax.ShapeDtypeStruct((m, n), out_dtype),
        compiler_params=_params(("parallel", "parallel")),
        name=name,
    )(a, w, *extra)


def _norm_matmul_kernel(x_ref, g_ref, w_ref, o_ref, xn_scr):
    @pl.when(pl.program_id(1) == 0)
    def _():
        xn_scr[...] = _rms(x_ref[...], g_ref[...]).astype(BF16)

    o_ref[...] = _dot(xn_scr[...], w_ref[...].astype(BF16)).astype(o_ref.dtype)


def _norm_matmul(x, gain, w, out_dtype, tm, tn, name):
    m, k = x.shape
    n = w.shape[1]
    return pl.pallas_call(
        _norm_matmul_kernel,
        grid=(m // tm, n // tn),
        in_specs=[pl.BlockSpec((tm, k), lambda i, j: (i, 0)), pl.BlockSpec((1, k), lambda i, j: (0, 0)),
                  pl.BlockSpec((k, tn), lambda i, j: (0, j))],
        out_specs=pl.BlockSpec((tm, tn), lambda i, j: (i, j)),
        out_shape=jax.ShapeDtypeStruct((m, n), out_dtype),
        scratch_shapes=[pltpu.VMEM((tm, k), BF16)],
        compiler_params=_params(("parallel", "arbitrary")),
        name=name,
    )(x, gain, w)


def _gla_levels(chunk):
    return [chunk >> (i + 1) for i in range(int(np.log2(chunk)))]


def _gla_exponent_matrix(chunk):
    c = chunk
    i = np.arange(c)[:, None]
    t = np.arange(c)[None, :]
    blocks = [(t <= i), (t > i)]
    for h in _gla_levels(c):
        r = (i // (2 * h)) * (2 * h) + h
        upper = i >= r
        blocks.append(np.where(upper, (t > r) & (t <= i), (t > i) & (t <= r)))
    m = np.concatenate(blocks, axis=0).astype(np.float32)
    z = np.zeros_like(m)
    return np.concatenate([np.concatenate([m, m, z, z], axis=1), np.concatenate([z, z, m, m], axis=1)], axis=0)


def _gla_kernel(q_ref, k_ref, v_ref, sm_ref, r_ref, wlr_ref, blr_ref, gn_ref, mst_ref, o_ref, s_scr, e0_scr,
                e1_scr):
    c = GLA_CHUNK
    seq = q_ref.shape[0]
    s_scr[...] = jnp.zeros_like(s_scr)
    ri = lax.broadcasted_iota(jnp.int32, (c, c), 0)
    ci = lax.broadcasted_iota(jnp.int32, (c, c), 1)
    rowi = lax.broadcasted_iota(jnp.int32, (c, GLA_DK), 0)
    levels = _gla_levels(c)
    assert GLA_GROUP == 4

    grp = GLA_GROUP
    gs = range(grp)

    n_trips = seq // (grp * c)

    def gate_logits(n):
        allrows = pl.ds(pl.multiple_of(n * (grp * c), grp * c), grp * c)
        return _dot(sm_ref[allrows, :].astype(BF16), wlr_ref[...]) + blr_ref[...]

    def decays(x, e_ref):
        lg = (jnp.minimum(x, 0.0) - jnp.log1p(jnp.exp(-jnp.abs(x)))) * (1.0 / GLA_TAU)
        hi, mid = _split2(lg)
        hm = [jnp.concatenate([hi[g * c:(g + 1) * c], mid[g * c:(g + 1) * c]], axis=0) for g in gs]
        rhs = jnp.concatenate([jnp.concatenate([hm[0], hm[1]], axis=1),
                               jnp.concatenate([hm[2], hm[3]], axis=1)], axis=0)
        e_ref[...] = jnp.exp(_dot(mst_ref[...], rhs))

    def trip(n, e_ref, between):
        row0 = pl.multiple_of(n * (grp * c), grp * c)
        rows = [pl.ds(row0 + g * c, c) for g in gs]
        ex = e_ref[...]
        nr = ex.shape[0] // 2
        e = [ex[0:nr, 0:GLA_DK], ex[0:nr, GLA_DK:], ex[nr:, 0:GLA_DK], ex[nr:, GLA_DK:]]
        q = [q_ref[rows[g], :] * (GLA_DK ** -0.5) for g in gs]
        k = [k_ref[rows[g], :] for g in gs]
        v = [v_ref[rows[g], :].astype(BF16) for g in gs]
        attn = [jnp.where(ri == ci, _dot_nt(q[g].astype(BF16), k[g].astype(BF16)), 0.0) for g in gs]
        for lvl, h in enumerate(levels):
            upper = (rowi & h) != 0
            sh = int(np.log2(2 * h))
            same = (ri >> sh) == (ci >> sh)
            for g in gs:
                f = e[g][(2 + lvl) * c:(3 + lvl) * c]
                qt = jnp.where(upper, q[g] * f, 0.0).astype(BF16)
                kt = jnp.where(upper, 0.0, k[g] * f).astype(BF16)
                attn[g] = attn[g] + jnp.where(same, _dot_nt(qt, kt), 0.0)
        vk = [_dot_tn(v[g], (k[g] * e[g][c:2 * c]).astype(BF16)) for g in gs]
        between()
        av = [_dot(attn[g].astype(BF16), v[g]) for g in gs]
        st = s_scr[...]
        for g in gs:
            o = _dot_nt((q[g] * e[g][0:c]).astype(BF16), st.astype(BF16)) + av[g]
            st = st * e[g][c - 1:c, :] + vk[g]
            on = _rms(o, gn_ref[...]) * _silu(r_ref[rows[g], :].astype(F32))
            o_ref[rows[g], :] = on.astype(o_ref.dtype)
        s_scr[...] = st

    assert n_trips % 2 == 0
    decays(gate_logits(0), e0_scr)

    def body(m, carry):
        x1 = gate_logits(2 * m + 1)
        trip(2 * m, e0_scr, lambda: decays(x1, e1_scr))
        x2 = gate_logits(jnp.minimum(2 * m + 2, n_trips - 1))
        trip(2 * m + 1, e1_scr, lambda: decays(x2, e0_scr))
        return carry

    lax.fori_loop(0, n_trips // 2, body, 0)


def _gla(pa, pb, ps, wlr, blr, gnorm, bsz, seq):
    t = bsz * seq
    mst = jnp.asarray(_gla_exponent_matrix(GLA_CHUNK), BF16)
    nq = GLA_QK // GLA_DK
    return pl.pallas_call(
        _gla_kernel,
        grid=(bsz, GLA_HEADS),
        in_specs=[
            pl.BlockSpec((seq, GLA_DK), lambda b, h: (b, h)),
            pl.BlockSpec((seq, GLA_DK), lambda b, h: (b, nq + h)),
            pl.BlockSpec((seq, GLA_DV), lambda b, h: (b, 2 * GLA_QK // GLA_DV + h)),
            pl.BlockSpec((seq, LANES), lambda b, h: (b, 0)),
            pl.BlockSpec((seq, GLA_DV), lambda b, h: (b, _PB_R // GLA_DV + h)),
            pl.BlockSpec((LANES, GLA_DK), lambda b, h: (0, h)),
            pl.BlockSpec((1, GLA_DK), lambda b, h: (0, h)),
            pl.BlockSpec((1, GLA_DV), lambda b, h: (0, 0)),
            pl.BlockSpec(mst.shape, lambda b, h: (0, 0)),
        ],
        out_specs=pl.BlockSpec((seq, GLA_DV), lambda b, h: (b, h)),
        out_shape=jax.ShapeDtypeStruct((t, GLA_V), BF16),
        scratch_shapes=[pltpu.VMEM((GLA_DV, GLA_DK), F32), pltpu.VMEM((mst.shape[0], 2 * GLA_DK), F32),
                        pltpu.VMEM((mst.shape[0], 2 * GLA_DK), F32)],
        compiler_params=_params(("parallel", "parallel")),
        name="gla",
    )(pa, pa, pa, ps, pb, wlr, blr, gnorm, mst)


def _gdn_kernel(q_ref, k_ref, v_ref, sm_ref, z_ref, gn_ref, o_ref, bms, ns, qms, os_, egl, s_scr):
    c = GDN_CHUNK
    seq = q_ref.shape[0]
    head0 = pl.program_id(1) * GDN_HPS

    ri = lax.broadcasted_iota(jnp.int32, (c, c), 0)
    ci = lax.broadcasted_iota(jnp.int32, (c, c), 1)
    lane = lax.broadcasted_iota(jnp.int32, (c, LANES), 1)
    tril = jnp.where(ri >= ci, 1.0, 0.0).astype(BF16)
    causal = ri >= ci
    n_sq = int(np.log2(c))
    lower_left = [(((ri ^ ci) >> (l + 1)) == 0) & ((ri & (1 << l)) != 0) & ((ci & (1 << l)) == 0) for l in range(n_sq)]

    grp = GDN_GROUP
    chains = [(g, j) for g in range(grp) for j in range(GDN_HPS)]

    def phase1(n, hooks):
        rows, q, k, kb, gb, rhs0 = {}, {}, {}, {}, {}, {}
        for g in range(grp):
            rows[g] = pl.ds(pl.multiple_of((n * grp + g) * c, c), c)
            sm = sm_ref[rows[g], :]
            for j in range(GDN_HPS):
                sl = slice(j * GDN_DK, (j + 1) * GDN_DK)
                q[g, j] = q_ref[rows[g], sl]
                k[g, j] = k_ref[rows[g], sl]
                gcol = jnp.sum(jnp.where(lane == SM_A + head0 + j, sm, 0.0), axis=-1, keepdims=True)
                beta = jnp.sum(jnp.where(lane == SM_B + head0 + j, sm, 0.0), axis=-1, keepdims=True)
                gb[g, j] = jnp.broadcast_to(gcol, (c, LANES))
                kb[g, j] = k[g, j] * beta
                rhs0[g, j] = v_ref[rows[g], sl] * beta
        cum_r, cum_c, kt = {}, {}, {}
        for ch in chains:
            hi, mid = _split2(gb[ch])
            r = _dot(tril, jnp.concatenate([hi, mid], axis=1))
            cum_r[ch] = r[:, :LANES] + r[:, LANES:]
        for ch in chains:
            cum_c[ch] = cum_r[ch].T
            kt[ch] = k[ch].T
        kk = {}
        for ch in chains:
            k16 = k[ch].astype(BF16)
            kk[ch] = _dot_nt(jnp.concatenate([kb[ch].astype(BF16), q[ch].astype(BF16)], axis=0), k16)
        low, at16 = {}, {}
        for ch in chains:
            dec = jnp.where(causal, jnp.exp(jnp.where(causal, cum_r[ch] - cum_c[ch], 0.0)), 0.0)
            low[ch] = jnp.where(ri > ci, kk[ch][:c] * dec, 0.0)
            at16[ch] = jnp.where(causal, kk[ch][c:] * dec, 0.0).astype(BF16)
        x = {ch: -jnp.where(lower_left[0], low[ch], 0.0) for ch in chains}
        for lvl in range(1, n_sq):
            e = {}
            for ch in chains:
                cm = jnp.where(lower_left[lvl], low[ch], 0.0)
                e[ch] = cm + _mm(x[ch], cm)
            for ch in chains:
                x[ch] = x[ch] - e[ch] - _mm(e[ch], x[ch])
            if lvl in hooks:
                hooks[lvl]()
        wu, egc = {}, {}
        for ch in chains:
            egc[ch] = jnp.exp(cum_r[ch])
            rhs = jnp.concatenate([kb[ch] * egc[ch], rhs0[ch]], axis=1)
            wu[ch] = (rhs + _mm(x[ch], rhs)).astype(BF16)
        for ch in chains:
            g, j = ch
            sl = slice(j * GDN_DK, (j + 1) * GDN_DK)
            glast = cum_r[ch][c - 1:c, :]
            kdt16 = (kt[ch] * jnp.exp(glast - cum_c[ch][0:1, :])).astype(BF16)
            r = _dot(jnp.concatenate([kdt16, at16[ch]], axis=0), wu[ch])
            bms[rows[g], sl] = (-r[:c, :GDN_DV]).astype(BF16)
            ns[rows[g], sl] = r[:c, GDN_DV:]
            qms[rows[g], sl] = (q[ch] * egc[ch] - r[c:, :GDN_DV]).astype(BF16)
            os_[rows[g], sl] = r[c:, GDN_DV:]
            egl[pl.ds(pl.multiple_of((n * grp + g) * 8, 8), 8), sl] = jnp.broadcast_to(jnp.exp(glast), (8, LANES))

    s_scr[...] = jnp.zeros_like(s_scr)

    def phase2(n):
        rows = pl.ds(pl.multiple_of(n * c, c), c)
        sls = [slice(j * GDN_DK, (j + 1) * GDN_DK) for j in range(GDN_HPS)]
        s = [s_scr[j] for j in range(GDN_HPS)]
        s16 = [sj.astype(BF16) for sj in s]
        r = [_dot(jnp.concatenate([bms[rows, sl], qms[rows, sl]], axis=0), s16[j]) for j, sl in enumerate(sls)]
        for j, sl in enumerate(sls):
            eg = egl[pl.ds(pl.multiple_of(n * 8, 8), 8), sl][0:1, :]
            s_scr[j] = s[j] * eg + r[j][:c] + ns[rows, sl]
            o = r[j][c:] + os_[rows, sl]
            on = _rms(o, gn_ref[...]) * _silu(z_ref[rows, sl].astype(F32))
            o_ref[rows, sl] = on.astype(o_ref.dtype)

    assert grp == 2 and n_sq >= 5
    n_trips = seq // (grp * c)
    phase1(0, {})

    def body(n, carry):
        first = (n - 1) * grp
        phase1(n, {1: lambda: phase2(first), n_sq - 2: lambda: phase2(first + 1)})
        return carry

    lax.fori_loop(1, n_trips, body, 0)
    for g in range(grp):
        phase2((n_trips - 1) * grp + g)


def _gdn(pg, pb, ps, gnorm, bsz, seq):
    t = bsz * seq
    nb = seq // GDN_CHUNK
    wd = GDN_HPS * GDN_DK
    ng = GDN_HEADS // GDN_HPS
    return pl.pallas_call(
        _gdn_kernel,
        grid=(bsz, ng),
        in_specs=[
            pl.BlockSpec((seq, wd), lambda b, h: (b, h)),
            pl.BlockSpec((seq, wd), lambda b, h: (b, ng + h)),
            pl.BlockSpec((seq, wd), lambda b, h: (b, 2 * ng + h)),
            pl.BlockSpec((seq, LANES), lambda b, h: (b, 0)),
            pl.BlockSpec((seq, wd), lambda b, h: (b, _PB_Z // wd + h)),
            pl.BlockSpec((1, GDN_DV), lambda b, h: (0, 0)),
        ],
        out_specs=pl.BlockSpec((seq, wd), lambda b, h: (b, h)),
        out_shape=jax.ShapeDtypeStruct((t, GDN_V), BF16),
        scratch_shapes=[
            pltpu.VMEM((seq, wd), BF16), pltpu.VMEM((seq, wd), F32), pltpu.VMEM((seq, wd), BF16),
            pltpu.VMEM((seq, wd), F32), pltpu.VMEM((nb * 8, wd), F32),
            pltpu.VMEM((GDN_HPS, GDN_DK, GDN_DV), F32),
        ],
        compiler_params=_params(("parallel", "parallel")),
        name="gdn",
    )(pg, pg, pg, ps, pb, gnorm)


def _xa_kernel(q_ref, mk_ref, mv_ref, o_ref):
    s = _dot_nt(q_ref[...], mk_ref[...]) * (XA_DH ** -0.5)
    p = jnp.exp(s - jnp.max(s, axis=-1, keepdims=True))
    den = jnp.sum(p, axis=-1, keepdims=True)
    o_ref[...] = (_dot(p.astype(BF16), mv_ref[...]) / den).astype(o_ref.dtype)


def _xa(pb, mkv, bsz, seq, ts=1024):
    t = bsz * seq
    nt = seq // ts
    return pl.pallas_call(
        _xa_kernel,
        grid=(bsz, XA_HEADS, nt),
        in_specs=[
            pl.BlockSpec((ts, XA_DH), lambda b, h, i: (b * nt + i, _PB_XQ // XA_DH + h)),
            pl.BlockSpec((MEM_LEN, XA_DH), lambda b, h, i: (b, h)),
            pl.BlockSpec((MEM_LEN, XA_DH), lambda b, h, i: (b, XA_HEADS + h)),
        ],
        out_specs=pl.BlockSpec((ts, XA_DH), lambda b, h, i: (b * nt + i, h)),
        out_shape=jax.ShapeDtypeStruct((t, XA_W), BF16),
        compiler_params=_params(("parallel", "parallel", "parallel")),
        name="xattn",
    )(pb, mkv, mkv)


def _merge_kernel(h_ref, oa_ref, ob_ref, oc_ref, ga_ref, gb_ref, gc_ref, wa_ref, wb_ref, wc_ref, wo_ref,
                  gpost_ref, o_ref):
    mixed = jax.nn.sigmoid(ga_ref[...].astype(F32)) * _dot(oa_ref[...], wa_ref[...])
    mixed = mixed + jax.nn.sigmoid(gb_ref[...].astype(F32)) * _dot(ob_ref[...], wb_ref[...])
    mixed = mixed + jax.nn.sigmoid(gc_ref[...].astype(F32)) * _dot(oc_ref[...], wc_ref[...])
    m = _dot(mixed.astype(BF16), wo_ref[...])
    o_ref[...] = h_ref[...] + _rms(m, gpost_ref[...])


def _merge(h, oa, ob, oc, pb, wa, wb, wc, wo, gpost, tm=256):
    t, d = h.shape
    row = lambda i: (i, 0)
    const = lambda i: (0, 0)
    once = pl.Buffered(1)
    return pl.pallas_call(
        _merge_kernel,
        grid=(t // tm,),
        in_specs=[
            pl.BlockSpec((tm, d), row),
            pl.BlockSpec((tm, GLA_V), row), pl.BlockSpec((tm, GDN_V), row), pl.BlockSpec((tm, XA_W), row),
            pl.BlockSpec((tm, d), lambda i: (i, 0)), pl.BlockSpec((tm, d), lambda i: (i, 1)),
            pl.BlockSpec((tm, d), lambda i: (i, 2)),
            pl.BlockSpec((GLA_V, d), const, pipeline_mode=once), pl.BlockSpec((GDN_V, d), const, pipeline_mode=once),
            pl.BlockSpec((XA_W, d), const, pipeline_mode=once), pl.BlockSpec((d, d), const, pipeline_mode=once),
            pl.BlockSpec((1, d), const),
        ],
        out_specs=pl.BlockSpec((tm, d), row),
        out_shape=jax.ShapeDtypeStruct((t, d), F32),
        compiler_params=_params(("parallel",)),
        name="merge",
    )(h, oa, ob, oc, pb, pb, pb, wa, wb, wc, wo, gpost)


_PA_GDN = 2 * GLA_QK + GLA_V
_PA_COLS = _PA_GDN + 2 * GDN_QK + GDN_V
_PB_R = N_BRANCH * D_MODEL
_PB_Z = _PB_R + GLA_V
_PB_XQ = _PB_Z + GDN_V
_PB_COLS = _PB_XQ + XA_W


_W_ORDER = (0, 1, 2, 5, 10, 4, 8, 9)
_W_SMALL = ((_IN_OFF[3], GLA_RANK), (_IN_OFF[6], 2 * GDN_HEADS))
_W_ALL_COLS = _PA_COLS + _PB_COLS + LANES
_W_RB = 512


def _w_block_rows():
    tab = []
    for i in _W_ORDER:
        assert (_IN_OFF[i + 1] - _IN_OFF[i]) % _W_RB == 0
        tab.extend(range(_IN_OFF[i], _IN_OFF[i + 1], _W_RB))
    return np.asarray(tab, np.int32)


def _relayout_kernel(tab_ref, wt_hbm, o_ref, buf, sem):
    j = pl.program_id(0)
    n_main = pl.num_programs(0) - 1

    def block_copy(blk):
        start = pl.multiple_of(tab_ref[blk], 16)
        return pltpu.make_async_copy(wt_hbm.at[pl.ds(start, _W_RB), :], buf.at[blk % 2], sem.at[blk % 2])

    @pl.when(j == 0)
    def _():
        block_copy(j).start()

    @pl.when(j + 1 < n_main)
    def _():
        block_copy(j + 1).start()

    @pl.when(j < n_main)
    def _():
        block_copy(j).wait()
        o_ref[...] = buf[j % 2].astype(BF16)

    @pl.when(j == n_main)
    def _():
        dst = 0
        copies = []
        for n, (row0, rows) in enumerate(_W_SMALL):
            copies.append(pltpu.make_async_copy(wt_hbm.at[pl.ds(row0, rows), :], buf.at[0, pl.ds(dst, rows), :],
                                                sem.at[n]))
            dst += rows
        for cp in copies:
            cp.start()
        for cp in copies:
            cp.wait()
        o_ref[0:dst, :] = buf[0, 0:dst, :].astype(BF16)
        o_ref[dst:, :] = jnp.zeros((_W_RB - dst, o_ref.shape[1]), BF16)


def _split_w_in(w_in_t):
    n, k = w_in_t.shape
    tab = _w_block_rows()
    return pl.pallas_call(
        _relayout_kernel,
        grid_spec=pltpu.PrefetchScalarGridSpec(
            num_scalar_prefetch=1,
            grid=(len(tab) + 1,),
            in_specs=[pl.BlockSpec(memory_space=pl.ANY)],
            out_specs=pl.BlockSpec((_W_RB, k), lambda i, tab: (i, 0)),
            scratch_shapes=[pltpu.VMEM((2, _W_RB, k), F32), pltpu.SemaphoreType.DMA((2,))],
        ),
        out_shape=jax.ShapeDtypeStruct((_W_ALL_COLS, k), BF16),
        compiler_params=_params(("arbitrary",)),
        name="w_in_relayout",
    )(jnp.asarray(tab), w_in_t)


def kernel(x, mem, n_ffn1_pre, w_ffn1_gu, w_ffn1_down, n_ffn1_post, n_mix_pre, w_in, gla_w_lr2, gla_b_lr,
           gla_norm, gdn_conv, gdn_a_log, gdn_dt_bias, gdn_norm, mem_norm, w_mem_kv, w_up_gla, w_up_gdn,
           w_up_xa, w_out, n_mix_post, n_ffn2_pre, w_ffn2_gu, w_ffn2_down, n_ffn2_post):
    bsz, seq, d = x.shape
    t = bsz * seq
    h = x.reshape(t, d)
    for l in range(n_ffn1_pre.shape[0]):
        row = lambda a: a[l][None, :]
        h, u = _ffn(h, row(n_ffn1_pre), w_ffn1_gu[l], w_ffn1_down[l],
                    row(n_ffn1_post), row(n_mix_pre), emit_next=True)

        w_all = _split_w_in(w_in[l].T)
        par = jnp.zeros((2, LANES), F32)
        par = par.at[0, SM_A:SM_A + GDN_HEADS].set(gdn_a_log[l]).at[1, SM_A:SM_A + GDN_HEADS].set(gdn_dt_bias[l])
        tn_g = 512
        pa = _matmul(u, w_all, 0, _PA_GDN, F32, 2048, 512, "in_proj_a")
        pg = _matmul(u, w_all, _PA_GDN, _PA_COLS - _PA_GDN, F32, seq, tn_g, "in_proj_g", body=_proj_conv_kernel,
                     extra=(gdn_conv[l],), extra_specs=(pl.BlockSpec((CONV_W, tn_g), lambda i, j: (0, j)),))
        pb = _matmul(u, w_all, _PA_COLS, _PB_COLS, BF16, 2048, 1024, "in_proj_b")
        ps = _matmul(u, w_all, _PA_COLS + _PB_COLS, LANES, F32, 2048, LANES, "in_proj_s", body=_proj_s_kernel,
                     extra=(par,), extra_specs=(pl.BlockSpec((2, LANES), lambda i, j: (0, 0)),))

        wlr = jnp.pad(gla_w_lr2[l], ((0, LANES - GLA_RANK), (0, 0))).astype(BF16)
        o_gla = _gla(pa, pb, ps, wlr, row(gla_b_lr), row(gla_norm), bsz, seq)
        o_gdn = _gdn(pg, pb, ps, row(gdn_norm), bsz, seq)

        mkv = _norm_matmul(mem.reshape(bsz * MEM_LEN, d), row(mem_norm), w_mem_kv[l], BF16,
                           bsz * MEM_LEN, 512, "mem_kv")
        o_xa = _xa(pb, mkv, bsz, seq)

        h = _merge(h, o_gla, o_gdn, o_xa, pb, w_up_gla[l].astype(BF16), w_up_gdn[l].astype(BF16),
                   w_up_xa[l].astype(BF16), w_out[l].astype(BF16), row(n_mix_post))
        h = _ffn(h, row(n_ffn2_pre), w_ffn2_gu[l], w_ffn2_down[l],
                 row(n_ffn2_post), row(n_ffn2_post), emit_next=False)
    return h.reshape(bsz, seq, d)
```

```python
import functools

import numpy as np
import jax
import jax.numpy as jnp
from jax import lax
from jax.experimental import pallas as pl
from jax.experimental.pallas import tpu as pltpu

F32 = jnp.float32
BF16 = jnp.bfloat16

D_MODEL = 2048
MEM_LEN = 256
EPS = 1e-6
GLA_HEADS = 4
GLA_DK = 128
GLA_DV = 256
GLA_QK = GLA_HEADS * GLA_DK
GLA_V = GLA_HEADS * GLA_DV
GLA_RANK = 16
GLA_TAU = 16.0
GDN_HEADS = 8
GDN_DK = 128
GDN_DV = 128
GDN_QK = GDN_HEADS * GDN_DK
GDN_V = GDN_HEADS * GDN_DV
CONV_W = 4
XA_HEADS = 4
XA_DH = 256
XA_W = XA_HEADS * XA_DH
N_BRANCH = 3
D_FF = 5632

LANES = 128
GLA_CHUNK = 64
GLA_GROUP = 4
GDN_CHUNK = 128
GDN_HPS = 4
GDN_GROUP = 2
VMEM_LIMIT = 56 * 1024 * 1024

_IN_SIZES = (GLA_QK, GLA_QK, GLA_V, GLA_RANK, GLA_V, 2 * GDN_QK + GDN_V, GDN_HEADS, GDN_HEADS, GDN_V,
             XA_W, N_BRANCH * D_MODEL)
_IN_OFF = tuple(int(v) for v in np.cumsum((0,) + _IN_SIZES))
SM_LR, SM_B, SM_A = 0, GLA_RANK, GLA_RANK + GDN_HEADS


def _dot(a, b):
    return jnp.dot(a, b, preferred_element_type=F32)


def _dot_nt(a, b):
    return lax.dot_general(a, b, (((1,), (1,)), ((), ())), preferred_element_type=F32)


def _dot_tn(a, b):
    return lax.dot_general(a, b, (((0,), (0,)), ((), ())), preferred_element_type=F32)


def _mm(a, b):
    return _dot(a.astype(BF16), b.astype(BF16))


def _split2(x):
    hi = x.astype(BF16)
    mid = (x - hi.astype(F32)).astype(BF16)
    return hi, mid


def _dot01(m01, x):
    hi, mid = _split2(x)
    return _dot(m01, hi) + _dot(m01, mid)


def _rms(x, gain):
    return x * lax.rsqrt(jnp.mean(x * x, axis=-1, keepdims=True) + EPS) * gain


def _silu(x):
    return x * jax.nn.sigmoid(x)


def _softplus(x):
    return jnp.maximum(x, 0.0) + jnp.log1p(jnp.exp(-jnp.abs(x)))


def _params(sem):
    return pltpu.CompilerParams(dimension_semantics=sem, vmem_limit_bytes=VMEM_LIMIT)


def _ffn_kernel(x_ref, gpre_ref, wgu_hbm, wd_hbm, gpost_ref, gnext_ref, *out_and_scratch, emit_next, tf):
    if emit_next:
        h_ref, un_ref, xn_scr, wg_buf, wu_buf, wd_buf, wg16, wu16, wd16, sem = out_and_scratch
    else:
        h_ref, xn_scr, wg_buf, wu_buf, wd_buf, wg16, wu16, wd16, sem = out_and_scratch
    i = pl.program_id(0)
    nf = D_FF // tf
    assert nf % 2 == 0

    def tile_copies(jj, slot):
        col = pl.multiple_of(jj * tf, tf)
        return (pltpu.make_async_copy(wgu_hbm.at[:, pl.ds(col, tf)], wg_buf.at[slot], sem.at[0, slot]),
                pltpu.make_async_copy(wgu_hbm.at[:, pl.ds(D_FF + col, tf)], wu_buf.at[slot], sem.at[1, slot]),
                pltpu.make_async_copy(wd_hbm.at[pl.ds(col, tf), :], wd_buf.at[slot], sem.at[2, slot]))

    @pl.when(i == 0)
    def _():
        for cp in tile_copies(0, 0):
            cp.start()

    xn_scr[...] = _rms(x_ref[...], gpre_ref[...]).astype(BF16)
    h_ref[...] = jnp.zeros_like(h_ref)

    def body(jj, carry):
        slot = jj % 2
        @pl.when(jj + 1 < nf)
        def _():
            for cp in tile_copies(jj + 1, 1 - slot):
                cp.start()

        @pl.when((jj + 1 == nf) & (i + 1 < pl.num_programs(0)))
        def _():
            for cp in tile_copies(0, 0):
                cp.start()

        for cp in tile_copies(jj, slot):
            cp.wait()
        wg16[...] = wg_buf[slot].astype(BF16)
        wu16[...] = wu_buf[slot].astype(BF16)
        wd16[...] = wd_buf[slot].astype(BF16)
        xn = xn_scr[...]
        g = _dot(xn, wg16[...])
        u = _dot(xn, wu16[...])
        hm = (_silu(g) * u).astype(BF16)
        h_ref[...] += _dot(hm, wd16[...])
        return carry

    lax.fori_loop(0, nf, body, 0)

    h = x_ref[...] + 0.5 * _rms(h_ref[...], gpost_ref[...])
    h_ref[...] = h
    if emit_next:
        un_ref[...] = _rms(h, gnext_ref[...]).astype(BF16)


def _ffn(x, gpre, w_gu, w_down, gpost, gnext, emit_next, tm=1024, tf=256):
    t, d = x.shape
    row = lambda i: (i, 0)
    const = lambda i: (0, 0)
    once = pl.Buffered(1)
    out_shape = [jax.ShapeDtypeStruct((t, d), F32)]
    out_specs = [pl.BlockSpec((tm, d), row, pipeline_mode=once)]
    if emit_next:
        out_shape.append(jax.ShapeDtypeStruct((t, d), BF16))
        out_specs.append(pl.BlockSpec((tm, d), row, pipeline_mode=once))
    res = pl.pallas_call(
        functools.partial(_ffn_kernel, emit_next=emit_next, tf=tf),
        grid=(t // tm,),
        in_specs=[
            pl.BlockSpec((tm, d), row),
            pl.BlockSpec((1, d), const),
            pl.BlockSpec(memory_space=pl.ANY),
            pl.BlockSpec(memory_space=pl.ANY),
            pl.BlockSpec((1, d), const),
            pl.BlockSpec((1, d), const),
        ],
        out_specs=out_specs,
        out_shape=out_shape,
        scratch_shapes=[pltpu.VMEM((tm, d), BF16),
                        pltpu.VMEM((2, d, tf), F32), pltpu.VMEM((2, d, tf), F32), pltpu.VMEM((2, tf, d), F32),
                        pltpu.VMEM((d, tf), BF16), pltpu.VMEM((d, tf), BF16), pltpu.VMEM((tf, d), BF16),
                        pltpu.SemaphoreType.DMA((3, 2))],
        compiler_params=_params(("arbitrary",)),
        name="ffn",
    )(x, gpre, w_gu, w_down, gpost, gnext)
    return res if emit_next else res[0]


def _matmul_kernel(a_ref, w_ref, o_ref):
    o_ref[...] = _dot_nt(a_ref[...], w_ref[...]).astype(o_ref.dtype)


def _proj_conv_kernel(a_ref, w_ref, cw_ref, o_ref):
    tm, tn = o_ref.shape
    rb = 256
    kind = pl.program_id(1) // (GDN_QK // tn)
    scale = jnp.where(kind == 0, GDN_DK ** -0.5, 1.0)
    w = cw_ref[...]
    r8 = lax.broadcasted_iota(jnp.int32, (8, tn), 0)

    sb = 32

    def finish(r, before):
        for s0 in range(0, rb, sb):
            y = o_ref[r + s0:r + s0 + sb, :]
            acc = y * w[CONV_W - 1:CONV_W, :]
            for sft in range(1, CONV_W):
                xs = pltpu.roll(y, sft, 0)
                top = jnp.where(r8 < sft, pltpu.roll(before, sft, 0), xs[0:8])
                acc = acc + jnp.concatenate([top, xs[8:]], axis=0) * w[CONV_W - 1 - sft:CONV_W - sft, :]
            before = y[sb - 8:sb]
            c = _silu(acc)
            for hh in range(tn // GDN_DK):
                sl = slice(hh * GDN_DK, (hh + 1) * GDN_DK)
                blk = c[:, sl]
                f = lax.rsqrt(jnp.sum(blk * blk, axis=-1, keepdims=True) + EPS) * scale
                o_ref[r + s0:r + s0 + sb, sl] = blk * jnp.where(kind == 2, 1.0, f)
        return before

    o_ref[0:rb, :] = _dot_nt(a_ref[0:rb, :], w_ref[...])
    before = jnp.zeros((8, tn), F32)
    for r in range(rb, tm + rb, rb):
        if r < tm:
            o_ref[r:r + rb, :] = _dot_nt(a_ref[r:r + rb, :], w_ref[...])
        before = finish(r - rb, before)


def _proj_s_kernel(a_ref, w_ref, par_ref, o_ref):
    y = _dot_nt(a_ref[...], w_ref[...])
    par = par_ref[...]
    lane = lax.broadcasted_iota(jnp.int32, y.shape, 1)
    g = -jnp.exp(par[0:1, :]) * _softplus(y + par[1:2, :])
    is_g = (lane >= SM_A) & (lane < SM_A + GDN_HEADS)
    is_b = (lane >= SM_B) & (lane < SM_B + GDN_HEADS)
    o_ref[...] = jnp.where(is_g, g, jnp.where(is_b, jax.nn.sigmoid(y), y))


def _matmul(a, w, col0, n, out_dtype, tm, tn, name, body=_matmul_kernel, extra=(), extra_specs=()):
    m, k = a.shape
    c0 = col0 // tn
    return pl.pallas_call(
        body,
        grid=(m // tm, n // tn),
        in_specs=[pl.BlockSpec((tm, k), lambda i, j: (i, 0)), pl.BlockSpec((tn, k), lambda i, j: (c0 + j, 0)),
                  *extra_specs],
        out_specs=pl.BlockSpec((tm, tn), lambda i, j: (i, j)),
        out_shape=jax.ShapeDtypeStruct((m, n), out_dtype),
        compiler_params=_params(("parallel", "parallel")),
        name=name,
    )(a, w, *extra)


def _norm_matmul_kernel(x_ref, g_ref, w_ref, o_ref, xn_scr):
    @pl.when(pl.program_id(1) == 0)
    def _():
        xn_scr[...] = _rms(x_ref[...], g_ref[...]).astype(BF16)

    o_ref[...] = _dot(xn_scr[...], w_ref[...].astype(BF16)).astype(o_ref.dtype)


def _norm_matmul(x, gain, w, out_dtype, tm, tn, name):
    m, k = x.shape
    n = w.shape[1]
    return pl.pallas_call(
        _norm_matmul_kernel,
        grid=(m // tm, n // tn),
        in_specs=[pl.BlockSpec((tm, k), lambda i, j: (i, 0)), pl.BlockSpec((1, k), lambda i, j: (0, 0)),
                  pl.BlockSpec((k, tn), lambda i, j: (0, j))],
        out_specs=pl.BlockSpec((tm, tn), lambda i, j: (i, j)),
        out_shape=jax.ShapeDtypeStruct((m, n), out_dtype),
        scratch_shapes=[pltpu.VMEM((tm, k), BF16)],
        compiler_params=_params(("parallel", "arbitrary")),
        name=name,
    )(x, gain, w)


def _gla_levels(chunk):
    return [chunk >> (i + 1) for i in range(int(np.log2(chunk)))]


def _gla_exponent_matrix(chunk):
    c = chunk
    i = np.arange(c)[:, None]
    t = np.arange(c)[None, :]
    blocks = [(t <= i), (t > i)]
    for h in _gla_levels(c):
        r = (i // (2 * h)) * (2 * h) + h
        upper = i >= r
        blocks.append(np.where(upper, (t > r) & (t <= i), (t > i) & (t <= r)))
    m = np.concatenate(blocks, axis=0).astype(np.float32)
    z = np.zeros_like(m)
    return np.concatenate([np.concatenate([m, m, z, z], axis=1), np.concatenate([z, z, m, m], axis=1)], axis=0)


def _gla_kernel(q_ref, k_ref, v_ref, sm_ref, r_ref, wlr_ref, blr_ref, gn_ref, mst_ref, o_ref, s_scr, e0_scr,
                e1_scr):
    c = GLA_CHUNK
    seq = q_ref.shape[0]
    s_scr[...] = jnp.zeros_like(s_scr)
    ri = lax.broadcasted_iota(jnp.int32, (c, c), 0)
    ci = lax.broadcasted_iota(jnp.int32, (c, c), 1)
    rowi = lax.broadcasted_iota(jnp.int32, (c, GLA_DK), 0)
    levels = _gla_levels(c)
    assert GLA_GROUP == 4

    grp = GLA_GROUP
    gs = range(grp)

    n_trips = seq // (grp * c)

    def gate_logits(n):
        allrows = pl.ds(pl.multiple_of(n * (grp * c), grp * c), grp * c)
        return _dot(sm_ref[allrows, :].astype(BF16), wlr_ref[...]) + blr_ref[...]

    def decays(x, e_ref):
        lg = (jnp.minimum(x, 0.0) - jnp.log1p(jnp.exp(-jnp.abs(x)))) * (1.0 / GLA_TAU)
        hi, mid = _split2(lg)
        hm = [jnp.concatenate([hi[g * c:(g + 1) * c], mid[g * c:(g + 1) * c]], axis=0) for g in gs]
        rhs = jnp.concatenate([jnp.concatenate([hm[0], hm[1]], axis=1),
                               jnp.concatenate([hm[2], hm[3]], axis=1)], axis=0)
        e_ref[...] = jnp.exp(_dot(mst_ref[...], rhs))

    def trip(n, e_ref, between):
        row0 = pl.multiple_of(n * (grp * c), grp * c)
        rows = [pl.ds(row0 + g * c, c) for g in gs]
        ex = e_ref[...]
        nr = ex.shape[0] // 2
        e = [ex[0:nr, 0:GLA_DK], ex[0:nr, GLA_DK:], ex[nr:, 0:GLA_DK], ex[nr:, GLA_DK:]]
        q = [q_ref[rows[g], :] * (GLA_DK ** -0.5) for g in gs]
        k = [k_ref[rows[g], :] for g in gs]
        v = [v_ref[rows[g], :].astype(BF16) for g in gs]
        attn = [jnp.where(ri == ci, _dot_nt(q[g].astype(BF16), k[g].astype(BF16)), 0.0) for g in gs]
        for lvl, h in enumerate(levels):
            upper = (rowi & h) != 0
            sh = int(np.log2(2 * h))
            same = (ri >> sh) == (ci >> sh)
            for g in gs:
                f = e[g][(2 + lvl) * c:(3 + lvl) * c]
                qt = jnp.where(upper, q[g] * f, 0.0).astype(BF16)
                kt = jnp.where(upper, 0.0, k[g] * f).astype(BF16)
                attn[g] = attn[g] + jnp.where(same, _dot_nt(qt, kt), 0.0)
        vk = [_dot_tn(v[g], (k[g] * e[g][c:2 * c]).astype(BF16)) for g in gs]
        between()
        av = [_dot(attn[g].astype(BF16), v[g]) for g in gs]
        st = s_scr[...]
        for g in gs:
            o = _dot_nt((q[g] * e[g][0:c]).astype(BF16), st.astype(BF16)) + av[g]
            st = st * e[g][c - 1:c, :] + vk[g]
            on = _rms(o, gn_ref[...]) * _silu(r_ref[rows[g], :].astype(F32))
            o_ref[rows[g], :] = on.astype(o_ref.dtype)
        s_scr[...] = st

    assert n_trips % 2 == 0
    decays(gate_logits(0), e0_scr)

    def body(m, carry):
        x1 = gate_logits(2 * m + 1)
        trip(2 * m, e0_scr, lambda: decays(x1, e1_scr))
        x2 = gate_logits(jnp.minimum(2 * m + 2, n_trips - 1))
        trip(2 * m + 1, e1_scr, lambda: decays(x2, e0_scr))
        return carry

    lax.fori_loop(0, n_trips // 2, body, 0)


def _gla(pa, pb, ps, wlr, blr, gnorm, bsz, seq):
    t = bsz * seq
    mst = jnp.asarray(_gla_exponent_matrix(GLA_CHUNK), BF16)
    nq = GLA_QK // GLA_DK
    return pl.pallas_call(
        _gla_kernel,
        grid=(bsz, GLA_HEADS),
        in_specs=[
            pl.BlockSpec((seq, GLA_DK), lambda b, h: (b, h)),
            pl.BlockSpec((seq, GLA_DK), lambda b, h: (b, nq + h)),
            pl.BlockSpec((seq, GLA_DV), lambda b, h: (b, 2 * GLA_QK // GLA_DV + h)),
            pl.BlockSpec((seq, LANES), lambda b, h: (b, 0)),
            pl.BlockSpec((seq, GLA_DV), lambda b, h: (b, _PB_R // GLA_DV + h)),
            pl.BlockSpec((LANES, GLA_DK), lambda b, h: (0, h)),
            pl.BlockSpec((1, GLA_DK), lambda b, h: (0, h)),
            pl.BlockSpec((1, GLA_DV), lambda b, h: (0, 0)),
            pl.BlockSpec(mst.shape, lambda b, h: (0, 0)),
        ],
        out_specs=pl.BlockSpec((seq, GLA_DV), lambda b, h: (b, h)),
        out_shape=jax.ShapeDtypeStruct((t, GLA_V), BF16),
        scratch_shapes=[pltpu.VMEM((GLA_DV, GLA_DK), F32), pltpu.VMEM((mst.shape[0], 2 * GLA_DK), F32),
                        pltpu.VMEM((mst.shape[0], 2 * GLA_DK), F32)],
        compiler_params=_params(("parallel", "parallel")),
        name="gla",
    )(pa, pa, pa, ps, pb, wlr, blr, gnorm, mst)


def _gdn_kernel(q_ref, k_ref, v_ref, sm_ref, z_ref, gn_ref, o_ref, bms, ns, qms, os_, egl, s_scr):
    c = GDN_CHUNK
    seq = q_ref.shape[0]
    head0 = pl.program_id(1) * GDN_HPS

    ri = lax.broadcasted_iota(jnp.int32, (c, c), 0)
    ci = lax.broadcasted_iota(jnp.int32, (c, c), 1)
    lane = lax.broadcasted_iota(jnp.int32, (c, LANES), 1)
    tril = jnp.where(ri >= ci, 1.0, 0.0).astype(BF16)
    causal = ri >= ci
    n_sq = int(np.log2(c))
    lower_left = [(((ri ^ ci) >> (l + 1)) == 0) & ((ri & (1 << l)) != 0) & ((ci & (1 << l)) == 0) for l in range(n_sq)]

    grp = GDN_GROUP
    chains = [(g, j) for g in range(grp) for j in range(GDN_HPS)]

    def phase1(n, hooks):
        rows, q, k, kb, gb, rhs0 = {}, {}, {}, {}, {}, {}
        for g in range(grp):
            rows[g] = pl.ds(pl.multiple_of((n * grp + g) * c, c), c)
            sm = sm_ref[rows[g], :]
            for j in range(GDN_HPS):
                sl = slice(j * GDN_DK, (j + 1) * GDN_DK)
                q[g, j] = q_ref[rows[g], sl]
                k[g, j] = k_ref[rows[g], sl]
                gcol = jnp.sum(jnp.where(lane == SM_A + head0 + j, sm, 0.0), axis=-1, keepdims=True)
                beta = jnp.sum(jnp.where(lane == SM_B + head0 + j, sm, 0.0), axis=-1, keepdims=True)
                gb[g, j] = jnp.broadcast_to(gcol, (c, LANES))
                kb[g, j] = k[g, j] * beta
                rhs0[g, j] = v_ref[rows[g], sl] * beta
        cum_r, cum_c, kt = {}, {}, {}
        for ch in chains:
            hi, mid = _split2(gb[ch])
            r = _dot(tril, jnp.concatenate([hi, mid], axis=1))
            cum_r[ch] = r[:, :LANES] + r[:, LANES:]
        for ch in chains:
            cum_c[ch] = cum_r[ch].T
            kt[ch] = k[ch].T
        kk = {}
        for ch in chains:
            k16 = k[ch].astype(BF16)
            kk[ch] = _dot_nt(jnp.concatenate([kb[ch].astype(BF16), q[ch].astype(BF16)], axis=0), k16)
        low, at16 = {}, {}
        for ch in chains:
            dec = jnp.where(causal, jnp.exp(jnp.where(causal, cum_r[ch] - cum_c[ch], 0.0)), 0.0)
            low[ch] = jnp.where(ri > ci, kk[ch][:c] * dec, 0.0)
            at16[ch] = jnp.where(causal, kk[ch][c:] * dec, 0.0).astype(BF16)
        x = {ch: -jnp.where(lower_left[0], low[ch], 0.0) for ch in chains}
        for lvl in range(1, n_sq):
            e = {}
            for ch in chains:
                cm = jnp.where(lower_left[lvl], low[ch], 0.0)
                e[ch] = cm + _mm(x[ch], cm)
            for ch in chains:
                x[ch] = x[ch] - e[ch] - _mm(e[ch], x[ch])
            if lvl in hooks:
                hooks[lvl]()
        wu, egc = {}, {}
        for ch in chains:
            egc[ch] = jnp.exp(cum_r[ch])
            rhs = jnp.concatenate([kb[ch] * egc[ch], rhs0[ch]], axis=1)
            wu[ch] = (rhs + _mm(x[ch], rhs)).astype(BF16)
        for ch in chains:
            g, j = ch
            sl = slice(j * GDN_DK, (j + 1) * GDN_DK)
            glast = cum_r[ch][c - 1:c, :]
            kdt16 = (kt[ch] * jnp.exp(glast - cum_c[ch][0:1, :])).astype(BF16)
            r = _dot(jnp.concatenate([kdt16, at16[ch]], axis=0), wu[ch])
            bms[rows[g], sl] = (-r[:c, :GDN_DV]).astype(BF16)
            ns[rows[g], sl] = r[:c, GDN_DV:]
            qms[rows[g], sl] = (q[ch] * egc[ch] - r[c:, :GDN_DV]).astype(BF16)
            os_[rows[g], sl] = r[c:, GDN_DV:]
            egl[pl.ds(pl.multiple_of((n * grp + g) * 8, 8), 8), sl] = jnp.broadcast_to(jnp.exp(glast), (8, LANES))

    s_scr[...] = jnp.zeros_like(s_scr)

    def phase2(n):
        rows = pl.ds(pl.multiple_of(n * c, c), c)
        sls = [slice(j * GDN_DK, (j + 1) * GDN_DK) for j in range(GDN_HPS)]
        s = [s_scr[j] for j in range(GDN_HPS)]
        s16 = [sj.astype(BF16) for sj in s]
        r = [_dot(jnp.concatenate([bms[rows, sl], qms[rows, sl]], axis=0), s16[j]) for j, sl in enumerate(sls)]
        for j, sl in enumerate(sls):
            eg = egl[pl.ds(pl.multiple_of(n * 8, 8), 8), sl][0:1, :]
            s_scr[j] = s[j] * eg + r[j][:c] + ns[rows, sl]
            o = r[j][c:] + os_[rows, sl]
            on = _rms(o, gn_ref[...]) * _silu(z_ref[rows, sl].astype(F32))
            o_ref[rows, sl] = on.astype(o_ref.dtype)

    assert grp == 2 and n_sq >= 5
    n_trips = seq // (grp * c)
    phase1(0, {})

    def body(n, carry):
        first = (n - 1) * grp
        phase1(n, {1: lambda: phase2(first), n_sq - 2: lambda: phase2(first + 1)})
        return carry

    lax.fori_loop(1, n_trips, body, 0)
    for g in range(grp):
        phase2((n_trips - 1) * grp + g)


def _gdn(pg, pb, ps, gnorm, bsz, seq):
    t = bsz * seq
    nb = seq // GDN_CHUNK
    wd = GDN_HPS * GDN_DK
    ng = GDN_HEADS // GDN_HPS
    return pl.pallas_call(
        _gdn_kernel,
        grid=(bsz, ng),
        in_specs=[
            pl.BlockSpec((seq, wd), lambda b, h: (b, h)),
            pl.BlockSpec((seq, wd), lambda b, h: (b, ng + h)),
            pl.BlockSpec((seq, wd), lambda b, h: (b, 2 * ng + h)),
            pl.BlockSpec((seq, LANES), lambda b, h: (b, 0)),
            pl.BlockSpec((seq, wd), lambda b, h: (b, _PB_Z // wd + h)),
            pl.BlockSpec((1, GDN_DV), lambda b, h: (0, 0)),
        ],
        out_specs=pl.BlockSpec((seq, wd), lambda b, h: (b, h)),
        out_shape=jax.ShapeDtypeStruct((t, GDN_V), BF16),
        scratch_shapes=[
            pltpu.VMEM((seq, wd), BF16), pltpu.VMEM((seq, wd), F32), pltpu.VMEM((seq, wd), BF16),
            pltpu.VMEM((seq, wd), F32), pltpu.VMEM((nb * 8, wd), F32),
            pltpu.VMEM((GDN_HPS, GDN_DK, GDN_DV), F32),
        ],
        compiler_params=_params(("parallel", "parallel")),
        name="gdn",
    )(pg, pg, pg, ps, pb, gnorm)


def _xa_kernel(q_ref, mk_ref, mv_ref, o_ref):
    s = _dot_nt(q_ref[...], mk_ref[...]) * (XA_DH ** -0.5)
    p = jnp.exp(s - jnp.max(s, axis=-1, keepdims=True))
    den = jnp.sum(p, axis=-1, keepdims=True)
    o_ref[...] = (_dot(p.astype(BF16), mv_ref[...]) / den).astype(o_ref.dtype)


def _xa(pb, mkv, bsz, seq, ts=2048):
    t = bsz * seq
    nt = seq // ts
    return pl.pallas_call(
        _xa_kernel,
        grid=(bsz, XA_HEADS, nt),
        in_specs=[
            pl.BlockSpec((ts, XA_DH), lambda b, h, i: (b * nt + i, _PB_XQ // XA_DH + h)),
            pl.BlockSpec((MEM_LEN, XA_DH), lambda b, h, i: (b, h)),
            pl.BlockSpec((MEM_LEN, XA_DH), lambda b, h, i: (b, XA_HEADS + h)),
        ],
        out_specs=pl.BlockSpec((ts, XA_DH), lambda b, h, i: (b * nt + i, h)),
        out_shape=jax.ShapeDtypeStruct((t, XA_W), BF16),
        compiler_params=_params(("parallel", "parallel", "parallel")),
        name="xattn",
    )(pb, mkv, mkv)


def _merge_kernel(h_ref, oa_ref, ob_ref, oc_ref, ga_ref, gb_ref, gc_ref, wa_ref, wb_ref, wc_ref, wo_ref,
                  gpost_ref, o_ref):
    mixed = jax.nn.sigmoid(ga_ref[...].astype(F32)) * _dot(oa_ref[...], wa_ref[...])
    mixed = mixed + jax.nn.sigmoid(gb_ref[...].astype(F32)) * _dot(ob_ref[...], wb_ref[...])
    mixed = mixed + jax.nn.sigmoid(gc_ref[...].astype(F32)) * _dot(oc_ref[...], wc_ref[...])
    m = _dot(mixed.astype(BF16), wo_ref[...])
    o_ref[...] = h_ref[...] + _rms(m, gpost_ref[...])


def _merge(h, oa, ob, oc, pb, wa, wb, wc, wo, gpost, tm=256):
    t, d = h.shape
    row = lambda i: (i, 0)
    const = lambda i: (0, 0)
    once = pl.Buffered(1)
    return pl.pallas_call(
        _merge_kernel,
        grid=(t // tm,),
        in_specs=[
            pl.BlockSpec((tm, d), row),
            pl.BlockSpec((tm, GLA_V), row), pl.BlockSpec((tm, GDN_V), row), pl.BlockSpec((tm, XA_W), row),
            pl.BlockSpec((tm, d), lambda i: (i, 0)), pl.BlockSpec((tm, d), lambda i: (i, 1)),
            pl.BlockSpec((tm, d), lambda i: (i, 2)),
            pl.BlockSpec((GLA_V, d), const, pipeline_mode=once), pl.BlockSpec((GDN_V, d), const, pipeline_mode=once),
            pl.BlockSpec((XA_W, d), const, pipeline_mode=once), pl.BlockSpec((d, d), const, pipeline_mode=once),
            pl.BlockSpec((1, d), const),
        ],
        out_specs=pl.BlockSpec((tm, d), row),
        out_shape=jax.ShapeDtypeStruct((t, d), F32),
        compiler_params=_params(("parallel",)),
        name="merge",
    )(h, oa, ob, oc, pb, pb, pb, wa, wb, wc, wo, gpost)


_PA_GDN = 2 * GLA_QK + GLA_V
_PA_COLS = _PA_GDN + 2 * GDN_QK + GDN_V
_PB_R = N_BRANCH * D_MODEL
_PB_Z = _PB_R + GLA_V
_PB_XQ = _PB_Z + GDN_V
_PB_COLS = _PB_XQ + XA_W


_W_ORDER = (0, 1, 2, 5, 10, 4, 8, 9)
_W_SMALL = ((_IN_OFF[3], GLA_RANK), (_IN_OFF[6], 2 * GDN_HEADS))
_W_ALL_COLS = _PA_COLS + _PB_COLS + LANES
_W_RB = 512


def _w_block_rows():
    tab = []
    for i in _W_ORDER:
        assert (_IN_OFF[i + 1] - _IN_OFF[i]) % _W_RB == 0
        tab.extend(range(_IN_OFF[i], _IN_OFF[i + 1], _W_RB))
    return np.asarray(tab, np.int32)


def _relayout_kernel(tab_ref, wt_hbm, o_ref, buf, sem):
    j = pl.program_id(0)
    n_main = pl.num_programs(0) - 1

    def block_copy(blk):
        start = pl.multiple_of(tab_ref[blk], 16)
        return pltpu.make_async_copy(wt_hbm.at[pl.ds(start, _W_RB), :], buf.at[blk % 2], sem.at[blk % 2])

    @pl.when(j == 0)
    def _():
        block_copy(j).start()

    @pl.when(j + 1 < n_main)
    def _():
        block_copy(j + 1).start()

    @pl.when(j < n_main)
    def _():
        block_copy(j).wait()
        o_ref[...] = buf[j % 2].astype(BF16)

    @pl.when(j == n_main)
    def _():
        dst = 0
        copies = []
        for n, (row0, rows) in enumerate(_W_SMALL):
            copies.append(pltpu.make_async_copy(wt_hbm.at[pl.ds(row0, rows), :], buf.at[0, pl.ds(dst, rows), :],
                                                sem.at[n]))
            dst += rows
        for cp in copies:
            cp.start()
        for cp in copies:
            cp.wait()
        o_ref[0:dst, :] = buf[0, 0:dst, :].astype(BF16)
        o_ref[dst:, :] = jnp.zeros((_W_RB - dst, o_ref.shape[1]), BF16)


def _split_w_in(w_in_t):
    n, k = w_in_t.shape
    tab = _w_block_rows()
    return pl.pallas_call(
        _relayout_kernel,
        grid_spec=pltpu.PrefetchScalarGridSpec(
            num_scalar_prefetch=1,
            grid=(len(tab) + 1,),
            in_specs=[pl.BlockSpec(memory_space=pl.ANY)],
            out_specs=pl.BlockSpec((_W_RB, k), lambda i, tab: (i, 0)),
            scratch_shapes=[pltpu.VMEM((2, _W_RB, k), F32), pltpu.SemaphoreType.DMA((2,))],
        ),
        out_shape=jax.ShapeDtypeStruct((_W_ALL_COLS, k), BF16),
        compiler_params=_params(("arbitrary",)),
        name="w_in_relayout",
    )(jnp.asarray(tab), w_in_t)


def kernel(x, mem, n_ffn1_pre, w_ffn1_gu, w_ffn1_down, n_ffn1_post, n_mix_pre, w_in, gla_w_lr2, gla_b_lr,
           gla_norm, gdn_conv, gdn_a_log, gdn_dt_bias, gdn_norm, mem_norm, w_mem_kv, w_up_gla, w_up_gdn,
           w_up_xa, w_out, n_mix_post, n_ffn2_pre, w_ffn2_gu, w_ffn2_down, n_ffn2_post):
    bsz, seq, d = x.shape
    t = bsz * seq
    h = x.reshape(t, d)
    for l in range(n_ffn1_pre.shape[0]):
        row = lambda a: a[l][None, :]
        h, u = _ffn(h, row(n_ffn1_pre), w_ffn1_gu[l], w_ffn1_down[l],
                    row(n_ffn1_post), row(n_mix_pre), emit_next=True)

        w_all = _split_w_in(w_in[l].T)
        par = jnp.zeros((2, LANES), F32)
        par = par.at[0, SM_A:SM_A + GDN_HEADS].set(gdn_a_log[l]).at[1, SM_A:SM_A + GDN_HEADS].set(gdn_dt_bias[l])
        tn_g = 1024
        pa = _matmul(u, w_all, 0, _PA_GDN, F32, 2048, 1024, "in_proj_a")
        pg = _matmul(u, w_all, _PA_GDN, _PA_COLS - _PA_GDN, F32, seq, tn_g, "in_proj_g", body=_proj_conv_kernel,
                     extra=(gdn_conv[l],), extra_specs=(pl.BlockSpec((CONV_W, tn_g), lambda i, j: (0, j)),))
        pb = _matmul(u, w_all, _PA_COLS, _PB_COLS, BF16, 2048, 1024, "in_proj_b")
        ps = _matmul(u, w_all, _PA_COLS + _PB_COLS, LANES, F32, 2048, LANES, "in_proj_s", body=_proj_s_kernel,
                     extra=(par,), extra_specs=(pl.BlockSpec((2, LANES), lambda i, j: (0, 0)),))

        wlr = jnp.pad(gla_w_lr2[l], ((0, LANES - GLA_RANK), (0, 0))).astype(BF16)
        o_gla = _gla(pa, pb, ps, wlr, row(gla_b_lr), row(gla_norm), bsz, seq)
        o_gdn = _gdn(pg, pb, ps, row(gdn_norm), bsz, seq)

        mkv = _norm_matmul(mem.reshape(bsz * MEM_LEN, d), row(mem_norm), w_mem_kv[l], BF16,
                           bsz * MEM_LEN, 512, "mem_kv")
        o_xa = _xa(pb, mkv, bsz, seq)

        h = _merge(h, o_gla, o_gdn, o_xa, pb, w_up_gla[l].astype(BF16), w_up_gdn[l].astype(BF16),
                   w_up_xa[l].astype(BF16), w_out[l].astype(BF16), row(n_mix_post))
        h = _ffn(h, row(n_ffn2_pre), w_ffn2_gu[l], w_ffn2_down[l],
                 row(n_ffn2_post), row(n_ffn2_post), emit_next=False)
    return h.reshape(bsz, seq, d)
```

```python
import functools

import numpy as np
import jax
import jax.numpy as jnp
from jax import lax
from jax.experimental import pallas as pl
from jax.experimental.pallas import tpu as pltpu

F32 = jnp.float32
BF16 = jnp.bfloat16

D_MODEL = 2048
MEM_LEN = 256
EPS = 1e-6
GLA_HEADS = 4
GLA_DK = 128
GLA_DV = 256
GLA_QK = GLA_HEADS * GLA_DK
GLA_V = GLA_HEADS * GLA_DV
GLA_RANK = 16
GLA_TAU = 16.0
GDN_HEADS = 8
GDN_DK = 128
GDN_DV = 128
GDN_QK = GDN_HEADS * GDN_DK
GDN_V = GDN_HEADS * GDN_DV
CONV_W = 4
XA_HEADS = 4
XA_DH = 256
XA_W = XA_HEADS * XA_DH
N_BRANCH = 3
D_FF = 5632

LANES = 128
GLA_CHUNK = 64
GLA_GROUP = 4
GDN_CHUNK = 128
GDN_HPS = 4
GDN_GROUP = 2
VMEM_LIMIT = 56 * 1024 * 1024

_IN_SIZES = (GLA_QK, GLA_QK, GLA_V, GLA_RANK, GLA_V, 2 * GDN_QK + GDN_V, GDN_HEADS, GDN_HEADS, GDN_V,
             XA_W, N_BRANCH * D_MODEL)
_IN_OFF = tuple(int(v) for v in np.cumsum((0,) + _IN_SIZES))
SM_LR, SM_B, SM_A = 0, GLA_RANK, GLA_RANK + GDN_HEADS


def _dot(a, b):
    return jnp.dot(a, b, preferred_element_type=F32)


def _dot_nt(a, b):
    return lax.dot_general(a, b, (((1,), (1,)), ((), ())), preferred_element_type=F32)


def _dot_tn(a, b):
    return lax.dot_general(a, b, (((0,), (0,)), ((), ())), preferred_element_type=F32)


def _mm(a, b):
    return _dot(a.astype(BF16), b.astype(BF16))


def _split2(x):
    hi = x.astype(BF16)
    mid = (x - hi.astype(F32)).astype(BF16)
    return hi, mid


def _rms(x, gain):
    return x * lax.rsqrt(jnp.mean(x * x, axis=-1, keepdims=True) + EPS) * gain


def _silu(x):
    return x * jax.nn.sigmoid(x)


def _softplus(x):
    return jnp.maximum(x, 0.0) + jnp.log1p(jnp.exp(-jnp.abs(x)))


def _params(sem):
    return pltpu.CompilerParams(dimension_semantics=sem, vmem_limit_bytes=VMEM_LIMIT)


def _ffn_kernel(x_ref, gpre_ref, wgu_hbm, wd_hbm, gpost_ref, gnext_ref, *out_and_scratch, emit_next, tf):
    if emit_next:
        h_ref, un_ref, xn_scr, wg_buf, wu_buf, wd_buf, wg16, wu16, wd16, sem = out_and_scratch
    else:
        h_ref, xn_scr, wg_buf, wu_buf, wd_buf, wg16, wu16, wd16, sem = out_and_scratch
    i = pl.program_id(0)
    nf = D_FF // tf
    assert nf % 2 == 0

    def tile_copies(jj, slot):
        col = pl.multiple_of(jj * tf, tf)
        return (pltpu.make_async_copy(wgu_hbm.at[:, pl.ds(col, tf)], wg_buf.at[slot], sem.at[0, slot]),
                pltpu.make_async_copy(wgu_hbm.at[:, pl.ds(D_FF + col, tf)], wu_buf.at[slot], sem.at[1, slot]),
                pltpu.make_async_copy(wd_hbm.at[pl.ds(col, tf), :], wd_buf.at[slot], sem.at[2, slot]))

    @pl.when(i == 0)
    def _():
        for cp in tile_copies(0, 0):
            cp.start()

    xn_scr[...] = _rms(x_ref[...], gpre_ref[...]).astype(BF16)
    h_ref[...] = jnp.zeros_like(h_ref)

    def body(jj, carry):
        slot = jj % 2
        @pl.when(jj + 1 < nf)
        def _():
            for cp in tile_copies(jj + 1, 1 - slot):
                cp.start()

        @pl.when((jj + 1 == nf) & (i + 1 < pl.num_programs(0)))
        def _():
            for cp in tile_copies(0, 0):
                cp.start()

        for cp in tile_copies(jj, slot):
            cp.wait()
        wg16[...] = wg_buf[slot].astype(BF16)
        wu16[...] = wu_buf[slot].astype(BF16)
        wd16[...] = wd_buf[slot].astype(BF16)
        xn = xn_scr[...]
        g = _dot(xn, wg16[...])
        u = _dot(xn, wu16[...])
        hm = (_silu(g) * u).astype(BF16)
        h_ref[...] += _dot(hm, wd16[...])
        return carry

    lax.fori_loop(0, nf, body, 0)

    h = x_ref[...] + 0.5 * _rms(h_ref[...], gpost_ref[...])
    h_ref[...] = h
    if emit_next:
        un_ref[...] = _rms(h, gnext_ref[...]).astype(BF16)


def _ffn(x, gpre, w_gu, w_down, gpost, gnext, emit_next, tm=1024, tf=256):
    t, d = x.shape
    row = lambda i: (i, 0)
    const = lambda i: (0, 0)
    once = pl.Buffered(1)
    out_shape = [jax.ShapeDtypeStruct((t, d), F32)]
    out_specs = [pl.BlockSpec((tm, d), row, pipeline_mode=once)]
    if emit_next:
        out_shape.append(jax.ShapeDtypeStruct((t, d), BF16))
        out_specs.append(pl.BlockSpec((tm, d), row, pipeline_mode=once))
    res = pl.pallas_call(
        functools.partial(_ffn_kernel, emit_next=emit_next, tf=tf),
        grid=(t // tm,),
        in_specs=[
            pl.BlockSpec((tm, d), row),
            pl.BlockSpec((1, d), const),
            pl.BlockSpec(memory_space=pl.ANY),
            pl.BlockSpec(memory_space=pl.ANY),
            pl.BlockSpec((1, d), const),
            pl.BlockSpec((1, d), const),
        ],
        out_specs=out_specs,
        out_shape=out_shape,
        scratch_shapes=[pltpu.VMEM((tm, d), BF16),
                        pltpu.VMEM((2, d, tf), F32), pltpu.VMEM((2, d, tf), F32), pltpu.VMEM((2, tf, d), F32),
                        pltpu.VMEM((d, tf), BF16), pltpu.VMEM((d, tf), BF16), pltpu.VMEM((tf, d), BF16),
                        pltpu.SemaphoreType.DMA((3, 2))],
        compiler_params=_params(("arbitrary",)),
        name="ffn",
    )(x, gpre, w_gu, w_down, gpost, gnext)
    return res if emit_next else res[0]


def _matmul_kernel(a_ref, w_ref, o_ref):
    o_ref[...] = _dot_nt(a_ref[...], w_ref[...]).astype(o_ref.dtype)


def _proj_conv_kernel(a_ref, w_ref, cw_ref, o_ref):
    tm, tn = o_ref.shape
    rb = 256
    kind = pl.program_id(1) // (GDN_QK // tn)
    scale = jnp.where(kind == 0, GDN_DK ** -0.5, 1.0)
    w = cw_ref[...]
    r8 = lax.broadcasted_iota(jnp.int32, (8, tn), 0)

    sb = 32

    def finish(r, before):
        for s0 in range(0, rb, sb):
            y = o_ref[r + s0:r + s0 + sb, :]
            acc = y * w[CONV_W - 1:CONV_W, :]
            for sft in range(1, CONV_W):
                xs = pltpu.roll(y, sft, 0)
                top = jnp.where(r8 < sft, pltpu.roll(before, sft, 0), xs[0:8])
                acc = acc + jnp.concatenate([top, xs[8:]], axis=0) * w[CONV_W - 1 - sft:CONV_W - sft, :]
            before = y[sb - 8:sb]
            c = _silu(acc)
            for hh in range(tn // GDN_DK):
                sl = slice(hh * GDN_DK, (hh + 1) * GDN_DK)
                blk = c[:, sl]
                f = lax.rsqrt(jnp.sum(blk * blk, axis=-1, keepdims=True) + EPS) * scale
                o_ref[r + s0:r + s0 + sb, sl] = blk * jnp.where(kind == 2, 1.0, f)
        return before

    o_ref[0:rb, :] = _dot_nt(a_ref[0:rb, :], w_ref[...])
    before = jnp.zeros((8, tn), F32)
    for r in range(rb, tm + rb, rb):
        if r < tm:
            o_ref[r:r + rb, :] = _dot_nt(a_ref[r:r + rb, :], w_ref[...])
        before = finish(r - rb, before)


def _proj_s_kernel(a_ref, w_ref, par_ref, o_ref):
    y = _dot_nt(a_ref[...], w_ref[...])
    par = par_ref[...]
    lane = lax.broadcasted_iota(jnp.int32, y.shape, 1)
    g = -jnp.exp(par[0:1, :]) * _softplus(y + par[1:2, :])
    is_g = (lane >= SM_A) & (lane < SM_A + GDN_HEADS)
    is_b = (lane >= SM_B) & (lane < SM_B + GDN_HEADS)
    o_ref[...] = jnp.where(is_g, g, jnp.where(is_b, jax.nn.sigmoid(y), y))


def _matmul(a, w, col0, n, out_dtype, tm, tn, name, body=_matmul_kernel, extra=(), extra_specs=()):
    m, k = a.shape
    c0 = col0 // tn
    return pl.pallas_call(
        body,
        grid=(m // tm, n // tn),
        in_specs=[pl.BlockSpec((tm, k), lambda i, j: (i, 0)), pl.BlockSpec((tn, k), lambda i, j: (c0 + j, 0)),
                  *extra_specs],
        out_specs=pl.BlockSpec((tm, tn), lambda i, j: (i, j)),
        out_shape=jax.ShapeDtypeStruct((m, n), out_dtype),
        compiler_params=_params(("parallel", "parallel")),
        name=name,
    )(a, w, *extra)


def _norm_matmul_kernel(x_ref, g_ref, w_ref, o_ref, xn_scr):
    @pl.when(pl.program_id(1) == 0)
    def _():
        xn_scr[...] = _rms(x_ref[...], g_ref[...]).astype(BF16)

    o_ref[...] = _dot(xn_scr[...], w_ref[...].astype(BF16)).astype(o_ref.dtype)


def _norm_matmul(x, gain, w, out_dtype, tm, tn, name):
    m, k = x.shape
    n = w.shape[1]
    return pl.pallas_call(
        _norm_matmul_kernel,
        grid=(m // tm, n // tn),
        in_specs=[pl.BlockSpec((tm, k), lambda i, j: (i, 0)), pl.BlockSpec((1, k), lambda i, j: (0, 0)),
                  pl.BlockSpec((k, tn), lambda i, j: (0, j))],
        out_specs=pl.BlockSpec((tm, tn), lambda i, j: (i, j)),
        out_shape=jax.ShapeDtypeStruct((m, n), out_dtype),
        scratch_shapes=[pltpu.VMEM((tm, k), BF16)],
        compiler_params=_params(("parallel", "arbitrary")),
        name=name,
    )(x, gain, w)


def _gla_levels(chunk):
    return [chunk >> (i + 1) for i in range(int(np.log2(chunk)))]


def _gla_exponent_matrix(chunk):
    c = chunk
    i = np.arange(c)[:, None]
    t = np.arange(c)[None, :]
    blocks = [(t <= i), (t > i)]
    for h in _gla_levels(c):
        r = (i // (2 * h)) * (2 * h) + h
        upper = i >= r
        blocks.append(np.where(upper, (t > r) & (t <= i), (t > i) & (t <= r)))
    m = np.concatenate(blocks, axis=0).astype(np.float32)
    z = np.zeros_like(m)
    return np.concatenate([np.concatenate([m, m, z, z], axis=1), np.concatenate([z, z, m, m], axis=1)], axis=0)


def _gla_kernel(q_ref, k_ref, v_ref, sm_ref, r_ref, wlr_ref, blr_ref, gn_ref, mst_ref, o_ref, s_scr, e0_scr,
                e1_scr):
    c = GLA_CHUNK
    seq = q_ref.shape[0]
    s_scr[...] = jnp.zeros_like(s_scr)
    ri = lax.broadcasted_iota(jnp.int32, (c, c), 0)
    ci = lax.broadcasted_iota(jnp.int32, (c, c), 1)
    rowi = lax.broadcasted_iota(jnp.int32, (c, GLA_DK), 0)
    levels = _gla_levels(c)
    assert GLA_GROUP == 4

    grp = GLA_GROUP
    gs = range(grp)

    n_trips = seq // (grp * c)

    def gate_logits(n):
        allrows = pl.ds(pl.multiple_of(n * (grp * c), grp * c), grp * c)
        return _dot(sm_ref[allrows, :].astype(BF16), wlr_ref[...]) + blr_ref[...]

    def decays(x, e_ref):
        lg = (jnp.minimum(x, 0.0) - jnp.log1p(jnp.exp(-jnp.abs(x)))) * (1.0 / GLA_TAU)
        hi, mid = _split2(lg)
        hm = [jnp.concatenate([hi[g * c:(g + 1) * c], mid[g * c:(g + 1) * c]], axis=0) for g in gs]
        rhs = jnp.concatenate([jnp.concatenate([hm[0], hm[1]], axis=1),
                               jnp.concatenate([hm[2], hm[3]], axis=1)], axis=0)
        e_ref[...] = jnp.exp(_dot(mst_ref[...], rhs))

    def trip(n, e_ref, between):
        row0 = pl.multiple_of(n * (grp * c), grp * c)
        rows = [pl.ds(row0 + g * c, c) for g in gs]
        ex = e_ref[...]
        nr = ex.shape[0] // 2
        e = [ex[0:nr, 0:GLA_DK], ex[0:nr, GLA_DK:], ex[nr:, 0:GLA_DK], ex[nr:, GLA_DK:]]
        q = [q_ref[rows[g], :] * (GLA_DK ** -0.5) for g in gs]
        k = [k_ref[rows[g], :] for g in gs]
        v = [v_ref[rows[g], :].astype(BF16) for g in gs]
        attn = [jnp.where(ri == ci, _dot_nt(q[g].astype(BF16), k[g].astype(BF16)), 0.0) for g in gs]
        for lvl, h in enumerate(levels):
            upper = (rowi & h) != 0
            sh = int(np.log2(2 * h))
            same = (ri >> sh) == (ci >> sh)
            for g in gs:
                f = e[g][(2 + lvl) * c:(3 + lvl) * c]
                qt = jnp.where(upper, q[g] * f, 0.0).astype(BF16)
                kt = jnp.where(upper, 0.0, k[g] * f).astype(BF16)
                attn[g] = attn[g] + jnp.where(same, _dot_nt(qt, kt), 0.0)
        vk = [_dot_tn(v[g], (k[g] * e[g][c:2 * c]).astype(BF16)) for g in gs]
        between()
        av = [_dot(attn[g].astype(BF16), v[g]) for g in gs]
        st = s_scr[...]
        for g in gs:
            o = _dot_nt((q[g] * e[g][0:c]).astype(BF16), st.astype(BF16)) + av[g]
            st = st * e[g][c - 1:c, :] + vk[g]
            on = _rms(o, gn_ref[...]) * _silu(r_ref[rows[g], :].astype(F32))
            o_ref[rows[g], :] = on.astype(o_ref.dtype)
        s_scr[...] = st

    assert n_trips % 2 == 0
    decays(gate_logits(0), e0_scr)

    def body(m, carry):
        x1 = gate_logits(2 * m + 1)
        trip(2 * m, e0_scr, lambda: decays(x1, e1_scr))
        x2 = gate_logits(jnp.minimum(2 * m + 2, n_trips - 1))
        trip(2 * m + 1, e1_scr, lambda: decays(x2, e0_scr))
        return carry

    lax.fori_loop(0, n_trips // 2, body, 0)


def _gla(pa, pb, ps, wlr, blr, gnorm, bsz, seq):
    t = bsz * seq
    mst = jnp.asarray(_gla_exponent_matrix(GLA_CHUNK), BF16)
    nq = GLA_QK // GLA_DK
    return pl.pallas_call(
        _gla_kernel,
        grid=(bsz, GLA_HEADS),
        in_specs=[
            pl.BlockSpec((seq, GLA_DK), lambda b, h: (b, h)),
            pl.BlockSpec((seq, GLA_DK), lambda b, h: (b, nq + h)),
            pl.BlockSpec((seq, GLA_DV), lambda b, h: (b, 2 * GLA_QK // GLA_DV + h)),
            pl.BlockSpec((seq, LANES), lambda b, h: (b, 0)),
            pl.BlockSpec((seq, GLA_DV), lambda b, h: (b, _PB_R // GLA_DV + h)),
            pl.BlockSpec((LANES, GLA_DK), lambda b, h: (0, h)),
            pl.BlockSpec((1, GLA_DK), lambda b, h: (0, h)),
            pl.BlockSpec((1, GLA_DV), lambda b, h: (0, 0)),
            pl.BlockSpec(mst.shape, lambda b, h: (0, 0)),
        ],
        out_specs=pl.BlockSpec((seq, GLA_DV), lambda b, h: (b, h)),
        out_shape=jax.ShapeDtypeStruct((t, GLA_V), BF16),
        scratch_shapes=[pltpu.VMEM((GLA_DV, GLA_DK), F32), pltpu.VMEM((mst.shape[0], 2 * GLA_DK), F32),
                        pltpu.VMEM((mst.shape[0], 2 * GLA_DK), F32)],
        compiler_params=_params(("parallel", "parallel")),
        name="gla",
    )(pa, pa, pa, ps, pb, wlr, blr, gnorm, mst)


def _gdn_kernel(q_ref, k_ref, v_ref, sm_ref, z_ref, gn_ref, o_ref, bms, ns, qms, os_, egl, s_scr):
    c = GDN_CHUNK
    seq = q_ref.shape[0]
    head0 = pl.program_id(1) * GDN_HPS

    ri = lax.broadcasted_iota(jnp.int32, (c, c), 0)
    ci = lax.broadcasted_iota(jnp.int32, (c, c), 1)
    lane = lax.broadcasted_iota(jnp.int32, (c, LANES), 1)
    tril = jnp.where(ri >= ci, 1.0, 0.0).astype(BF16)
    causal = ri >= ci
    n_sq = int(np.log2(c))
    lower_left = [(((ri ^ ci) >> (l + 1)) == 0) & ((ri & (1 << l)) != 0) & ((ci & (1 << l)) == 0) for l in range(n_sq)]

    grp = GDN_GROUP
    chains = [(g, j) for g in range(grp) for j in range(GDN_HPS)]

    def phase1(n, hooks):
        rows, q, k, kb, gb, rhs0 = {}, {}, {}, {}, {}, {}
        for g in range(grp):
            rows[g] = pl.ds(pl.multiple_of((n * grp + g) * c, c), c)
            sm = sm_ref[rows[g], :]
            for j in range(GDN_HPS):
                sl = slice(j * GDN_DK, (j + 1) * GDN_DK)
                q[g, j] = q_ref[rows[g], sl]
                k[g, j] = k_ref[rows[g], sl]
                gcol = jnp.sum(jnp.where(lane == SM_A + head0 + j, sm, 0.0), axis=-1, keepdims=True)
                beta = jnp.sum(jnp.where(lane == SM_B + head0 + j, sm, 0.0), axis=-1, keepdims=True)
                gb[g, j] = jnp.broadcast_to(gcol, (c, LANES))
                kb[g, j] = k[g, j] * beta
                rhs0[g, j] = v_ref[rows[g], sl] * beta
        cum_r, cum_c, kt = {}, {}, {}
        for ch in chains:
            hi, mid = _split2(gb[ch])
            r = _dot(tril, jnp.concatenate([hi, mid], axis=1))
            cum_r[ch] = r[:, :LANES] + r[:, LANES:]
        for ch in chains:
            cum_c[ch] = cum_r[ch].T
            kt[ch] = k[ch].T
        kk = {}
        for ch in chains:
            k16 = k[ch].astype(BF16)
            kk[ch] = _dot_nt(jnp.concatenate([kb[ch].astype(BF16), q[ch].astype(BF16)], axis=0), k16)
        low, at16 = {}, {}
        for ch in chains:
            dec = jnp.where(causal, jnp.exp(jnp.where(causal, cum_r[ch] - cum_c[ch], 0.0)), 0.0)
            low[ch] = jnp.where(ri > ci, kk[ch][:c] * dec, 0.0)
            at16[ch] = jnp.where(causal, kk[ch][c:] * dec, 0.0).astype(BF16)
        x = {ch: -jnp.where(lower_left[0], low[ch], 0.0) for ch in chains}
        for lvl in range(1, n_sq):
            e = {}
            for ch in chains:
                cm = jnp.where(lower_left[lvl], low[ch], 0.0)
                e[ch] = cm + _mm(x[ch], cm)
            for ch in chains:
                x[ch] = x[ch] - e[ch] - _mm(e[ch], x[ch])
            if lvl in hooks:
                hooks[lvl]()
        wu, egc = {}, {}
        for ch in chains:
            egc[ch] = jnp.exp(cum_r[ch])
            rhs = jnp.concatenate([kb[ch] * egc[ch], rhs0[ch]], axis=1)
            wu[ch] = (rhs + _mm(x[ch], rhs)).astype(BF16)
        for ch in chains:
            g, j = ch
            sl = slice(j * GDN_DK, (j + 1) * GDN_DK)
            glast = cum_r[ch][c - 1:c, :]
            kdt16 = (kt[ch] * jnp.exp(glast - cum_c[ch][0:1, :])).astype(BF16)
            r = _dot(jnp.concatenate([kdt16, at16[ch]], axis=0), wu[ch])
            bms[rows[g], sl] = (-r[:c, :GDN_DV]).astype(BF16)
            ns[rows[g], sl] = r[:c, GDN_DV:]
            qms[rows[g], sl] = (q[ch] * egc[ch] - r[c:, :GDN_DV]).astype(BF16)
            os_[rows[g], sl] = r[c:, GDN_DV:]
            egl[pl.ds(pl.multiple_of((n * grp + g) * 8, 8), 8), sl] = jnp.broadcast_to(jnp.exp(glast), (8, LANES))

    s_scr[...] = jnp.zeros_like(s_scr)

    def phase2(n):
        rows = pl.ds(pl.multiple_of(n * c, c), c)
        sls = [slice(j * GDN_DK, (j + 1) * GDN_DK) for j in range(GDN_HPS)]
        s = [s_scr[j] for j in range(GDN_HPS)]
        s16 = [sj.astype(BF16) for sj in s]
        r = [_dot(jnp.concatenate([bms[rows, sl], qms[rows, sl]], axis=0), s16[j]) for j, sl in enumerate(sls)]
        for j, sl in enumerate(sls):
            eg = egl[pl.ds(pl.multiple_of(n * 8, 8), 8), sl][0:1, :]
            s_scr[j] = s[j] * eg + r[j][:c] + ns[rows, sl]
            o = r[j][c:] + os_[rows, sl]
            on = _rms(o, gn_ref[...]) * _silu(z_ref[rows, sl].astype(F32))
            o_ref[rows, sl] = on.astype(o_ref.dtype)

    assert grp == 2 and n_sq >= 5
    n_trips = seq // (grp * c)
    phase1(0, {})

    def body(n, carry):
        first = (n - 1) * grp
        phase1(n, {1: lambda: phase2(first), n_sq - 2: lambda: phase2(first + 1)})
        return carry

    lax.fori_loop(1, n_trips, body, 0)
    for g in range(grp):
        phase2((n_trips - 1) * grp + g)


def _gdn(pg, pb, ps, gnorm, bsz, seq):
    t = bsz * seq
    nb = seq // GDN_CHUNK
    wd = GDN_HPS * GDN_DK
    ng = GDN_HEADS // GDN_HPS
    return pl.pallas_call(
        _gdn_kernel,
        grid=(bsz, ng),
        in_specs=[
            pl.BlockSpec((seq, wd), lambda b, h: (b, h)),
            pl.BlockSpec((seq, wd), lambda b, h: (b, ng + h)),
            pl.BlockSpec((seq, wd), lambda b, h: (b, 2 * ng + h)),
            pl.BlockSpec((seq, LANES), lambda b, h: (b, 0)),
            pl.BlockSpec((seq, wd), lambda b, h: (b, _PB_Z // wd + h)),
            pl.BlockSpec((1, GDN_DV), lambda b, h: (0, 0)),
        ],
        out_specs=pl.BlockSpec((seq, wd), lambda b, h: (b, h)),
        out_shape=jax.ShapeDtypeStruct((t, GDN_V), BF16),
        scratch_shapes=[
            pltpu.VMEM((seq, wd), BF16), pltpu.VMEM((seq, wd), F32), pltpu.VMEM((seq, wd), BF16),
            pltpu.VMEM((seq, wd), F32), pltpu.VMEM((nb * 8, wd), F32),
            pltpu.VMEM((GDN_HPS, GDN_DK, GDN_DV), F32),
        ],
        compiler_params=_params(("parallel", "parallel")),
        name="gdn",
    )(pg, pg, pg, ps, pb, gnorm)


def _xa_kernel(q_ref, mk_ref, mv_ref, o_ref):
    s = _dot_nt(q_ref[...], mk_ref[...]) * (XA_DH ** -0.5)
    p = jnp.exp(s - jnp.max(s, axis=-1, keepdims=True))
    den = jnp.sum(p, axis=-1, keepdims=True)
    o_ref[...] = (_dot(p.astype(BF16), mv_ref[...]) / den).astype(o_ref.dtype)


def _xa(pb, mkv, bsz, seq, ts=2048):
    t = bsz * seq
    nt = seq // ts
    return pl.pallas_call(
        _xa_kernel,
        grid=(bsz, XA_HEADS, nt),
        in_specs=[
            pl.BlockSpec((ts, XA_DH), lambda b, h, i: (b * nt + i, _PB_XQ // XA_DH + h)),
            pl.BlockSpec((MEM_LEN, XA_DH), lambda b, h, i: (b, h)),
            pl.BlockSpec((MEM_LEN, XA_DH), lambda b, h, i: (b, XA_HEADS + h)),
        ],
        out_specs=pl.BlockSpec((ts, XA_DH), lambda b, h, i: (b * nt + i, h)),
        out_shape=jax.ShapeDtypeStruct((t, XA_W), BF16),
        compiler_params=_params(("parallel", "parallel", "parallel")),
        name="xattn",
    )(pb, mkv, mkv)


def _merge_kernel(h_ref, oa_ref, ob_ref, oc_ref, ga_ref, gb_ref, gc_ref, wa_ref, wb_ref, wc_ref, wo_ref,
                  gpost_ref, o_ref):
    mixed = jax.nn.sigmoid(ga_ref[...].astype(F32)) * _dot(oa_ref[...], wa_ref[...])
    mixed = mixed + jax.nn.sigmoid(gb_ref[...].astype(F32)) * _dot(ob_ref[...], wb_ref[...])
    mixed = mixed + jax.nn.sigmoid(gc_ref[...].astype(F32)) * _dot(oc_ref[...], wc_ref[...])
    m = _dot(mixed.astype(BF16), wo_ref[...])
    o_ref[...] = h_ref[...] + _rms(m, gpost_ref[...])


def _merge(h, oa, ob, oc, pb, wa, wb, wc, wo, gpost, tm=256):
    t, d = h.shape
    row = lambda i: (i, 0)
    const = lambda i: (0, 0)
    once = pl.Buffered(1)
    return pl.pallas_call(
        _merge_kernel,
        grid=(t // tm,),
        in_specs=[
            pl.BlockSpec((tm, d), row),
            pl.BlockSpec((tm, GLA_V), row), pl.BlockSpec((tm, GDN_V), row), pl.BlockSpec((tm, XA_W), row),
            pl.BlockSpec((tm, d), lambda i: (i, 0)), pl.BlockSpec((tm, d), lambda i: (i, 1)),
            pl.BlockSpec((tm, d), lambda i: (i, 2)),
            pl.BlockSpec((GLA_V, d), const, pipeline_mode=once), pl.BlockSpec((GDN_V, d), const, pipeline_mode=once),
            pl.BlockSpec((XA_W, d), const, pipeline_mode=once), pl.BlockSpec((d, d), const, pipeline_mode=once),
            pl.BlockSpec((1, d), const),
        ],
        out_specs=pl.BlockSpec((tm, d), row),
        out_shape=jax.ShapeDtypeStruct((t, d), F32),
        compiler_params=_params(("parallel",)),
        name="merge",
    )(h, oa, ob, oc, pb, pb, pb, wa, wb, wc, wo, gpost)


_PA_GDN = 2 * GLA_QK + GLA_V
_PA_COLS = _PA_GDN + 2 * GDN_QK + GDN_V
_PB_R = N_BRANCH * D_MODEL
_PB_Z = _PB_R + GLA_V
_PB_XQ = _PB_Z + GDN_V
_PB_COLS = _PB_XQ + XA_W


_W_ORDER = (0, 1, 2, 5, 10, 4, 8, 9)
_W_SMALL = ((_IN_OFF[3], GLA_RANK), (_IN_OFF[6], 2 * GDN_HEADS))
_W_ALL_COLS = _PA_COLS + _PB_COLS + LANES
_W_RB = 512


def _w_block_rows():
    tab = []
    for i in _W_ORDER:
        assert (_IN_OFF[i + 1] - _IN_OFF[i]) % _W_RB == 0
        tab.extend(range(_IN_OFF[i], _IN_OFF[i + 1], _W_RB))
    return np.asarray(tab, np.int32)


def _relayout_kernel(tab_ref, wt_hbm, o_ref, buf, sem):
    j = pl.program_id(0)
    n_main = pl.num_programs(0) - 1

    def block_copy(blk):
        start = pl.multiple_of(tab_ref[blk], 16)
        return pltpu.make_async_copy(wt_hbm.at[pl.ds(start, _W_RB), :], buf.at[blk % 2], sem.at[blk % 2])

    @pl.when(j == 0)
    def _():
        block_copy(j).start()

    @pl.when(j + 1 < n_main)
    def _():
        block_copy(j + 1).start()

    @pl.when(j < n_main)
    def _():
        block_copy(j).wait()
        o_ref[...] = buf[j % 2].astype(BF16)

    @pl.when(j == n_main)
    def _():
        dst = 0
        copies = []
        for n, (row0, rows) in enumerate(_W_SMALL):
            copies.append(pltpu.make_async_copy(wt_hbm.at[pl.ds(row0, rows), :], buf.at[0, pl.ds(dst, rows), :],
                                                sem.at[n]))
            dst += rows
        for cp in copies:
            cp.start()
        for cp in copies:
            cp.wait()
        o_ref[0:dst, :] = buf[0, 0:dst, :].astype(BF16)
        o_ref[dst:, :] = jnp.zeros((_W_RB - dst, o_ref.shape[1]), BF16)


def _split_w_in(w_in_t):
    n, k = w_in_t.shape
    tab = _w_block_rows()
    assert len(_W_SMALL) <= 2
    return pl.pallas_call(
        _relayout_kernel,
        grid_spec=pltpu.PrefetchScalarGridSpec(
            num_scalar_prefetch=1,
            grid=(len(tab) + 1,),
            in_specs=[pl.BlockSpec(memory_space=pl.ANY)],
            out_specs=pl.BlockSpec((_W_RB, k), lambda i, tab: (i, 0)),
            scratch_shapes=[pltpu.VMEM((2, _W_RB, k), F32), pltpu.SemaphoreType.DMA((2,))],
        ),
        out_shape=jax.ShapeDtypeStruct((_W_ALL_COLS, k), BF16),
        compiler_params=_params(("arbitrary",)),
        name="w_in_relayout",
    )(jnp.asarray(tab), w_in_t)


def kernel(x, mem, n_ffn1_pre, w_ffn1_gu, w_ffn1_down, n_ffn1_post, n_mix_pre, w_in, gla_w_lr2, gla_b_lr,
           gla_norm, gdn_conv, gdn_a_log, gdn_dt_bias, gdn_norm, mem_norm, w_mem_kv, w_up_gla, w_up_gdn,
           w_up_xa, w_out, n_mix_post, n_ffn2_pre, w_ffn2_gu, w_ffn2_down, n_ffn2_post):
    bsz, seq, d = x.shape
    t = bsz * seq
    h = x.reshape(t, d)
    for l in range(n_ffn1_pre.shape[0]):
        row = lambda a: a[l][None, :]
        h, u = _ffn(h, row(n_ffn1_pre), w_ffn1_gu[l], w_ffn1_down[l],
                    row(n_ffn1_post), row(n_mix_pre), emit_next=True)

        w_all = _split_w_in(w_in[l].T)
        par = jnp.zeros((2, LANES), F32)
        par = par.at[0, SM_A:SM_A + GDN_HEADS].set(gdn_a_log[l]).at[1, SM_A:SM_A + GDN_HEADS].set(gdn_dt_bias[l])
        tn_g = 512
        pa = _matmul(u, w_all, 0, _PA_GDN, F32, 2048, 1024, "in_proj_a")
        pg = _matmul(u, w_all, _PA_GDN, _PA_COLS - _PA_GDN, F32, seq, tn_g, "in_proj_g", body=_proj_conv_kernel,
                     extra=(gdn_conv[l],), extra_specs=(pl.BlockSpec((CONV_W, tn_g), lambda i, j: (0, j)),))
        pb = _matmul(u, w_all, _PA_COLS, _PB_COLS, BF16, 2048, 1024, "in_proj_b")
        ps = _matmul(u, w_all, _PA_COLS + _PB_COLS, LANES, F32, 2048, LANES, "in_proj_s", body=_proj_s_kernel,
                     extra=(par,), extra_specs=(pl.BlockSpec((2, LANES), lambda i, j: (0, 0)),))

        wlr = jnp.pad(gla_w_lr2[l], ((0, LANES - GLA_RANK), (0, 0))).astype(BF16)
        o_gla = _gla(pa, pb, ps, wlr, row(gla_b_lr), row(gla_norm), bsz, seq)
        o_gdn = _gdn(pg, pb, ps, row(gdn_norm), bsz, seq)

        mkv = _norm_matmul(mem.reshape(bsz * MEM_LEN, d), row(mem_norm), w_mem_kv[l], BF16,
                           bsz * MEM_LEN, 512, "mem_kv")
        o_xa = _xa(pb, mkv, bsz, seq)

        h = _merge(h, o_gla, o_gdn, o_xa, pb, w_up_gla[l].astype(BF16), w_up_gdn[l].astype(BF16),
                   w_up_xa[l].astype(BF16), w_out[l].astype(BF16), row(n_mix_post))
        h = _ffn(h, row(n_ffn2_pre), w_ffn2_gu[l], w_ffn2_down[l],
                 row(n_ffn2_post), row(n_ffn2_post), emit_next=False)
    return h.reshape(bsz, seq, d)
```

```python
import functools

import numpy as np
import jax
import jax.numpy as jnp
from jax import lax
from jax.experimental import pallas as pl
from jax.experimental.pallas import tpu as pltpu

F32 = jnp.float32
BF16 = jnp.bfloat16

D_MODEL = 2048
MEM_LEN = 256
EPS = 1e-6
GLA_HEADS = 4
GLA_DK = 128
GLA_DV = 256
GLA_QK = GLA_HEADS * GLA_DK
GLA_V = GLA_HEADS * GLA_DV
GLA_RANK = 16
GLA_TAU = 16.0
GDN_HEADS = 8
GDN_DK = 128
GDN_DV = 128
GDN_QK = GDN_HEADS * GDN_DK
GDN_V = GDN_HEADS * GDN_DV
CONV_W = 4
XA_HEADS = 4
XA_DH = 256
XA_W = XA_HEADS * XA_DH
N_BRANCH = 3
D_FF = 5632

LANES = 128
GLA_CHUNK = 64
GLA_GROUP = 4
GDN_CHUNK = 128
GDN_HPS = 4
GDN_GROUP = 2
VMEM_LIMIT = 56 * 1024 * 1024

_IN_SIZES = (GLA_QK, GLA_QK, GLA_V, GLA_RANK, GLA_V, 2 * GDN_QK + GDN_V, GDN_HEADS, GDN_HEADS, GDN_V,
             XA_W, N_BRANCH * D_MODEL)
_IN_OFF = tuple(int(v) for v in np.cumsum((0,) + _IN_SIZES))
SM_LR, SM_B, SM_A = 0, GLA_RANK, GLA_RANK + GDN_HEADS


def _dot(a, b):
    return jnp.dot(a, b, preferred_element_type=F32)


def _dot_nt(a, b):
    return lax.dot_general(a, b, (((1,), (1,)), ((), ())), preferred_element_type=F32)


def _dot_tn(a, b):
    return lax.dot_general(a, b, (((0,), (0,)), ((), ())), preferred_element_type=F32)


def _mm(a, b):
    return _dot(a.astype(BF16), b.astype(BF16))


def _split2(x):
    hi = x.astype(BF16)
    mid = (x - hi.astype(F32)).astype(BF16)
    return hi, mid


def _rms(x, gain):
    return x * lax.rsqrt(jnp.mean(x * x, axis=-1, keepdims=True) + EPS) * gain


def _silu(x):
    return x * jax.nn.sigmoid(x)


def _softplus(x):
    return jnp.maximum(x, 0.0) + jnp.log1p(jnp.exp(-jnp.abs(x)))


def _params(sem):
    return pltpu.CompilerParams(dimension_semantics=sem, vmem_limit_bytes=VMEM_LIMIT)


def _ffn_kernel(x_ref, gpre_ref, wgu_hbm, wd_hbm, gpost_ref, gnext_ref, *out_and_scratch, emit_next, tf):
    if emit_next:
        h_ref, un_ref, xn_scr, wg_buf, wu_buf, wd_buf, wg16, wu16, wd16, sem = out_and_scratch
    else:
        h_ref, xn_scr, wg_buf, wu_buf, wd_buf, wg16, wu16, wd16, sem = out_and_scratch
    i = pl.program_id(0)
    nf = D_FF // tf
    assert nf % 2 == 0

    def tile_copies(jj, slot):
        col = pl.multiple_of(jj * tf, tf)
        return (pltpu.make_async_copy(wgu_hbm.at[:, pl.ds(col, tf)], wg_buf.at[slot], sem.at[0, slot]),
                pltpu.make_async_copy(wgu_hbm.at[:, pl.ds(D_FF + col, tf)], wu_buf.at[slot], sem.at[1, slot]),
                pltpu.make_async_copy(wd_hbm.at[pl.ds(col, tf), :], wd_buf.at[slot], sem.at[2, slot]))

    @pl.when(i == 0)
    def _():
        for cp in tile_copies(0, 0):
            cp.start()

    xn_scr[...] = _rms(x_ref[...], gpre_ref[...]).astype(BF16)
    h_ref[...] = jnp.zeros_like(h_ref)

    def body(jj, carry):
        slot = jj % 2
        @pl.when(jj + 1 < nf)
        def _():
            for cp in tile_copies(jj + 1, 1 - slot):
                cp.start()

        @pl.when((jj + 1 == nf) & (i + 1 < pl.num_programs(0)))
        def _():
            for cp in tile_copies(0, 0):
                cp.start()

        for cp in tile_copies(jj, slot):
            cp.wait()
        wg16[...] = wg_buf[slot].astype(BF16)
        wu16[...] = wu_buf[slot].astype(BF16)
        wd16[...] = wd_buf[slot].astype(BF16)
        xn = xn_scr[...]
        g = _dot(xn, wg16[...])
        u = _dot(xn, wu16[...])
        hm = (_silu(g) * u).astype(BF16)
        h_ref[...] += _dot(hm, wd16[...])
        return carry

    lax.fori_loop(0, nf, body, 0)

    h = x_ref[...] + 0.5 * _rms(h_ref[...], gpost_ref[...])
    h_ref[...] = h
    if emit_next:
        un_ref[...] = _rms(h, gnext_ref[...]).astype(BF16)


def _ffn(x, gpre, w_gu, w_down, gpost, gnext, emit_next, tm=1024, tf=256):
    t, d = x.shape
    row = lambda i: (i, 0)
    const = lambda i: (0, 0)
    once = pl.Buffered(1)
    out_shape = [jax.ShapeDtypeStruct((t, d), F32)]
    out_specs = [pl.BlockSpec((tm, d), row, pipeline_mode=once)]
    if emit_next:
        out_shape.append(jax.ShapeDtypeStruct((t, d), BF16))
        out_specs.append(pl.BlockSpec((tm, d), row, pipeline_mode=once))
    res = pl.pallas_call(
        functools.partial(_ffn_kernel, emit_next=emit_next, tf=tf),
        grid=(t // tm,),
        in_specs=[
            pl.BlockSpec((tm, d), row),
            pl.BlockSpec((1, d), const),
            pl.BlockSpec(memory_space=pl.ANY),
            pl.BlockSpec(memory_space=pl.ANY),
            pl.BlockSpec((1, d), const),
            pl.BlockSpec((1, d), const),
        ],
        out_specs=out_specs,
        out_shape=out_shape,
        scratch_shapes=[pltpu.VMEM((tm, d), BF16),
                        pltpu.VMEM((2, d, tf), F32), pltpu.VMEM((2, d, tf), F32), pltpu.VMEM((2, tf, d), F32),
                        pltpu.VMEM((d, tf), BF16), pltpu.VMEM((d, tf), BF16), pltpu.VMEM((tf, d), BF16),
                        pltpu.SemaphoreType.DMA((3, 2))],
        compiler_params=_params(("arbitrary",)),
        name="ffn",
    )(x, gpre, w_gu, w_down, gpost, gnext)
    return res if emit_next else res[0]


def _matmul_kernel(a_ref, w_ref, o_ref):
    o_ref[...] = _dot_nt(a_ref[...], w_ref[...]).astype(o_ref.dtype)


def _proj_conv_kernel(a_ref, w_ref, cw_ref, o_ref):
    tm, tn = o_ref.shape
    rb = 256
    kind = pl.program_id(1) // (GDN_QK // tn)
    scale = jnp.where(kind == 0, GDN_DK ** -0.5, 1.0)
    w = cw_ref[...]
    r8 = lax.broadcasted_iota(jnp.int32, (8, tn), 0)

    sb = 32

    def finish(r, before):
        for s0 in range(0, rb, sb):
            y = o_ref[r + s0:r + s0 + sb, :]
            acc = y * w[CONV_W - 1:CONV_W, :]
            for sft in range(1, CONV_W):
                xs = pltpu.roll(y, sft, 0)
                top = jnp.where(r8 < sft, pltpu.roll(before, sft, 0), xs[0:8])
                acc = acc + jnp.concatenate([top, xs[8:]], axis=0) * w[CONV_W - 1 - sft:CONV_W - sft, :]
            before = y[sb - 8:sb]
            c = _silu(acc)
            for hh in range(tn // GDN_DK):
                sl = slice(hh * GDN_DK, (hh + 1) * GDN_DK)
                blk = c[:, sl]
                f = lax.rsqrt(jnp.sum(blk * blk, axis=-1, keepdims=True) + EPS) * scale
                o_ref[r + s0:r + s0 + sb, sl] = blk * jnp.where(kind == 2, 1.0, f)
        return before

    o_ref[0:rb, :] = _dot_nt(a_ref[0:rb, :], w_ref[...])
    before = jnp.zeros((8, tn), F32)
    for r in range(rb, tm + rb, rb):
        if r < tm:
            o_ref[r:r + rb, :] = _dot_nt(a_ref[r:r + rb, :], w_ref[...])
        before = finish(r - rb, before)


def _proj_s_kernel(a_ref, w_ref, par_ref, o_ref):
    y = _dot_nt(a_ref[...], w_ref[...])
    par = par_ref[...]
    lane = lax.broadcasted_iota(jnp.int32, y.shape, 1)
    g = -jnp.exp(par[0:1, :]) * _softplus(y + par[1:2, :])
    is_g = (lane >= SM_A) & (lane < SM_A + GDN_HEADS)
    is_b = (lane >= SM_B) & (lane < SM_B + GDN_HEADS)
    o_ref[...] = jnp.where(is_g, g, jnp.where(is_b, jax.nn.sigmoid(y), y))


def _matmul(a, w, col0, n, out_dtype, tm, tn, name, body=_matmul_kernel, extra=(), extra_specs=()):
    m, k = a.shape
    c0 = col0 // tn
    return pl.pallas_call(
        body,
        grid=(m // tm, n // tn),
        in_specs=[pl.BlockSpec((tm, k), lambda i, j: (i, 0)), pl.BlockSpec((tn, k), lambda i, j: (c0 + j, 0)),
                  *extra_specs],
        out_specs=pl.BlockSpec((tm, tn), lambda i, j: (i, j)),
        out_shape=jax.ShapeDtypeStruct((m, n), out_dtype),
        compiler_params=_params(("parallel", "parallel")),
        name=name,
    )(a, w, *extra)


def _norm_matmul_kernel(x_ref, g_ref, w_ref, o_ref, xn_scr):
    @pl.when(pl.program_id(1) == 0)
    def _():
        xn_scr[...] = _rms(x_ref[...], g_ref[...]).astype(BF16)

    o_ref[...] = _dot(xn_scr[...], w_ref[...].astype(BF16)).astype(o_ref.dtype)


def _norm_matmul(x, gain, w, out_dtype, tm, tn, name):
    m, k = x.shape
    n = w.shape[1]
    return pl.pallas_call(
        _norm_matmul_kernel,
        grid=(m // tm, n // tn),
        in_specs=[pl.BlockSpec((tm, k), lambda i, j: (i, 0)), pl.BlockSpec((1, k), lambda i, j: (0, 0)),
                  pl.BlockSpec((k, tn), lambda i, j: (0, j))],
        out_specs=pl.BlockSpec((tm, tn), lambda i, j: (i, j)),
        out_shape=jax.ShapeDtypeStruct((m, n), out_dtype),
        scratch_shapes=[pltpu.VMEM((tm, k), BF16)],
        compiler_params=_params(("parallel", "arbitrary")),
        name=name,
    )(x, gain, w)


def _gla_levels(chunk):
    return [chunk >> (i + 1) for i in range(int(np.log2(chunk)))]


def _gla_exponent_matrix(chunk):
    c = chunk
    i = np.arange(c)[:, None]
    t = np.arange(c)[None, :]
    blocks = [(t <= i), (t > i)]
    for h in _gla_levels(c):
        r = (i // (2 * h)) * (2 * h) + h
        upper = i >= r
        blocks.append(np.where(upper, (t > r) & (t <= i), (t > i) & (t <= r)))
    m = np.concatenate(blocks, axis=0).astype(np.float32)
    z = np.zeros_like(m)
    return np.concatenate([np.concatenate([m, m, z, z], axis=1), np.concatenate([z, z, m, m], axis=1)], axis=0)


def _gla_kernel(q_ref, k_ref, v_ref, sm_ref, r_ref, wlr_ref, blr_ref, gn_ref, mst_ref, o_ref, s_scr, e0_scr,
                e1_scr):
    c = GLA_CHUNK
    seq = q_ref.shape[0]
    s_scr[...] = jnp.zeros_like(s_scr)
    ri = lax.broadcasted_iota(jnp.int32, (c, c), 0)
    ci = lax.broadcasted_iota(jnp.int32, (c, c), 1)
    rowi = lax.broadcasted_iota(jnp.int32, (c, GLA_DK), 0)
    levels = _gla_levels(c)
    assert GLA_GROUP == 4

    grp = GLA_GROUP
    gs = range(grp)

    n_trips = seq // (grp * c)

    def gate_logits(n):
        allrows = pl.ds(pl.multiple_of(n * (grp * c), grp * c), grp * c)
        return _dot(sm_ref[allrows, :].astype(BF16), wlr_ref[...]) + blr_ref[...]

    def decays(x, e_ref):
        lg = (jnp.minimum(x, 0.0) - jnp.log1p(jnp.exp(-jnp.abs(x)))) * (1.0 / GLA_TAU)
        hi, mid = _split2(lg)
        hm = [jnp.concatenate([hi[g * c:(g + 1) * c], mid[g * c:(g + 1) * c]], axis=0) for g in gs]
        rhs = jnp.concatenate([jnp.concatenate([hm[0], hm[1]], axis=1),
                               jnp.concatenate([hm[2], hm[3]], axis=1)], axis=0)
        e_ref[...] = jnp.exp(_dot(mst_ref[...], rhs))

    def trip(n, e_ref, between):
        row0 = pl.multiple_of(n * (grp * c), grp * c)
        rows = [pl.ds(row0 + g * c, c) for g in gs]
        ex = e_ref[...]
        nr = ex.shape[0] // 2
        e = [ex[0:nr, 0:GLA_DK], ex[0:nr, GLA_DK:], ex[nr:, 0:GLA_DK], ex[nr:, GLA_DK:]]
        q = [q_ref[rows[g], :] * (GLA_DK ** -0.5) for g in gs]
        k = [k_ref[rows[g], :] for g in gs]
        v = [v_ref[rows[g], :].astype(BF16) for g in gs]
        attn = [jnp.where(ri == ci, _dot_nt(q[g].astype(BF16), k[g].astype(BF16)), 0.0) for g in gs]
        for lvl, h in enumerate(levels):
            upper = (rowi & h) != 0
            sh = int(np.log2(2 * h))
            same = (ri >> sh) == (ci >> sh)
            for g in gs:
                f = e[g][(2 + lvl) * c:(3 + lvl) * c]
                qt = jnp.where(upper, q[g] * f, 0.0).astype(BF16)
                kt = jnp.where(upper, 0.0, k[g] * f).astype(BF16)
                attn[g] = attn[g] + jnp.where(same, _dot_nt(qt, kt), 0.0)
        vk = [_dot_tn(v[g], (k[g] * e[g][c:2 * c]).astype(BF16)) for g in gs]
        between()
        av = [_dot(attn[g].astype(BF16), v[g]) for g in gs]
        st = s_scr[...]
        for g in gs:
            o = _dot_nt((q[g] * e[g][0:c]).astype(BF16), st.astype(BF16)) + av[g]
            st = st * e[g][c - 1:c, :] + vk[g]
            on = _rms(o, gn_ref[...]) * _silu(r_ref[rows[g], :].astype(F32))
            o_ref[rows[g], :] = on.astype(o_ref.dtype)
        s_scr[...] = st

    assert n_trips % 2 == 0
    decays(gate_logits(0), e0_scr)

    def body(m, carry):
        x1 = gate_logits(2 * m + 1)
        trip(2 * m, e0_scr, lambda: decays(x1, e1_scr))
        x2 = gate_logits(jnp.minimum(2 * m + 2, n_trips - 1))
        trip(2 * m + 1, e1_scr, lambda: decays(x2, e0_scr))
        return carry

    lax.fori_loop(0, n_trips // 2, body, 0)


def _gla(pa, pb, ps, wlr, blr, gnorm, bsz, seq):
    t = bsz * seq
    mst = jnp.asarray(_gla_exponent_matrix(GLA_CHUNK), BF16)
    nq = GLA_QK // GLA_DK
    return pl.pallas_call(
        _gla_kernel,
        grid=(bsz, GLA_HEADS),
        in_specs=[
            pl.BlockSpec((seq, GLA_DK), lambda b, h: (b, h)),
            pl.BlockSpec((seq, GLA_DK), lambda b, h: (b, nq + h)),
            pl.BlockSpec((seq, GLA_DV), lambda b, h: (b, 2 * GLA_QK // GLA_DV + h)),
            pl.BlockSpec((seq, LANES), lambda b, h: (b, 0)),
            pl.BlockSpec((seq, GLA_DV), lambda b, h: (b, _PB_R // GLA_DV + h)),
            pl.BlockSpec((LANES, GLA_DK), lambda b, h: (0, h)),
            pl.BlockSpec((1, GLA_DK), lambda b, h: (0, h)),
            pl.BlockSpec((1, GLA_DV), lambda b, h: (0, 0)),
            pl.BlockSpec(mst.shape, lambda b, h: (0, 0)),
        ],
        out_specs=pl.BlockSpec((seq, GLA_DV), lambda b, h: (b, h)),
        out_shape=jax.ShapeDtypeStruct((t, GLA_V), BF16),
        scratch_shapes=[pltpu.VMEM((GLA_DV, GLA_DK), F32), pltpu.VMEM((mst.shape[0], 2 * GLA_DK), F32),
                        pltpu.VMEM((mst.shape[0], 2 * GLA_DK), F32)],
        compiler_params=_params(("parallel", "parallel")),
        name="gla",
    )(pa, pa, pa, ps, pb, wlr, blr, gnorm, mst)


def _gdn_kernel(q_ref, k_ref, v_ref, sm_ref, z_ref, gn_ref, o_ref, bms, ns, qms, os_, egl, s_scr):
    c = GDN_CHUNK
    seq = q_ref.shape[0]
    head0 = pl.program_id(1) * GDN_HPS

    ri = lax.broadcasted_iota(jnp.int32, (c, c), 0)
    ci = lax.broadcasted_iota(jnp.int32, (c, c), 1)
    lane = lax.broadcasted_iota(jnp.int32, (c, LANES), 1)
    tril = jnp.where(ri >= ci, 1.0, 0.0).astype(BF16)
    causal = ri >= ci
    n_sq = int(np.log2(c))
    lower_left = [(((ri ^ ci) >> (l + 1)) == 0) & ((ri & (1 << l)) != 0) & ((ci & (1 << l)) == 0) for l in range(n_sq)]

    grp = GDN_GROUP
    chains = [(g, j) for g in range(grp) for j in range(GDN_HPS)]

    def phase1(n, hooks):
        rows, q, k, kb, gb, rhs0 = {}, {}, {}, {}, {}, {}
        for g in range(grp):
            rows[g] = pl.ds(pl.multiple_of((n * grp + g) * c, c), c)
            sm = sm_ref[rows[g], :]
            for j in range(GDN_HPS):
                sl = slice(j * GDN_DK, (j + 1) * GDN_DK)
                q[g, j] = q_ref[rows[g], sl]
                k[g, j] = k_ref[rows[g], sl]
                gcol = jnp.sum(jnp.where(lane == SM_A + head0 + j, sm, 0.0), axis=-1, keepdims=True)
                beta = jnp.sum(jnp.where(lane == SM_B + head0 + j, sm, 0.0), axis=-1, keepdims=True)
                gb[g, j] = jnp.broadcast_to(gcol, (c, LANES))
                kb[g, j] = k[g, j] * beta
                rhs0[g, j] = v_ref[rows[g], sl] * beta
        cum_r, cum_c, kt = {}, {}, {}
        for ch in chains:
            hi, mid = _split2(gb[ch])
            r = _dot(tril, jnp.concatenate([hi, mid], axis=1))
            cum_r[ch] = r[:, :LANES] + r[:, LANES:]
        for ch in chains:
            cum_c[ch] = cum_r[ch].T
            kt[ch] = k[ch].T
        kk = {}
        for ch in chains:
            k16 = k[ch].astype(BF16)
            kk[ch] = _dot_nt(jnp.concatenate([kb[ch].astype(BF16), q[ch].astype(BF16)], axis=0), k16)
        low, at16 = {}, {}
        for ch in chains:
            dec = jnp.where(causal, jnp.exp(jnp.where(causal, cum_r[ch] - cum_c[ch], 0.0)), 0.0)
            low[ch] = jnp.where(ri > ci, kk[ch][:c] * dec, 0.0)
            at16[ch] = jnp.where(causal, kk[ch][c:] * dec, 0.0).astype(BF16)
        x = {ch: -jnp.where(lower_left[0], low[ch], 0.0) for ch in chains}
        for lvl in range(1, n_sq):
            e = {}
            for ch in chains:
                cm = jnp.where(lower_left[lvl], low[ch], 0.0)
                e[ch] = cm + _mm(x[ch], cm)
            for ch in chains:
                x[ch] = x[ch] - e[ch] - _mm(e[ch], x[ch])
            if lvl in hooks:
                hooks[lvl]()
        wu, egc = {}, {}
        for ch in chains:
            egc[ch] = jnp.exp(cum_r[ch])
            rhs = jnp.concatenate([kb[ch] * egc[ch], rhs0[ch]], axis=1)
            wu[ch] = (rhs + _mm(x[ch], rhs)).astype(BF16)
        for ch in chains:
            g, j = ch
            sl = slice(j * GDN_DK, (j + 1) * GDN_DK)
            glast = cum_r[ch][c - 1:c, :]
            kdt16 = (kt[ch] * jnp.exp(glast - cum_c[ch][0:1, :])).astype(BF16)
            r = _dot(jnp.concatenate([kdt16, at16[ch]], axis=0), wu[ch])
            bms[rows[g], sl] = (-r[:c, :GDN_DV]).astype(BF16)
            ns[rows[g], sl] = r[:c, GDN_DV:]
            qms[rows[g], sl] = (q[ch] * egc[ch] - r[c:, :GDN_DV]).astype(BF16)
            os_[rows[g], sl] = r[c:, GDN_DV:]
            egl[pl.ds(pl.multiple_of((n * grp + g) * 8, 8), 8), sl] = jnp.broadcast_to(jnp.exp(glast), (8, LANES))

    s_scr[...] = jnp.zeros_like(s_scr)

    def phase2(n):
        rows = pl.ds(pl.multiple_of(n * c, c), c)
        sls = [slice(j * GDN_DK, (j + 1) * GDN_DK) for j in range(GDN_HPS)]
        s = [s_scr[j] for j in range(GDN_HPS)]
        s16 = [sj.astype(BF16) for sj in s]
        r = [_dot(jnp.concatenate([bms[rows, sl], qms[rows, sl]], axis=0), s16[j]) for j, sl in enumerate(sls)]
        for j, sl in enumerate(sls):
            eg = egl[pl.ds(pl.multiple_of(n * 8, 8), 8), sl][0:1, :]
            s_scr[j] = s[j] * eg + r[j][:c] + ns[rows, sl]
            o = r[j][c:] + os_[rows, sl]
            on = _rms(o, gn_ref[...]) * _silu(z_ref[rows, sl].astype(F32))
            o_ref[rows, sl] = on.astype(o_ref.dtype)

    assert grp == 2 and n_sq >= 5
    n_trips = seq // (grp * c)
    phase1(0, {})

    def body(n, carry):
        first = (n - 1) * grp
        phase1(n, {1: lambda: phase2(first), n_sq - 2: lambda: phase2(first + 1)})
        return carry

    lax.fori_loop(1, n_trips, body, 0)
    for g in range(grp):
        phase2((n_trips - 1) * grp + g)


def _gdn(pg, pb, ps, gnorm, bsz, seq):
    t = bsz * seq
    nb = seq // GDN_CHUNK
    wd = GDN_HPS * GDN_DK
    ng = GDN_HEADS // GDN_HPS
    return pl.pallas_call(
        _gdn_kernel,
        grid=(bsz, ng),
        in_specs=[
            pl.BlockSpec((seq, wd), lambda b, h: (b, h)),
            pl.BlockSpec((seq, wd), lambda b, h: (b, ng + h)),
            pl.BlockSpec((seq, wd), lambda b, h: (b, 2 * ng + h)),
            pl.BlockSpec((seq, LANES), lambda b, h: (b, 0)),
            pl.BlockSpec((seq, wd), lambda b, h: (b, _PB_Z // wd + h)),
            pl.BlockSpec((1, GDN_DV), lambda b, h: (0, 0)),
        ],
        out_specs=pl.BlockSpec((seq, wd), lambda b, h: (b, h)),
        out_shape=jax.ShapeDtypeStruct((t, GDN_V), BF16),
        scratch_shapes=[
            pltpu.VMEM((seq, wd), BF16), pltpu.VMEM((seq, wd), F32), pltpu.VMEM((seq, wd), BF16),
            pltpu.VMEM((seq, wd), F32), pltpu.VMEM((nb * 8, wd), F32),
            pltpu.VMEM((GDN_HPS, GDN_DK, GDN_DV), F32),
        ],
        compiler_params=_params(("parallel", "parallel")),
        name="gdn",
    )(pg, pg, pg, ps, pb, gnorm)


def _xa_kernel(q_ref, mk_ref, mv_ref, o_ref):
    s = _dot_nt(q_ref[...], mk_ref[...]) * (XA_DH ** -0.5)
    p = jnp.exp(s - jnp.max(s, axis=-1, keepdims=True))
    den = jnp.sum(p, axis=-1, keepdims=True)
    o_ref[...] = (_dot(p.astype(BF16), mv_ref[...]) / den).astype(o_ref.dtype)


def _xa(pb, mkv, bsz, seq, ts=2048):
    t = bsz * seq
    nt = seq // ts
    return pl.pallas_call(
        _xa_kernel,
        grid=(bsz, XA_HEADS, nt),
        in_specs=[
            pl.BlockSpec((ts, XA_DH), lambda b, h, i: (b * nt + i, _PB_XQ // XA_DH + h)),
            pl.BlockSpec((MEM_LEN, XA_DH), lambda b, h, i: (b, h)),
            pl.BlockSpec((MEM_LEN, XA_DH), lambda b, h, i: (b, XA_HEADS + h)),
        ],
        out_specs=pl.BlockSpec((ts, XA_DH), lambda b, h, i: (b * nt + i, h)),
        out_shape=jax.ShapeDtypeStruct((t, XA_W), BF16),
        compiler_params=_params(("parallel", "parallel", "parallel")),
        name="xattn",
    )(pb, mkv, mkv)


def _merge_kernel(h_ref, oa_ref, ob_ref, oc_ref, ga_ref, gb_ref, gc_ref, wa_hbm, wb_hbm, wc_hbm, wo_hbm,
                  gpost_ref, o_ref, wa_ref, wb_ref, wc_ref, wo_ref, stage, sem):
    @pl.when(pl.program_id(0) == 0)
    def _():
        rb = stage.shape[1]
        chunks = [(src, dst, r) for src, dst in ((wa_hbm, wa_ref), (wb_hbm, wb_ref), (wc_hbm, wc_ref), (wo_hbm, wo_ref))
                  for r in range(0, dst.shape[0], rb)]

        def chunk_copy(n):
            src, _, r = chunks[n]
            return pltpu.make_async_copy(src.at[pl.ds(r, rb), :], stage.at[n % 2], sem.at[n % 2])

        chunk_copy(0).start()
        for n, (_, dst, r) in enumerate(chunks):
            if n + 1 < len(chunks):
                chunk_copy(n + 1).start()
            chunk_copy(n).wait()
            dst[r:r + rb, :] = stage[n % 2].astype(BF16)

    mixed = jax.nn.sigmoid(ga_ref[...].astype(F32)) * _dot(oa_ref[...], wa_ref[...])
    mixed = mixed + jax.nn.sigmoid(gb_ref[...].astype(F32)) * _dot(ob_ref[...], wb_ref[...])
    mixed = mixed + jax.nn.sigmoid(gc_ref[...].astype(F32)) * _dot(oc_ref[...], wc_ref[...])
    m = _dot(mixed.astype(BF16), wo_ref[...])
    o_ref[...] = h_ref[...] + _rms(m, gpost_ref[...])


def _merge(h, oa, ob, oc, pb, wa, wb, wc, wo, gpost, tm=256, rb=256):
    t, d = h.shape
    row = lambda i: (i, 0)
    const = lambda i: (0, 0)
    hbm = pl.BlockSpec(memory_space=pl.ANY)
    return pl.pallas_call(
        _merge_kernel,
        grid=(t // tm,),
        in_specs=[
            pl.BlockSpec((tm, d), row),
            pl.BlockSpec((tm, GLA_V), row), pl.BlockSpec((tm, GDN_V), row), pl.BlockSpec((tm, XA_W), row),
            pl.BlockSpec((tm, d), lambda i: (i, 0)), pl.BlockSpec((tm, d), lambda i: (i, 1)),
            pl.BlockSpec((tm, d), lambda i: (i, 2)),
            hbm, hbm, hbm, hbm,
            pl.BlockSpec((1, d), const),
        ],
        out_specs=pl.BlockSpec((tm, d), row),
        out_shape=jax.ShapeDtypeStruct((t, d), F32),
        scratch_shapes=[pltpu.VMEM((GLA_V, d), BF16), pltpu.VMEM((GDN_V, d), BF16), pltpu.VMEM((XA_W, d), BF16),
                        pltpu.VMEM((d, d), BF16), pltpu.VMEM((2, rb, d), F32), pltpu.SemaphoreType.DMA((2,))],
        compiler_params=_params(("arbitrary",)),
        name="merge",
    )(h, oa, ob, oc, pb, pb, pb, wa, wb, wc, wo, gpost)


_PA_GDN = 2 * GLA_QK + GLA_V
_PA_COLS = _PA_GDN + 2 * GDN_QK + GDN_V
_PB_R = N_BRANCH * D_MODEL
_PB_Z = _PB_R + GLA_V
_PB_XQ = _PB_Z + GDN_V
_PB_COLS = _PB_XQ + XA_W


_W_ORDER = (0, 1, 2, 5, 10, 4, 8, 9)
_W_SMALL = ((_IN_OFF[3], GLA_RANK), (_IN_OFF[6], 2 * GDN_HEADS))
_W_ALL_COLS = _PA_COLS + _PB_COLS + LANES
_W_RB = 512


def _w_block_rows():
    tab = []
    for i in _W_ORDER:
        assert (_IN_OFF[i + 1] - _IN_OFF[i]) % _W_RB == 0
        tab.extend(range(_IN_OFF[i], _IN_OFF[i + 1], _W_RB))
    return np.asarray(tab, np.int32)


def _relayout_kernel(tab_ref, wt_hbm, o_ref, buf, sem):
    j = pl.program_id(0)
    n_main = pl.num_programs(0) - 1

    def block_copy(blk):
        start = pl.multiple_of(tab_ref[blk], 16)
        return pltpu.make_async_copy(wt_hbm.at[pl.ds(start, _W_RB), :], buf.at[blk % 2], sem.at[blk % 2])

    @pl.when(j == 0)
    def _():
        block_copy(j).start()

    @pl.when(j + 1 < n_main)
    def _():
        block_copy(j + 1).start()

    @pl.when(j < n_main)
    def _():
        block_copy(j).wait()
        o_ref[...] = buf[j % 2].astype(BF16)

    @pl.when(j == n_main)
    def _():
        dst = 0
        copies = []
        for n, (row0, rows) in enumerate(_W_SMALL):
            copies.append(pltpu.make_async_copy(wt_hbm.at[pl.ds(row0, rows), :], buf.at[0, pl.ds(dst, rows), :],
                                                sem.at[n]))
            dst += rows
        for cp in copies:
            cp.start()
        for cp in copies:
            cp.wait()
        o_ref[0:dst, :] = buf[0, 0:dst, :].astype(BF16)
        o_ref[dst:, :] = jnp.zeros((_W_RB - dst, o_ref.shape[1]), BF16)


def _split_w_in(w_in_t):
    n, k = w_in_t.shape
    tab = _w_block_rows()
    assert len(_W_SMALL) <= 2
    return pl.pallas_call(
        _relayout_kernel,
        grid_spec=pltpu.PrefetchScalarGridSpec(
            num_scalar_prefetch=1,
            grid=(len(tab) + 1,),
            in_specs=[pl.BlockSpec(memory_space=pl.ANY)],
            out_specs=pl.BlockSpec((_W_RB, k), lambda i, tab: (i, 0)),
            scratch_shapes=[pltpu.VMEM((2, _W_RB, k), F32), pltpu.SemaphoreType.DMA((2,))],
        ),
        out_shape=jax.ShapeDtypeStruct((_W_ALL_COLS, k), BF16),
        compiler_params=_params(("arbitrary",)),
        name="w_in_relayout",
    )(jnp.asarray(tab), w_in_t)


def kernel(x, mem, n_ffn1_pre, w_ffn1_gu, w_ffn1_down, n_ffn1_post, n_mix_pre, w_in, gla_w_lr2, gla_b_lr,
           gla_norm, gdn_conv, gdn_a_log, gdn_dt_bias, gdn_norm, mem_norm, w_mem_kv, w_up_gla, w_up_gdn,
           w_up_xa, w_out, n_mix_post, n_ffn2_pre, w_ffn2_gu, w_ffn2_down, n_ffn2_post):
    bsz, seq, d = x.shape
    t = bsz * seq
    h = x.reshape(t, d)
    for l in range(n_ffn1_pre.shape[0]):
        row = lambda a: a[l][None, :]
        h, u = _ffn(h, row(n_ffn1_pre), w_ffn1_gu[l], w_ffn1_down[l],
                    row(n_ffn1_post), row(n_mix_pre), emit_next=True)

        w_all = _split_w_in(w_in[l].T)
        par = jnp.zeros((2, LANES), F32)
        par = par.at[0, SM_A:SM_A + GDN_HEADS].set(gdn_a_log[l]).at[1, SM_A:SM_A + GDN_HEADS].set(gdn_dt_bias[l])
        tn_g = 512
        pa = _matmul(u, w_all, 0, _PA_GDN, F32, 2048, 1024, "in_proj_a")
        pg = _matmul(u, w_all, _PA_GDN, _PA_COLS - _PA_GDN, F32, seq, tn_g, "in_proj_g", body=_proj_conv_kernel,
                     extra=(gdn_conv[l],), extra_specs=(pl.BlockSpec((CONV_W, tn_g), lambda i, j: (0, j)),))
        pb = _matmul(u, w_all, _PA_COLS, _PB_COLS, BF16, 2048, 1024, "in_proj_b")
        ps = _matmul(u, w_all, _PA_COLS + _PB_COLS, LANES, F32, 2048, LANES, "in_proj_s", body=_proj_s_kernel,
                     extra=(par,), extra_specs=(pl.BlockSpec((2, LANES), lambda i, j: (0, 0)),))

        wlr = jnp.pad(gla_w_lr2[l], ((0, LANES - GLA_RANK), (0, 0))).astype(BF16)
        o_gla = _gla(pa, pb, ps, wlr, row(gla_b_lr), row(gla_norm), bsz, seq)
        o_gdn = _gdn(pg, pb, ps, row(gdn_norm), bsz, seq)

        mkv = _norm_matmul(mem.reshape(bsz * MEM_LEN, d), row(mem_norm), w_mem_kv[l], BF16,
                           bsz * MEM_LEN, 512, "mem_kv")
        o_xa = _xa(pb, mkv, bsz, seq)

        h = _merge(h, o_gla, o_gdn, o_xa, pb, w_up_gla[l], w_up_gdn[l], w_up_xa[l], w_out[l], row(n_mix_post))
        h = _ffn(h, row(n_ffn2_pre), w_ffn2_gu[l], w_ffn2_down[l],
                 row(n_ffn2_post), row(n_ffn2_post), emit_next=False)
    return h.reshape(bsz, seq, d)
```

```python
import functools

import numpy as np
import jax
import jax.numpy as jnp
from jax import lax
from jax.experimental import pallas as pl
from jax.experimental.pallas import tpu as pltpu

F32 = jnp.float32
BF16 = jnp.bfloat16

D_MODEL = 2048
MEM_LEN = 256
EPS = 1e-6
GLA_HEADS = 4
GLA_DK = 128
GLA_DV = 256
GLA_QK = GLA_HEADS * GLA_DK
GLA_V = GLA_HEADS * GLA_DV
GLA_RANK = 16
GLA_TAU = 16.0
GDN_HEADS = 8
GDN_DK = 128
GDN_DV = 128
GDN_QK = GDN_HEADS * GDN_DK
GDN_V = GDN_HEADS * GDN_DV
CONV_W = 4
XA_HEADS = 4
XA_DH = 256
XA_W = XA_HEADS * XA_DH
N_BRANCH = 3
D_FF = 5632

LANES = 128
GLA_CHUNK = 64
GLA_GROUP = 4
GDN_CHUNK = 128
GDN_HPS = 4
GDN_GROUP = 2
VMEM_LIMIT = 56 * 1024 * 1024

_IN_SIZES = (GLA_QK, GLA_QK, GLA_V, GLA_RANK, GLA_V, 2 * GDN_QK + GDN_V, GDN_HEADS, GDN_HEADS, GDN_V,
             XA_W, N_BRANCH * D_MODEL)
_IN_OFF = tuple(int(v) for v in np.cumsum((0,) + _IN_SIZES))
SM_LR, SM_B, SM_A = 0, GLA_RANK, GLA_RANK + GDN_HEADS


def _dot(a, b):
    return jnp.dot(a, b, preferred_element_type=F32)


def _dot_nt(a, b):
    return lax.dot_general(a, b, (((1,), (1,)), ((), ())), preferred_element_type=F32)


def _dot_tn(a, b):
    return lax.dot_general(a, b, (((0,), (0,)), ((), ())), preferred_element_type=F32)


def _mm(a, b):
    return _dot(a.astype(BF16), b.astype(BF16))


def _split2(x):
    hi = x.astype(BF16)
    mid = (x - hi.astype(F32)).astype(BF16)
    return hi, mid


def _rms(x, gain):
    return x * lax.rsqrt(jnp.mean(x * x, axis=-1, keepdims=True) + EPS) * gain


def _silu(x):
    return x * jax.nn.sigmoid(x)


def _softplus(x):
    return jnp.maximum(x, 0.0) + jnp.log1p(jnp.exp(-jnp.abs(x)))


def _params(sem):
    return pltpu.CompilerParams(dimension_semantics=sem, vmem_limit_bytes=VMEM_LIMIT)


def _ffn_kernel(x_ref, gpre_ref, wgu_hbm, wd_hbm, gpost_ref, gnext_ref, *out_and_scratch, emit_next, tf):
    if emit_next:
        h_ref, un_ref, xn_scr, wg_buf, wu_buf, wd_buf, wg16, wu16, wd16, sem = out_and_scratch
    else:
        h_ref, xn_scr, wg_buf, wu_buf, wd_buf, wg16, wu16, wd16, sem = out_and_scratch
    i = pl.program_id(0)
    nf = D_FF // tf
    assert nf % 2 == 0

    def tile_copies(jj, slot):
        col = pl.multiple_of(jj * tf, tf)
        return (pltpu.make_async_copy(wgu_hbm.at[:, pl.ds(col, tf)], wg_buf.at[slot], sem.at[0, slot]),
                pltpu.make_async_copy(wgu_hbm.at[:, pl.ds(D_FF + col, tf)], wu_buf.at[slot], sem.at[1, slot]),
                pltpu.make_async_copy(wd_hbm.at[pl.ds(col, tf), :], wd_buf.at[slot], sem.at[2, slot]))

    @pl.when(i == 0)
    def _():
        for cp in tile_copies(0, 0):
            cp.start()

    xn_scr[...] = _rms(x_ref[...], gpre_ref[...]).astype(BF16)
    h_ref[...] = jnp.zeros_like(h_ref)

    def body(jj, carry):
        slot = jj % 2
        @pl.when(jj + 1 < nf)
        def _():
            for cp in tile_copies(jj + 1, 1 - slot):
                cp.start()

        @pl.when((jj + 1 == nf) & (i + 1 < pl.num_programs(0)))
        def _():
            for cp in tile_copies(0, 0):
                cp.start()

        for cp in tile_copies(jj, slot):
            cp.wait()
        wg16[...] = wg_buf[slot].astype(BF16)
        wu16[...] = wu_buf[slot].astype(BF16)
        wd16[...] = wd_buf[slot].astype(BF16)
        xn = xn_scr[...]
        g = _dot(xn, wg16[...])
        u = _dot(xn, wu16[...])
        hm = (_silu(g) * u).astype(BF16)
        h_ref[...] += _dot(hm, wd16[...])
        return carry

    lax.fori_loop(0, nf, body, 0)

    h = x_ref[...] + 0.5 * _rms(h_ref[...], gpost_ref[...])
    h_ref[...] = h
    if emit_next:
        un_ref[...] = _rms(h, gnext_ref[...]).astype(BF16)


def _ffn(x, gpre, w_gu, w_down, gpost, gnext, emit_next, tm=1024, tf=256):
    t, d = x.shape
    row = lambda i: (i, 0)
    const = lambda i: (0, 0)
    once = pl.Buffered(1)
    out_shape = [jax.ShapeDtypeStruct((t, d), F32)]
    out_specs = [pl.BlockSpec((tm, d), row, pipeline_mode=once)]
    if emit_next:
        out_shape.append(jax.ShapeDtypeStruct((t, d), BF16))
        out_specs.append(pl.BlockSpec((tm, d), row, pipeline_mode=once))
    res = pl.pallas_call(
        functools.partial(_ffn_kernel, emit_next=emit_next, tf=tf),
        grid=(t // tm,),
        in_specs=[
            pl.BlockSpec((tm, d), row),
            pl.BlockSpec((1, d), const),
            pl.BlockSpec(memory_space=pl.ANY),
            pl.BlockSpec(memory_space=pl.ANY),
            pl.BlockSpec((1, d), const),
            pl.BlockSpec((1, d), const),
        ],
        out_specs=out_specs,
        out_shape=out_shape,
        scratch_shapes=[pltpu.VMEM((tm, d), BF16),
                        pltpu.VMEM((2, d, tf), F32), pltpu.VMEM((2, d, tf), F32), pltpu.VMEM((2, tf, d), F32),
                        pltpu.VMEM((d, tf), BF16), pltpu.VMEM((d, tf), BF16), pltpu.VMEM((tf, d), BF16),
                        pltpu.SemaphoreType.DMA((3, 2))],
        compiler_params=_params(("arbitrary",)),
        name="ffn",
    )(x, gpre, w_gu, w_down, gpost, gnext)
    return res if emit_next else res[0]


def _matmul_kernel(a_ref, w_ref, o_ref):
    o_ref[...] = _dot_nt(a_ref[...], w_ref[...]).astype(o_ref.dtype)


def _proj_conv_kernel(a_ref, w_ref, cw_ref, o_ref):
    tm, tn = o_ref.shape
    rb = 256
    kind = pl.program_id(1) // (GDN_QK // tn)
    scale = jnp.where(kind == 0, GDN_DK ** -0.5, 1.0)
    w = cw_ref[...]
    r8 = lax.broadcasted_iota(jnp.int32, (8, tn), 0)

    sb = 32

    def finish(r, before):
        for s0 in range(0, rb, sb):
            y = o_ref[r + s0:r + s0 + sb, :]
            acc = y * w[CONV_W - 1:CONV_W, :]
            for sft in range(1, CONV_W):
                xs = pltpu.roll(y, sft, 0)
                top = jnp.where(r8 < sft, pltpu.roll(before, sft, 0), xs[0:8])
                acc = acc + jnp.concatenate([top, xs[8:]], axis=0) * w[CONV_W - 1 - sft:CONV_W - sft, :]
            before = y[sb - 8:sb]
            c = _silu(acc)
            for hh in range(tn // GDN_DK):
                sl = slice(hh * GDN_DK, (hh + 1) * GDN_DK)
                blk = c[:, sl]
                f = lax.rsqrt(jnp.sum(blk * blk, axis=-1, keepdims=True) + EPS) * scale
                o_ref[r + s0:r + s0 + sb, sl] = blk * jnp.where(kind == 2, 1.0, f)
        return before

    o_ref[0:rb, :] = _dot_nt(a_ref[0:rb, :], w_ref[...])
    before = jnp.zeros((8, tn), F32)
    for r in range(rb, tm + rb, rb):
        if r < tm:
            o_ref[r:r + rb, :] = _dot_nt(a_ref[r:r + rb, :], w_ref[...])
        before = finish(r - rb, before)


def _proj_b_kernel(a_ref, w_ref, ws_ref, par_ref, o_ref, os_ref):
    o_ref[...] = _dot_nt(a_ref[...], w_ref[...]).astype(o_ref.dtype)

    @pl.when(pl.program_id(1) == 0)
    def _():
        y = _dot_nt(a_ref[...], ws_ref[...])
        par = par_ref[...]
        lane = lax.broadcasted_iota(jnp.int32, y.shape, 1)
        g = -jnp.exp(par[0:1, :]) * _softplus(y + par[1:2, :])
        is_g = (lane >= SM_A) & (lane < SM_A + GDN_HEADS)
        is_b = (lane >= SM_B) & (lane < SM_B + GDN_HEADS)
        os_ref[...] = jnp.where(is_g, g, jnp.where(is_b, jax.nn.sigmoid(y), y))


def _proj_b(a, w, par, tm=2048, tn=1024):
    m, k = a.shape
    c0 = _PA_COLS // tn
    cs = (_PA_COLS + _PB_COLS) // LANES
    return pl.pallas_call(
        _proj_b_kernel,
        grid=(m // tm, _PB_COLS // tn),
        in_specs=[pl.BlockSpec((tm, k), lambda i, j: (i, 0)), pl.BlockSpec((tn, k), lambda i, j: (c0 + j, 0)),
                  pl.BlockSpec((LANES, k), lambda i, j: (cs, 0)), pl.BlockSpec((2, LANES), lambda i, j: (0, 0))],
        out_specs=[pl.BlockSpec((tm, tn), lambda i, j: (i, j)), pl.BlockSpec((tm, LANES), lambda i, j: (i, 0))],
        out_shape=[jax.ShapeDtypeStruct((m, _PB_COLS), BF16), jax.ShapeDtypeStruct((m, LANES), F32)],
        compiler_params=_params(("parallel", "arbitrary")),
        name="in_proj_b",
    )(a, w, w, par)


def _matmul(a, w, col0, n, out_dtype, tm, tn, name, body=_matmul_kernel, extra=(), extra_specs=()):
    m, k = a.shape
    c0 = col0 // tn
    return pl.pallas_call(
        body,
        grid=(m // tm, n // tn),
        in_specs=[pl.BlockSpec((tm, k), lambda i, j: (i, 0)), pl.BlockSpec((tn, k), lambda i, j: (c0 + j, 0)),
                  *extra_specs],
        out_specs=pl.BlockSpec((tm, tn), lambda i, j: (i, j)),
        out_shape=jax.ShapeDtypeStruct((m, n), out_dtype),
        compiler_params=_params(("parallel", "parallel")),
        name=name,
    )(a, w, *extra)


def _norm_matmul_kernel(x_ref, g_ref, w_ref, o_ref, xn_scr):
    @pl.when(pl.program_id(1) == 0)
    def _():
        xn_scr[...] = _rms(x_ref[...], g_ref[...]).astype(BF16)

    o_ref[...] = _dot(xn_scr[...], w_ref[...].astype(BF16)).astype(o_ref.dtype)


def _norm_matmul(x, gain, w, out_dtype, tm, tn, name):
    m, k = x.shape
    n = w.shape[1]
    return pl.pallas_call(
        _norm_matmul_kernel,
        grid=(m // tm, n // tn),
        in_specs=[pl.BlockSpec((tm, k), lambda i, j: (i, 0)), pl.BlockSpec((1, k), lambda i, j: (0, 0)),
                  pl.BlockSpec((k, tn), lambda i, j: (0, j))],
        out_specs=pl.BlockSpec((tm, tn), lambda i, j: (i, j)),
        out_shape=jax.ShapeDtypeStruct((m, n), out_dtype),
        scratch_shapes=[pltpu.VMEM((tm, k), BF16)],
        compiler_params=_params(("parallel", "arbitrary")),
        name=name,
    )(x, gain, w)


def _gla_levels(chunk):
    return [chunk >> (i + 1) for i in range(int(np.log2(chunk)))]


def _gla_exponent_matrix(chunk):
    c = chunk
    i = np.arange(c)[:, None]
    t = np.arange(c)[None, :]
    blocks = [(t <= i), (t > i)]
    for h in _gla_levels(c):
        r = (i // (2 * h)) * (2 * h) + h
        upper = i >= r
        blocks.append(np.where(upper, (t > r) & (t <= i), (t > i) & (t <= r)))
    m = np.concatenate(blocks, axis=0).astype(np.float32)
    z = np.zeros_like(m)
    return np.concatenate([np.concatenate([m, m, z, z], axis=1), np.concatenate([z, z, m, m], axis=1)], axis=0)


def _gla_kernel(q_ref, k_ref, v_ref, sm_ref, r_ref, wlr_ref, blr_ref, gn_ref, mst_ref, o_ref, s_scr, e0_scr,
                e1_scr):
    c = GLA_CHUNK
    seq = q_ref.shape[0]
    s_scr[...] = jnp.zeros_like(s_scr)
    ri = lax.broadcasted_iota(jnp.int32, (c, c), 0)
    ci = lax.broadcasted_iota(jnp.int32, (c, c), 1)
    rowi = lax.broadcasted_iota(jnp.int32, (c, GLA_DK), 0)
    levels = _gla_levels(c)
    assert GLA_GROUP == 4

    grp = GLA_GROUP
    gs = range(grp)

    n_trips = seq // (grp * c)

    def gate_logits(n):
        allrows = pl.ds(pl.multiple_of(n * (grp * c), grp * c), grp * c)
        return _dot(sm_ref[allrows, :].astype(BF16), wlr_ref[...]) + blr_ref[...]

    def decays(x, e_ref):
        lg = (jnp.minimum(x, 0.0) - jnp.log1p(jnp.exp(-jnp.abs(x)))) * (1.0 / GLA_TAU)
        hi, mid = _split2(lg)
        hm = [jnp.concatenate([hi[g * c:(g + 1) * c], mid[g * c:(g + 1) * c]], axis=0) for g in gs]
        rhs = jnp.concatenate([jnp.concatenate([hm[0], hm[1]], axis=1),
                               jnp.concatenate([hm[2], hm[3]], axis=1)], axis=0)
        e_ref[...] = jnp.exp(_dot(mst_ref[...], rhs))

    def trip(n, e_ref, between):
        row0 = pl.multiple_of(n * (grp * c), grp * c)
        rows = [pl.ds(row0 + g * c, c) for g in gs]
        ex = e_ref[...]
        nr = ex.shape[0] // 2
        e = [ex[0:nr, 0:GLA_DK], ex[0:nr, GLA_DK:], ex[nr:, 0:GLA_DK], ex[nr:, GLA_DK:]]
        q = [q_ref[rows[g], :] * (GLA_DK ** -0.5) for g in gs]
        k = [k_ref[rows[g], :] for g in gs]
        v = [v_ref[rows[g], :].astype(BF16) for g in gs]
        attn = [jnp.where(ri == ci, _dot_nt(q[g].astype(BF16), k[g].astype(BF16)), 0.0) for g in gs]
        for lvl, h in enumerate(levels):
            upper = (rowi & h) != 0
            sh = int(np.log2(2 * h))
            same = (ri >> sh) == (ci >> sh)
            for g in gs:
                f = e[g][(2 + lvl) * c:(3 + lvl) * c]
                qt = jnp.where(upper, q[g] * f, 0.0).astype(BF16)
                kt = jnp.where(upper, 0.0, k[g] * f).astype(BF16)
                attn[g] = attn[g] + jnp.where(same, _dot_nt(qt, kt), 0.0)
        vk = [_dot_tn(v[g], (k[g] * e[g][c:2 * c]).astype(BF16)) for g in gs]
        between()
        av = [_dot(attn[g].astype(BF16), v[g]) for g in gs]
        st = s_scr[...]
        for g in gs:
            o = _dot_nt((q[g] * e[g][0:c]).astype(BF16), st.astype(BF16)) + av[g]
            st = st * e[g][c - 1:c, :] + vk[g]
            on = _rms(o, gn_ref[...]) * _silu(r_ref[rows[g], :].astype(F32))
            o_ref[rows[g], :] = on.astype(o_ref.dtype)
        s_scr[...] = st

    assert n_trips % 2 == 0
    decays(gate_logits(0), e0_scr)

    def body(m, carry):
        x1 = gate_logits(2 * m + 1)
        trip(2 * m, e0_scr, lambda: decays(x1, e1_scr))
        x2 = gate_logits(jnp.minimum(2 * m + 2, n_trips - 1))
        trip(2 * m + 1, e1_scr, lambda: decays(x2, e0_scr))
        return carry

    lax.fori_loop(0, n_trips // 2, body, 0)


def _gla(pa, pb, ps, wlr, blr, gnorm, bsz, seq):
    t = bsz * seq
    mst = jnp.asarray(_gla_exponent_matrix(GLA_CHUNK), BF16)
    nq = GLA_QK // GLA_DK
    return pl.pallas_call(
        _gla_kernel,
        grid=(bsz, GLA_HEADS),
        in_specs=[
            pl.BlockSpec((seq, GLA_DK), lambda b, h: (b, h)),
            pl.BlockSpec((seq, GLA_DK), lambda b, h: (b, nq + h)),
            pl.BlockSpec((seq, GLA_DV), lambda b, h: (b, 2 * GLA_QK // GLA_DV + h)),
            pl.BlockSpec((seq, LANES), lambda b, h: (b, 0)),
            pl.BlockSpec((seq, GLA_DV), lambda b, h: (b, _PB_R // GLA_DV + h)),
            pl.BlockSpec((LANES, GLA_DK), lambda b, h: (0, h)),
            pl.BlockSpec((1, GLA_DK), lambda b, h: (0, h)),
            pl.BlockSpec((1, GLA_DV), lambda b, h: (0, 0)),
            pl.BlockSpec(mst.shape, lambda b, h: (0, 0)),
        ],
        out_specs=pl.BlockSpec((seq, GLA_DV), lambda b, h: (b, h)),
        out_shape=jax.ShapeDtypeStruct((t, GLA_V), BF16),
        scratch_shapes=[pltpu.VMEM((GLA_DV, GLA_DK), F32), pltpu.VMEM((mst.shape[0], 2 * GLA_DK), F32),
                        pltpu.VMEM((mst.shape[0], 2 * GLA_DK), F32)],
        compiler_params=_params(("parallel", "parallel")),
        name="gla",
    )(pa, pa, pa, ps, pb, wlr, blr, gnorm, mst)


def _gdn_kernel(q_ref, k_ref, v_ref, sm_ref, z_ref, gn_ref, o_ref, bms, ns, qms, os_, egl, s_scr):
    c = GDN_CHUNK
    seq = q_ref.shape[0]
    head0 = pl.program_id(1) * GDN_HPS

    ri = lax.broadcasted_iota(jnp.int32, (c, c), 0)
    ci = lax.broadcasted_iota(jnp.int32, (c, c), 1)
    lane = lax.broadcasted_iota(jnp.int32, (c, LANES), 1)
    tril = jnp.where(ri >= ci, 1.0, 0.0).astype(BF16)
    causal = ri >= ci
    n_sq = int(np.log2(c))
    lower_left = [(((ri ^ ci) >> (l + 1)) == 0) & ((ri & (1 << l)) != 0) & ((ci & (1 << l)) == 0) for l in range(n_sq)]

    grp = GDN_GROUP
    chains = [(g, j) for g in range(grp) for j in range(GDN_HPS)]

    def phase1(n, hooks):
        rows, q, k, kb, gb, rhs0 = {}, {}, {}, {}, {}, {}
        for g in range(grp):
            rows[g] = pl.ds(pl.multiple_of((n * grp + g) * c, c), c)
            sm = sm_ref[rows[g], :]
            for j in range(GDN_HPS):
                sl = slice(j * GDN_DK, (j + 1) * GDN_DK)
                q[g, j] = q_ref[rows[g], sl]
                k[g, j] = k_ref[rows[g], sl]
                gcol = jnp.sum(jnp.where(lane == SM_A + head0 + j, sm, 0.0), axis=-1, keepdims=True)
                beta = jnp.sum(jnp.where(lane == SM_B + head0 + j, sm, 0.0), axis=-1, keepdims=True)
                gb[g, j] = jnp.broadcast_to(gcol, (c, LANES))
                kb[g, j] = k[g, j] * beta
                rhs0[g, j] = v_ref[rows[g], sl] * beta
        cum_r, cum_c, kt = {}, {}, {}
        for ch in chains:
            hi, mid = _split2(gb[ch])
            r = _dot(tril, jnp.concatenate([hi, mid], axis=1))
            cum_r[ch] = r[:, :LANES] + r[:, LANES:]
        for ch in chains:
            cum_c[ch] = cum_r[ch].T
            kt[ch] = k[ch].T
        kk = {}
        for ch in chains:
            k16 = k[ch].astype(BF16)
            kk[ch] = _dot_nt(jnp.concatenate([kb[ch].astype(BF16), q[ch].astype(BF16)], axis=0), k16)
        low, at16 = {}, {}
        for ch in chains:
            dec = jnp.where(causal, jnp.exp(jnp.where(causal, cum_r[ch] - cum_c[ch], 0.0)), 0.0)
            low[ch] = jnp.where(ri > ci, kk[ch][:c] * dec, 0.0)
            at16[ch] = jnp.where(causal, kk[ch][c:] * dec, 0.0).astype(BF16)
        x = {ch: -jnp.where(lower_left[0], low[ch], 0.0) for ch in chains}
        for lvl in range(1, n_sq):
            e = {}
            for ch in chains:
                cm = jnp.where(lower_left[lvl], low[ch], 0.0)
                e[ch] = cm + _mm(x[ch], cm)
            for ch in chains:
                x[ch] = x[ch] - e[ch] - _mm(e[ch], x[ch])
            if lvl in hooks:
                hooks[lvl]()
        wu, egc = {}, {}
        for ch in chains:
            egc[ch] = jnp.exp(cum_r[ch])
            rhs = jnp.concatenate([kb[ch] * egc[ch], rhs0[ch]], axis=1)
            wu[ch] = (rhs + _mm(x[ch], rhs)).astype(BF16)
        for ch in chains:
            g, j = ch
            sl = slice(j * GDN_DK, (j + 1) * GDN_DK)
            glast = cum_r[ch][c - 1:c, :]
            kdt16 = (kt[ch] * jnp.exp(glast - cum_c[ch][0:1, :])).astype(BF16)
            r = _dot(jnp.concatenate([kdt16, at16[ch]], axis=0), wu[ch])
            bms[rows[g], sl] = (-r[:c, :GDN_DV]).astype(BF16)
            ns[rows[g], sl] = r[:c, GDN_DV:]
            qms[rows[g], sl] = (q[ch] * egc[ch] - r[c:, :GDN_DV]).astype(BF16)
            os_[rows[g], sl] = r[c:, GDN_DV:]
            egl[pl.ds(pl.multiple_of((n * grp + g) * 8, 8), 8), sl] = jnp.broadcast_to(jnp.exp(glast), (8, LANES))

    s_scr[...] = jnp.zeros_like(s_scr)

    def phase2(n):
        rows = pl.ds(pl.multiple_of(n * c, c), c)
        sls = [slice(j * GDN_DK, (j + 1) * GDN_DK) for j in range(GDN_HPS)]
        s = [s_scr[j] for j in range(GDN_HPS)]
        s16 = [sj.astype(BF16) for sj in s]
        r = [_dot(jnp.concatenate([bms[rows, sl], qms[rows, sl]], axis=0), s16[j]) for j, sl in enumerate(sls)]
        for j, sl in enumerate(sls):
            eg = egl[pl.ds(pl.multiple_of(n * 8, 8), 8), sl][0:1, :]
            s_scr[j] = s[j] * eg + r[j][:c] + ns[rows, sl]
            o = r[j][c:] + os_[rows, sl]
            on = _rms(o, gn_ref[...]) * _silu(z_ref[rows, sl].astype(F32))
            o_ref[rows, sl] = on.astype(o_ref.dtype)

    assert grp == 2 and n_sq >= 5
    n_trips = seq // (grp * c)
    phase1(0, {})

    def body(n, carry):
        first = (n - 1) * grp
        phase1(n, {1: lambda: phase2(first), n_sq - 2: lambda: phase2(first + 1)})
        return carry

    lax.fori_loop(1, n_trips, body, 0)
    for g in range(grp):
        phase2((n_trips - 1) * grp + g)


def _gdn(pg, pb, ps, gnorm, bsz, seq):
    t = bsz * seq
    nb = seq // GDN_CHUNK
    wd = GDN_HPS * GDN_DK
    ng = GDN_HEADS // GDN_HPS
    return pl.pallas_call(
        _gdn_kernel,
        grid=(bsz, ng),
        in_specs=[
            pl.BlockSpec((seq, wd), lambda b, h: (b, h)),
            pl.BlockSpec((seq, wd), lambda b, h: (b, ng + h)),
            pl.BlockSpec((seq, wd), lambda b, h: (b, 2 * ng + h)),
            pl.BlockSpec((seq, LANES), lambda b, h: (b, 0)),
            pl.BlockSpec((seq, wd), lambda b, h: (b, _PB_Z // wd + h)),
            pl.BlockSpec((1, GDN_DV), lambda b, h: (0, 0)),
        ],
        out_specs=pl.BlockSpec((seq, wd), lambda b, h: (b, h)),
        out_shape=jax.ShapeDtypeStruct((t, GDN_V), BF16),
        scratch_shapes=[
            pltpu.VMEM((seq, wd), BF16), pltpu.VMEM((seq, wd), F32), pltpu.VMEM((seq, wd), BF16),
            pltpu.VMEM((seq, wd), F32), pltpu.VMEM((nb * 8, wd), F32),
            pltpu.VMEM((GDN_HPS, GDN_DK, GDN_DV), F32),
        ],
        compiler_params=_params(("parallel", "parallel")),
        name="gdn",
    )(pg, pg, pg, ps, pb, gnorm)


def _xa_kernel(q_ref, mk_ref, mv_ref, o_ref):
    s = _dot_nt(q_ref[...], mk_ref[...]) * (XA_DH ** -0.5)
    p = jnp.exp(s - jnp.max(s, axis=-1, keepdims=True))
    den = jnp.sum(p, axis=-1, keepdims=True)
    o_ref[...] = (_dot(p.astype(BF16), mv_ref[...]) / den).astype(o_ref.dtype)


def _xa(pb, mkv, bsz, seq, ts=2048):
    t = bsz * seq
    nt = seq // ts
    return pl.pallas_call(
        _xa_kernel,
        grid=(bsz, XA_HEADS, nt),
        in_specs=[
            pl.BlockSpec((ts, XA_DH), lambda b, h, i: (b * nt + i, _PB_XQ // XA_DH + h)),
            pl.BlockSpec((MEM_LEN, XA_DH), lambda b, h, i: (b, h)),
            pl.BlockSpec((MEM_LEN, XA_DH), lambda b, h, i: (b, XA_HEADS + h)),
        ],
        out_specs=pl.BlockSpec((ts, XA_DH), lambda b, h, i: (b * nt + i, h)),
        out_shape=jax.ShapeDtypeStruct((t, XA_W), BF16),
        compiler_params=_params(("parallel", "parallel", "parallel")),
        name="xattn",
    )(pb, mkv, mkv)


def _merge_kernel(h_ref, oa_ref, ob_ref, oc_ref, ga_ref, gb_ref, gc_ref, wa_hbm, wb_hbm, wc_hbm, wo_hbm,
                  gpost_ref, o_ref, wa_ref, wb_ref, wc_ref, wo_ref, stage, sem):
    @pl.when(pl.program_id(0) == 0)
    def _():
        rb = stage.shape[1]
        chunks = [(src, dst, r) for src, dst in ((wa_hbm, wa_ref), (wb_hbm, wb_ref), (wc_hbm, wc_ref), (wo_hbm, wo_ref))
                  for r in range(0, dst.shape[0], rb)]

        def chunk_copy(n):
            src, _, r = chunks[n]
            return pltpu.make_async_copy(src.at[pl.ds(r, rb), :], stage.at[n % 2], sem.at[n % 2])

        chunk_copy(0).start()
        for n, (_, dst, r) in enumerate(chunks):
            if n + 1 < len(chunks):
                chunk_copy(n + 1).start()
            chunk_copy(n).wait()
            dst[r:r + rb, :] = stage[n % 2].astype(BF16)

    mixed = jax.nn.sigmoid(ga_ref[...].astype(F32)) * _dot(oa_ref[...], wa_ref[...])
    mixed = mixed + jax.nn.sigmoid(gb_ref[...].astype(F32)) * _dot(ob_ref[...], wb_ref[...])
    mixed = mixed + jax.nn.sigmoid(gc_ref[...].astype(F32)) * _dot(oc_ref[...], wc_ref[...])
    m = _dot(mixed.astype(BF16), wo_ref[...])
    o_ref[...] = h_ref[...] + _rms(m, gpost_ref[...])


def _merge(h, oa, ob, oc, pb, wa, wb, wc, wo, gpost, tm=256, rb=256):
    t, d = h.shape
    row = lambda i: (i, 0)
    const = lambda i: (0, 0)
    hbm = pl.BlockSpec(memory_space=pl.ANY)
    return pl.pallas_call(
        _merge_kernel,
        grid=(t // tm,),
        in_specs=[
            pl.BlockSpec((tm, d), row),
            pl.BlockSpec((tm, GLA_V), row), pl.BlockSpec((tm, GDN_V), row), pl.BlockSpec((tm, XA_W), row),
            pl.BlockSpec((tm, d), lambda i: (i, 0)), pl.BlockSpec((tm, d), lambda i: (i, 1)),
            pl.BlockSpec((tm, d), lambda i: (i, 2)),
            hbm, hbm, hbm, hbm,
            pl.BlockSpec((1, d), const),
        ],
        out_specs=pl.BlockSpec((tm, d), row),
        out_shape=jax.ShapeDtypeStruct((t, d), F32),
        scratch_shapes=[pltpu.VMEM((GLA_V, d), BF16), pltpu.VMEM((GDN_V, d), BF16), pltpu.VMEM((XA_W, d), BF16),
                        pltpu.VMEM((d, d), BF16), pltpu.VMEM((2, rb, d), F32), pltpu.SemaphoreType.DMA((2,))],
        compiler_params=_params(("arbitrary",)),
        name="merge",
    )(h, oa, ob, oc, pb, pb, pb, wa, wb, wc, wo, gpost)


_PA_GDN = 2 * GLA_QK + GLA_V
_PA_COLS = _PA_GDN + 2 * GDN_QK + GDN_V
_PB_R = N_BRANCH * D_MODEL
_PB_Z = _PB_R + GLA_V
_PB_XQ = _PB_Z + GDN_V
_PB_COLS = _PB_XQ + XA_W


_W_ORDER = (0, 1, 2, 5, 10, 4, 8, 9)
_W_SMALL = ((_IN_OFF[3], GLA_RANK), (_IN_OFF[6], 2 * GDN_HEADS))
_W_ALL_COLS = _PA_COLS + _PB_COLS + LANES
_W_RB = 512


def _w_block_rows():
    tab = []
    for i in _W_ORDER:
        assert (_IN_OFF[i + 1] - _IN_OFF[i]) % _W_RB == 0
        tab.extend(range(_IN_OFF[i], _IN_OFF[i + 1], _W_RB))
    return np.asarray(tab, np.int32)


def _relayout_kernel(tab_ref, wt_hbm, o_ref, buf, sem):
    j = pl.program_id(0)
    n_main = pl.num_programs(0) - 1

    def block_copy(blk):
        start = pl.multiple_of(tab_ref[blk], 16)
        return pltpu.make_async_copy(wt_hbm.at[pl.ds(start, _W_RB), :], buf.at[blk % 2], sem.at[blk % 2])

    @pl.when(j == 0)
    def _():
        block_copy(j).start()

    @pl.when(j + 1 < n_main)
    def _():
        block_copy(j + 1).start()

    @pl.when(j < n_main)
    def _():
        block_copy(j).wait()
        o_ref[...] = buf[j % 2].astype(BF16)

    @pl.when(j == n_main)
    def _():
        dst = 0
        copies = []
        for n, (row0, rows) in enumerate(_W_SMALL):
            copies.append(pltpu.make_async_copy(wt_hbm.at[pl.ds(row0, rows), :], buf.at[0, pl.ds(dst, rows), :],
                                                sem.at[n]))
            dst += rows
        for cp in copies:
            cp.start()
        for cp in copies:
            cp.wait()
        o_ref[0:dst, :] = buf[0, 0:dst, :].astype(BF16)
        o_ref[dst:, :] = jnp.zeros((_W_RB - dst, o_ref.shape[1]), BF16)


def _split_w_in(w_in_t):
    n, k = w_in_t.shape
    tab = _w_block_rows()
    assert len(_W_SMALL) <= 2
    return pl.pallas_call(
        _relayout_kernel,
        grid_spec=pltpu.PrefetchScalarGridSpec(
            num_scalar_prefetch=1,
            grid=(len(tab) + 1,),
            in_specs=[pl.BlockSpec(memory_space=pl.ANY)],
            out_specs=pl.BlockSpec((_W_RB, k), lambda i, tab: (i, 0)),
            scratch_shapes=[pltpu.VMEM((2, _W_RB, k), F32), pltpu.SemaphoreType.DMA((2,))],
        ),
        out_shape=jax.ShapeDtypeStruct((_W_ALL_COLS, k), BF16),
        compiler_params=_params(("arbitrary",)),
        name="w_in_relayout",
    )(jnp.asarray(tab), w_in_t)


def kernel(x, mem, n_ffn1_pre, w_ffn1_gu, w_ffn1_down, n_ffn1_post, n_mix_pre, w_in, gla_w_lr2, gla_b_lr,
           gla_norm, gdn_conv, gdn_a_log, gdn_dt_bias, gdn_norm, mem_norm, w_mem_kv, w_up_gla, w_up_gdn,
           w_up_xa, w_out, n_mix_post, n_ffn2_pre, w_ffn2_gu, w_ffn2_down, n_ffn2_post):
    bsz, seq, d = x.shape
    t = bsz * seq
    h = x.reshape(t, d)
    for l in range(n_ffn1_pre.shape[0]):
        row = lambda a: a[l][None, :]
        h, u = _ffn(h, row(n_ffn1_pre), w_ffn1_gu[l], w_ffn1_down[l],
                    row(n_ffn1_post), row(n_mix_pre), emit_next=True)

        w_all = _split_w_in(w_in[l].T)
        par = jnp.zeros((2, LANES), F32)
        par = par.at[0, SM_A:SM_A + GDN_HEADS].set(gdn_a_log[l]).at[1, SM_A:SM_A + GDN_HEADS].set(gdn_dt_bias[l])
        tn_g = 512
        pa = _matmul(u, w_all, 0, _PA_GDN, F32, 2048, 1024, "in_proj_a")
        pg = _matmul(u, w_all, _PA_GDN, _PA_COLS - _PA_GDN, F32, seq, tn_g, "in_proj_g", body=_proj_conv_kernel,
                     extra=(gdn_conv[l],), extra_specs=(pl.BlockSpec((CONV_W, tn_g), lambda i, j: (0, j)),))
        pb, ps = _proj_b(u, w_all, par)

        wlr = jnp.pad(gla_w_lr2[l], ((0, LANES - GLA_RANK), (0, 0))).astype(BF16)
        o_gla = _gla(pa, pb, ps, wlr, row(gla_b_lr), row(gla_norm), bsz, seq)
        o_gdn = _gdn(pg, pb, ps, row(gdn_norm), bsz, seq)

        mkv = _norm_matmul(mem.reshape(bsz * MEM_LEN, d), row(mem_norm), w_mem_kv[l], BF16,
                           bsz * MEM_LEN, 512, "mem_kv")
        o_xa = _xa(pb, mkv, bsz, seq)

        h = _merge(h, o_gla, o_gdn, o_xa, pb, w_up_gla[l], w_up_gdn[l], w_up_xa[l], w_out[l], row(n_mix_post))
        h = _ffn(h, row(n_ffn2_pre), w_ffn2_gu[l], w_ffn2_down[l],
                 row(n_ffn2_post), row(n_ffn2_post), emit_next=False)
    return h.reshape(bsz, seq, d)
```

```python
import functools

import numpy as np
import jax
import jax.numpy as jnp
from jax import lax
from jax.experimental import pallas as pl
from jax.experimental.pallas import tpu as pltpu

F32 = jnp.float32
BF16 = jnp.bfloat16

D_MODEL = 2048
MEM_LEN = 256
EPS = 1e-6
GLA_HEADS = 4
GLA_DK = 128
GLA_DV = 256
GLA_QK = GLA_HEADS * GLA_DK
GLA_V = GLA_HEADS * GLA_DV
GLA_RANK = 16
GLA_TAU = 16.0
GDN_HEADS = 8
GDN_DK = 128
GDN_DV = 128
GDN_QK = GDN_HEADS * GDN_DK
GDN_V = GDN_HEADS * GDN_DV
CONV_W = 4
XA_HEADS = 4
XA_DH = 256
XA_W = XA_HEADS * XA_DH
N_BRANCH = 3
D_FF = 5632

LANES = 128
GLA_CHUNK = 64
GLA_GROUP = 4
GDN_CHUNK = 128
GDN_HPS = 4
GDN_GROUP = 2
VMEM_LIMIT = 56 * 1024 * 1024

_IN_SIZES = (GLA_QK, GLA_QK, GLA_V, GLA_RANK, GLA_V, 2 * GDN_QK + GDN_V, GDN_HEADS, GDN_HEADS, GDN_V,
             XA_W, N_BRANCH * D_MODEL)
_IN_OFF = tuple(int(v) for v in np.cumsum((0,) + _IN_SIZES))
SM_LR, SM_B, SM_A = 0, GLA_RANK, GLA_RANK + GDN_HEADS


def _dot(a, b):
    return jnp.dot(a, b, preferred_element_type=F32)


def _dot_nt(a, b):
    return lax.dot_general(a, b, (((1,), (1,)), ((), ())), preferred_element_type=F32)


def _dot_tn(a, b):
    return lax.dot_general(a, b, (((0,), (0,)), ((), ())), preferred_element_type=F32)


def _mm(a, b):
    return _dot(a.astype(BF16), b.astype(BF16))


def _split2(x):
    hi = x.astype(BF16)
    mid = (x - hi.astype(F32)).astype(BF16)
    return hi, mid


def _rms(x, gain):
    return x * lax.rsqrt(jnp.mean(x * x, axis=-1, keepdims=True) + EPS) * gain


def _silu(x):
    return x * jax.nn.sigmoid(x)


def _softplus(x):
    return jnp.maximum(x, 0.0) + jnp.log1p(jnp.exp(-jnp.abs(x)))


def _params(sem):
    return pltpu.CompilerParams(dimension_semantics=sem, vmem_limit_bytes=VMEM_LIMIT)


def _ffn_kernel(x_ref, gpre_ref, wgu_hbm, wd_hbm, gpost_ref, gnext_ref, *out_and_scratch, emit_next, tf):
    if emit_next:
        h_ref, un_ref, xn_scr, wg_buf, wu_buf, wd_buf, wg16, wu16, wd16, sem = out_and_scratch
    else:
        h_ref, xn_scr, wg_buf, wu_buf, wd_buf, wg16, wu16, wd16, sem = out_and_scratch
    i = pl.program_id(0)
    nf = D_FF // tf
    assert nf % 2 == 0

    def tile_copies(jj, slot):
        col = pl.multiple_of(jj * tf, tf)
        return (pltpu.make_async_copy(wgu_hbm.at[:, pl.ds(col, tf)], wg_buf.at[slot], sem.at[0, slot]),
                pltpu.make_async_copy(wgu_hbm.at[:, pl.ds(D_FF + col, tf)], wu_buf.at[slot], sem.at[1, slot]),
                pltpu.make_async_copy(wd_hbm.at[pl.ds(col, tf), :], wd_buf.at[slot], sem.at[2, slot]))

    @pl.when(i == 0)
    def _():
        for cp in tile_copies(0, 0):
            cp.start()

    xn_scr[...] = _rms(x_ref[...], gpre_ref[...]).astype(BF16)
    h_ref[...] = jnp.zeros_like(h_ref)

    def body(jj, carry):
        slot = jj % 2
        @pl.when(jj + 1 < nf)
        def _():
            for cp in tile_copies(jj + 1, 1 - slot):
                cp.start()

        @pl.when((jj + 1 == nf) & (i + 1 < pl.num_programs(0)))
        def _():
            for cp in tile_copies(0, 0):
                cp.start()

        for cp in tile_copies(jj, slot):
            cp.wait()
        wg16[...] = wg_buf[slot].astype(BF16)
        wu16[...] = wu_buf[slot].astype(BF16)
        wd16[...] = wd_buf[slot].astype(BF16)
        xn = xn_scr[...]
        g = _dot(xn, wg16[...])
        u = _dot(xn, wu16[...])
        hm = (_silu(g) * u).astype(BF16)
        h_ref[...] += _dot(hm, wd16[...])
        return carry

    lax.fori_loop(0, nf, body, 0)

    h = x_ref[...] + 0.5 * _rms(h_ref[...], gpost_ref[...])
    h_ref[...] = h
    if emit_next:
        un_ref[...] = _rms(h, gnext_ref[...]).astype(BF16)


def _ffn(x, gpre, w_gu, w_down, gpost, gnext, emit_next, tm=1024, tf=256):
    t, d = x.shape
    row = lambda i: (i, 0)
    const = lambda i: (0, 0)
    once = pl.Buffered(1)
    out_shape = [jax.ShapeDtypeStruct((t, d), F32)]
    out_specs = [pl.BlockSpec((tm, d), row, pipeline_mode=once)]
    if emit_next:
        out_shape.append(jax.ShapeDtypeStruct((t, d), BF16))
        out_specs.append(pl.BlockSpec((tm, d), row, pipeline_mode=once))
    res = pl.pallas_call(
        functools.partial(_ffn_kernel, emit_next=emit_next, tf=tf),
        grid=(t // tm,),
        in_specs=[
            pl.BlockSpec((tm, d), row),
            pl.BlockSpec((1, d), const),
            pl.BlockSpec(memory_space=pl.ANY),
            pl.BlockSpec(memory_space=pl.ANY),
            pl.BlockSpec((1, d), const),
            pl.BlockSpec((1, d), const),
        ],
        out_specs=out_specs,
        out_shape=out_shape,
        scratch_shapes=[pltpu.VMEM((tm, d), BF16),
                        pltpu.VMEM((2, d, tf), F32), pltpu.VMEM((2, d, tf), F32), pltpu.VMEM((2, tf, d), F32),
                        pltpu.VMEM((d, tf), BF16), pltpu.VMEM((d, tf), BF16), pltpu.VMEM((tf, d), BF16),
                        pltpu.SemaphoreType.DMA((3, 2))],
        compiler_params=_params(("arbitrary",)),
        name="ffn",
    )(x, gpre, w_gu, w_down, gpost, gnext)
    return res if emit_next else res[0]


def _matmul_kernel(a_ref, w_ref, o_ref):
    o_ref[...] = _dot_nt(a_ref[...], w_ref[...]).astype(o_ref.dtype)


def _proj_conv_kernel(a_ref, w_ref, cw_ref, o_ref):
    tm, tn = o_ref.shape
    rb = 256
    kind = pl.program_id(1) // (GDN_QK // tn)
    scale = jnp.where(kind == 0, GDN_DK ** -0.5, 1.0)
    w = cw_ref[...]
    r8 = lax.broadcasted_iota(jnp.int32, (8, tn), 0)

    sb = 32

    def finish(r, before):
        for s0 in range(0, rb, sb):
            y = o_ref[r + s0:r + s0 + sb, :]
            acc = y * w[CONV_W - 1:CONV_W, :]
            for sft in range(1, CONV_W):
                xs = pltpu.roll(y, sft, 0)
                top = jnp.where(r8 < sft, pltpu.roll(before, sft, 0), xs[0:8])
                acc = acc + jnp.concatenate([top, xs[8:]], axis=0) * w[CONV_W - 1 - sft:CONV_W - sft, :]
            before = y[sb - 8:sb]
            c = _silu(acc)
            for hh in range(tn // GDN_DK):
                sl = slice(hh * GDN_DK, (hh + 1) * GDN_DK)
                blk = c[:, sl]
                f = lax.rsqrt(jnp.sum(blk * blk, axis=-1, keepdims=True) + EPS) * scale
                o_ref[r + s0:r + s0 + sb, sl] = blk * jnp.where(kind == 2, 1.0, f)
        return before

    o_ref[0:rb, :] = _dot_nt(a_ref[0:rb, :], w_ref[...])
    before = jnp.zeros((8, tn), F32)
    for r in range(rb, tm + rb, rb):
        if r < tm:
            o_ref[r:r + rb, :] = _dot_nt(a_ref[r:r + rb, :], w_ref[...])
        before = finish(r - rb, before)


def _proj_b_kernel(a_ref, w_ref, ws_ref, par_ref, o_ref, os_ref):
    o_ref[...] = _dot_nt(a_ref[...], w_ref[...]).astype(o_ref.dtype)

    @pl.when(pl.program_id(1) == 0)
    def _():
        y = _dot_nt(a_ref[...], ws_ref[...])
        par = par_ref[...]
        lane = lax.broadcasted_iota(jnp.int32, y.shape, 1)
        g = -jnp.exp(par[0:1, :]) * _softplus(y + par[1:2, :])
        is_g = (lane >= SM_A) & (lane < SM_A + GDN_HEADS)
        is_b = (lane >= SM_B) & (lane < SM_B + GDN_HEADS)
        os_ref[...] = jnp.where(is_g, g, jnp.where(is_b, jax.nn.sigmoid(y), y))


def _proj_b(a, w, par, tm=2048, tn=1024):
    m, k = a.shape
    c0 = _PA_COLS // tn
    cs = (_PA_COLS + _PB_COLS) // LANES
    return pl.pallas_call(
        _proj_b_kernel,
        grid=(m // tm, _PB_COLS // tn),
        in_specs=[pl.BlockSpec((tm, k), lambda i, j: (i, 0)), pl.BlockSpec((tn, k), lambda i, j: (c0 + j, 0)),
                  pl.BlockSpec((LANES, k), lambda i, j: (cs, 0)), pl.BlockSpec((2, LANES), lambda i, j: (0, 0))],
        out_specs=[pl.BlockSpec((tm, tn), lambda i, j: (i, j)), pl.BlockSpec((tm, LANES), lambda i, j: (i, 0))],
        out_shape=[jax.ShapeDtypeStruct((m, _PB_COLS), BF16), jax.ShapeDtypeStruct((m, LANES), F32)],
        compiler_params=_params(("parallel", "arbitrary")),
        name="in_proj_b",
    )(a, w, w, par)


def _matmul(a, w, col0, n, out_dtype, tm, tn, name, body=_matmul_kernel, extra=(), extra_specs=()):
    m, k = a.shape
    c0 = col0 // tn
    return pl.pallas_call(
        body,
        grid=(m // tm, n // tn),
        in_specs=[pl.BlockSpec((tm, k), lambda i, j: (i, 0)), pl.BlockSpec((tn, k), lambda i, j: (c0 + j, 0)),
                  *extra_specs],
        out_specs=pl.BlockSpec((tm, tn), lambda i, j: (i, j)),
        out_shape=jax.ShapeDtypeStruct((m, n), out_dtype),
        compiler_params=_params(("parallel", "parallel")),
        name=name,
    )(a, w, *extra)


def _norm_matmul_kernel(x_ref, g_ref, w_ref, o_ref, xn_scr):
    @pl.when(pl.program_id(1) == 0)
    def _():
        xn_scr[...] = _rms(x_ref[...], g_ref[...]).astype(BF16)

    o_ref[...] = _dot(xn_scr[...], w_ref[...].astype(BF16)).astype(o_ref.dtype)


def _norm_matmul(x, gain, w, out_dtype, tm, tn, name):
    m, k = x.shape
    n = w.shape[1]
    return pl.pallas_call(
        _norm_matmul_kernel,
        grid=(m // tm, n // tn),
        in_specs=[pl.BlockSpec((tm, k), lambda i, j: (i, 0)), pl.BlockSpec((1, k), lambda i, j: (0, 0)),
                  pl.BlockSpec((k, tn), lambda i, j: (0, j))],
        out_specs=pl.BlockSpec((tm, tn), lambda i, j: (i, j)),
        out_shape=jax.ShapeDtypeStruct((m, n), out_dtype),
        scratch_shapes=[pltpu.VMEM((tm, k), BF16)],
        compiler_params=_params(("parallel", "arbitrary")),
        name=name,
    )(x, gain, w)


def _gla_levels(chunk):
    return [chunk >> (i + 1) for i in range(int(np.log2(chunk)))]


def _gla_exponent_matrix(chunk):
    c = chunk
    i = np.arange(c)[:, None]
    t = np.arange(c)[None, :]
    blocks = [(t <= i), (t > i)]
    for h in _gla_levels(c):
        r = (i // (2 * h)) * (2 * h) + h
        upper = i >= r
        blocks.append(np.where(upper, (t > r) & (t <= i), (t > i) & (t <= r)))
    m = np.concatenate(blocks, axis=0).astype(np.float32)
    z = np.zeros_like(m)
    return np.concatenate([np.concatenate([m, m, z, z], axis=1), np.concatenate([z, z, m, m], axis=1)], axis=0)


def _gla_kernel(q_ref, k_ref, v_ref, sm_ref, r_ref, wlr_ref, blr_ref, gn_ref, mst_ref, o_ref, s_scr, e0_scr,
                e1_scr):
    c = GLA_CHUNK
    seq = q_ref.shape[0]
    s_scr[...] = jnp.zeros_like(s_scr)
    ri = lax.broadcasted_iota(jnp.int32, (c, c), 0)
    ci = lax.broadcasted_iota(jnp.int32, (c, c), 1)
    rowi = lax.broadcasted_iota(jnp.int32, (c, GLA_DK), 0)
    levels = _gla_levels(c)
    assert GLA_GROUP == 4

    grp = GLA_GROUP
    gs = range(grp)

    n_trips = seq // (grp * c)

    def gate_logits(n):
        allrows = pl.ds(pl.multiple_of(n * (grp * c), grp * c), grp * c)
        return _dot(sm_ref[allrows, :].astype(BF16), wlr_ref[...]) + blr_ref[...]

    def decays(x, e_ref):
        lg = (jnp.minimum(x, 0.0) - jnp.log1p(jnp.exp(-jnp.abs(x)))) * (1.0 / GLA_TAU)
        hi, mid = _split2(lg)
        hm = [jnp.concatenate([hi[g * c:(g + 1) * c], mid[g * c:(g + 1) * c]], axis=0) for g in gs]
        rhs = jnp.concatenate([jnp.concatenate([hm[0], hm[1]], axis=1),
                               jnp.concatenate([hm[2], hm[3]], axis=1)], axis=0)
        e_ref[...] = jnp.exp(_dot(mst_ref[...], rhs))

    def trip(n, e_ref, between):
        row0 = pl.multiple_of(n * (grp * c), grp * c)
        rows = [pl.ds(row0 + g * c, c) for g in gs]
        ex = e_ref[...]
        nr = ex.shape[0] // 2
        e = [ex[0:nr, 0:GLA_DK], ex[0:nr, GLA_DK:], ex[nr:, 0:GLA_DK], ex[nr:, GLA_DK:]]
        q = [q_ref[rows[g], :] * (GLA_DK ** -0.5) for g in gs]
        k = [k_ref[rows[g], :] for g in gs]
        v = [v_ref[rows[g], :].astype(BF16) for g in gs]
        attn = [jnp.where(ri == ci, _dot_nt(q[g].astype(BF16), k[g].astype(BF16)), 0.0) for g in gs]
        for lvl, h in enumerate(levels):
            upper = (rowi & h) != 0
            sh = int(np.log2(2 * h))
            same = (ri >> sh) == (ci >> sh)
            for g in gs:
                f = e[g][(2 + lvl) * c:(3 + lvl) * c]
                qt = jnp.where(upper, q[g] * f, 0.0).astype(BF16)
                kt = jnp.where(upper, 0.0, k[g] * f).astype(BF16)
                attn[g] = attn[g] + jnp.where(same, _dot_nt(qt, kt), 0.0)
        vk = [_dot_tn(v[g], (k[g] * e[g][c:2 * c]).astype(BF16)) for g in gs]
        between()
        av = [_dot(attn[g].astype(BF16), v[g]) for g in gs]
        st = s_scr[...]
        for g in gs:
            o = _dot_nt((q[g] * e[g][0:c]).astype(BF16), st.astype(BF16)) + av[g]
            st = st * e[g][c - 1:c, :] + vk[g]
            on = _rms(o, gn_ref[...]) * _silu(r_ref[rows[g], :].astype(F32))
            o_ref[rows[g], :] = on.astype(o_ref.dtype)
        s_scr[...] = st

    assert n_trips % 2 == 0
    decays(gate_logits(0), e0_scr)

    def body(m, carry):
        x1 = gate_logits(2 * m + 1)
        trip(2 * m, e0_scr, lambda: decays(x1, e1_scr))
        x2 = gate_logits(jnp.minimum(2 * m + 2, n_trips - 1))
        trip(2 * m + 1, e1_scr, lambda: decays(x2, e0_scr))
        return carry

    lax.fori_loop(0, n_trips // 2, body, 0)


def _gla(pa, pb, ps, wlr, blr, gnorm, bsz, seq):
    t = bsz * seq
    mst = jnp.asarray(_gla_exponent_matrix(GLA_CHUNK), BF16)
    nq = GLA_QK // GLA_DK
    return pl.pallas_call(
        _gla_kernel,
        grid=(bsz, GLA_HEADS),
        in_specs=[
            pl.BlockSpec((seq, GLA_DK), lambda b, h: (b, h)),
            pl.BlockSpec((seq, GLA_DK), lambda b, h: (b, nq + h)),
            pl.BlockSpec((seq, GLA_DV), lambda b, h: (b, 2 * GLA_QK // GLA_DV + h)),
            pl.BlockSpec((seq, LANES), lambda b, h: (b, 0)),
            pl.BlockSpec((seq, GLA_DV), lambda b, h: (b, _PB_R // GLA_DV + h)),
            pl.BlockSpec((LANES, GLA_DK), lambda b, h: (0, h)),
            pl.BlockSpec((1, GLA_DK), lambda b, h: (0, h)),
            pl.BlockSpec((1, GLA_DV), lambda b, h: (0, 0)),
            pl.BlockSpec(mst.shape, lambda b, h: (0, 0)),
        ],
        out_specs=pl.BlockSpec((seq, GLA_DV), lambda b, h: (b, h)),
        out_shape=jax.ShapeDtypeStruct((t, GLA_V), BF16),
        scratch_shapes=[pltpu.VMEM((GLA_DV, GLA_DK), F32), pltpu.VMEM((mst.shape[0], 2 * GLA_DK), F32),
                        pltpu.VMEM((mst.shape[0], 2 * GLA_DK), F32)],
        compiler_params=_params(("parallel", "parallel")),
        name="gla",
    )(pa, pa, pa, ps, pb, wlr, blr, gnorm, mst)


def _gdn_kernel(q_ref, k_ref, v_ref, sm_ref, z_ref, gn_ref, o_ref, bms, ns, qms, os_, egl, s_scr):
    c = GDN_CHUNK
    seq = q_ref.shape[0]
    head0 = pl.program_id(1) * GDN_HPS

    ri = lax.broadcasted_iota(jnp.int32, (c, c), 0)
    ci = lax.broadcasted_iota(jnp.int32, (c, c), 1)
    lane = lax.broadcasted_iota(jnp.int32, (c, LANES), 1)
    tril = jnp.where(ri >= ci, 1.0, 0.0).astype(BF16)
    causal = ri >= ci
    n_sq = int(np.log2(c))
    lower_left = [(((ri ^ ci) >> (l + 1)) == 0) & ((ri & (1 << l)) != 0) & ((ci & (1 << l)) == 0) for l in range(n_sq)]

    grp = GDN_GROUP
    chains = [(g, j) for g in range(grp) for j in range(GDN_HPS)]

    def phase1(n, hooks):
        rows, q, k, kb, gb, rhs0 = {}, {}, {}, {}, {}, {}
        for g in range(grp):
            rows[g] = pl.ds(pl.multiple_of((n * grp + g) * c, c), c)
            sm = sm_ref[rows[g], :]
            for j in range(GDN_HPS):
                sl = slice(j * GDN_DK, (j + 1) * GDN_DK)
                q[g, j] = q_ref[rows[g], sl]
                k[g, j] = k_ref[rows[g], sl]
                gcol = jnp.sum(jnp.where(lane == SM_A + head0 + j, sm, 0.0), axis=-1, keepdims=True)
                beta = jnp.sum(jnp.where(lane == SM_B + head0 + j, sm, 0.0), axis=-1, keepdims=True)
                gb[g, j] = jnp.broadcast_to(gcol, (c, LANES))
                kb[g, j] = k[g, j] * beta
                rhs0[g, j] = v_ref[rows[g], sl] * beta
        cum_r, cum_c, kt = {}, {}, {}
        for ch in chains:
            hi, mid = _split2(gb[ch])
            r = _dot(tril, jnp.concatenate([hi, mid], axis=1))
            cum_r[ch] = r[:, :LANES] + r[:, LANES:]
        for ch in chains:
            cum_c[ch] = cum_r[ch].T
            kt[ch] = k[ch].T
        kk = {}
        for ch in chains:
            k16 = k[ch].astype(BF16)
            kk[ch] = _dot_nt(jnp.concatenate([kb[ch].astype(BF16), q[ch].astype(BF16)], axis=0), k16)
        low, at16 = {}, {}
        for ch in chains:
            dec = jnp.where(causal, jnp.exp(jnp.where(causal, cum_r[ch] - cum_c[ch], 0.0)), 0.0)
            low[ch] = jnp.where(ri > ci, kk[ch][:c] * dec, 0.0)
            at16[ch] = jnp.where(causal, kk[ch][c:] * dec, 0.0).astype(BF16)
        x = {ch: -jnp.where(lower_left[0], low[ch], 0.0) for ch in chains}
        for lvl in range(1, n_sq):
            e = {}
            for ch in chains:
                cm = jnp.where(lower_left[lvl], low[ch], 0.0)
                e[ch] = cm + _mm(x[ch], cm)
            for ch in chains:
                x[ch] = x[ch] - e[ch] - _mm(e[ch], x[ch])
            if lvl in hooks:
                hooks[lvl]()
        wu, egc = {}, {}
        for ch in chains:
            egc[ch] = jnp.exp(cum_r[ch])
            rhs = jnp.concatenate([kb[ch] * egc[ch], rhs0[ch]], axis=1)
            wu[ch] = (rhs + _mm(x[ch], rhs)).astype(BF16)
        for ch in chains:
            g, j = ch
            sl = slice(j * GDN_DK, (j + 1) * GDN_DK)
            glast = cum_r[ch][c - 1:c, :]
            kdt16 = (kt[ch] * jnp.exp(glast - cum_c[ch][0:1, :])).astype(BF16)
            r = _dot(jnp.concatenate([kdt16, at16[ch]], axis=0), wu[ch])
            bms[rows[g], sl] = (-r[:c, :GDN_DV]).astype(BF16)
            ns[rows[g], sl] = r[:c, GDN_DV:]
            qms[rows[g], sl] = (q[ch] * egc[ch] - r[c:, :GDN_DV]).astype(BF16)
            os_[rows[g], sl] = r[c:, GDN_DV:]
            egl[pl.ds(pl.multiple_of((n * grp + g) * 8, 8), 8), sl] = jnp.broadcast_to(jnp.exp(glast), (8, LANES))

    s_scr[...] = jnp.zeros_like(s_scr)

    def phase2(n):
        rows = pl.ds(pl.multiple_of(n * c, c), c)
        sls = [slice(j * GDN_DK, (j + 1) * GDN_DK) for j in range(GDN_HPS)]
        s = [s_scr[j] for j in range(GDN_HPS)]
        s16 = [sj.astype(BF16) for sj in s]
        r = [_dot(jnp.concatenate([bms[rows, sl], qms[rows, sl]], axis=0), s16[j]) for j, sl in enumerate(sls)]
        for j, sl in enumerate(sls):
            eg = egl[pl.ds(pl.multiple_of(n * 8, 8), 8), sl][0:1, :]
            s_scr[j] = s[j] * eg + r[j][:c] + ns[rows, sl]
            o = r[j][c:] + os_[rows, sl]
            on = _rms(o, gn_ref[...]) * _silu(z_ref[rows, sl].astype(F32))
            o_ref[rows, sl] = on.astype(o_ref.dtype)

    assert grp == 2 and n_sq >= 5
    n_trips = seq // (grp * c)
    phase1(0, {})

    def body(n, carry):
        first = (n - 1) * grp
        phase1(n, {2: lambda: phase2(first), 4: lambda: phase2(first + 1)})
        return carry

    lax.fori_loop(1, n_trips, body, 0)
    for g in range(grp):
        phase2((n_trips - 1) * grp + g)


def _gdn(pg, pb, ps, gnorm, bsz, seq):
    t = bsz * seq
    nb = seq // GDN_CHUNK
    wd = GDN_HPS * GDN_DK
    ng = GDN_HEADS // GDN_HPS
    return pl.pallas_call(
        _gdn_kernel,
        grid=(bsz, ng),
        in_specs=[
            pl.BlockSpec((seq, wd), lambda b, h: (b, h)),
            pl.BlockSpec((seq, wd), lambda b, h: (b, ng + h)),
            pl.BlockSpec((seq, wd), lambda b, h: (b, 2 * ng + h)),
            pl.BlockSpec((seq, LANES), lambda b, h: (b, 0)),
            pl.BlockSpec((seq, wd), lambda b, h: (b, _PB_Z // wd + h)),
            pl.BlockSpec((1, GDN_DV), lambda b, h: (0, 0)),
        ],
        out_specs=pl.BlockSpec((seq, wd), lambda b, h: (b, h)),
        out_shape=jax.ShapeDtypeStruct((t, GDN_V), BF16),
        scratch_shapes=[
            pltpu.VMEM((seq, wd), BF16), pltpu.VMEM((seq, wd), F32), pltpu.VMEM((seq, wd), BF16),
            pltpu.VMEM((seq, wd), F32), pltpu.VMEM((nb * 8, wd), F32),
            pltpu.VMEM((GDN_HPS, GDN_DK, GDN_DV), F32),
        ],
        compiler_params=_params(("parallel", "parallel")),
        name="gdn",
    )(pg, pg, pg, ps, pb, gnorm)


def _xa_kernel(q_ref, mk_ref, mv_ref, o_ref):
    s = _dot_nt(q_ref[...], mk_ref[...]) * (XA_DH ** -0.5)
    p = jnp.exp(s - jnp.max(s, axis=-1, keepdims=True))
    den = jnp.sum(p, axis=-1, keepdims=True)
    o_ref[...] = (_dot(p.astype(BF16), mv_ref[...]) / den).astype(o_ref.dtype)


def _xa(pb, mkv, bsz, seq, ts=2048):
    t = bsz * seq
    nt = seq // ts
    return pl.pallas_call(
        _xa_kernel,
        grid=(bsz, XA_HEADS, nt),
        in_specs=[
            pl.BlockSpec((ts, XA_DH), lambda b, h, i: (b * nt + i, _PB_XQ // XA_DH + h)),
            pl.BlockSpec((MEM_LEN, XA_DH), lambda b, h, i: (b, h)),
            pl.BlockSpec((MEM_LEN, XA_DH), lambda b, h, i: (b, XA_HEADS + h)),
        ],
        out_specs=pl.BlockSpec((ts, XA_DH), lambda b, h, i: (b * nt + i, h)),
        out_shape=jax.ShapeDtypeStruct((t, XA_W), BF16),
        compiler_params=_params(("parallel", "parallel", "parallel")),
        name="xattn",
    )(pb, mkv, mkv)


def _merge_kernel(h_ref, oa_ref, ob_ref, oc_ref, ga_ref, gb_ref, gc_ref, wa_hbm, wb_hbm, wc_hbm, wo_hbm,
                  gpost_ref, o_ref, wa_ref, wb_ref, wc_ref, wo_ref, stage, sem):
    @pl.when(pl.program_id(0) == 0)
    def _():
        rb = stage.shape[1]
        chunks = [(src, dst, r) for src, dst in ((wa_hbm, wa_ref), (wb_hbm, wb_ref), (wc_hbm, wc_ref), (wo_hbm, wo_ref))
                  for r in range(0, dst.shape[0], rb)]

        def chunk_copy(n):
            src, _, r = chunks[n]
            return pltpu.make_async_copy(src.at[pl.ds(r, rb), :], stage.at[n % 2], sem.at[n % 2])

        chunk_copy(0).start()
        for n, (_, dst, r) in enumerate(chunks):
            if n + 1 < len(chunks):
                chunk_copy(n + 1).start()
            chunk_copy(n).wait()
            dst[r:r + rb, :] = stage[n % 2].astype(BF16)

    mixed = jax.nn.sigmoid(ga_ref[...].astype(F32)) * _dot(oa_ref[...], wa_ref[...])
    mixed = mixed + jax.nn.sigmoid(gb_ref[...].astype(F32)) * _dot(ob_ref[...], wb_ref[...])
    mixed = mixed + jax.nn.sigmoid(gc_ref[...].astype(F32)) * _dot(oc_ref[...], wc_ref[...])
    m = _dot(mixed.astype(BF16), wo_ref[...])
    o_ref[...] = h_ref[...] + _rms(m, gpost_ref[...])


def _merge(h, oa, ob, oc, pb, wa, wb, wc, wo, gpost, tm=256, rb=256):
    t, d = h.shape
    row = lambda i: (i, 0)
    const = lambda i: (0, 0)
    hbm = pl.BlockSpec(memory_space=pl.ANY)
    return pl.pallas_call(
        _merge_kernel,
        grid=(t // tm,),
        in_specs=[
            pl.BlockSpec((tm, d), row),
            pl.BlockSpec((tm, GLA_V), row), pl.BlockSpec((tm, GDN_V), row), pl.BlockSpec((tm, XA_W), row),
            pl.BlockSpec((tm, d), lambda i: (i, 0)), pl.BlockSpec((tm, d), lambda i: (i, 1)),
            pl.BlockSpec((tm, d), lambda i: (i, 2)),
            hbm, hbm, hbm, hbm,
            pl.BlockSpec((1, d), const),
        ],
        out_specs=pl.BlockSpec((tm, d), row),
        out_shape=jax.ShapeDtypeStruct((t, d), F32),
        scratch_shapes=[pltpu.VMEM((GLA_V, d), BF16), pltpu.VMEM((GDN_V, d), BF16), pltpu.VMEM((XA_W, d), BF16),
                        pltpu.VMEM((d, d), BF16), pltpu.VMEM((2, rb, d), F32), pltpu.SemaphoreType.DMA((2,))],
        compiler_params=_params(("arbitrary",)),
        name="merge",
    )(h, oa, ob, oc, pb, pb, pb, wa, wb, wc, wo, gpost)


_PA_GDN = 2 * GLA_QK + GLA_V
_PA_COLS = _PA_GDN + 2 * GDN_QK + GDN_V
_PB_R = N_BRANCH * D_MODEL
_PB_Z = _PB_R + GLA_V
_PB_XQ = _PB_Z + GDN_V
_PB_COLS = _PB_XQ + XA_W


_W_ORDER = (0, 1, 2, 5, 10, 4, 8, 9)
_W_SMALL = ((_IN_OFF[3], GLA_RANK), (_IN_OFF[6], 2 * GDN_HEADS))
_W_ALL_COLS = _PA_COLS + _PB_COLS + LANES
_W_RB = 512


def _w_block_rows():
    tab = []
    for i in _W_ORDER:
        assert (_IN_OFF[i + 1] - _IN_OFF[i]) % _W_RB == 0
        tab.extend(range(_IN_OFF[i], _IN_OFF[i + 1], _W_RB))
    return np.asarray(tab, np.int32)


def _relayout_kernel(tab_ref, wt_hbm, o_ref, buf, sem):
    j = pl.program_id(0)
    n_main = pl.num_programs(0) - 1

    def block_copy(blk):
        start = pl.multiple_of(tab_ref[blk], 16)
        return pltpu.make_async_copy(wt_hbm.at[pl.ds(start, _W_RB), :], buf.at[blk % 2], sem.at[blk % 2])

    @pl.when(j == 0)
    def _():
        block_copy(j).start()

    @pl.when(j + 1 < n_main)
    def _():
        block_copy(j + 1).start()

    @pl.when(j < n_main)
    def _():
        block_copy(j).wait()
        o_ref[...] = buf[j % 2].astype(BF16)

    @pl.when(j == n_main)
    def _():
        dst = 0
        copies = []
        for n, (row0, rows) in enumerate(_W_SMALL):
            copies.append(pltpu.make_async_copy(wt_hbm.at[pl.ds(row0, rows), :], buf.at[0, pl.ds(dst, rows), :],
                                                sem.at[n]))
            dst += rows
        for cp in copies:
            cp.start()
        for cp in copies:
            cp.wait()
        o_ref[0:dst, :] = buf[0, 0:dst, :].astype(BF16)
        o_ref[dst:, :] = jnp.zeros((_W_RB - dst, o_ref.shape[1]), BF16)


def _split_w_in(w_in_t):
    n, k = w_in_t.shape
    tab = _w_block_rows()
    assert len(_W_SMALL) <= 2
    return pl.pallas_call(
        _relayout_kernel,
        grid_spec=pltpu.PrefetchScalarGridSpec(
            num_scalar_prefetch=1,
            grid=(len(tab) + 1,),
            in_specs=[pl.BlockSpec(memory_space=pl.ANY)],
            out_specs=pl.BlockSpec((_W_RB, k), lambda i, tab: (i, 0)),
            scratch_shapes=[pltpu.VMEM((2, _W_RB, k), F32), pltpu.SemaphoreType.DMA((2,))],
        ),
        out_shape=jax.ShapeDtypeStruct((_W_ALL_COLS, k), BF16),
        compiler_params=_params(("arbitrary",)),
        name="w_in_relayout",
    )(jnp.asarray(tab), w_in_t)


def kernel(x, mem, n_ffn1_pre, w_ffn1_gu, w_ffn1_down, n_ffn1_post, n_mix_pre, w_in, gla_w_lr2, gla_b_lr,
           gla_norm, gdn_conv, gdn_a_log, gdn_dt_bias, gdn_norm, mem_norm, w_mem_kv, w_up_gla, w_up_gdn,
           w_up_xa, w_out, n_mix_post, n_ffn2_pre, w_ffn2_gu, w_ffn2_down, n_ffn2_post):
    bsz, seq, d = x.shape
    t = bsz * seq
    h = x.reshape(t, d)
    for l in range(n_ffn1_pre.shape[0]):
        row = lambda a: a[l][None, :]
        h, u = _ffn(h, row(n_ffn1_pre), w_ffn1_gu[l], w_ffn1_down[l],
                    row(n_ffn1_post), row(n_mix_pre), emit_next=True)

        w_all = _split_w_in(w_in[l].T)
        par = jnp.zeros((2, LANES), F32)
        par = par.at[0, SM_A:SM_A + GDN_HEADS].set(gdn_a_log[l]).at[1, SM_A:SM_A + GDN_HEADS].set(gdn_dt_bias[l])
        tn_g = 512
        pa = _matmul(u, w_all, 0, _PA_GDN, F32, 2048, 1024, "in_proj_a")
        pg = _matmul(u, w_all, _PA_GDN, _PA_COLS - _PA_GDN, F32, seq, tn_g, "in_proj_g", body=_proj_conv_kernel,
                     extra=(gdn_conv[l],), extra_specs=(pl.BlockSpec((CONV_W, tn_g), lambda i, j: (0, j)),))
        pb, ps = _proj_b(u, w_all, par)

        wlr = jnp.pad(gla_w_lr2[l], ((0, LANES - GLA_RANK), (0, 0))).astype(BF16)
        o_gla = _gla(pa, pb, ps, wlr, row(gla_b_lr), row(gla_norm), bsz, seq)
        o_gdn = _gdn(pg, pb, ps, row(gdn_norm), bsz, seq)

        mkv = _norm_matmul(mem.reshape(bsz * MEM_LEN, d), row(mem_norm), w_mem_kv[l], BF16,
                           bsz * MEM_LEN, 512, "mem_kv")
        o_xa = _xa(pb, mkv, bsz, seq)

        h = _merge(h, o_gla, o_gdn, o_xa, pb, w_up_gla[l], w_up_gdn[l], w_up_xa[l], w_out[l], row(n_mix_post))
        h = _ffn(h, row(n_ffn2_pre), w_ffn2_gu[l], w_ffn2_down[l],
                 row(n_ffn2_post), row(n_ffn2_post), emit_next=False)
    return h.reshape(bsz, seq, d)
```

```python
import functools

import numpy as np
import jax
import jax.numpy as jnp
from jax import lax
from jax.experimental import pallas as pl
from jax.experimental.pallas import tpu as pltpu

F32 = jnp.float32
BF16 = jnp.bfloat16

D_MODEL = 2048
MEM_LEN = 256
EPS = 1e-6
GLA_HEADS = 4
GLA_DK = 128
GLA_DV = 256
GLA_QK = GLA_HEADS * GLA_DK
GLA_V = GLA_HEADS * GLA_DV
GLA_RANK = 16
GLA_TAU = 16.0
GDN_HEADS = 8
GDN_DK = 128
GDN_DV = 128
GDN_QK = GDN_HEADS * GDN_DK
GDN_V = GDN_HEADS * GDN_DV
CONV_W = 4
XA_HEADS = 4
XA_DH = 256
XA_W = XA_HEADS * XA_DH
N_BRANCH = 3
D_FF = 5632

LANES = 128
GLA_CHUNK = 64
GLA_GROUP = 4
GDN_CHUNK = 128
GDN_HPS = 4
GDN_GROUP = 2
VMEM_LIMIT = 56 * 1024 * 1024

_IN_SIZES = (GLA_QK, GLA_QK, GLA_V, GLA_RANK, GLA_V, 2 * GDN_QK + GDN_V, GDN_HEADS, GDN_HEADS, GDN_V,
             XA_W, N_BRANCH * D_MODEL)
_IN_OFF = tuple(int(v) for v in np.cumsum((0,) + _IN_SIZES))
SM_LR, SM_B, SM_A = 0, GLA_RANK, GLA_RANK + GDN_HEADS


def _dot(a, b):
    return jnp.dot(a, b, preferred_element_type=F32)


def _dot_nt(a, b):
    return lax.dot_general(a, b, (((1,), (1,)), ((), ())), preferred_element_type=F32)


def _dot_tn(a, b):
    return lax.dot_general(a, b, (((0,), (0,)), ((), ())), preferred_element_type=F32)


def _mm(a, b):
    return _dot(a.astype(BF16), b.astype(BF16))


def _split2(x):
    hi = x.astype(BF16)
    mid = (x - hi.astype(F32)).astype(BF16)
    return hi, mid


def _rms(x, gain):
    return x * lax.rsqrt(jnp.mean(x * x, axis=-1, keepdims=True) + EPS) * gain


def _silu(x):
    return x * jax.nn.sigmoid(x)


def _softplus(x):
    return jnp.maximum(x, 0.0) + jnp.log1p(jnp.exp(-jnp.abs(x)))


def _params(sem):
    return pltpu.CompilerParams(dimension_semantics=sem, vmem_limit_bytes=VMEM_LIMIT)


def _ffn_kernel(x_ref, gpre_ref, wgu_hbm, wd_hbm, gpost_ref, gnext_ref, *out_and_scratch, emit_next, tf):
    if emit_next:
        h_ref, un_ref, xn_scr, wg_buf, wu_buf, wd_buf, sem = out_and_scratch
    else:
        h_ref, xn_scr, wg_buf, wu_buf, wd_buf, sem = out_and_scratch
    i = pl.program_id(0)
    nf = D_FF // tf
    assert nf % 2 == 0

    def tile_copies(jj, slot):
        col = pl.multiple_of(jj * tf, tf)
        return (pltpu.make_async_copy(wgu_hbm.at[:, pl.ds(col, tf)], wg_buf.at[slot], sem.at[0, slot]),
                pltpu.make_async_copy(wgu_hbm.at[:, pl.ds(D_FF + col, tf)], wu_buf.at[slot], sem.at[1, slot]),
                pltpu.make_async_copy(wd_hbm.at[pl.ds(col, tf), :], wd_buf.at[slot], sem.at[2, slot]))

    @pl.when(i == 0)
    def _():
        for cp in tile_copies(0, 0):
            cp.start()

    xn_scr[...] = _rms(x_ref[...], gpre_ref[...]).astype(BF16)
    h_ref[...] = jnp.zeros_like(h_ref)

    def body(jj, carry):
        slot = jj % 2
        @pl.when(jj + 1 < nf)
        def _():
            for cp in tile_copies(jj + 1, 1 - slot):
                cp.start()

        @pl.when((jj + 1 == nf) & (i + 1 < pl.num_programs(0)))
        def _():
            for cp in tile_copies(0, 0):
                cp.start()

        for cp in tile_copies(jj, slot):
            cp.wait()
        xn = xn_scr[...].astype(F32)
        g = _dot(xn, wg_buf[slot])
        u = _dot(xn, wu_buf[slot])
        hm = _silu(g) * u
        h_ref[...] += _dot(hm, wd_buf[slot])
        return carry

    lax.fori_loop(0, nf, body, 0)

    h = x_ref[...] + 0.5 * _rms(h_ref[...], gpost_ref[...])
    h_ref[...] = h
    if emit_next:
        un_ref[...] = _rms(h, gnext_ref[...]).astype(BF16)


def _ffn(x, gpre, w_gu, w_down, gpost, gnext, emit_next, tm=1024, tf=256):
    t, d = x.shape
    row = lambda i: (i, 0)
    const = lambda i: (0, 0)
    once = pl.Buffered(1)
    out_shape = [jax.ShapeDtypeStruct((t, d), F32)]
    out_specs = [pl.BlockSpec((tm, d), row, pipeline_mode=once)]
    if emit_next:
        out_shape.append(jax.ShapeDtypeStruct((t, d), BF16))
        out_specs.append(pl.BlockSpec((tm, d), row, pipeline_mode=once))
    res = pl.pallas_call(
        functools.partial(_ffn_kernel, emit_next=emit_next, tf=tf),
        grid=(t // tm,),
        in_specs=[
            pl.BlockSpec((tm, d), row),
            pl.BlockSpec((1, d), const),
            pl.BlockSpec(memory_space=pl.ANY),
            pl.BlockSpec(memory_space=pl.ANY),
            pl.BlockSpec((1, d), const),
            pl.BlockSpec((1, d), const),
        ],
        out_specs=out_specs,
        out_shape=out_shape,
        scratch_shapes=[pltpu.VMEM((tm, d), BF16),
                        pltpu.VMEM((2, d, tf), F32), pltpu.VMEM((2, d, tf), F32), pltpu.VMEM((2, tf, d), F32),
                        pltpu.SemaphoreType.DMA((3, 2))],
        compiler_params=_params(("arbitrary",)),
        name="ffn",
    )(x, gpre, w_gu, w_down, gpost, gnext)
    return res if emit_next else res[0]


def _matmul_kernel(a_ref, w_ref, o_ref):
    o_ref[...] = _dot_nt(a_ref[...], w_ref[...]).astype(o_ref.dtype)


def _proj_conv_kernel(a_ref, w_ref, cw_ref, o_ref):
    tm, tn = o_ref.shape
    rb = 256
    kind = pl.program_id(1) // (GDN_QK // tn)
    scale = jnp.where(kind == 0, GDN_DK ** -0.5, 1.0)
    w = cw_ref[...]
    r8 = lax.broadcasted_iota(jnp.int32, (8, tn), 0)

    sb = 32

    def finish(r, before):
        for s0 in range(0, rb, sb):
            y = o_ref[r + s0:r + s0 + sb, :]
            acc = y * w[CONV_W - 1:CONV_W, :]
            for sft in range(1, CONV_W):
                xs = pltpu.roll(y, sft, 0)
                top = jnp.where(r8 < sft, pltpu.roll(before, sft, 0), xs[0:8])
                acc = acc + jnp.concatenate([top, xs[8:]], axis=0) * w[CONV_W - 1 - sft:CONV_W - sft, :]
            before = y[sb - 8:sb]
            c = _silu(acc)
            for hh in range(tn // GDN_DK):
                sl = slice(hh * GDN_DK, (hh + 1) * GDN_DK)
                blk = c[:, sl]
                f = lax.rsqrt(jnp.sum(blk * blk, axis=-1, keepdims=True) + EPS) * scale
                o_ref[r + s0:r + s0 + sb, sl] = blk * jnp.where(kind == 2, 1.0, f)
        return before

    o_ref[0:rb, :] = _dot_nt(a_ref[0:rb, :], w_ref[...])
    before = jnp.zeros((8, tn), F32)
    for r in range(rb, tm + rb, rb):
        if r < tm:
            o_ref[r:r + rb, :] = _dot_nt(a_ref[r:r + rb, :], w_ref[...])
        before = finish(r - rb, before)


def _proj_b_kernel(a_ref, w_ref, ws_ref, par_ref, o_ref, os_ref):
    o_ref[...] = _dot_nt(a_ref[...], w_ref[...]).astype(o_ref.dtype)

    @pl.when(pl.program_id(1) == 0)
    def _():
        y = _dot_nt(a_ref[...], ws_ref[...])
        par = par_ref[...]
        lane = lax.broadcasted_iota(jnp.int32, y.shape, 1)
        g = -jnp.exp(par[0:1, :]) * _softplus(y + par[1:2, :])
        is_g = (lane >= SM_A) & (lane < SM_A + GDN_HEADS)
        is_b = (lane >= SM_B) & (lane < SM_B + GDN_HEADS)
        os_ref[...] = jnp.where(is_g, g, jnp.where(is_b, jax.nn.sigmoid(y), y))


def _proj_b(a, w, par, tm=2048, tn=1024):
    m, k = a.shape
    c0 = _PA_COLS // tn
    cs = (_PA_COLS + _PB_COLS) // LANES
    return pl.pallas_call(
        _proj_b_kernel,
        grid=(m // tm, _PB_COLS // tn),
        in_specs=[pl.BlockSpec((tm, k), lambda i, j: (i, 0)), pl.BlockSpec((tn, k), lambda i, j: (c0 + j, 0)),
                  pl.BlockSpec((LANES, k), lambda i, j: (cs, 0)), pl.BlockSpec((2, LANES), lambda i, j: (0, 0))],
        out_specs=[pl.BlockSpec((tm, tn), lambda i, j: (i, j)), pl.BlockSpec((tm, LANES), lambda i, j: (i, 0))],
        out_shape=[jax.ShapeDtypeStruct((m, _PB_COLS), BF16), jax.ShapeDtypeStruct((m, LANES), F32)],
        compiler_params=_params(("parallel", "arbitrary")),
        name="in_proj_b",
    )(a, w, w, par)


def _matmul(a, w, col0, n, out_dtype, tm, tn, name, body=_matmul_kernel, extra=(), extra_specs=()):
    m, k = a.shape
    c0 = col0 // tn
    return pl.pallas_call(
        body,
        grid=(m // tm, n // tn),
        in_specs=[pl.BlockSpec((tm, k), lambda i, j: (i, 0)), pl.BlockSpec((tn, k), lambda i, j: (c0 + j, 0)),
                  *extra_specs],
        out_specs=pl.BlockSpec((tm, tn), lambda i, j: (i, j)),
        out_shape=jax.ShapeDtypeStruct((m, n), out_dtype),
        compiler_params=_params(("parallel", "parallel")),
        name=name,
    )(a, w, *extra)


def _norm_matmul_kernel(x_ref, g_ref, w_ref, o_ref, xn_scr):
    @pl.when(pl.program_id(1) == 0)
    def _():
        xn_scr[...] = _rms(x_ref[...], g_ref[...]).astype(BF16)

    o_ref[...] = _dot(xn_scr[...], w_ref[...].astype(BF16)).astype(o_ref.dtype)


def _norm_matmul(x, gain, w, out_dtype, tm, tn, name):
    m, k = x.shape
    n = w.shape[1]
    return pl.pallas_call(
        _norm_matmul_kernel,
        grid=(m // tm, n // tn),
        in_specs=[pl.BlockSpec((tm, k), lambda i, j: (i, 0)), pl.BlockSpec((1, k), lambda i, j: (0, 0)),
                  pl.BlockSpec((k, tn), lambda i, j: (0, j))],
        out_specs=pl.BlockSpec((tm, tn), lambda i, j: (i, j)),
        out_shape=jax.ShapeDtypeStruct((m, n), out_dtype),
        scratch_shapes=[pltpu.VMEM((tm, k), BF16)],
        compiler_params=_params(("parallel", "arbitrary")),
        name=name,
    )(x, gain, w)


def _gla_levels(chunk):
    return [chunk >> (i + 1) for i in range(int(np.log2(chunk)))]


def _gla_exponent_matrix(chunk):
    c = chunk
    i = np.arange(c)[:, None]
    t = np.arange(c)[None, :]
    blocks = [(t <= i), (t > i)]
    for h in _gla_levels(c):
        r = (i // (2 * h)) * (2 * h) + h
        upper = i >= r
        blocks.append(np.where(upper, (t > r) & (t <= i), (t > i) & (t <= r)))
    m = np.concatenate(blocks, axis=0).astype(np.float32)
    z = np.zeros_like(m)
    return np.concatenate([np.concatenate([m, m, z, z], axis=1), np.concatenate([z, z, m, m], axis=1)], axis=0)


def _gla_kernel(q_ref, k_ref, v_ref, sm_ref, r_ref, wlr_ref, blr_ref, gn_ref, mst_ref, o_ref, s_scr, e0_scr,
                e1_scr):
    c = GLA_CHUNK
    seq = q_ref.shape[0]
    s_scr[...] = jnp.zeros_like(s_scr)
    ri = lax.broadcasted_iota(jnp.int32, (c, c), 0)
    ci = lax.broadcasted_iota(jnp.int32, (c, c), 1)
    rowi = lax.broadcasted_iota(jnp.int32, (c, GLA_DK), 0)
    levels = _gla_levels(c)
    assert GLA_GROUP == 4

    grp = GLA_GROUP
    gs = range(grp)

    n_trips = seq // (grp * c)

    def gate_logits(n):
        allrows = pl.ds(pl.multiple_of(n * (grp * c), grp * c), grp * c)
        return _dot(sm_ref[allrows, :].astype(BF16), wlr_ref[...]) + blr_ref[...]

    def decays(x, e_ref):
        lg = (jnp.minimum(x, 0.0) - jnp.log1p(jnp.exp(-jnp.abs(x)))) * (1.0 / GLA_TAU)
        hi, mid = _split2(lg)
        hm = [jnp.concatenate([hi[g * c:(g + 1) * c], mid[g * c:(g + 1) * c]], axis=0) for g in gs]
        rhs = jnp.concatenate([jnp.concatenate([hm[0], hm[1]], axis=1),
                               jnp.concatenate([hm[2], hm[3]], axis=1)], axis=0)
        e_ref[...] = jnp.exp(_dot(mst_ref[...], rhs))

    def trip(n, e_ref, between):
        row0 = pl.multiple_of(n * (grp * c), grp * c)
        rows = [pl.ds(row0 + g * c, c) for g in gs]
        ex = e_ref[...]
        nr = ex.shape[0] // 2
        e = [ex[0:nr, 0:GLA_DK], ex[0:nr, GLA_DK:], ex[nr:, 0:GLA_DK], ex[nr:, GLA_DK:]]
        q = [q_ref[rows[g], :] * (GLA_DK ** -0.5) for g in gs]
        k = [k_ref[rows[g], :] for g in gs]
        v = [v_ref[rows[g], :].astype(BF16) for g in gs]
        attn = [jnp.where(ri == ci, _dot_nt(q[g].astype(BF16), k[g].astype(BF16)), 0.0) for g in gs]
        for lvl, h in enumerate(levels):
            upper = (rowi & h) != 0
            sh = int(np.log2(2 * h))
            same = (ri >> sh) == (ci >> sh)
            for g in gs:
                f = e[g][(2 + lvl) * c:(3 + lvl) * c]
                qt = jnp.where(upper, q[g] * f, 0.0).astype(BF16)
                kt = jnp.where(upper, 0.0, k[g] * f).astype(BF16)
                attn[g] = attn[g] + jnp.where(same, _dot_nt(qt, kt), 0.0)
        vk = [_dot_tn(v[g], (k[g] * e[g][c:2 * c]).astype(BF16)) for g in gs]
        between()
        av = [_dot(attn[g].astype(BF16), v[g]) for g in gs]
        st = s_scr[...]
        for g in gs:
            o = _dot_nt((q[g] * e[g][0:c]).astype(BF16), st.astype(BF16)) + av[g]
            st = st * e[g][c - 1:c, :] + vk[g]
            on = _rms(o, gn_ref[...]) * _silu(r_ref[rows[g], :].astype(F32))
            o_ref[rows[g], :] = on.astype(o_ref.dtype)
        s_scr[...] = st

    assert n_trips % 2 == 0
    decays(gate_logits(0), e0_scr)

    def body(m, carry):
        x1 = gate_logits(2 * m + 1)
        trip(2 * m, e0_scr, lambda: decays(x1, e1_scr))
        x2 = gate_logits(jnp.minimum(2 * m + 2, n_trips - 1))
        trip(2 * m + 1, e1_scr, lambda: decays(x2, e0_scr))
        return carry

    lax.fori_loop(0, n_trips // 2, body, 0)


def _gla(pa, pb, ps, wlr, blr, gnorm, bsz, seq):
    t = bsz * seq
    mst = jnp.asarray(_gla_exponent_matrix(GLA_CHUNK), BF16)
    nq = GLA_QK // GLA_DK
    return pl.pallas_call(
        _gla_kernel,
        grid=(bsz, GLA_HEADS),
        in_specs=[
            pl.BlockSpec((seq, GLA_DK), lambda b, h: (b, h)),
            pl.BlockSpec((seq, GLA_DK), lambda b, h: (b, nq + h)),
            pl.BlockSpec((seq, GLA_DV), lambda b, h: (b, 2 * GLA_QK // GLA_DV + h)),
            pl.BlockSpec((seq, LANES), lambda b, h: (b, 0)),
            pl.BlockSpec((seq, GLA_DV), lambda b, h: (b, _PB_R // GLA_DV + h)),
            pl.BlockSpec((LANES, GLA_DK), lambda b, h: (0, h)),
            pl.BlockSpec((1, GLA_DK), lambda b, h: (0, h)),
            pl.BlockSpec((1, GLA_DV), lambda b, h: (0, 0)),
            pl.BlockSpec(mst.shape, lambda b, h: (0, 0)),
        ],
        out_specs=pl.BlockSpec((seq, GLA_DV), lambda b, h: (b, h)),
        out_shape=jax.ShapeDtypeStruct((t, GLA_V), BF16),
        scratch_shapes=[pltpu.VMEM((GLA_DV, GLA_DK), F32), pltpu.VMEM((mst.shape[0], 2 * GLA_DK), F32),
                        pltpu.VMEM((mst.shape[0], 2 * GLA_DK), F32)],
        compiler_params=_params(("parallel", "parallel")),
        name="gla",
    )(pa, pa, pa, ps, pb, wlr, blr, gnorm, mst)


def _gdn_kernel(q_ref, k_ref, v_ref, sm_ref, z_ref, gn_ref, o_ref, bms, ns, qms, os_, egl, s_scr):
    c = GDN_CHUNK
    seq = q_ref.shape[0]
    head0 = pl.program_id(1) * GDN_HPS

    ri = lax.broadcasted_iota(jnp.int32, (c, c), 0)
    ci = lax.broadcasted_iota(jnp.int32, (c, c), 1)
    lane = lax.broadcasted_iota(jnp.int32, (c, LANES), 1)
    tril = jnp.where(ri >= ci, 1.0, 0.0).astype(BF16)
    causal = ri >= ci
    n_sq = int(np.log2(c))
    lower_left = [(((ri ^ ci) >> (l + 1)) == 0) & ((ri & (1 << l)) != 0) & ((ci & (1 << l)) == 0) for l in range(n_sq)]

    grp = GDN_GROUP
    chains = [(g, j) for g in range(grp) for j in range(GDN_HPS)]

    def phase1(n, hooks):
        rows, q, k, kb, gb, rhs0 = {}, {}, {}, {}, {}, {}
        for g in range(grp):
            rows[g] = pl.ds(pl.multiple_of((n * grp + g) * c, c), c)
            sm = sm_ref[rows[g], :]
            for j in range(GDN_HPS):
                sl = slice(j * GDN_DK, (j + 1) * GDN_DK)
                q[g, j] = q_ref[rows[g], sl]
                k[g, j] = k_ref[rows[g], sl]
                gcol = jnp.sum(jnp.where(lane == SM_A + head0 + j, sm, 0.0), axis=-1, keepdims=True)
                beta = jnp.sum(jnp.where(lane == SM_B + head0 + j, sm, 0.0), axis=-1, keepdims=True)
                gb[g, j] = jnp.broadcast_to(gcol, (c, LANES))
                kb[g, j] = k[g, j] * beta
                rhs0[g, j] = v_ref[rows[g], sl] * beta
        cum_r, cum_c, kt = {}, {}, {}
        for ch in chains:
            hi, mid = _split2(gb[ch])
            r = _dot(tril, jnp.concatenate([hi, mid], axis=1))
            cum_r[ch] = r[:, :LANES] + r[:, LANES:]
        for ch in chains:
            cum_c[ch] = cum_r[ch].T
            kt[ch] = k[ch].T
        kk = {}
        for ch in chains:
            k16 = k[ch].astype(BF16)
            kk[ch] = _dot_nt(jnp.concatenate([kb[ch].astype(BF16), q[ch].astype(BF16)], axis=0), k16)
        low, at16 = {}, {}
        for ch in chains:
            dec = jnp.where(causal, jnp.exp(jnp.where(causal, cum_r[ch] - cum_c[ch], 0.0)), 0.0)
            low[ch] = jnp.where(ri > ci, kk[ch][:c] * dec, 0.0)
            at16[ch] = jnp.where(causal, kk[ch][c:] * dec, 0.0).astype(BF16)
        x = {ch: -jnp.where(lower_left[0], low[ch], 0.0) for ch in chains}
        for lvl in range(1, n_sq):
            e = {}
            for ch in chains:
                cm = jnp.where(lower_left[lvl], low[ch], 0.0)
                e[ch] = cm + _mm(x[ch], cm)
            for ch in chains:
                x[ch] = x[ch] - e[ch] - _mm(e[ch], x[ch])
            if lvl in hooks:
                hooks[lvl]()
        wu, egc = {}, {}
        for ch in chains:
            egc[ch] = jnp.exp(cum_r[ch])
            rhs = jnp.concatenate([kb[ch] * egc[ch], rhs0[ch]], axis=1)
            wu[ch] = (rhs + _mm(x[ch], rhs)).astype(BF16)
        for ch in chains:
            g, j = ch
            sl = slice(j * GDN_DK, (j + 1) * GDN_DK)
            glast = cum_r[ch][c - 1:c, :]
            kdt16 = (kt[ch] * jnp.exp(glast - cum_c[ch][0:1, :])).astype(BF16)
            r = _dot(jnp.concatenate([kdt16, at16[ch]], axis=0), wu[ch])
            bms[rows[g], sl] = (-r[:c, :GDN_DV]).astype(BF16)
            ns[rows[g], sl] = r[:c, GDN_DV:]
            qms[rows[g], sl] = (q[ch] * egc[ch] - r[c:, :GDN_DV]).astype(BF16)
            os_[rows[g], sl] = r[c:, GDN_DV:]
            egl[pl.ds(pl.multiple_of((n * grp + g) * 8, 8), 8), sl] = jnp.broadcast_to(jnp.exp(glast), (8, LANES))

    s_scr[...] = jnp.zeros_like(s_scr)

    def phase2(n):
        rows = pl.ds(pl.multiple_of(n * c, c), c)
        sls = [slice(j * GDN_DK, (j + 1) * GDN_DK) for j in range(GDN_HPS)]
        s = [s_scr[j] for j in range(GDN_HPS)]
        s16 = [sj.astype(BF16) for sj in s]
        r = [_dot(jnp.concatenate([bms[rows, sl], qms[rows, sl]], axis=0), s16[j]) for j, sl in enumerate(sls)]
        for j, sl in enumerate(sls):
            eg = egl[pl.ds(pl.multiple_of(n * 8, 8), 8), sl][0:1, :]
            s_scr[j] = s[j] * eg + r[j][:c] + ns[rows, sl]
            o = r[j][c:] + os_[rows, sl]
            on = _rms(o, gn_ref[...]) * _silu(z_ref[rows, sl].astype(F32))
            o_ref[rows, sl] = on.astype(o_ref.dtype)

    assert grp == 2 and n_sq >= 5
    n_trips = seq // (grp * c)
    phase1(0, {})

    def body(n, carry):
        first = (n - 1) * grp
        phase1(n, {2: lambda: phase2(first), 4: lambda: phase2(first + 1)})
        return carry

    lax.fori_loop(1, n_trips, body, 0)
    for g in range(grp):
        phase2((n_trips - 1) * grp + g)


def _gdn(pg, pb, ps, gnorm, bsz, seq):
    t = bsz * seq
    nb = seq // GDN_CHUNK
    wd = GDN_HPS * GDN_DK
    ng = GDN_HEADS // GDN_HPS
    return pl.pallas_call(
        _gdn_kernel,
        grid=(bsz, ng),
        in_specs=[
            pl.BlockSpec((seq, wd), lambda b, h: (b, h)),
            pl.BlockSpec((seq, wd), lambda b, h: (b, ng + h)),
            pl.BlockSpec((seq, wd), lambda b, h: (b, 2 * ng + h)),
            pl.BlockSpec((seq, LANES), lambda b, h: (b, 0)),
            pl.BlockSpec((seq, wd), lambda b, h: (b, _PB_Z // wd + h)),
            pl.BlockSpec((1, GDN_DV), lambda b, h: (0, 0)),
        ],
        out_specs=pl.BlockSpec((seq, wd), lambda b, h: (b, h)),
        out_shape=jax.ShapeDtypeStruct((t, GDN_V), BF16),
        scratch_shapes=[
            pltpu.VMEM((seq, wd), BF16), pltpu.VMEM((seq, wd), F32), pltpu.VMEM((seq, wd), BF16),
            pltpu.VMEM((seq, wd), F32), pltpu.VMEM((nb * 8, wd), F32),
            pltpu.VMEM((GDN_HPS, GDN_DK, GDN_DV), F32),
        ],
        compiler_params=_params(("parallel", "parallel")),
        name="gdn",
    )(pg, pg, pg, ps, pb, gnorm)


def _xa_kernel(q_ref, mk_ref, mv_ref, o_ref):
    s = _dot_nt(q_ref[...], mk_ref[...]) * (XA_DH ** -0.5)
    p = jnp.exp(s - jnp.max(s, axis=-1, keepdims=True))
    den = jnp.sum(p, axis=-1, keepdims=True)
    o_ref[...] = (_dot(p.astype(BF16), mv_ref[...]) / den).astype(o_ref.dtype)


def _xa(pb, mkv, bsz, seq, ts=2048):
    t = bsz * seq
    nt = seq // ts
    return pl.pallas_call(
        _xa_kernel,
        grid=(bsz, XA_HEADS, nt),
        in_specs=[
            pl.BlockSpec((ts, XA_DH), lambda b, h, i: (b * nt + i, _PB_XQ // XA_DH + h)),
            pl.BlockSpec((MEM_LEN, XA_DH), lambda b, h, i: (b, h)),
            pl.BlockSpec((MEM_LEN, XA_DH), lambda b, h, i: (b, XA_HEADS + h)),
        ],
        out_specs=pl.BlockSpec((ts, XA_DH), lambda b, h, i: (b * nt + i, h)),
        out_shape=jax.ShapeDtypeStruct((t, XA_W), BF16),
        compiler_params=_params(("parallel", "parallel", "parallel")),
        name="xattn",
    )(pb, mkv, mkv)


def _merge_kernel(h_ref, oa_ref, ob_ref, oc_ref, ga_ref, gb_ref, gc_ref, wa_hbm, wb_hbm, wc_hbm, wo_hbm,
                  gpost_ref, o_ref, wa_ref, wb_ref, wc_ref, wo_ref, stage, sem):
    @pl.when(pl.program_id(0) == 0)
    def _():
        rb = stage.shape[1]
        chunks = [(src, dst, r) for src, dst in ((wa_hbm, wa_ref), (wb_hbm, wb_ref), (wc_hbm, wc_ref), (wo_hbm, wo_ref))
                  for r in range(0, dst.shape[0], rb)]

        def chunk_copy(n):
            src, _, r = chunks[n]
            return pltpu.make_async_copy(src.at[pl.ds(r, rb), :], stage.at[n % 2], sem.at[n % 2])

        chunk_copy(0).start()
        for n, (_, dst, r) in enumerate(chunks):
            if n + 1 < len(chunks):
                chunk_copy(n + 1).start()
            chunk_copy(n).wait()
            dst[r:r + rb, :] = stage[n % 2].astype(BF16)

    mixed = jax.nn.sigmoid(ga_ref[...].astype(F32)) * _dot(oa_ref[...], wa_ref[...])
    mixed = mixed + jax.nn.sigmoid(gb_ref[...].astype(F32)) * _dot(ob_ref[...], wb_ref[...])
    mixed = mixed + jax.nn.sigmoid(gc_ref[...].astype(F32)) * _dot(oc_ref[...], wc_ref[...])
    m = _dot(mixed.astype(BF16), wo_ref[...])
    o_ref[...] = h_ref[...] + _rms(m, gpost_ref[...])


def _merge(h, oa, ob, oc, pb, wa, wb, wc, wo, gpost, tm=256, rb=256):
    t, d = h.shape
    row = lambda i: (i, 0)
    const = lambda i: (0, 0)
    hbm = pl.BlockSpec(memory_space=pl.ANY)
    return pl.pallas_call(
        _merge_kernel,
        grid=(t // tm,),
        in_specs=[
            pl.BlockSpec((tm, d), row),
            pl.BlockSpec((tm, GLA_V), row), pl.BlockSpec((tm, GDN_V), row), pl.BlockSpec((tm, XA_W), row),
            pl.BlockSpec((tm, d), lambda i: (i, 0)), pl.BlockSpec((tm, d), lambda i: (i, 1)),
            pl.BlockSpec((tm, d), lambda i: (i, 2)),
            hbm, hbm, hbm, hbm,
            pl.BlockSpec((1, d), const),
        ],
        out_specs=pl.BlockSpec((tm, d), row),
        out_shape=jax.ShapeDtypeStruct((t, d), F32),
        scratch_shapes=[pltpu.VMEM((GLA_V, d), BF16), pltpu.VMEM((GDN_V, d), BF16), pltpu.VMEM((XA_W, d), BF16),
                        pltpu.VMEM((d, d), BF16), pltpu.VMEM((2, rb, d), F32), pltpu.SemaphoreType.DMA((2,))],
        compiler_params=_params(("arbitrary",)),
        name="merge",
    )(h, oa, ob, oc, pb, pb, pb, wa, wb, wc, wo, gpost)


_PA_GDN = 2 * GLA_QK + GLA_V
_PA_COLS = _PA_GDN + 2 * GDN_QK + GDN_V
_PB_R = N_BRANCH * D_MODEL
_PB_Z = _PB_R + GLA_V
_PB_XQ = _PB_Z + GDN_V
_PB_COLS = _PB_XQ + XA_W


_W_ORDER = (0, 1, 2, 5, 10, 4, 8, 9)
_W_SMALL = ((_IN_OFF[3], GLA_RANK), (_IN_OFF[6], 2 * GDN_HEADS))
_W_ALL_COLS = _PA_COLS + _PB_COLS + LANES
_W_RB = 512


def _w_block_rows():
    tab = []
    for i in _W_ORDER:
        assert (_IN_OFF[i + 1] - _IN_OFF[i]) % _W_RB == 0
        tab.extend(range(_IN_OFF[i], _IN_OFF[i + 1], _W_RB))
    return np.asarray(tab, np.int32)


def _relayout_kernel(tab_ref, wt_hbm, o_ref, buf, sem):
    j = pl.program_id(0)
    n_main = pl.num_programs(0) - 1

    def block_copy(blk):
        start = pl.multiple_of(tab_ref[blk], 16)
        return pltpu.make_async_copy(wt_hbm.at[pl.ds(start, _W_RB), :], buf.at[blk % 2], sem.at[blk % 2])

    @pl.when(j == 0)
    def _():
        block_copy(j).start()

    @pl.when(j + 1 < n_main)
    def _():
        block_copy(j + 1).start()

    @pl.when(j < n_main)
    def _():
        block_copy(j).wait()
        o_ref[...] = buf[j % 2].astype(BF16)

    @pl.when(j == n_main)
    def _():
        dst = 0
        copies = []
        for n, (row0, rows) in enumerate(_W_SMALL):
            copies.append(pltpu.make_async_copy(wt_hbm.at[pl.ds(row0, rows), :], buf.at[0, pl.ds(dst, rows), :],
                                                sem.at[n]))
            dst += rows
        for cp in copies:
            cp.start()
        for cp in copies:
            cp.wait()
        o_ref[0:dst, :] = buf[0, 0:dst, :].astype(BF16)
        o_ref[dst:, :] = jnp.zeros((_W_RB - dst, o_ref.shape[1]), BF16)


def _split_w_in(w_in_t):
    n, k = w_in_t.shape
    tab = _w_block_rows()
    assert len(_W_SMALL) <= 2
    return pl.pallas_call(
        _relayout_kernel,
        grid_spec=pltpu.PrefetchScalarGridSpec(
            num_scalar_prefetch=1,
            grid=(len(tab) + 1,),
            in_specs=[pl.BlockSpec(memory_space=pl.ANY)],
            out_specs=pl.BlockSpec((_W_RB, k), lambda i, tab: (i, 0)),
            scratch_shapes=[pltpu.VMEM((2, _W_RB, k), F32), pltpu.SemaphoreType.DMA((2,))],
        ),
        out_shape=jax.ShapeDtypeStruct((_W_ALL_COLS, k), BF16),
        compiler_params=_params(("arbitrary",)),
        name="w_in_relayout",
    )(jnp.asarray(tab), w_in_t)


def kernel(x, mem, n_ffn1_pre, w_ffn1_gu, w_ffn1_down, n_ffn1_post, n_mix_pre, w_in, gla_w_lr2, gla_b_lr,
           gla_norm, gdn_conv, gdn_a_log, gdn_dt_bias, gdn_norm, mem_norm, w_mem_kv, w_up_gla, w_up_gdn,
           w_up_xa, w_out, n_mix_post, n_ffn2_pre, w_ffn2_gu, w_ffn2_down, n_ffn2_post):
    bsz, seq, d = x.shape
    t = bsz * seq
    h = x.reshape(t, d)
    for l in range(n_ffn1_pre.shape[0]):
        row = lambda a: a[l][None, :]
        h, u = _ffn(h, row(n_ffn1_pre), w_ffn1_gu[l], w_ffn1_down[l],
                    row(n_ffn1_post), row(n_mix_pre), emit_next=True)

        w_all = _split_w_in(w_in[l].T)
        par = jnp.zeros((2, LANES), F32)
        par = par.at[0, SM_A:SM_A + GDN_HEADS].set(gdn_a_log[l]).at[1, SM_A:SM_A + GDN_HEADS].set(gdn_dt_bias[l])
        tn_g = 512
        pa = _matmul(u, w_all, 0, _PA_GDN, F32, 2048, 1024, "in_proj_a")
        pg = _matmul(u, w_all, _PA_GDN, _PA_COLS - _PA_GDN, F32, seq, tn_g, "in_proj_g", body=_proj_conv_kernel,
                     extra=(gdn_conv[l],), extra_specs=(pl.BlockSpec((CONV_W, tn_g), lambda i, j: (0, j)),))
        pb, ps = _proj_b(u, w_all, par)

        wlr = jnp.pad(gla_w_lr2[l], ((0, LANES - GLA_RANK), (0, 0))).astype(BF16)
        o_gla = _gla(pa, pb, ps, wlr, row(gla_b_lr), row(gla_norm), bsz, seq)
        o_gdn = _gdn(pg, pb, ps, row(gdn_norm), bsz, seq)

        mkv = _norm_matmul(mem.reshape(bsz * MEM_LEN, d), row(mem_norm), w_mem_kv[l], BF16,
                           bsz * MEM_LEN, 512, "mem_kv")
        o_xa = _xa(pb, mkv, bsz, seq)

        h = _merge(h, o_gla, o_gdn, o_xa, pb, w_up_gla[l], w_up_gdn[l], w_up_xa[l], w_out[l], row(n_mix_post))
        h = _ffn(h, row(n_ffn2_pre), w_ffn2_gu[l], w_ffn2_down[l],
                 row(n_ffn2_post), row(n_ffn2_post), emit_next=False)
    return h.reshape(bsz, seq, d)
```
